```python
import jax, jax.numpy as jnp
from jax import lax
import numpy as np

D_MODEL = 1024
BATCH = 16
SEQ = 4096
DEPTH = 2
DEC_BATCH = 8
DEC_SEQ = 16
PAST_LEN = 2048

CHUNK = 64
Q_BLOCK = 128
N_EVEN = (DEPTH + 1) // 2
N_ODD = DEPTH // 2
FFN_DIM = 2816
MLA_HEADS = 8
MLA_NOPE = 64
MLA_ROPE = 32
MLA_QK = MLA_NOPE + MLA_ROPE
MLA_V = 64
MLA_Q_LORA = 256
MLA_KV_LORA = 128
ROPE_THETA = 10000.0
LRU_WIDTH = 512
LRU_BLOCKS = 8
LRU_BLOCK = LRU_WIDTH // LRU_BLOCKS
LRU_C = 8.0
CONV_WIDTH = 4
EVEN_IN = MLA_Q_LORA + MLA_KV_LORA + MLA_ROPE + 2 * LRU_WIDTH
EVEN_MIX = MLA_HEADS * MLA_V + LRU_WIDTH
FOX_HEADS = 16
FOX_HEAD_DIM = 64
FOX_WIDTH = FOX_HEADS * FOX_HEAD_DIM
ODD_IN = 3 * FOX_WIDTH + FOX_HEADS
MEM_TOKENS = 256
MEM_HEADS = 4
MEM_HEAD_DIM = 128
MEM_WIDTH = MEM_HEADS * MEM_HEAD_DIM
NORM_EPS = 1e-6
NEG_INF = -1e30

kernel_name = 'streaming_mla_rglru_fox_macaron_step'


def _rms_norm(x, g):
    xf = x.astype(jnp.float32)
    y = xf * lax.rsqrt(jnp.mean(xf * xf, axis=-1, keepdims=True) + NORM_EPS)
    return (y * g.astype(jnp.float32)).astype(x.dtype)


def _swiglu(x, w_in, w_out):
    gate, up = jnp.split(x @ w_in, 2, axis=-1)
    return (jax.nn.silu(gate) * up) @ w_out


def _rope(x, pos):
    half = x.shape[-1] // 2
    inv_freq = ROPE_THETA ** (-jnp.arange(half, dtype=jnp.float32) / half)
    ang = pos.astype(jnp.float32)[:, None] * inv_freq[None, :]
    ang = ang.reshape(ang.shape[:1] + (1,) * (x.ndim - 3) + (half,))
    cos, sin = jnp.cos(ang), jnp.sin(ang)
    xf = x.astype(jnp.float32)
    x1, x2 = xf[..., :half], xf[..., half:]
    return jnp.concatenate([x1 * cos - x2 * sin, x2 * cos + x1 * sin], axis=-1).astype(x.dtype)


def _attend_block(q, k, v, mask, bias):
    s = jnp.einsum('bqhd,bkhd->bhqk', q, k).astype(jnp.float32) * (q.shape[-1] ** -0.5)
    if bias is not None:
        s = s + bias
    if mask is not None:
        s = jnp.where(mask, s, NEG_INF)
    p = jax.nn.softmax(s, axis=-1)
    return jnp.einsum('bhqk,bkhd->bqhd', p.astype(v.dtype), v)


def _attention(q, k, v, q_pos, k_pos, chunk_causal, c_q=None, c_k=None):
    def block(qb, qp, cqb):
        if chunk_causal:
            mask = (k_pos[None, :] // CHUNK) <= (qp[:, None] // CHUNK)
        else:
            mask = k_pos[None, :] <= qp[:, None]
        bias = None
        if cqb is not None:
            bias = jnp.swapaxes(cqb, 1, 2)[..., :, None] - jnp.swapaxes(c_k, 1, 2)[..., None, :]
        return _attend_block(qb, k, v, mask, bias)

    B, Tq = q.shape[0], q.shape[1]
    if Tq <= Q_BLOCK:
        return block(q, q_pos, c_q)
    nb = Tq // Q_BLOCK
    to_blocks = lambda a: jnp.swapaxes(a.reshape((B, nb, Q_BLOCK) + a.shape[2:]), 0, 1)
    pos_b = q_pos.reshape(nb, Q_BLOCK)
    if c_q is None:
        out = lax.map(lambda xs: block(xs[0], xs[1], None), (to_blocks(q), pos_b))
    else:
        out = lax.map(lambda xs: block(xs[0], xs[1], xs[2]), (to_blocks(q), pos_b, to_blocks(c_q)))
    return jnp.swapaxes(out, 0, 1).reshape((B, Tq) + out.shape[3:])


def _linear_combine(e1, e2):
    a1, b1 = e1
    a2, b2 = e2
    return a1 * a2, a2 * b1 + b2


def _even_mixer(h, pos, w_in, g_qlat, g_kvlat, w_uq, w_ukv, g_q, g_k, conv_w, conv_b,
                gate_w, gate_b, lam, w_out, past_latent, past_krope, lru_h0, conv_prev):
    B, T, _ = h.shape
    z = h @ w_in
    c_q, c_kv, k_rope, x_rec, x_gate = jnp.split(
        z, [MLA_Q_LORA, MLA_Q_LORA + MLA_KV_LORA, MLA_Q_LORA + MLA_KV_LORA + MLA_ROPE,
            MLA_Q_LORA + MLA_KV_LORA + MLA_ROPE + LRU_WIDTH], axis=-1)
    q = (_rms_norm(c_q, g_qlat) @ w_uq).reshape(B, T, MLA_HEADS, MLA_QK)
    q = _rms_norm(jnp.concatenate([q[..., :MLA_NOPE], _rope(q[..., MLA_NOPE:], pos)], axis=-1), g_q)
    latent_new = _rms_norm(c_kv, g_kvlat)
    krope_new = _rope(k_rope, pos)
    latent = jnp.concatenate([past_latent, latent_new], axis=1)
    krope = jnp.concatenate([past_krope, krope_new], axis=1)
    L = latent.shape[1]
    kv = (latent @ w_ukv).reshape(B, L, MLA_HEADS, MLA_NOPE + MLA_V)
    k = _rms_norm(jnp.concatenate(
        [kv[..., :MLA_NOPE], jnp.broadcast_to(krope[:, :, None, :], (B, L, MLA_HEADS, MLA_ROPE)).astype(kv.dtype)],
        axis=-1), g_k)
    v = kv[..., MLA_NOPE:]
    attn = _attention(q, k, v, pos, jnp.arange(L), True).reshape(B, T, MLA_HEADS * MLA_V)
    u = jnp.concatenate([conv_prev, x_rec], axis=1)
    xc = conv_b + sum(conv_w[j] * u[:, j:j + T] for j in range(CONV_WIDTH))
    gates = jnp.einsum('btnc,ncd->btnd', xc.reshape(B, T, LRU_BLOCKS, LRU_BLOCK), gate_w) + gate_b
    r = jax.nn.sigmoid(gates[..., :LRU_BLOCK].astype(jnp.float32)).reshape(B, T, LRU_WIDTH)
    i = jax.nn.sigmoid(gates[..., LRU_BLOCK:].astype(jnp.float32)).reshape(B, T, LRU_WIDTH)
    log_a = -LRU_C * r * jax.nn.softplus(-lam.astype(jnp.float32))
    a = jnp.exp(log_a)
    b = jnp.sqrt(-jnp.expm1(2.0 * log_a)) * (i * xc.astype(jnp.float32))
    b = b.at[:, 0].add(a[:, 0] * lru_h0.astype(jnp.float32))
    _, hs = lax.associative_scan(_linear_combine, (a, b), axis=1)
    y_rec = jax.nn.gelu(x_gate) * hs.astype(h.dtype)
    out = jnp.concatenate([attn, y_rec], axis=-1) @ w_out
    return out, latent_new, krope_new, hs[:, -1].astype(h.dtype), u[:, -(CONV_WIDTH - 1):]


def _odd_mixer(h, pos, w_in, b_f, g_q, g_k, w_out, past_k, past_v, past_logf):
    B, T, _ = h.shape
    q, k, v, f = jnp.split(h @ w_in, [FOX_WIDTH, 2 * FOX_WIDTH, 3 * FOX_WIDTH], axis=-1)
    q = _rms_norm(q.reshape(B, T, FOX_HEADS, FOX_HEAD_DIM), g_q)
    k_new = _rms_norm(k.reshape(B, T, FOX_HEADS, FOX_HEAD_DIM), g_k)
    v_new = v.reshape(B, T, FOX_HEADS, FOX_HEAD_DIM)
    logf_new = jax.nn.log_sigmoid((f + b_f).astype(jnp.float32))
    P = past_k.shape[1]
    k_all = jnp.concatenate([past_k, k_new], axis=1)
    v_all = jnp.concatenate([past_v, v_new], axis=1)
    logf_all = jnp.concatenate([past_logf.astype(jnp.float32), logf_new], axis=1)
    c = jnp.cumsum(logf_all, axis=1)
    attn = _attention(q, k_all, v_all, pos, jnp.arange(P + T), False, c[:, P:], c)
    out = attn.reshape(B, T, FOX_WIDTH) @ w_out
    return out, k_new, v_new, logf_new


def _mem_kv(mem, g_src, w_kv, g_k):
    B, M, _ = mem.shape
    kv = (_rms_norm(mem, g_src) @ w_kv).reshape(B, M, MEM_HEADS, 2 * MEM_HEAD_DIM)
    return _rms_norm(kv[..., :MEM_HEAD_DIM], g_k), kv[..., MEM_HEAD_DIM:]


def _mem_attend(h, w_q, g_q, mem_k, mem_v, w_o):
    B, T, _ = h.shape
    q = _rms_norm((h @ w_q).reshape(B, T, MEM_HEADS, MEM_HEAD_DIM), g_q)
    return _attend_block(q, mem_k, mem_v, None, None).reshape(B, T, MEM_WIDTH) @ w_o


def _layer(x, li, pos, prm, mem_k, mem_v, state):
    h = x + 0.5 * _swiglu(_rms_norm(x, prm['norm_ffn1'][li]), prm['ffn1_w_in'][li], prm['ffn1_w_out'][li])
    hn = _rms_norm(h, prm['norm_mix'][li])
    j = li // 2
    if li % 2 == 0:
        mix, *new = _even_mixer(hn, pos, prm['ev_w_in'][j], prm['ev_g_qlat'][j], prm['ev_g_kvlat'][j],
                                prm['ev_w_uq'][j], prm['ev_w_ukv'][j], prm['ev_g_q'][j], prm['ev_g_k'][j],
                                prm['ev_conv_w'][j], prm['ev_conv_b'][j], prm['ev_gate_w'][j],
                                prm['ev_gate_b'][j], prm['ev_lambda'][j], prm['ev_w_out'][j], *state)
    else:
        mix, *new = _odd_mixer(hn, pos, prm['od_w_in'][j], prm['od_b_f'][j], prm['od_g_q'][j],
                               prm['od_g_k'][j], prm['od_w_out'][j], *state)
    h = h + mix
    h = h + _mem_attend(_rms_norm(h, prm['norm_mem'][li]), prm['mem_w_q'][li], prm['mem_g_q'][li],
                        mem_k, mem_v, prm['mem_w_o'][li])
    h = h + 0.5 * _swiglu(_rms_norm(h, prm['norm_ffn2'][li]), prm['ffn2_w_in'][li], prm['ffn2_w_out'][li])
    return h, new


def _trunk(x, past_len, prm, mem_kvs, even_states, odd_states):
    pos = past_len + jnp.arange(x.shape[1])
    even_new, odd_new = [], []
    for li in range(DEPTH):
        st = even_states[li // 2] if li % 2 == 0 else odd_states[li // 2]
        x, new = _layer(x, li, pos, prm, mem_kvs[li][0], mem_kvs[li][1], st)
        if li % 2 == 0:
            even_new.append(new)
        else:
            odd_new.append(new)
    return x, even_new, odd_new


def setup_inputs(seed: int = 0) -> dict:
    key = jax.random.key(seed)
    ks = iter(jax.random.split(key, 64))
    f32 = jnp.float32

    def w(shape, fan_in):
        return jax.random.normal(next(ks), shape, f32) * (fan_in ** -0.5)

    def gain(shape):
        return 1.0 + 0.01 * jax.random.normal(next(ks), shape, f32)

    def rnd(shape, s=1.0):
        return s * jax.random.normal(next(ks), shape, f32)

    u = jax.random.uniform(next(ks), (N_EVEN, LRU_WIDTH), f32, 0.9, 0.999)
    a0 = u ** (1.0 / LRU_C)
    lam = jnp.log(a0) - jnp.log1p(-a0)
    return {
        'x_prompt': rnd((BATCH, SEQ, D_MODEL)),
        'x_sample': rnd((DEC_BATCH, DEC_SEQ, D_MODEL)),
        'mem_prompt': rnd((BATCH, MEM_TOKENS, D_MODEL)),
        'cache_mla_latent': rnd((N_EVEN, DEC_BATCH, PAST_LEN, MLA_KV_LORA)),
        'cache_mla_krope': rnd((N_EVEN, DEC_BATCH, PAST_LEN, MLA_ROPE)),
        'state_lru_h': rnd((N_EVEN, DEC_BATCH, LRU_WIDTH), 0.5),
        'state_lru_conv': rnd((N_EVEN, DEC_BATCH, CONV_WIDTH - 1, LRU_WIDTH)),
        'cache_fox_k': rnd((N_ODD, DEC_BATCH, PAST_LEN, FOX_HEADS, FOX_HEAD_DIM)),
        'cache_fox_v': rnd((N_ODD, DEC_BATCH, PAST_LEN, FOX_HEADS, FOX_HEAD_DIM)),
        'cache_fox_logf': jax.nn.log_sigmoid(3.0 + rnd((N_ODD, DEC_BATCH, PAST_LEN, FOX_HEADS), 0.5)),
        'cache_mem_k': rnd((DEPTH, DEC_BATCH, MEM_TOKENS, MEM_HEADS, MEM_HEAD_DIM)),
        'cache_mem_v': rnd((DEPTH, DEC_BATCH, MEM_TOKENS, MEM_HEADS, MEM_HEAD_DIM)),
        'norm_ffn1': gain((DEPTH, D_MODEL)),
        'ffn1_w_in': w((DEPTH, D_MODEL, 2 * FFN_DIM), D_MODEL),
        'ffn1_w_out': w((DEPTH, FFN_DIM, D_MODEL), FFN_DIM),
        'norm_mix': gain((DEPTH, D_MODEL)),
        'norm_mem': gain((DEPTH, D_MODEL)),
        'norm_mem_src': gain((DEPTH, D_MODEL)),
        'mem_w_q': w((DEPTH, D_MODEL, MEM_WIDTH), D_MODEL),
        'mem_w_kv': w((DEPTH, D_MODEL, 2 * MEM_WIDTH), D_MODEL),
        'mem_w_o': w((DEPTH, MEM_WIDTH, D_MODEL), MEM_WIDTH),
        'mem_g_q': gain((DEPTH, MEM_HEAD_DIM)),
        'mem_g_k': gain((DEPTH, MEM_HEAD_DIM)),
        'norm_ffn2': gain((DEPTH, D_MODEL)),
        'ffn2_w_in': w((DEPTH, D_MODEL, 2 * FFN_DIM), D_MODEL),
        'ffn2_w_out': w((DEPTH, FFN_DIM, D_MODEL), FFN_DIM),
        'ev_w_in': w((N_EVEN, D_MODEL, EVEN_IN), D_MODEL),
        'ev_g_qlat': gain((N_EVEN, MLA_Q_LORA)),
        'ev_g_kvlat': gain((N_EVEN, MLA_KV_LORA)),
        'ev_w_uq': w((N_EVEN, MLA_Q_LORA, MLA_HEADS * MLA_QK), MLA_Q_LORA),
        'ev_w_ukv': w((N_EVEN, MLA_KV_LORA, MLA_HEADS * (MLA_NOPE + MLA_V)), MLA_KV_LORA),
        'ev_g_q': gain((N_EVEN, MLA_QK)),
        'ev_g_k': gain((N_EVEN, MLA_QK)),
        'ev_conv_w': w((N_EVEN, CONV_WIDTH, LRU_WIDTH), CONV_WIDTH),
        'ev_conv_b': rnd((N_EVEN, LRU_WIDTH), 0.01),
        'ev_gate_w': w((N_EVEN, LRU_BLOCKS, LRU_BLOCK, 2 * LRU_BLOCK), LRU_BLOCK),
        'ev_gate_b': rnd((N_EVEN, LRU_BLOCKS, 2 * LRU_BLOCK), 0.1),
        'ev_lambda': lam,
        'ev_w_out': w((N_EVEN, EVEN_MIX, D_MODEL), EVEN_MIX),
        'od_w_in': w((N_ODD, D_MODEL, ODD_IN), D_MODEL),
        'od_b_f': 3.0 + rnd((N_ODD, FOX_HEADS), 0.1),
        'od_g_q': gain((N_ODD, FOX_HEAD_DIM)),
        'od_g_k': gain((N_ODD, FOX_HEAD_DIM)),
        'od_w_out': w((N_ODD, FOX_WIDTH, D_MODEL), FOX_WIDTH),
    }


def reference(x_prompt, x_sample, mem_prompt,
              cache_mla_latent, cache_mla_krope, state_lru_h, state_lru_conv,
              cache_fox_k, cache_fox_v, cache_fox_logf, cache_mem_k, cache_mem_v,
              norm_ffn1, ffn1_w_in, ffn1_w_out, norm_mix, norm_mem, norm_mem_src,
              mem_w_q, mem_w_kv, mem_w_o, mem_g_q, mem_g_k, norm_ffn2, ffn2_w_in, ffn2_w_out,
              ev_w_in, ev_g_qlat, ev_g_kvlat, ev_w_uq, ev_w_ukv, ev_g_q, ev_g_k,
              ev_conv_w, ev_conv_b, ev_gate_w, ev_gate_b, ev_lambda, ev_w_out,
              od_w_in, od_b_f, od_g_q, od_g_k, od_w_out):
    prm = dict(norm_ffn1=norm_ffn1, ffn1_w_in=ffn1_w_in, ffn1_w_out=ffn1_w_out, norm_mix=norm_mix,
               norm_mem=norm_mem, mem_w_q=mem_w_q, mem_w_o=mem_w_o, mem_g_q=mem_g_q,
               norm_ffn2=norm_ffn2, ffn2_w_in=ffn2_w_in, ffn2_w_out=ffn2_w_out,
               ev_w_in=ev_w_in, ev_g_qlat=ev_g_qlat, ev_g_kvlat=ev_g_kvlat, ev_w_uq=ev_w_uq,
               ev_w_ukv=ev_w_ukv, ev_g_q=ev_g_q, ev_g_k=ev_g_k, ev_conv_w=ev_conv_w,
               ev_conv_b=ev_conv_b, ev_gate_w=ev_gate_w, ev_gate_b=ev_gate_b, ev_lambda=ev_lambda,
               ev_w_out=ev_w_out, od_w_in=od_w_in, od_b_f=od_b_f, od_g_q=od_g_q, od_g_k=od_g_k,
               od_w_out=od_w_out)

    B, dt = x_prompt.shape[0], x_prompt.dtype
    mem_p = [_mem_kv(mem_prompt, norm_mem_src[li], mem_w_kv[li], mem_g_k[li]) for li in range(DEPTH)]
    ev0 = [(jnp.zeros((B, 0, MLA_KV_LORA), dt), jnp.zeros((B, 0, MLA_ROPE), dt),
            jnp.zeros((B, LRU_WIDTH), dt), jnp.zeros((B, CONV_WIDTH - 1, LRU_WIDTH), dt))
           for _ in range(N_EVEN)]
    od0 = [(jnp.zeros((B, 0, FOX_HEADS, FOX_HEAD_DIM), dt), jnp.zeros((B, 0, FOX_HEADS, FOX_HEAD_DIM), dt),
            jnp.zeros((B, 0, FOX_HEADS), jnp.float32)) for _ in range(N_ODD)]
    y_prompt, ev_p, od_p = _trunk(x_prompt, 0, prm, mem_p, ev0, od0)

    past_len = cache_mla_latent.shape[2]
    mem_s = [(cache_mem_k[li], cache_mem_v[li]) for li in range(DEPTH)]
    ev_s = [(cache_mla_latent[j], cache_mla_krope[j], state_lru_h[j], state_lru_conv[j]) for j in range(N_EVEN)]
    od_s = [(cache_fox_k[j], cache_fox_v[j], cache_fox_logf[j]) for j in range(N_ODD)]
    y_sample, ev_n, od_n = _trunk(x_sample, past_len, prm, mem_s, ev_s, od_s)

    p_mla_latent, p_mla_krope, p_lru_h, p_lru_conv = [jnp.stack([s[f] for s in ev_p]) for f in range(4)]
    p_fox_k, p_fox_v, p_fox_logf = [jnp.stack([s[f] for s in od_p]) for f in range(3)]
    p_mem_k = jnp.stack([kv[0] for kv in mem_p])
    p_mem_v = jnp.stack([kv[1] for kv in mem_p])
    s_mla_latent, s_mla_krope, s_lru_h, s_lru_conv = [jnp.stack([s[f] for s in ev_n]) for f in range(4)]
    s_fox_k, s_fox_v, s_fox_logf = [jnp.stack([s[f] for s in od_n]) for f in range(3)]
    return (y_prompt, y_sample, p_mla_latent, p_mla_krope, p_lru_h, p_lru_conv, p_fox_k, p_fox_v,
            p_fox_logf, p_mem_k, p_mem_v, s_mla_latent, s_mla_krope, s_lru_h, s_lru_conv,
            s_fox_k, s_fox_v, s_fox_logf)
```

```python
import functools
import math

import jax
import jax.numpy as jnp
from jax import lax
from jax.experimental import pallas as pl
from jax.experimental.pallas import tpu as pltpu

F32 = jnp.float32
BF16 = jnp.bfloat16

NORM_EPS = 1e-6
NEG_INF = -1e30
CHUNK = 64
LANES = 128
ROPE_THETA = 10000.0
LRU_C = 8.0
MLA_HEADS = 8
MLA_NOPE = 64
MLA_ROPE = 32
MLA_QK = MLA_NOPE + MLA_ROPE
MLA_V = 64
MLA_Q_LORA = 256
MLA_KV_LORA = 128
LRU_WIDTH = 512
LRU_BLOCKS = 8
CONV_WIDTH = 4
FOX_HEADS = 16
FOX_HEAD_DIM = 64
FOX_WIDTH = FOX_HEADS * FOX_HEAD_DIM
MEM_HEADS = 4
MEM_HEAD_DIM = 128
MEM_WIDTH = MEM_HEADS * MEM_HEAD_DIM
HALO = 8

VMEM_LIMIT = 56 * 1024 * 1024


def _dot(a, b):
    return jnp.dot(a, b, preferred_element_type=F32)


def _dot_nt(a, b):
    return lax.dot_general(a, b, (((1,), (1,)), ((), ())), preferred_element_type=F32)


def _rms(x, g):
    return x * lax.rsqrt(jnp.mean(x * x, axis=-1, keepdims=True) + NORM_EPS) * g


def _head_rms(x, g, n_live):
    ss = jnp.sum(x * x, axis=-1, keepdims=True) * (1.0 / n_live)
    return x * lax.rsqrt(ss + NORM_EPS) * g


def _sigmoid(x):
    return 1.0 / (1.0 + jnp.exp(-x))


def _log1p(y):
    u = 1.0 + y
    d = u - 1.0
    return jnp.where(d == 0.0, y, jnp.log(u) * (y / jnp.where(d == 0.0, 1.0, d)))


def _softplus(x):
    return jnp.maximum(x, 0.0) + _log1p(jnp.exp(-jnp.abs(x)))


def _gelu_tanh(x):
    return 0.5 * x * (1.0 + jnp.tanh(math.sqrt(2.0 / math.pi) * (x + 0.044715 * (x * x * x))))


def _split_bf16(x, parts):
    out = []
    r = x
    for _ in range(parts):
        p = r.astype(BF16)
        out.append(p)
        r = r - p.astype(F32)
    return out


def _rope128(x, c, s1, s2):
    return x * c + pltpu.roll(x, LANES - 16, 1) * s1 + pltpu.roll(x, 16, 1) * s2


def _const_spec(shape):
    nd = len(shape)
    return pl.BlockSpec(shape, lambda *_: (0,) * nd, pipeline_mode=pl.Buffered(1))


def _params(*sem):
    return pltpu.CompilerParams(dimension_semantics=sem, vmem_limit_bytes=VMEM_LIMIT)


def _row_tile(n, cap):
    t = min(n, cap)
    assert n % t == 0, (n, t)
    return t


def _ffn_body(x_ref, g_ref, wg_ref, wu_ref, wo_ref, o_ref, *, n_chunks):
    x = x_ref[...]
    hb = _rms(x, g_ref[...]).astype(BF16)
    fc = wg_ref.shape[1] // n_chunks
    acc = jnp.zeros_like(x)
    for c in range(n_chunks):
        sl = slice(c * fc, (c + 1) * fc)
        gate = _dot(hb, wg_ref[:, sl])
        up = _dot(hb, wu_ref[:, sl])
        act = (gate * _sigmoid(gate) * up).astype(BF16)
        acc = acc + _dot(act, wo_ref[sl, :])
    o_ref[...] = x + 0.5 * acc


def _ffn(x2, w):
    n, d = x2.shape
    f = w["wg"].shape[1]
    tm = _row_tile(n, 512)
    return pl.pallas_call(
        functools.partial(_ffn_body, n_chunks=2),
        grid=(n // tm,),
        in_specs=[pl.BlockSpec((tm, d), lambda i: (i, 0)), _const_spec((1, d)),
                  _const_spec((d, f)), _const_spec((d, f)), _const_spec((f, d))],
        out_specs=pl.BlockSpec((tm, d), lambda i: (i, 0)),
        out_shape=jax.ShapeDtypeStruct((n, d), F32),
        compiler_params=_params("parallel"),
        name="ffn",
    )(x2, w["g"], w["wg"], w["wu"], w["wo"])


def _out_proj_body(*refs, n_parts):
    h_ref = refs[0]
    o_ref = refs[1 + 2 * n_parts]
    acc = h_ref[...]
    for i in range(n_parts):
        acc = acc + _dot(refs[1 + i][...], refs[1 + n_parts + i][...])
    o_ref[...] = acc


def _out_proj(h2, parts, weights):
    n, d = h2.shape
    tm = _row_tile(n, 512)
    in_specs = [pl.BlockSpec((tm, d), lambda i: (i, 0))]
    in_specs += [pl.BlockSpec((tm, p.shape[1]), lambda i: (i, 0)) for p in parts]
    in_specs += [_const_spec(w.shape) for w in weights]
    return pl.pallas_call(
        functools.partial(_out_proj_body, n_parts=len(parts)),
        grid=(n // tm,),
        in_specs=in_specs,
        out_specs=pl.BlockSpec((tm, d), lambda i: (i, 0)),
        out_shape=jax.ShapeDtypeStruct((n, d), F32),
        compiler_params=_params("parallel"),
        name="out_proj",
    )(h2, *parts, *weights)


def _mem_kv_body(m_ref, g_ref, wk_ref, wv_ref, gk_ref, k32_ref, v32_ref, kb_ref, vb_ref):
    hb = _rms(m_ref[0], g_ref[...]).astype(BF16)
    k = _dot(hb, wk_ref[...])
    v = _dot(hb, wv_ref[...])
    for hd in range(MEM_HEADS):
        sl = slice(hd * MEM_HEAD_DIM, (hd + 1) * MEM_HEAD_DIM)
        kh = _head_rms(k[:, sl], gk_ref[...], MEM_HEAD_DIM)
        k32_ref[0, :, sl] = kh
        kb_ref[0, :, sl] = kh.astype(BF16)
    v32_ref[0] = v
    vb_ref[0] = v.astype(BF16)


def _mem_kv(mem, w):
    b, m, d = mem.shape
    blk = lambda: pl.BlockSpec((1, m, MEM_WIDTH), lambda i: (i, 0, 0))
    return pl.pallas_call(
        _mem_kv_body,
        grid=(b,),
        in_specs=[pl.BlockSpec((1, m, d), lambda i: (i, 0, 0)), _const_spec((1, d)),
                  _const_spec((d, MEM_WIDTH)), _const_spec((d, MEM_WIDTH)), _const_spec((1, MEM_HEAD_DIM))],
        out_specs=[blk(), blk(), blk(), blk()],
        out_shape=[jax.ShapeDtypeStruct((b, m, MEM_WIDTH), F32), jax.ShapeDtypeStruct((b, m, MEM_WIDTH), F32),
                   jax.ShapeDtypeStruct((b, m, MEM_WIDTH), BF16), jax.ShapeDtypeStruct((b, m, MEM_WIDTH), BF16)],
        compiler_params=_params("parallel"),
        name="mem_kv",
    )(mem, w["g_src"], w["wk"], w["wv"], w["gk"])


def _mem_attn_body(h_ref, g_ref, wq_ref, gq_ref, mk_ref, mv_ref, wo_ref, o_ref):
    h = h_ref[0]
    hb = _rms(h, g_ref[...]).astype(BF16)
    q = _dot(hb, wq_ref[...])
    outs = []
    for hd in range(MEM_HEADS):
        sl = slice(hd * MEM_HEAD_DIM, (hd + 1) * MEM_HEAD_DIM)
        qh = _head_rms(q[:, sl], gq_ref[...], MEM_HEAD_DIM).astype(BF16)
        s = _dot_nt(qh, mk_ref[0, :, sl])
        e = jnp.exp(s - jnp.max(s, axis=-1, keepdims=True))
        p = e / jnp.sum(e, axis=-1, keepdims=True)
        outs.append(_dot(p.astype(BF16), mv_ref[0, :, sl]).astype(BF16))
    o_ref[0] = h + _dot(jnp.concatenate(outs, axis=-1), wo_ref[...])


def _mem_attn(h, mk, mv, w):
    b, t, d = h.shape
    m = mk.shape[1]
    tm = _row_tile(t, 512)
    return pl.pallas_call(
        _mem_attn_body,
        grid=(b, t // tm),
        in_specs=[pl.BlockSpec((1, tm, d), lambda i, j: (i, j, 0)), _const_spec((1, d)),
                  _const_spec((d, MEM_WIDTH)), _const_spec((1, MEM_HEAD_DIM)),
                  pl.BlockSpec((1, m, MEM_WIDTH), lambda i, j: (i, 0, 0)),
                  pl.BlockSpec((1, m, MEM_WIDTH), lambda i, j: (i, 0, 0)),
                  _const_spec((MEM_WIDTH, d))],
        out_specs=pl.BlockSpec((1, tm, d), lambda i, j: (i, j, 0)),
        out_shape=jax.ShapeDtypeStruct((b, t, d), F32),
        compiler_params=_params("parallel", "parallel"),
        name="mem_attn",
    )(h, w["g"], w["wq"], w["gq"], mk, mv, w["wo"])


def _mla_in_body(h_ref, g_ref, wcq_ref, gql_ref, wuq_ref, gq_ref, wckv_ref, gkv_ref, wkr_ref,
                 c_ref, s1_ref, s2_ref, q_ref, lat_ref, krp_ref):
    hb = _rms(h_ref[0], g_ref[...]).astype(BF16)
    c, s1, s2 = c_ref[...], s1_ref[...], s2_ref[...]
    cq = _rms(_dot(hb, wcq_ref[...]), gql_ref[...]).astype(BF16)
    q = _dot(cq, wuq_ref[...])
    for hd in range(MLA_HEADS):
        sl = slice(hd * LANES, (hd + 1) * LANES)
        qh = _rope128(q[:, sl], c, s1, s2)
        q_ref[0, :, sl] = _head_rms(qh, gq_ref[...], MLA_QK).astype(BF16)
    lat_ref[0] = _rms(_dot(hb, wckv_ref[...]), gkv_ref[...])
    krp_ref[0] = _rope128(_dot(hb, wkr_ref[...]), c, s1, s2)


def _mla_in(h, w, tables):
    b, t, d = h.shape
    tm = _row_tile(t, 512)
    row = lambda n: pl.BlockSpec((1, tm, n), lambda i, j: (i, j, 0))
    tab = lambda: pl.BlockSpec((tm, LANES), lambda i, j: (j, 0))
    return pl.pallas_call(
        _mla_in_body,
        grid=(b, t // tm),
        in_specs=[row(d), _const_spec((1, d)),
                  _const_spec((d, MLA_Q_LORA)), _const_spec((1, MLA_Q_LORA)),
                  _const_spec((MLA_Q_LORA, MLA_HEADS * LANES)), _const_spec((1, LANES)),
                  _const_spec((d, MLA_KV_LORA)), _const_spec((1, MLA_KV_LORA)), _const_spec((d, LANES)),
                  tab(), tab(), tab()],
        out_specs=[row(MLA_HEADS * LANES), row(MLA_KV_LORA), row(LANES)],
        out_shape=[jax.ShapeDtypeStruct((b, t, MLA_HEADS * LANES), BF16),
                   jax.ShapeDtypeStruct((b, t, MLA_KV_LORA), F32),
                   jax.ShapeDtypeStruct((b, t, LANES), F32)],
        compiler_params=_params("parallel", "parallel"),
        name="mla_in",
    )(h, w["g_mix"], w["wcq"], w["g_qlat"], w["wuq"], w["gq"], w["wckv"], w["g_kvlat"], w["wkr"], *tables)


def _mla_kv_body(lat_ref, krp_ref, wuk_ref, wuv_ref, gk_ref, k_ref, v_ref):
    lb = lat_ref[0].astype(BF16)
    krp = krp_ref[0]
    kn = _dot(lb, wuk_ref[...])
    for hd in range(MLA_HEADS):
        sl = slice(hd * LANES, (hd + 1) * LANES)
        k_ref[0, :, sl] = _head_rms(kn[:, sl] + krp, gk_ref[...], MLA_QK).astype(BF16)
    v_ref[0] = _dot(lb, wuv_ref[...]).astype(BF16)


def _mla_kv(lat, krp, w):
    b, l, _ = lat.shape
    tl = _row_tile(l, 512) if l % 512 == 0 else l
    row = lambda n: pl.BlockSpec((1, tl, n), lambda i, j: (i, j, 0))
    return pl.pallas_call(
        _mla_kv_body,
        grid=(b, l // tl),
        in_specs=[row(MLA_KV_LORA), row(LANES), _const_spec((MLA_KV_LORA, MLA_HEADS * LANES)),
                  _const_spec((MLA_KV_LORA, MLA_HEADS * MLA_V)), _const_spec((1, LANES))],
        out_specs=[row(MLA_HEADS * LANES), row(MLA_HEADS * MLA_V)],
        out_shape=[jax.ShapeDtypeStruct((b, l, MLA_HEADS * LANES), BF16),
                   jax.ShapeDtypeStruct((b, l, MLA_HEADS * MLA_V), BF16)],
        compiler_params=_params("parallel", "parallel"),
        name="mla_kv",
    )(lat, krp, w["wuk"], w["wuv"], w["gk"])


def _lru_body(h_ref, g_ref, wrec_ref, wgate_ref, cw_ref, cb_ref, wr_ref, wi_ref, br_ref, bi_ref, lam_ref,
              cprev_ref, h0_ref, y_ref, hl_ref, cl_ref, buf, a_s, b_s, hcar, *, tm):
    @pl.when(pl.program_id(1) == 0)
    def _():
        buf[0:HALO, :] = cprev_ref[0]
        hcar[...] = h0_ref[0]

    hb = _rms(h_ref[0], g_ref[...]).astype(BF16)
    xr = _dot(hb, wrec_ref[...])
    xg = _dot(hb, wgate_ref[...])
    buf[HALO:HALO + tm, :] = xr
    xc = cb_ref[...] + xr * cw_ref[CONV_WIDTH - 1:CONV_WIDTH, :]
    for j in range(CONV_WIDTH - 1):
        off = HALO - (CONV_WIDTH - 1) + j
        xc = xc + cw_ref[j:j + 1, :] * buf[off:off + tm, :]
    xcb = xc.astype(BF16)
    r = _sigmoid(_dot(xcb, wr_ref[...]) + br_ref[...])
    i = _sigmoid(_dot(xcb, wi_ref[...]) + bi_ref[...])
    log_a = (-LRU_C) * r * _softplus(-lam_ref[...])
    a = jnp.exp(log_a)
    b = jnp.sqrt(-jnp.tanh(log_a) * (a * a + 1.0)) * (i * xc)
    a_s[...] = a
    b_s[...] = b

    def step(t, hprev):
        hnew = a_s[pl.ds(t, 1), :] * hprev + b_s[pl.ds(t, 1), :]
        b_s[pl.ds(t, 1), :] = hnew
        return hnew

    hfin = lax.fori_loop(0, tm, step, hcar[...])
    hcar[...] = hfin
    y_ref[0] = (_gelu_tanh(xg) * b_s[...]).astype(BF16)
    hl_ref[0] = hfin
    tail = buf[tm:tm + HALO, :]
    buf[0:HALO, :] = tail
    cl_ref[0] = tail


def _lru(h, w, conv_prev8, h0):
    b, t, d = h.shape
    tm = _row_tile(t, 512)
    wd = LRU_WIDTH
    vec = lambda: _const_spec((1, wd))
    return pl.pallas_call(
        functools.partial(_lru_body, tm=tm),
        grid=(b, t // tm),
        in_specs=[pl.BlockSpec((1, tm, d), lambda i, j: (i, j, 0)), _const_spec((1, d)),
                  _const_spec((d, wd)), _const_spec((d, wd)), _const_spec((CONV_WIDTH, wd)), vec(),
                  _const_spec((wd, wd)), _const_spec((wd, wd)), vec(), vec(), vec(),
                  pl.BlockSpec((1, HALO, wd), lambda i, j: (i, 0, 0)),
                  pl.BlockSpec((1, 1, wd), lambda i, j: (i, 0, 0))],
        out_specs=[pl.BlockSpec((1, tm, wd), lambda i, j: (i, j, 0)),
                   pl.BlockSpec((1, 1, wd), lambda i, j: (i, 0, 0)),
                   pl.BlockSpec((1, HALO, wd), lambda i, j: (i, 0, 0))],
        out_shape=[jax.ShapeDtypeStruct((b, t, wd), BF16), jax.ShapeDtypeStruct((b, 1, wd), F32),
                   jax.ShapeDtypeStruct((b, HALO, wd), F32)],
        scratch_shapes=[pltpu.VMEM((tm + HALO, wd), F32), pltpu.VMEM((tm, wd), F32),
                        pltpu.VMEM((tm, wd), F32), pltpu.VMEM((1, wd), F32)],
        compiler_params=_params("parallel", "arbitrary"),
        name="lru",
    )(h, w["g_mix"], w["wrec"], w["wgate"], w["conv_w"], w["conv_b"], w["wr"], w["wi"], w["br"], w["bi"],
      w["lam"], conv_prev8, h0)


def _group_rms(x, e, et, g):
    hi, lo = _split_bf16(x * x, 2)
    ss = _dot(hi, e) + _dot(lo, e)
    inv = lax.rsqrt(ss * (1.0 / FOX_HEAD_DIM) + NORM_EPS)
    ih, il = _split_bf16(inv, 2)
    return x * (_dot(ih, et) + _dot(il, et)) * g


def _cumsum_rows(x, ltri):
    out = None
    for p in _split_bf16(x, 3):
        d = _dot(ltri, p)
        out = d if out is None else out + d
    return out


def _fox_in_body(h_ref, g_ref, wq_ref, wk_ref, wv_ref, wf_ref, bf_ref, gq_ref, gk_ref, e_ref, et_ref,
                 ltri_ref, c0_ref, q_ref, k32_ref, v32_ref, kb_ref, vb_ref, lf_ref, c_ref, lf_s, carry,
                 *, tm, tc):
    @pl.when(pl.program_id(1) == 0)
    def _():
        carry[...] = c0_ref[0]

    hb = _rms(h_ref[0], g_ref[...]).astype(BF16)
    e, et = e_ref[...], et_ref[...]
    q_ref[0] = _group_rms(_dot(hb, wq_ref[...]), e, et, gq_ref[...]).astype(BF16)
    k = _group_rms(_dot(hb, wk_ref[...]), e, et, gk_ref[...])
    k32_ref[0] = k
    kb_ref[0] = k.astype(BF16)
    v = _dot(hb, wv_ref[...])
    v32_ref[0] = v
    vb_ref[0] = v.astype(BF16)
    logf = -_softplus(-(_dot(hb, wf_ref[...]) + bf_ref[...]))
    lf_ref[0] = logf
    if tc > tm:
        lf_s[...] = jnp.zeros_like(lf_s)
    lf_s[0:tm, :] = logf
    c = carry[...] + _cumsum_rows(lf_s[...], ltri_ref[...])[0:tm, :]
    c_ref[0] = c
    carry[...] = c[tm - 1:tm, :]


def _fox_in(h, w, c0):
    b, t, d = h.shape
    tm = _row_tile(t, 512)
    tc = max(tm, LANES)
    ltri = jnp.tril(jnp.ones((tc, tc), F32)).astype(BF16)
    row = lambda n: pl.BlockSpec((1, tm, n), lambda i, j: (i, j, 0))
    fw = FOX_WIDTH
    return pl.pallas_call(
        functools.partial(_fox_in_body, tm=tm, tc=tc),
        grid=(b, t // tm),
        in_specs=[row(d), _const_spec((1, d)),
                  _const_spec((d, fw)), _const_spec((d, fw)), _const_spec((d, fw)), _const_spec((d, LANES)),
                  _const_spec((1, LANES)), _const_spec((1, fw)), _const_spec((1, fw)),
                  _const_spec((fw, LANES)), _const_spec((LANES, fw)), _const_spec((tc, tc)),
                  pl.BlockSpec((1, 1, LANES), lambda i, j: (i, 0, 0))],
        out_specs=[row(fw), row(fw), row(fw), row(fw), row(fw), row(LANES), row(LANES)],
        out_shape=[jax.ShapeDtypeStruct((b, t, fw), BF16), jax.ShapeDtypeStruct((b, t, fw), F32),
                   jax.ShapeDtypeStruct((b, t, fw), F32), jax.ShapeDtypeStruct((b, t, fw), BF16),
                   jax.ShapeDtypeStruct((b, t, fw), BF16), jax.ShapeDtypeStruct((b, t, LANES), F32),
                   jax.ShapeDtypeStruct((b, t, LANES), F32)],
        scratch_shapes=[pltpu.VMEM((tc, LANES), F32), pltpu.VMEM((1, LANES), F32)],
        compiler_params=_params("parallel", "arbitrary"),
        name="fox_in",
    )(h, w["g_mix"], w["wq"], w["wk"], w["wv"], w["wf"], w["bf"], w["gq"], w["gk"], w["e"], w["et"], ltri, c0)


def _cumsum_body(x_ref, ltri_ref, c_ref, carry):
    @pl.when(pl.program_id(1) == 0)
    def _():
        carry[...] = jnp.zeros_like(carry)

    c = carry[...] + _cumsum_rows(x_ref[0], ltri_ref[...])
    c_ref[0] = c
    carry[...] = c[c.shape[0] - 1:, :]


def _cumsum_time(x):
    b, p, n = x.shape
    tc = _row_tile(p, 512)
    ltri = jnp.tril(jnp.ones((tc, tc), F32)).astype(BF16)
    return pl.pallas_call(
        _cumsum_body,
        grid=(b, p // tc),
        in_specs=[pl.BlockSpec((1, tc, n), lambda i, j: (i, j, 0)), _const_spec((tc, tc))],
        out_specs=pl.BlockSpec((1, tc, n), lambda i, j: (i, j, 0)),
        out_shape=jax.ShapeDtypeStruct((b, p, n), F32),
        scratch_shapes=[pltpu.VMEM((1, n), F32)],
        compiler_params=_params("parallel", "arbitrary"),
        name="cumsum_time",
    )(x, ltri)


def _flash_body(*refs, tq, tk, n_k, past, kv_len, chunk_causal, per_head_qk, has_bias):
    if has_bias:
        q_ref, k_ref, v_ref, cq_ref, ct_ref, o_ref, m_s, l_s, acc_s = refs
    else:
        q_ref, k_ref, v_ref, o_ref, m_s, l_s, acc_s = refs
    pair = pl.program_id(1)
    q_start = past + pl.program_id(2) * tq
    lane = lax.broadcasted_iota(jnp.int32, (1, LANES), 1)
    q = q_ref[0]
    if per_head_qk:
        qs = (q[:, :LANES], q[:, LANES:])
    else:
        zero = jnp.zeros_like(q)
        qs = (jnp.where(lane < FOX_HEAD_DIM, q, zero), jnp.where(lane >= FOX_HEAD_DIM, q, zero))
    if has_bias:
        cq_tile = cq_ref[0]
        lane_q = lax.broadcasted_iota(jnp.int32, cq_tile.shape, 1)
        cqs = [jnp.sum(jnp.where(lane_q == 2 * pair + hh, cq_tile, 0.0), axis=-1, keepdims=True)
               for hh in range(2)]
    m_s[...] = jnp.full_like(m_s, NEG_INF)
    l_s[...] = jnp.zeros_like(l_s)
    acc_s[...] = jnp.zeros_like(acc_s)

    def tile(kt, masked):
        k0 = pl.multiple_of(kt * tk, tk)
        kblk = k_ref[0, pl.ds(k0, tk), :]
        vblk = v_ref[0, pl.ds(k0, tk), :]
        if masked:
            qpos = q_start + lax.broadcasted_iota(jnp.int32, (tq, tk), 0)
            kpos = k0 + lax.broadcasted_iota(jnp.int32, (tq, tk), 1)
            if chunk_causal:
                shift = int(math.log2(CHUNK))
                vis = lax.shift_right_logical(kpos, shift) <= lax.shift_right_logical(qpos, shift)
            else:
                vis = kpos <= qpos
            vis = jnp.logical_and(vis, kpos < kv_len)
        if has_bias:
            ct = ct_ref[0, :, pl.ds(k0, tk)]
            sub_k = lax.broadcasted_iota(jnp.int32, ct.shape, 0)
        for hh in range(2):
            kh = kblk[:, hh * LANES:(hh + 1) * LANES] if per_head_qk else kblk
            s = _dot_nt(qs[hh], kh)
            if has_bias:
                ck = jnp.sum(jnp.where(sub_k == 2 * pair + hh, ct, 0.0), axis=0, keepdims=True)
                s = s + (cqs[hh] - ck)
            if masked:
                s = jnp.where(vis, s, NEG_INF)
            m_old = m_s[hh]
            m_new = jnp.maximum(m_old, jnp.max(s, axis=-1, keepdims=True))
            alpha = jnp.exp(m_old - m_new)
            p = jnp.exp(s - m_new)
            l_s[hh] = alpha * l_s[hh] + jnp.sum(p, axis=-1, keepdims=True)
            acc_s[hh] = alpha * acc_s[hh] + _dot(p.astype(BF16), vblk)
            m_s[hh] = m_new

    def full_tile(kt, carry):
        tile(kt, False)
        return carry

    def masked_tile(kt, carry):
        tile(kt, True)
        return carry

    n_full = jnp.minimum(q_start // tk, kv_len // tk)
    q_last = q_start + tq - 1
    k_hi = (q_last // CHUNK + 1) * CHUNK if chunk_causal else q_last + 1
    n_end = jnp.minimum((k_hi + tk - 1) // tk, n_k)
    lax.fori_loop(0, n_full, full_tile, 0)
    lax.fori_loop(n_full, n_end, masked_tile, 0)
    out0 = acc_s[0] / l_s[0]
    out1 = acc_s[1] / l_s[1]
    o_ref[0] = jnp.where(lane < FOX_HEAD_DIM, out0, out1).astype(BF16)


def _flash(q, k, v, *, past, kv_len, chunk_causal, per_head_qk, cq=None, ct=None):
    b, t, _ = q.shape
    lp = k.shape[1]
    n_pairs = v.shape[2] // LANES
    qk_w = 2 * LANES if per_head_qk else LANES
    if past == 0 and t % 512 == 0 and lp % 512 == 0:
        tq = tk = 512
    else:
        tq = t if t <= LANES else LANES
        tk = LANES
    assert t % tq == 0 and lp % tk == 0, (t, lp, tq, tk)
    has_bias = cq is not None
    in_specs = [pl.BlockSpec((1, tq, qk_w), lambda i, j, s: (i, s, j)),
                pl.BlockSpec((1, lp, qk_w), lambda i, j, s: (i, 0, j)),
                pl.BlockSpec((1, lp, LANES), lambda i, j, s: (i, 0, j))]
    args = [q, k, v]
    if has_bias:
        in_specs += [pl.BlockSpec((1, tq, LANES), lambda i, j, s: (i, s, 0)),
                     pl.BlockSpec((1, FOX_HEADS, lp), lambda i, j, s: (i, 0, 0))]
        args += [cq, ct]
    return pl.pallas_call(
        functools.partial(_flash_body, tq=tq, tk=tk, n_k=lp // tk, past=past, kv_len=kv_len,
                          chunk_causal=chunk_causal, per_head_qk=per_head_qk, has_bias=has_bias),
        grid=(b, n_pairs, t // tq),
        in_specs=in_specs,
        out_specs=pl.BlockSpec((1, tq, LANES), lambda i, j, s: (i, s, j)),
        out_shape=jax.ShapeDtypeStruct((b, t, n_pairs * LANES), BF16),
        scratch_shapes=[pltpu.VMEM((2, tq, 1), F32), pltpu.VMEM((2, tq, 1), F32),
                        pltpu.VMEM((2, tq, LANES), F32)],
        compiler_params=_params("parallel", "parallel", "arbitrary"),
        name="flash_mla" if chunk_causal else "flash_fox",
    )(*args)


def _row(v):
    return v.reshape(1, -1).astype(F32)


def _pad_lanes(x, lo, total):
    pad = [(0, 0)] * (x.ndim - 1) + [(lo, total - lo - x.shape[-1])]
    return jnp.pad(x, pad)


def _ffn_weights(g, w_in, w_out):
    f = w_out.shape[0]
    return dict(g=_row(g), wg=w_in[:, :f].astype(BF16), wu=w_in[:, f:].astype(BF16), wo=w_out.astype(BF16))


def _mem_weights(g, g_src, w_q, w_kv, w_o, g_q, g_k):
    d = w_q.shape[0]
    kv = w_kv.reshape(d, MEM_HEADS, 2, MEM_HEAD_DIM)
    return dict(g=_row(g), g_src=_row(g_src), wq=w_q.astype(BF16), wo=w_o.astype(BF16),
                wk=kv[:, :, 0].reshape(d, MEM_WIDTH).astype(BF16),
                wv=kv[:, :, 1].reshape(d, MEM_WIDTH).astype(BF16),
                gq=_row(g_q) * (MEM_HEAD_DIM ** -0.5), gk=_row(g_k))


def _even_weights(g_mix, w_in, g_qlat, g_kvlat, w_uq, w_ukv, g_q, g_k, conv_w, conv_b, gate_w, gate_b, lam, w_out):
    d = w_in.shape[0]
    o1 = MLA_Q_LORA
    o2 = o1 + MLA_KV_LORA
    o3 = o2 + MLA_ROPE
    o4 = o3 + LRU_WIDTH
    uq = _pad_lanes(w_uq.reshape(MLA_Q_LORA, MLA_HEADS, MLA_QK), 0, LANES)
    ukv = w_ukv.reshape(MLA_KV_LORA, MLA_HEADS, MLA_NOPE + MLA_V)
    uk = _pad_lanes(ukv[:, :, :MLA_NOPE], 0, LANES)
    blk = LRU_WIDTH // LRU_BLOCKS
    eye = jnp.eye(LRU_BLOCKS, dtype=F32)
    wr = jnp.einsum("ncd,nm->ncmd", gate_w[:, :, :blk], eye).reshape(LRU_WIDTH, LRU_WIDTH)
    wi = jnp.einsum("ncd,nm->ncmd", gate_w[:, :, blk:], eye).reshape(LRU_WIDTH, LRU_WIDTH)
    return dict(
        g_mix=_row(g_mix), wcq=w_in[:, :o1].astype(BF16), g_qlat=_row(g_qlat),
        wuq=uq.reshape(MLA_Q_LORA, MLA_HEADS * LANES).astype(BF16),
        gq=_pad_lanes(_row(g_q), 0, LANES) * (MLA_QK ** -0.5),
        wckv=w_in[:, o1:o2].astype(BF16), g_kvlat=_row(g_kvlat),
        wkr=_pad_lanes(w_in[:, o2:o3], MLA_NOPE, LANES).astype(BF16),
        wuk=uk.reshape(MLA_KV_LORA, MLA_HEADS * LANES).astype(BF16),
        wuv=ukv[:, :, MLA_NOPE:].reshape(MLA_KV_LORA, MLA_HEADS * MLA_V).astype(BF16),
        gk=_pad_lanes(_row(g_k), 0, LANES),
        wrec=w_in[:, o3:o4].astype(BF16), wgate=w_in[:, o4:].astype(BF16),
        conv_w=conv_w.astype(F32), conv_b=_row(conv_b), wr=wr.astype(BF16), wi=wi.astype(BF16),
        br=_row(gate_b[:, :blk]), bi=_row(gate_b[:, blk:]), lam=_row(lam),
        wo_attn=w_out[:MLA_HEADS * MLA_V].astype(BF16), wo_rec=w_out[MLA_HEADS * MLA_V:].astype(BF16))


def _odd_weights(g_mix, w_in, b_f, g_q, g_k, w_out):
    fw = FOX_WIDTH
    head_of_lane = jnp.arange(fw) // FOX_HEAD_DIM
    e = (head_of_lane[:, None] == jnp.arange(LANES)[None, :]).astype(BF16)
    return dict(
        g_mix=_row(g_mix), wq=w_in[:, :fw].astype(BF16), wk=w_in[:, fw:2 * fw].astype(BF16),
        wv=w_in[:, 2 * fw:3 * fw].astype(BF16), wf=_pad_lanes(w_in[:, 3 * fw:], 0, LANES).astype(BF16),
        bf=_pad_lanes(_row(b_f), 0, LANES),
        gq=jnp.tile(_row(g_q), (1, FOX_HEADS)) * (FOX_HEAD_DIM ** -0.5),
        gk=jnp.tile(_row(g_k), (1, FOX_HEADS)), e=e, et=e.T, wo=w_out.astype(BF16))


def _rope_tables(pos):
    half = MLA_ROPE // 2
    inv_freq = ROPE_THETA ** (-jnp.arange(half, dtype=F32) / half)
    ang = pos.astype(F32)[:, None] * inv_freq[None, :]
    cos, sin = jnp.cos(ang), jnp.sin(ang)
    zeros = jnp.zeros_like(sin)
    c = jnp.concatenate([jnp.ones((pos.shape[0], MLA_NOPE), F32), cos, cos,
                         jnp.ones((pos.shape[0], LANES - MLA_QK), F32)], axis=-1)
    s1 = _pad_lanes(jnp.concatenate([-sin, zeros], axis=-1), MLA_NOPE, LANES)
    s2 = _pad_lanes(jnp.concatenate([zeros, sin], axis=-1), MLA_NOPE, LANES)
    return c, s1, s2


def _pad_rows(x, total):
    return jnp.pad(x, [(0, 0), (0, total - x.shape[1])] + [(0, 0)] * (x.ndim - 2))


def _kv_pad_len(t, past):
    l = past + t
    if past == 0 and t % 512 == 0:
        return l
    return -(-l // LANES) * LANES


def _even_layer(h, past, w, state):
    b, t, d = h.shape
    past_lat, past_krope, h0, conv_prev = state
    q, lat_new, krp_new = _mla_in(h, w, _rope_tables(past + jnp.arange(t)))
    lp = _kv_pad_len(t, past)
    lat_all = _pad_rows(jnp.concatenate([past_lat, lat_new], axis=1), lp)
    krp_all = _pad_rows(jnp.concatenate([_pad_lanes(past_krope, MLA_NOPE, LANES), krp_new], axis=1), lp)
    k, v = _mla_kv(lat_all, krp_all, w)
    attn = _flash(q, k, v, past=past, kv_len=past + t, chunk_causal=True, per_head_qk=True)
    conv_prev8 = jnp.pad(conv_prev, ((0, 0), (HALO - (CONV_WIDTH - 1), 0), (0, 0)))
    y_rec, h_last, conv_last = _lru(h, w, conv_prev8, h0[:, None, :])
    out = _out_proj(h.reshape(b * t, d), [attn.reshape(b * t, -1), y_rec.reshape(b * t, -1)],
                    [w["wo_attn"], w["wo_rec"]]).reshape(b, t, d)
    new = (lat_new, krp_new[:, :, MLA_NOPE:MLA_QK], h_last[:, 0], conv_last[:, HALO - (CONV_WIDTH - 1):])
    return out, new


def _odd_layer(h, past, w, state):
    b, t, d = h.shape
    past_k, past_v, past_logf = state
    if past > 0:
        c_past = _cumsum_time(_pad_lanes(past_logf.astype(F32), 0, LANES))
        c0 = c_past[:, past - 1:past, :]
    else:
        c_past = jnp.zeros((b, 0, LANES), F32)
        c0 = jnp.zeros((b, 1, LANES), F32)
    q, k32, v32, kb, vb, logf, c_new = _fox_in(h, w, c0)
    lp = _kv_pad_len(t, past)
    k_all = _pad_rows(jnp.concatenate([past_k.reshape(b, past, FOX_WIDTH).astype(BF16), kb], axis=1), lp)
    v_all = _pad_rows(jnp.concatenate([past_v.reshape(b, past, FOX_WIDTH).astype(BF16), vb], axis=1), lp)
    c_all = _pad_rows(jnp.concatenate([c_past, c_new], axis=1), lp)
    ct = jnp.swapaxes(c_all[:, :, :FOX_HEADS], 1, 2)
    attn = _flash(q, k_all, v_all, past=past, kv_len=past + t, chunk_causal=False, per_head_qk=False,
                  cq=c_new, ct=ct)
    out = _out_proj(h.reshape(b * t, d), [attn.reshape(b * t, -1)], [w["wo"]]).reshape(b, t, d)
    new = (k32.reshape(b, t, FOX_HEADS, FOX_HEAD_DIM), v32.reshape(b, t, FOX_HEADS, FOX_HEAD_DIM),
           logf[:, :, :FOX_HEADS])
    return out, new


def _trunk(x, past, layers, mem_kvs, even_states, odd_states):
    b, t, d = x.shape
    even_new, odd_new = [], []
    for li, lw in enumerate(layers):
        h = _ffn(x.reshape(b * t, d), lw["ffn1"]).reshape(b, t, d)
        if li % 2 == 0:
            h, new = _even_layer(h, past, lw["mix"], even_states[li // 2])
            even_new.append(new)
        else:
            h, new = _odd_layer(h, past, lw["mix"], odd_states[li // 2])
            odd_new.append(new)
        h = _mem_attn(h, mem_kvs[li][0], mem_kvs[li][1], lw["mem"])
        x = _ffn(h.reshape(b * t, d), lw["ffn2"]).reshape(b, t, d)
    return x, even_new, odd_new


def kernel(x_prompt, x_sample, mem_prompt, cache_mla_latent, cache_mla_krope, state_lru_h, state_lru_conv, cache_fox_k, cache_fox_v, cache_fox_logf, cache_mem_k, cache_mem_v, norm_ffn1, ffn1_w_in, ffn1_w_out, norm_mix, norm_mem, norm_mem_src, mem_w_q, mem_w_kv, mem_w_o, mem_g_q, mem_g_k, norm_ffn2, ffn2_w_in, ffn2_w_out, ev_w_in, ev_g_qlat, ev_g_kvlat, ev_w_uq, ev_w_ukv, ev_g_q, ev_g_k, ev_conv_w, ev_conv_b, ev_gate_w, ev_gate_b, ev_lambda, ev_w_out, od_w_in, od_b_f, od_g_q, od_g_k, od_w_out):
    depth = norm_ffn1.shape[0]
    n_even, n_odd = (depth + 1) // 2, depth // 2
    b, _, _ = x_prompt.shape
    bs = x_sample.shape[0]
    past = cache_mla_latent.shape[2] if n_even else cache_fox_k.shape[2]

    layers = []
    for li in range(depth):
        j = li // 2
        if li % 2 == 0:
            mix = _even_weights(norm_mix[li], ev_w_in[j], ev_g_qlat[j], ev_g_kvlat[j], ev_w_uq[j], ev_w_ukv[j],
                                ev_g_q[j], ev_g_k[j], ev_conv_w[j], ev_conv_b[j], ev_gate_w[j], ev_gate_b[j],
                                ev_lambda[j], ev_w_out[j])
        else:
            mix = _odd_weights(norm_mix[li], od_w_in[j], od_b_f[j], od_g_q[j], od_g_k[j], od_w_out[j])
        layers.append(dict(
            ffn1=_ffn_weights(norm_ffn1[li], ffn1_w_in[li], ffn1_w_out[li]),
            ffn2=_ffn_weights(norm_ffn2[li], ffn2_w_in[li], ffn2_w_out[li]),
            mem=_mem_weights(norm_mem[li], norm_mem_src[li], mem_w_q[li], mem_w_kv[li], mem_w_o[li],
                             mem_g_q[li], mem_g_k[li]),
            mix=mix))

    mem_p = [_mem_kv(mem_prompt, lw["mem"]) for lw in layers]
    ev0 = [(jnp.zeros((b, 0, MLA_KV_LORA), F32), jnp.zeros((b, 0, MLA_ROPE), F32),
            jnp.zeros((b, LRU_WIDTH), F32), jnp.zeros((b, CONV_WIDTH - 1, LRU_WIDTH), F32))
           for _ in range(n_even)]
    od0 = [(jnp.zeros((b, 0, FOX_HEADS, FOX_HEAD_DIM), F32), jnp.zeros((b, 0, FOX_HEADS, FOX_HEAD_DIM), F32),
            jnp.zeros((b, 0, FOX_HEADS), F32)) for _ in range(n_odd)]
    y_prompt, ev_p, od_p = _trunk(x_prompt, 0, layers, [(m[2], m[3]) for m in mem_p], ev0, od0)

    m_tok = cache_mem_k.shape[2]
    mem_s = [(cache_mem_k[li].reshape(bs, m_tok, MEM_WIDTH).astype(BF16),
              cache_mem_v[li].reshape(bs, m_tok, MEM_WIDTH).astype(BF16)) for li in range(depth)]
    ev_s = [(cache_mla_latent[j], cache_mla_krope[j], state_lru_h[j], state_lru_conv[j]) for j in range(n_even)]
    od_s = [(cache_fox_k[j], cache_fox_v[j], cache_fox_logf[j]) for j in range(n_odd)]
    y_sample, ev_n, od_n = _trunk(x_sample, past, layers, mem_s, ev_s, od_s)

    mem_shape = (b, m_tok, MEM_HEADS, MEM_HEAD_DIM)
    p_even = [jnp.stack([s[f] for s in ev_p]) for f in range(4)]
    p_odd = [jnp.stack([s[f] for s in od_p]) for f in range(3)]
    p_mem_k = jnp.stack([m[0].reshape(mem_shape) for m in mem_p])
    p_mem_v = jnp.stack([m[1].reshape(mem_shape) for m in mem_p])
    s_even = [jnp.stack([s[f] for s in ev_n]) for f in range(4)]
    s_odd = [jnp.stack([s[f] for s in od_n]) for f in range(3)]
    return (y_prompt, y_sample, *p_even, *p_odd, p_mem_k, p_mem_v, *s_even, *s_odd)
```

```python
import functools
import math

import jax
import jax.numpy as jnp
from jax import lax
from jax.experimental import pallas as pl
from jax.experimental.pallas import tpu as pltpu

F32 = jnp.float32
BF16 = jnp.bfloat16

NORM_EPS = 1e-6
NEG_INF = -1e30
LOG2E = math.log2(math.e)
CHUNK = 64
LANES = 128
ROPE_THETA = 10000.0
LRU_C = 8.0
MLA_HEADS = 8
MLA_NOPE = 64
MLA_ROPE = 32
MLA_QK = MLA_NOPE + MLA_ROPE
MLA_V = 64
MLA_Q_LORA = 256
MLA_KV_LORA = 128
LRU_WIDTH = 512
LRU_BLOCKS = 8
CONV_WIDTH = 4
FOX_HEADS = 16
FOX_HEAD_DIM = 64
FOX_WIDTH = FOX_HEADS * FOX_HEAD_DIM
MEM_HEADS = 4
MEM_HEAD_DIM = 128
MEM_WIDTH = MEM_HEADS * MEM_HEAD_DIM
HALO = 8

VMEM_LIMIT = 56 * 1024 * 1024


def _dot(a, b):
    return jnp.dot(a, b, preferred_element_type=F32)


def _dot_nt(a, b):
    return lax.dot_general(a, b, (((1,), (1,)), ((), ())), preferred_element_type=F32)


def _rms(x, g):
    return x * lax.rsqrt(jnp.mean(x * x, axis=-1, keepdims=True) + NORM_EPS) * g


def _head_rms(x, g, n_live):
    ss = jnp.sum(x * x, axis=-1, keepdims=True) * (1.0 / n_live)
    return x * lax.rsqrt(ss + NORM_EPS) * g


def _sigmoid(x):
    return 1.0 / (1.0 + jnp.exp(-x))


def _log1p(y):
    u = 1.0 + y
    d = u - 1.0
    return jnp.where(d == 0.0, y, jnp.log(u) * (y / jnp.where(d == 0.0, 1.0, d)))


def _softplus(x):
    return jnp.maximum(x, 0.0) + _log1p(jnp.exp(-jnp.abs(x)))


def _gelu_tanh(x):
    return 0.5 * x * (1.0 + jnp.tanh(math.sqrt(2.0 / math.pi) * (x + 0.044715 * (x * x * x))))


def _split_bf16(x, parts):
    out = []
    r = x
    for _ in range(parts):
        p = r.astype(BF16)
        out.append(p)
        r = r - p.astype(F32)
    return out


def _rope128(x, c, s1, s2):
    return x * c + pltpu.roll(x, LANES - 16, 1) * s1 + pltpu.roll(x, 16, 1) * s2


def _const_spec(shape):
    nd = len(shape)
    return pl.BlockSpec(shape, lambda *_: (0,) * nd, pipeline_mode=pl.Buffered(1))


def _params(*sem):
    return pltpu.CompilerParams(dimension_semantics=sem, vmem_limit_bytes=VMEM_LIMIT)


def _row_tile(n, cap):
    t = min(n, cap)
    assert n % t == 0, (n, t)
    return t


def _ffn_body(x_ref, g_ref, wg_ref, wu_ref, wo_ref, o_ref, *, n_chunks):
    x = x_ref[...]
    hb = _rms(x, g_ref[...]).astype(BF16)
    fc = wg_ref.shape[1] // n_chunks
    acc = jnp.zeros_like(x)
    for c in range(n_chunks):
        sl = slice(c * fc, (c + 1) * fc)
        gate = _dot(hb, wg_ref[:, sl])
        up = _dot(hb, wu_ref[:, sl])
        act = (gate * _sigmoid(gate) * up).astype(BF16)
        acc = acc + _dot(act, wo_ref[sl, :])
    o_ref[...] = x + 0.5 * acc


def _ffn(x2, w):
    n, d = x2.shape
    f = w["wg"].shape[1]
    tm = _row_tile(n, 512)
    return pl.pallas_call(
        functools.partial(_ffn_body, n_chunks=2),
        grid=(n // tm,),
        in_specs=[pl.BlockSpec((tm, d), lambda i: (i, 0)), _const_spec((1, d)),
                  _const_spec((d, f)), _const_spec((d, f)), _const_spec((f, d))],
        out_specs=pl.BlockSpec((tm, d), lambda i: (i, 0)),
        out_shape=jax.ShapeDtypeStruct((n, d), F32),
        compiler_params=_params("parallel"),
        name="ffn",
    )(x2, w["g"], w["wg"], w["wu"], w["wo"])


def _out_proj_body(*refs, n_parts):
    h_ref = refs[0]
    o_ref = refs[1 + 2 * n_parts]
    acc = h_ref[...]
    for i in range(n_parts):
        acc = acc + _dot(refs[1 + i][...], refs[1 + n_parts + i][...])
    o_ref[...] = acc


def _out_proj(h2, parts, weights):
    n, d = h2.shape
    tm = _row_tile(n, 512)
    in_specs = [pl.BlockSpec((tm, d), lambda i: (i, 0))]
    in_specs += [pl.BlockSpec((tm, p.shape[1]), lambda i: (i, 0)) for p in parts]
    in_specs += [_const_spec(w.shape) for w in weights]
    return pl.pallas_call(
        functools.partial(_out_proj_body, n_parts=len(parts)),
        grid=(n // tm,),
        in_specs=in_specs,
        out_specs=pl.BlockSpec((tm, d), lambda i: (i, 0)),
        out_shape=jax.ShapeDtypeStruct((n, d), F32),
        compiler_params=_params("parallel"),
        name="out_proj",
    )(h2, *parts, *weights)


def _mem_kv_body(m_ref, g_ref, wk_ref, wv_ref, gk_ref, k32_ref, v32_ref, kb_ref, vb_ref):
    hb = _rms(m_ref[0], g_ref[...]).astype(BF16)
    k = _dot(hb, wk_ref[...])
    v = _dot(hb, wv_ref[...])
    for hd in range(MEM_HEADS):
        sl = slice(hd * MEM_HEAD_DIM, (hd + 1) * MEM_HEAD_DIM)
        kh = _head_rms(k[:, sl], gk_ref[...], MEM_HEAD_DIM)
        k32_ref[0, :, sl] = kh
        kb_ref[0, :, sl] = kh.astype(BF16)
    v32_ref[0] = v
    vb_ref[0] = v.astype(BF16)


def _mem_kv(mem, w):
    b, m, d = mem.shape
    blk = lambda: pl.BlockSpec((1, m, MEM_WIDTH), lambda i: (i, 0, 0))
    return pl.pallas_call(
        _mem_kv_body,
        grid=(b,),
        in_specs=[pl.BlockSpec((1, m, d), lambda i: (i, 0, 0)), _const_spec((1, d)),
                  _const_spec((d, MEM_WIDTH)), _const_spec((d, MEM_WIDTH)), _const_spec((1, MEM_HEAD_DIM))],
        out_specs=[blk(), blk(), blk(), blk()],
        out_shape=[jax.ShapeDtypeStruct((b, m, MEM_WIDTH), F32), jax.ShapeDtypeStruct((b, m, MEM_WIDTH), F32),
                   jax.ShapeDtypeStruct((b, m, MEM_WIDTH), BF16), jax.ShapeDtypeStruct((b, m, MEM_WIDTH), BF16)],
        compiler_params=_params("parallel"),
        name="mem_kv",
    )(mem, w["g_src"], w["wk"], w["wv"], w["gk"])


def _mem_attn_body(h_ref, g_ref, wq_ref, gq_ref, mk_ref, mv_ref, wo_ref, o_ref):
    h = h_ref[0]
    hb = _rms(h, g_ref[...]).astype(BF16)
    q = _dot(hb, wq_ref[...])
    outs = []
    for hd in range(MEM_HEADS):
        sl = slice(hd * MEM_HEAD_DIM, (hd + 1) * MEM_HEAD_DIM)
        qh = _head_rms(q[:, sl], gq_ref[...], MEM_HEAD_DIM).astype(BF16)
        s = _dot_nt(qh, mk_ref[0, :, sl])
        e = jnp.exp(s - jnp.max(s, axis=-1, keepdims=True))
        p = e / jnp.sum(e, axis=-1, keepdims=True)
        outs.append(_dot(p.astype(BF16), mv_ref[0, :, sl]).astype(BF16))
    o_ref[0] = h + _dot(jnp.concatenate(outs, axis=-1), wo_ref[...])


def _mem_attn(h, mk, mv, w):
    b, t, d = h.shape
    m = mk.shape[1]
    tm = _row_tile(t, 512)
    return pl.pallas_call(
        _mem_attn_body,
        grid=(b, t // tm),
        in_specs=[pl.BlockSpec((1, tm, d), lambda i, j: (i, j, 0)), _const_spec((1, d)),
                  _const_spec((d, MEM_WIDTH)), _const_spec((1, MEM_HEAD_DIM)),
                  pl.BlockSpec((1, m, MEM_WIDTH), lambda i, j: (i, 0, 0)),
                  pl.BlockSpec((1, m, MEM_WIDTH), lambda i, j: (i, 0, 0)),
                  _const_spec((MEM_WIDTH, d))],
        out_specs=pl.BlockSpec((1, tm, d), lambda i, j: (i, j, 0)),
        out_shape=jax.ShapeDtypeStruct((b, t, d), F32),
        compiler_params=_params("parallel", "parallel"),
        name="mem_attn",
    )(h, w["g"], w["wq"], w["gq"], mk, mv, w["wo"])


def _mla_in_body(h_ref, g_ref, wcq_ref, gql_ref, wuq_ref, gq_ref, wckv_ref, gkv_ref, wkr_ref,
                 c_ref, s1_ref, s2_ref, q_ref, lat_ref, krp_ref):
    hb = _rms(h_ref[0], g_ref[...]).astype(BF16)
    c, s1, s2 = c_ref[...], s1_ref[...], s2_ref[...]
    cq = _rms(_dot(hb, wcq_ref[...]), gql_ref[...]).astype(BF16)
    q = _dot(cq, wuq_ref[...])
    for hd in range(MLA_HEADS):
        sl = slice(hd * LANES, (hd + 1) * LANES)
        qh = _rope128(q[:, sl], c, s1, s2)
        q_ref[0, :, sl] = _head_rms(qh, gq_ref[...], MLA_QK).astype(BF16)
    lat_ref[0] = _rms(_dot(hb, wckv_ref[...]), gkv_ref[...])
    krp_ref[0] = _rope128(_dot(hb, wkr_ref[...]), c, s1, s2)


def _mla_in(h, w, tables):
    b, t, d = h.shape
    tm = _row_tile(t, 512)
    row = lambda n: pl.BlockSpec((1, tm, n), lambda i, j: (i, j, 0))
    tab = lambda: pl.BlockSpec((tm, LANES), lambda i, j: (j, 0))
    return pl.pallas_call(
        _mla_in_body,
        grid=(b, t // tm),
        in_specs=[row(d), _const_spec((1, d)),
                  _const_spec((d, MLA_Q_LORA)), _const_spec((1, MLA_Q_LORA)),
                  _const_spec((MLA_Q_LORA, MLA_HEADS * LANES)), _const_spec((1, LANES)),
                  _const_spec((d, MLA_KV_LORA)), _const_spec((1, MLA_KV_LORA)), _const_spec((d, LANES)),
                  tab(), tab(), tab()],
        out_specs=[row(MLA_HEADS * LANES), row(MLA_KV_LORA), row(LANES)],
        out_shape=[jax.ShapeDtypeStruct((b, t, MLA_HEADS * LANES), BF16),
                   jax.ShapeDtypeStruct((b, t, MLA_KV_LORA), F32),
                   jax.ShapeDtypeStruct((b, t, LANES), F32)],
        compiler_params=_params("parallel", "parallel"),
        name="mla_in",
    )(h, w["g_mix"], w["wcq"], w["g_qlat"], w["wuq"], w["gq"], w["wckv"], w["g_kvlat"], w["wkr"], *tables)


def _mla_kv_body(lat_ref, krp_ref, wuk_ref, wuvt_ref, gk_ref, k_ref, vt_ref):
    lb = lat_ref[0].astype(BF16)
    krp = krp_ref[0]
    kn = _dot(lb, wuk_ref[...])
    for hd in range(MLA_HEADS):
        sl = slice(hd * LANES, (hd + 1) * LANES)
        k_ref[0, :, sl] = _head_rms(kn[:, sl] + krp, gk_ref[...], MLA_QK).astype(BF16)
    vt_ref[0] = _dot_nt(wuvt_ref[...], lb).astype(BF16)


def _mla_kv(lat, krp, w):
    b, l, _ = lat.shape
    tl = _row_tile(l, 512) if l % 512 == 0 else l
    row = lambda n: pl.BlockSpec((1, tl, n), lambda i, j: (i, j, 0))
    vw = MLA_HEADS * MLA_V
    return pl.pallas_call(
        _mla_kv_body,
        grid=(b, l // tl),
        in_specs=[row(MLA_KV_LORA), row(LANES), _const_spec((MLA_KV_LORA, MLA_HEADS * LANES)),
                  _const_spec((vw, MLA_KV_LORA)), _const_spec((1, LANES))],
        out_specs=[row(MLA_HEADS * LANES), pl.BlockSpec((1, vw, tl), lambda i, j: (i, 0, j))],
        out_shape=[jax.ShapeDtypeStruct((b, l, MLA_HEADS * LANES), BF16),
                   jax.ShapeDtypeStruct((b, vw, l), BF16)],
        compiler_params=_params("parallel", "parallel"),
        name="mla_kv",
    )(lat, krp, w["wuk"], w["wuvt"], w["gk"])


def _lru_body(h_ref, g_ref, wrec_ref, wgate_ref, cw_ref, cb_ref, wr_ref, wi_ref, br_ref, bi_ref, lam_ref,
              cprev_ref, h0_ref, y_ref, hl_ref, cl_ref, buf, a_s, b_s, hcar, *, tm):
    @pl.when(pl.program_id(1) == 0)
    def _():
        buf[0:HALO, :] = cprev_ref[0]
        hcar[...] = h0_ref[0]

    hb = _rms(h_ref[0], g_ref[...]).astype(BF16)
    xr = _dot(hb, wrec_ref[...])
    xg = _dot(hb, wgate_ref[...])
    buf[HALO:HALO + tm, :] = xr
    xc = cb_ref[...] + xr * cw_ref[CONV_WIDTH - 1:CONV_WIDTH, :]
    for j in range(CONV_WIDTH - 1):
        off = HALO - (CONV_WIDTH - 1) + j
        xc = xc + cw_ref[j:j + 1, :] * buf[off:off + tm, :]
    xcb = xc.astype(BF16)
    r = _sigmoid(_dot(xcb, wr_ref[...]) + br_ref[...])
    i = _sigmoid(_dot(xcb, wi_ref[...]) + bi_ref[...])
    log_a = (-LRU_C) * r * _softplus(-lam_ref[...])
    a = jnp.exp(log_a)
    b = jnp.sqrt(-jnp.tanh(log_a) * (a * a + 1.0)) * (i * xc)
    a_s[...] = a
    b_s[...] = b

    def step(t, hprev):
        hnew = a_s[pl.ds(t, 1), :] * hprev + b_s[pl.ds(t, 1), :]
        b_s[pl.ds(t, 1), :] = hnew
        return hnew

    hfin = lax.fori_loop(0, tm, step, hcar[...])
    hcar[...] = hfin
    y_ref[0] = (_gelu_tanh(xg) * b_s[...]).astype(BF16)
    hl_ref[0] = hfin
    tail = buf[tm:tm + HALO, :]
    buf[0:HALO, :] = tail
    cl_ref[0] = tail


def _lru(h, w, conv_prev8, h0):
    b, t, d = h.shape
    tm = _row_tile(t, 512)
    wd = LRU_WIDTH
    vec = lambda: _const_spec((1, wd))
    return pl.pallas_call(
        functools.partial(_lru_body, tm=tm),
        grid=(b, t // tm),
        in_specs=[pl.BlockSpec((1, tm, d), lambda i, j: (i, j, 0)), _const_spec((1, d)),
                  _const_spec((d, wd)), _const_spec((d, wd)), _const_spec((CONV_WIDTH, wd)), vec(),
                  _const_spec((wd, wd)), _const_spec((wd, wd)), vec(), vec(), vec(),
                  pl.BlockSpec((1, HALO, wd), lambda i, j: (i, 0, 0)),
                  pl.BlockSpec((1, 1, wd), lambda i, j: (i, 0, 0))],
        out_specs=[pl.BlockSpec((1, tm, wd), lambda i, j: (i, j, 0)),
                   pl.BlockSpec((1, 1, wd), lambda i, j: (i, 0, 0)),
                   pl.BlockSpec((1, HALO, wd), lambda i, j: (i, 0, 0))],
        out_shape=[jax.ShapeDtypeStruct((b, t, wd), BF16), jax.ShapeDtypeStruct((b, 1, wd), F32),
                   jax.ShapeDtypeStruct((b, HALO, wd), F32)],
        scratch_shapes=[pltpu.VMEM((tm + HALO, wd), F32), pltpu.VMEM((tm, wd), F32),
                        pltpu.VMEM((tm, wd), F32), pltpu.VMEM((1, wd), F32)],
        compiler_params=_params("parallel", "arbitrary"),
        name="lru",
    )(h, w["g_mix"], w["wrec"], w["wgate"], w["conv_w"], w["conv_b"], w["wr"], w["wi"], w["br"], w["bi"],
      w["lam"], conv_prev8, h0)


def _group_rms(x, e, et, g):
    hi, lo = _split_bf16(x * x, 2)
    ss = _dot(hi, e) + _dot(lo, e)
    inv = lax.rsqrt(ss * (1.0 / FOX_HEAD_DIM) + NORM_EPS)
    ih, il = _split_bf16(inv, 2)
    return x * (_dot(ih, et) + _dot(il, et)) * g


def _cumsum_rows(x, ltri):
    out = None
    for p in _split_bf16(x, 3):
        d = _dot(ltri, p)
        out = d if out is None else out + d
    return out


def _fox_aug(x, keep, bias):
    blocks = [x[:, (hd // 2) * LANES:(hd // 2 + 1) * LANES] for hd in range(FOX_HEADS)]
    return jnp.concatenate(blocks, axis=-1) * keep + bias


def _fox_key_bias(c, place):
    parts = jnp.concatenate(_split_bf16(c * (-LOG2E), 3), axis=-1)
    return _dot(parts, place)


def _fox_in_body(h_ref, g_ref, wq_ref, wk_ref, wv_ref, wf_ref, bf_ref, gq_ref, gk_ref, e_ref, et_ref,
                 keep_ref, ones_ref, place_ref, ltri_ref, c0_ref,
                 q_ref, k32_ref, v32_ref, ka_ref, vb_ref, lf_ref, c_ref, lf_s, carry, *, tm, tc):
    @pl.when(pl.program_id(1) == 0)
    def _():
        carry[...] = c0_ref[0]

    hb = _rms(h_ref[0], g_ref[...]).astype(BF16)
    e, et, keep = e_ref[...], et_ref[...], keep_ref[...]
    q = _group_rms(_dot(hb, wq_ref[...]), e, et, gq_ref[...])
    q_ref[0] = _fox_aug(q, keep, ones_ref[...]).astype(BF16)
    k = _group_rms(_dot(hb, wk_ref[...]), e, et, gk_ref[...])
    k32_ref[0] = k
    v = _dot(hb, wv_ref[...])
    v32_ref[0] = v
    vb_ref[0] = v.astype(BF16)
    logf = -_softplus(-(_dot(hb, wf_ref[...]) + bf_ref[...]))
    lf_ref[0] = logf
    if tc > tm:
        lf_s[...] = jnp.zeros_like(lf_s)
    lf_s[0:tm, :] = logf
    c = carry[...] + _cumsum_rows(lf_s[...], ltri_ref[...])[0:tm, :]
    c_ref[0] = c
    carry[...] = c[tm - 1:tm, :]
    ka_ref[0] = _fox_aug(k, keep, _fox_key_bias(c, place_ref[...])).astype(BF16)


def _fox_in(h, w, c0):
    b, t, d = h.shape
    tm = _row_tile(t, 512)
    tc = max(tm, LANES)
    ltri = jnp.tril(jnp.ones((tc, tc), F32)).astype(BF16)
    row = lambda n: pl.BlockSpec((1, tm, n), lambda i, j: (i, j, 0))
    fw, aw = FOX_WIDTH, FOX_HEADS * LANES
    return pl.pallas_call(
        functools.partial(_fox_in_body, tm=tm, tc=tc),
        grid=(b, t // tm),
        in_specs=[row(d), _const_spec((1, d)),
                  _const_spec((d, fw)), _const_spec((d, fw)), _const_spec((d, fw)), _const_spec((d, LANES)),
                  _const_spec((1, LANES)), _const_spec((1, fw)), _const_spec((1, fw)),
                  _const_spec((fw, LANES)), _const_spec((LANES, fw)),
                  _const_spec((1, aw)), _const_spec((1, aw)), _const_spec((3 * LANES, aw)),
                  _const_spec((tc, tc)), pl.BlockSpec((1, 1, LANES), lambda i, j: (i, 0, 0))],
        out_specs=[row(aw), row(fw), row(fw), row(aw), row(fw), row(LANES), row(LANES)],
        out_shape=[jax.ShapeDtypeStruct((b, t, aw), BF16), jax.ShapeDtypeStruct((b, t, fw), F32),
                   jax.ShapeDtypeStruct((b, t, fw), F32), jax.ShapeDtypeStruct((b, t, aw), BF16),
                   jax.ShapeDtypeStruct((b, t, fw), BF16), jax.ShapeDtypeStruct((b, t, LANES), F32),
                   jax.ShapeDtypeStruct((b, t, LANES), F32)],
        scratch_shapes=[pltpu.VMEM((tc, LANES), F32), pltpu.VMEM((1, LANES), F32)],
        compiler_params=_params("parallel", "arbitrary"),
        name="fox_in",
    )(h, w["g_mix"], w["wq"], w["wk"], w["wv"], w["wf"], w["bf"], w["gq"], w["gk"], w["e"], w["et"],
      w["keep"], w["ones"], w["place"], ltri, c0)


def _fox_past_body(k_ref, lf_ref, keep_ref, place_ref, ltri_ref, ka_ref, c_ref, carry):
    @pl.when(pl.program_id(1) == 0)
    def _():
        carry[...] = jnp.zeros_like(carry)

    c = carry[...] + _cumsum_rows(lf_ref[0], ltri_ref[...])
    c_ref[0] = c
    carry[...] = c[c.shape[0] - 1:, :]
    ka_ref[0] = _fox_aug(k_ref[0], keep_ref[...], _fox_key_bias(c, place_ref[...])).astype(BF16)


def _fox_past(past_k, past_logf, w):
    b, p, n = past_logf.shape
    tc = _row_tile(p, 512)
    ltri = jnp.tril(jnp.ones((tc, tc), F32)).astype(BF16)
    fw, aw = FOX_WIDTH, FOX_HEADS * LANES
    row = lambda m: pl.BlockSpec((1, tc, m), lambda i, j: (i, j, 0))
    return pl.pallas_call(
        _fox_past_body,
        grid=(b, p // tc),
        in_specs=[row(fw), row(n), _const_spec((1, aw)), _const_spec((3 * LANES, aw)), _const_spec((tc, tc))],
        out_specs=[row(aw), row(n)],
        out_shape=[jax.ShapeDtypeStruct((b, p, aw), BF16), jax.ShapeDtypeStruct((b, p, n), F32)],
        scratch_shapes=[pltpu.VMEM((1, n), F32)],
        compiler_params=_params("parallel", "arbitrary"),
        name="fox_past",
    )(past_k, past_logf, w["keep"], w["place"], ltri)


def _flash_body(q_ref, k_ref, vt_ref, o_ref, m_s, l_s, acc_s, *, tq, tqs, tk, tks, n_k, past, kv_len,
                chunk_causal):
    q_start = past + pl.program_id(2) * tq
    q = q_ref[0]
    streams = [(hh, r0, c0) for r0 in range(0, tk, tks) for hh in range(2) for c0 in range(0, tq, tqs)]
    m_s[...] = jnp.full_like(m_s, NEG_INF)
    l_s[...] = jnp.zeros_like(l_s)
    acc_s[...] = jnp.zeros_like(acc_s)

    def tile(kt, masked):
        k0 = pl.multiple_of(kt * tk, tk)
        kblk = k_ref[0, pl.ds(k0, tk), :]
        vt = vt_ref[0, :, pl.ds(k0, tk)]

        def scores(stream):
            hh, r0, c0 = stream
            head = slice(hh * LANES, (hh + 1) * LANES)
            return _dot_nt(kblk[r0:r0 + tks, head], q[c0:c0 + tqs, head])

        ahead = 2
        pending = [scores(st) for st in streams[:ahead]]
        for idx, (hh, r0, c0) in enumerate(streams):
            cols = slice(c0, c0 + tqs)
            s = pending.pop(0)
            if masked:
                kpos = k0 + r0 + lax.broadcasted_iota(jnp.int32, (tks, tqs), 0)
                qpos = q_start + c0 + lax.broadcasted_iota(jnp.int32, (tks, tqs), 1)
                if chunk_causal:
                    shift = int(math.log2(CHUNK))
                    vis = lax.shift_right_logical(kpos, shift) <= lax.shift_right_logical(qpos, shift)
                else:
                    vis = kpos <= qpos
                s = jnp.where(jnp.logical_and(vis, kpos < kv_len), s, NEG_INF)
            m_old = m_s[hh, :, cols]
            m_new = jnp.maximum(m_old, jnp.max(s, axis=0, keepdims=True))
            alpha = jnp.exp2(m_old - m_new)
            p = jnp.exp2(s - m_new)
            l_s[hh, :, cols] = alpha * l_s[hh, :, cols] + jnp.sum(p, axis=0, keepdims=True)
            pv = _dot(vt[:, r0:r0 + tks], p.astype(BF16))
            if idx + ahead < len(streams):
                pending.append(scores(streams[idx + ahead]))
            acc_s[hh, :, cols] = alpha * acc_s[hh, :, cols] + pv
            m_s[hh, :, cols] = m_new

    def full_tile(kt, carry):
        tile(kt, False)
        return carry

    def masked_tile(kt, carry):
        tile(kt, True)
        return carry

    n_full = jnp.minimum(q_start // tk, kv_len // tk)
    q_last = q_start + tq - 1
    k_hi = (q_last // CHUNK + 1) * CHUNK if chunk_causal else q_last + 1
    n_end = jnp.minimum((k_hi + tk - 1) // tk, n_k)
    lax.fori_loop(0, n_full, full_tile, 0)
    lax.fori_loop(n_full, n_end, masked_tile, 0)
    sub = lax.broadcasted_iota(jnp.int32, (LANES, tq), 0)
    out_t = jnp.where(sub < LANES // 2, acc_s[0] / l_s[0], acc_s[1] / l_s[1])
    o_ref[0] = out_t.T.astype(BF16)


def _flash(q, k, vt, *, past, kv_len, chunk_causal):
    b, t, w = q.shape
    lp = k.shape[1]
    n_pairs = w // (2 * LANES)
    t_pad = max(t, LANES)
    if t_pad > t:
        q = _pad_rows(q, t_pad)
    if past == 0 and t % 512 == 0 and lp % 512 == 0:
        tq = tk = 512
    elif t_pad == LANES:
        tq, tk = LANES, lp
    else:
        tq = tk = LANES
    assert t_pad % tq == 0 and lp % tk == 0, (t, lp, tq, tk)
    tks = 256 if tk % 256 == 0 and tq > LANES else tk
    out = pl.pallas_call(
        functools.partial(_flash_body, tq=tq, tqs=tq, tk=tk, tks=tks, n_k=lp // tk, past=past, kv_len=kv_len,
                          chunk_causal=chunk_causal),
        grid=(b, n_pairs, t_pad // tq),
        in_specs=[pl.BlockSpec((1, tq, 2 * LANES), lambda i, j, s: (i, s, j)),
                  pl.BlockSpec((1, lp, 2 * LANES), lambda i, j, s: (i, 0, j)),
                  pl.BlockSpec((1, LANES, lp), lambda i, j, s: (i, j, 0))],
        out_specs=pl.BlockSpec((1, tq, LANES), lambda i, j, s: (i, s, j)),
        out_shape=jax.ShapeDtypeStruct((b, t_pad, n_pairs * LANES), BF16),
        scratch_shapes=[pltpu.VMEM((2, 1, tq), F32), pltpu.VMEM((2, 1, tq), F32),
                        pltpu.VMEM((2, LANES, tq), F32)],
        compiler_params=_params("parallel", "parallel", "arbitrary"),
        name="flash_mla" if chunk_causal else "flash_fox",
    )(q, k, vt)
    return out[:, :t]


def _row(v):
    return v.reshape(1, -1).astype(F32)


def _pad_lanes(x, lo, total):
    pad = [(0, 0)] * (x.ndim - 1) + [(lo, total - lo - x.shape[-1])]
    return jnp.pad(x, pad)


def _ffn_weights(g, w_in, w_out):
    f = w_out.shape[0]
    return dict(g=_row(g), wg=w_in[:, :f].astype(BF16), wu=w_in[:, f:].astype(BF16), wo=w_out.astype(BF16))


def _mem_weights(g, g_src, w_q, w_kv, w_o, g_q, g_k):
    d = w_q.shape[0]
    kv = w_kv.reshape(d, MEM_HEADS, 2, MEM_HEAD_DIM)
    return dict(g=_row(g), g_src=_row(g_src), wq=w_q.astype(BF16), wo=w_o.astype(BF16),
                wk=kv[:, :, 0].reshape(d, MEM_WIDTH).astype(BF16),
                wv=kv[:, :, 1].reshape(d, MEM_WIDTH).astype(BF16),
                gq=_row(g_q) * (MEM_HEAD_DIM ** -0.5), gk=_row(g_k))


def _even_weights(g_mix, w_in, g_qlat, g_kvlat, w_uq, w_ukv, g_q, g_k, conv_w, conv_b, gate_w, gate_b, lam, w_out):
    d = w_in.shape[0]
    o1 = MLA_Q_LORA
    o2 = o1 + MLA_KV_LORA
    o3 = o2 + MLA_ROPE
    o4 = o3 + LRU_WIDTH
    uq = _pad_lanes(w_uq.reshape(MLA_Q_LORA, MLA_HEADS, MLA_QK), 0, LANES)
    ukv = w_ukv.reshape(MLA_KV_LORA, MLA_HEADS, MLA_NOPE + MLA_V)
    uk = _pad_lanes(ukv[:, :, :MLA_NOPE], 0, LANES)
    blk = LRU_WIDTH // LRU_BLOCKS
    eye = jnp.eye(LRU_BLOCKS, dtype=F32)
    wr = jnp.einsum("ncd,nm->ncmd", gate_w[:, :, :blk], eye).reshape(LRU_WIDTH, LRU_WIDTH)
    wi = jnp.einsum("ncd,nm->ncmd", gate_w[:, :, blk:], eye).reshape(LRU_WIDTH, LRU_WIDTH)
    return dict(
        g_mix=_row(g_mix), wcq=w_in[:, :o1].astype(BF16), g_qlat=_row(g_qlat),
        wuq=uq.reshape(MLA_Q_LORA, MLA_HEADS * LANES).astype(BF16),
        gq=_pad_lanes(_row(g_q), 0, LANES) * (MLA_QK ** -0.5 * LOG2E),
        wckv=w_in[:, o1:o2].astype(BF16), g_kvlat=_row(g_kvlat),
        wkr=_pad_lanes(w_in[:, o2:o3], MLA_NOPE, LANES).astype(BF16),
        wuk=uk.reshape(MLA_KV_LORA, MLA_HEADS * LANES).astype(BF16),
        wuvt=ukv[:, :, MLA_NOPE:].reshape(MLA_KV_LORA, MLA_HEADS * MLA_V).T.astype(BF16),
        gk=_pad_lanes(_row(g_k), 0, LANES),
        wrec=w_in[:, o3:o4].astype(BF16), wgate=w_in[:, o4:].astype(BF16),
        conv_w=conv_w.astype(F32), conv_b=_row(conv_b), wr=wr.astype(BF16), wi=wi.astype(BF16),
        br=_row(gate_b[:, :blk]), bi=_row(gate_b[:, blk:]), lam=_row(lam),
        wo_attn=w_out[:MLA_HEADS * MLA_V].astype(BF16), wo_rec=w_out[MLA_HEADS * MLA_V:].astype(BF16))


def _odd_weights(g_mix, w_in, b_f, g_q, g_k, w_out):
    fw = FOX_WIDTH
    head_of_lane = jnp.arange(fw) // FOX_HEAD_DIM
    e = (head_of_lane[:, None] == jnp.arange(LANES)[None, :]).astype(BF16)
    lane = jnp.arange(FOX_HEADS * LANES)
    hd, within = lane // LANES, lane % LANES
    own_low = hd % 2 == 0
    keep = jnp.where(own_low, within < FOX_HEAD_DIM, within >= FOX_HEAD_DIM)
    part = within - jnp.where(own_low, FOX_HEAD_DIM, 0)
    is_bias = (part >= 0) & (part < 3)
    src = part * LANES + hd
    place = ((jnp.arange(3 * LANES)[:, None] == src[None, :]) & is_bias[None, :]).astype(BF16)
    return dict(
        keep=keep.astype(F32)[None, :], ones=is_bias.astype(F32)[None, :], place=place,
        g_mix=_row(g_mix), wq=w_in[:, :fw].astype(BF16), wk=w_in[:, fw:2 * fw].astype(BF16),
        wv=w_in[:, 2 * fw:3 * fw].astype(BF16), wf=_pad_lanes(w_in[:, 3 * fw:], 0, LANES).astype(BF16),
        bf=_pad_lanes(_row(b_f), 0, LANES),
        gq=jnp.tile(_row(g_q), (1, FOX_HEADS)) * (FOX_HEAD_DIM ** -0.5 * LOG2E),
        gk=jnp.tile(_row(g_k), (1, FOX_HEADS)), e=e, et=e.T, wo=w_out.astype(BF16))


def _rope_tables(pos):
    half = MLA_ROPE // 2
    inv_freq = ROPE_THETA ** (-jnp.arange(half, dtype=F32) / half)
    ang = pos.astype(F32)[:, None] * inv_freq[None, :]
    cos, sin = jnp.cos(ang), jnp.sin(ang)
    zeros = jnp.zeros_like(sin)
    c = jnp.concatenate([jnp.ones((pos.shape[0], MLA_NOPE), F32), cos, cos,
                         jnp.ones((pos.shape[0], LANES - MLA_QK), F32)], axis=-1)
    s1 = _pad_lanes(jnp.concatenate([-sin, zeros], axis=-1), MLA_NOPE, LANES)
    s2 = _pad_lanes(jnp.concatenate([zeros, sin], axis=-1), MLA_NOPE, LANES)
    return c, s1, s2


def _pad_rows(x, total):
    return jnp.pad(x, [(0, 0), (0, total - x.shape[1])] + [(0, 0)] * (x.ndim - 2))


def _kv_pad_len(t, past):
    l = past + t
    if past == 0 and t % 512 == 0:
        return l
    return -(-l // LANES) * LANES


def _even_layer(h, past, w, state):
    b, t, d = h.shape
    past_lat, past_krope, h0, conv_prev = state
    q, lat_new, krp_new = _mla_in(h, w, _rope_tables(past + jnp.arange(t)))
    lp = _kv_pad_len(t, past)
    lat_all = _pad_rows(jnp.concatenate([past_lat, lat_new], axis=1), lp)
    krp_all = _pad_rows(jnp.concatenate([_pad_lanes(past_krope, MLA_NOPE, LANES), krp_new], axis=1), lp)
    k, vt = _mla_kv(lat_all, krp_all, w)
    attn = _flash(q, k, vt, past=past, kv_len=past + t, chunk_causal=True)
    conv_prev8 = jnp.pad(conv_prev, ((0, 0), (HALO - (CONV_WIDTH - 1), 0), (0, 0)))
    y_rec, h_last, conv_last = _lru(h, w, conv_prev8, h0[:, None, :])
    out = _out_proj(h.reshape(b * t, d), [attn.reshape(b * t, -1), y_rec.reshape(b * t, -1)],
                    [w["wo_attn"], w["wo_rec"]]).reshape(b, t, d)
    new = (lat_new, krp_new[:, :, MLA_NOPE:MLA_QK], h_last[:, 0], conv_last[:, HALO - (CONV_WIDTH - 1):])
    return out, new


def _odd_layer(h, past, w, state):
    b, t, d = h.shape
    past_k, past_v, past_logf = state
    if past > 0:
        ka_past, c_past = _fox_past(past_k.reshape(b, past, FOX_WIDTH),
                                    _pad_lanes(past_logf.astype(F32), 0, LANES), w)
        c0 = c_past[:, past - 1:past, :]
    else:
        ka_past = jnp.zeros((b, 0, FOX_HEADS * LANES), BF16)
        c0 = jnp.zeros((b, 1, LANES), F32)
    q, k32, v32, ka_new, vb, logf, _ = _fox_in(h, w, c0)
    lp = _kv_pad_len(t, past)
    k_all = _pad_rows(jnp.concatenate([ka_past, ka_new], axis=1), lp)
    v_all = _pad_rows(jnp.concatenate([past_v.reshape(b, past, FOX_WIDTH).astype(BF16), vb], axis=1), lp)
    attn = _flash(q, k_all, jnp.swapaxes(v_all, 1, 2), past=past, kv_len=past + t, chunk_causal=False)
    out = _out_proj(h.reshape(b * t, d), [attn.reshape(b * t, -1)], [w["wo"]]).reshape(b, t, d)
    new = (k32.reshape(b, t, FOX_HEADS, FOX_HEAD_DIM), v32.reshape(b, t, FOX_HEADS, FOX_HEAD_DIM),
           logf[:, :, :FOX_HEADS])
    return out, new


def _trunk(x, past, layers, mem_kvs, even_states, odd_states):
    b, t, d = x.shape
    even_new, odd_new = [], []
    for li, lw in enumerate(layers):
        h = _ffn(x.reshape(b * t, d), lw["ffn1"]).reshape(b, t, d)
        if li % 2 == 0:
            h, new = _even_layer(h, past, lw["mix"], even_states[li // 2])
            even_new.append(new)
        else:
            h, new = _odd_layer(h, past, lw["mix"], odd_states[li // 2])
            odd_new.append(new)
        h = _mem_attn(h, mem_kvs[li][0], mem_kvs[li][1], lw["mem"])
        x = _ffn(h.reshape(b * t, d), lw["ffn2"]).reshape(b, t, d)
    return x, even_new, odd_new


def kernel(x_prompt, x_sample, mem_prompt, cache_mla_latent, cache_mla_krope, state_lru_h, state_lru_conv, cache_fox_k, cache_fox_v, cache_fox_logf, cache_mem_k, cache_mem_v, norm_ffn1, ffn1_w_in, ffn1_w_out, norm_mix, norm_mem, norm_mem_src, mem_w_q, mem_w_kv, mem_w_o, mem_g_q, mem_g_k, norm_ffn2, ffn2_w_in, ffn2_w_out, ev_w_in, ev_g_qlat, ev_g_kvlat, ev_w_uq, ev_w_ukv, ev_g_q, ev_g_k, ev_conv_w, ev_conv_b, ev_gate_w, ev_gate_b, ev_lambda, ev_w_out, od_w_in, od_b_f, od_g_q, od_g_k, od_w_out):
    depth = norm_ffn1.shape[0]
    n_even, n_odd = (depth + 1) // 2, depth // 2
    b, _, _ = x_prompt.shape
    bs = x_sample.shape[0]
    past = cache_mla_latent.shape[2] if n_even else cache_fox_k.shape[2]

    layers = []
    for li in range(depth):
        j = li // 2
        if li % 2 == 0:
            mix = _even_weights(norm_mix[li], ev_w_in[j], ev_g_qlat[j], ev_g_kvlat[j], ev_w_uq[j], ev_w_ukv[j],
                                ev_g_q[j], ev_g_k[j], ev_conv_w[j], ev_conv_b[j], ev_gate_w[j], ev_gate_b[j],
                                ev_lambda[j], ev_w_out[j])
        else:
            mix = _odd_weights(norm_mix[li], od_w_in[j], od_b_f[j], od_g_q[j], od_g_k[j], od_w_out[j])
        layers.append(dict(
            ffn1=_ffn_weights(norm_ffn1[li], ffn1_w_in[li], ffn1_w_out[li]),
            ffn2=_ffn_weights(norm_ffn2[li], ffn2_w_in[li], ffn2_w_out[li]),
            mem=_mem_weights(norm_mem[li], norm_mem_src[li], mem_w_q[li], mem_w_kv[li], mem_w_o[li],
                             mem_g_q[li], mem_g_k[li]),
            mix=mix))

    mem_p = [_mem_kv(mem_prompt, lw["mem"]) for lw in layers]
    ev0 = [(jnp.zeros((b, 0, MLA_KV_LORA), F32), jnp.zeros((b, 0, MLA_ROPE), F32),
            jnp.zeros((b, LRU_WIDTH), F32), jnp.zeros((b, CONV_WIDTH - 1, LRU_WIDTH), F32))
           for _ in range(n_even)]
    od0 = [(jnp.zeros((b, 0, FOX_HEADS, FOX_HEAD_DIM), F32), jnp.zeros((b, 0, FOX_HEADS, FOX_HEAD_DIM), F32),
            jnp.zeros((b, 0, FOX_HEADS), F32)) for _ in range(n_odd)]
    y_prompt, ev_p, od_p = _trunk(x_prompt, 0, layers, [(m[2], m[3]) for m in mem_p], ev0, od0)

    m_tok = cache_mem_k.shape[2]
    mem_s = [(cache_mem_k[li].reshape(bs, m_tok, MEM_WIDTH).astype(BF16),
              cache_mem_v[li].reshape(bs, m_tok, MEM_WIDTH).astype(BF16)) for li in range(depth)]
    ev_s = [(cache_mla_latent[j], cache_mla_krope[j], state_lru_h[j], state_lru_conv[j]) for j in range(n_even)]
    od_s = [(cache_fox_k[j], cache_fox_v[j], cache_fox_logf[j]) for j in range(n_odd)]
    y_sample, ev_n, od_n = _trunk(x_sample, past, layers, mem_s, ev_s, od_s)

    mem_shape = (b, m_tok, MEM_HEADS, MEM_HEAD_DIM)
    p_even = [jnp.stack([s[f] for s in ev_p]) for f in range(4)]
    p_odd = [jnp.stack([s[f] for s in od_p]) for f in range(3)]
    p_mem_k = jnp.stack([m[0].reshape(mem_shape) for m in mem_p])
    p_mem_v = jnp.stack([m[1].reshape(mem_shape) for m in mem_p])
    s_even = [jnp.stack([s[f] for s in ev_n]) for f in range(4)]
    s_odd = [jnp.stack([s[f] for s in od_n]) for f in range(3)]
    return (y_prompt, y_sample, *p_even, *p_odd, p_mem_k, p_mem_v, *s_even, *s_odd)
```

```python
import functools
import math

import jax
import jax.numpy as jnp
from jax import lax
from jax.experimental import pallas as pl
from jax.experimental.pallas import tpu as pltpu

F32 = jnp.float32
BF16 = jnp.bfloat16

NORM_EPS = 1e-6
NEG_INF = -1e30
LOG2E = math.log2(math.e)
CHUNK = 64
LANES = 128
ROPE_THETA = 10000.0
LRU_C = 8.0
MLA_HEADS = 8
MLA_NOPE = 64
MLA_ROPE = 32
MLA_QK = MLA_NOPE + MLA_ROPE
MLA_V = 64
MLA_Q_LORA = 256
MLA_KV_LORA = 128
LRU_WIDTH = 512
LRU_BLOCKS = 8
CONV_WIDTH = 4
FOX_HEADS = 16
FOX_HEAD_DIM = 64
FOX_WIDTH = FOX_HEADS * FOX_HEAD_DIM
MEM_HEADS = 4
MEM_HEAD_DIM = 128
MEM_WIDTH = MEM_HEADS * MEM_HEAD_DIM
HALO = 8

VMEM_LIMIT = 56 * 1024 * 1024


def _dot(a, b):
    return jnp.dot(a, b, preferred_element_type=F32)


def _dot_nt(a, b):
    return lax.dot_general(a, b, (((1,), (1,)), ((), ())), preferred_element_type=F32)


def _rms(x, g):
    return x * lax.rsqrt(jnp.mean(x * x, axis=-1, keepdims=True) + NORM_EPS) * g


def _head_rms(x, g, n_live):
    ss = jnp.sum(x * x, axis=-1, keepdims=True) * (1.0 / n_live)
    return x * lax.rsqrt(ss + NORM_EPS) * g


def _sigmoid(x):
    return 1.0 / (1.0 + jnp.exp(-x))


def _log1p(y):
    u = 1.0 + y
    d = u - 1.0
    return jnp.where(d == 0.0, y, jnp.log(u) * (y / jnp.where(d == 0.0, 1.0, d)))


def _softplus(x):
    return jnp.maximum(x, 0.0) + _log1p(jnp.exp(-jnp.abs(x)))


def _gelu_tanh(x):
    return 0.5 * x * (1.0 + jnp.tanh(math.sqrt(2.0 / math.pi) * (x + 0.044715 * (x * x * x))))


def _split_bf16(x, parts):
    out = []
    r = x
    for _ in range(parts):
        p = r.astype(BF16)
        out.append(p)
        r = r - p.astype(F32)
    return out


def _rope128(x, c, s1, s2):
    return x * c + pltpu.roll(x, LANES - 16, 1) * s1 + pltpu.roll(x, 16, 1) * s2


def _const_spec(shape):
    nd = len(shape)
    return pl.BlockSpec(shape, lambda *_: (0,) * nd, pipeline_mode=pl.Buffered(1))


def _params(*sem):
    return pltpu.CompilerParams(dimension_semantics=sem, vmem_limit_bytes=VMEM_LIMIT)


def _row_tile(n, cap):
    t = min(n, cap)
    assert n % t == 0, (n, t)
    return t


FFN_CHUNKS = 2


def _swiglu_half_step(x, g_ref, wg_ref, wu_ref, wo_ref):
    hb = _rms(x, g_ref[...]).astype(BF16)
    fc = wg_ref.shape[1] // FFN_CHUNKS
    acc = jnp.zeros_like(x)
    for c in range(FFN_CHUNKS):
        sl = slice(c * fc, (c + 1) * fc)
        gate = _dot(hb, wg_ref[:, sl])
        up = _dot(hb, wu_ref[:, sl])
        act = (gate * _sigmoid(gate) * up).astype(BF16)
        acc = acc + _dot(act, wo_ref[sl, :])
    return x + 0.5 * acc


def _ffn_body(x_ref, g_ref, wg_ref, wu_ref, wo_ref, o_ref):
    o_ref[...] = _swiglu_half_step(x_ref[...], g_ref, wg_ref, wu_ref, wo_ref)


def _ffn(x2, w):
    n, d = x2.shape
    f = w["wg"].shape[1]
    tm = _row_tile(n, 512)
    return pl.pallas_call(
        _ffn_body,
        grid=(n // tm,),
        in_specs=[pl.BlockSpec((tm, d), lambda i: (i, 0)), _const_spec((1, d)),
                  _const_spec((d, f)), _const_spec((d, f)), _const_spec((f, d))],
        out_specs=pl.BlockSpec((tm, d), lambda i: (i, 0)),
        out_shape=jax.ShapeDtypeStruct((n, d), F32),
        compiler_params=_params("parallel"),
        name="ffn",
    )(x2, w["g"], w["wg"], w["wu"], w["wo"])


def _mem_cross_attention(h, g_ref, wq_ref, gq_ref, mk_ref, mv_ref, wo_ref):
    hb = _rms(h, g_ref[...]).astype(BF16)
    q = _dot(hb, wq_ref[...])
    outs = []
    for hd in range(MEM_HEADS):
        sl = slice(hd * MEM_HEAD_DIM, (hd + 1) * MEM_HEAD_DIM)
        qh = _head_rms(q[:, sl], gq_ref[...], MEM_HEAD_DIM).astype(BF16)
        s = _dot_nt(qh, mk_ref[0, :, sl])
        e = jnp.exp(s - jnp.max(s, axis=-1, keepdims=True))
        p = e / jnp.sum(e, axis=-1, keepdims=True)
        outs.append(_dot(p.astype(BF16), mv_ref[0, :, sl]).astype(BF16))
    return h + _dot(jnp.concatenate(outs, axis=-1), wo_ref[...])


def _post_mixer_body(*refs, n_parts):
    h_ref = refs[0]
    parts = refs[1:1 + n_parts]
    w_parts = refs[1 + n_parts:1 + 2 * n_parts]
    (mg_ref, mwq_ref, mgq_ref, mk_ref, mv_ref, mwo_ref,
     fg_ref, fwg_ref, fwu_ref, fwo_ref, o_ref) = refs[1 + 2 * n_parts:]
    h = h_ref[0]
    for p_ref, w_ref in zip(parts, w_parts):
        h = h + _dot(p_ref[0], w_ref[...])
    h = _mem_cross_attention(h, mg_ref, mwq_ref, mgq_ref, mk_ref, mv_ref, mwo_ref)
    o_ref[0] = _swiglu_half_step(h, fg_ref, fwg_ref, fwu_ref, fwo_ref)


def _post_mixer(h, parts, w_parts, mk, mv, wm, wf):
    b, t, d = h.shape
    m = mk.shape[1]
    f = wf["wg"].shape[1]
    tm = _row_tile(t, 512)
    row = lambda n: pl.BlockSpec((1, tm, n), lambda i, j: (i, j, 0))
    mem = lambda: pl.BlockSpec((1, m, MEM_WIDTH), lambda i, j: (i, 0, 0))
    in_specs = [row(d)] + [row(p.shape[2]) for p in parts] + [_const_spec(w.shape) for w in w_parts]
    in_specs += [_const_spec((1, d)), _const_spec((d, MEM_WIDTH)), _const_spec((1, MEM_HEAD_DIM)), mem(), mem(),
                 _const_spec((MEM_WIDTH, d)),
                 _const_spec((1, d)), _const_spec((d, f)), _const_spec((d, f)), _const_spec((f, d))]
    return pl.pallas_call(
        functools.partial(_post_mixer_body, n_parts=len(parts)),
        grid=(b, t // tm),
        in_specs=in_specs,
        out_specs=row(d),
        out_shape=jax.ShapeDtypeStruct((b, t, d), F32),
        compiler_params=_params("parallel", "parallel"),
        name="post_mixer",
    )(h, *parts, *w_parts, wm["g"], wm["wq"], wm["gq"], mk, mv, wm["wo"],
      wf["g"], wf["wg"], wf["wu"], wf["wo"])


def _mem_kv_body(m_ref, g_ref, wk_ref, wv_ref, gk_ref, k32_ref, v32_ref, kb_ref, vb_ref):
    hb = _rms(m_ref[0], g_ref[...]).astype(BF16)
    k = _dot(hb, wk_ref[...])
    v = _dot(hb, wv_ref[...])
    for hd in range(MEM_HEADS):
        sl = slice(hd * MEM_HEAD_DIM, (hd + 1) * MEM_HEAD_DIM)
        kh = _head_rms(k[:, sl], gk_ref[...], MEM_HEAD_DIM)
        k32_ref[0, :, sl] = kh
        kb_ref[0, :, sl] = kh.astype(BF16)
    v32_ref[0] = v
    vb_ref[0] = v.astype(BF16)


def _mem_kv(mem, w):
    b, m, d = mem.shape
    blk = lambda: pl.BlockSpec((1, m, MEM_WIDTH), lambda i: (i, 0, 0))
    return pl.pallas_call(
        _mem_kv_body,
        grid=(b,),
        in_specs=[pl.BlockSpec((1, m, d), lambda i: (i, 0, 0)), _const_spec((1, d)),
                  _const_spec((d, MEM_WIDTH)), _const_spec((d, MEM_WIDTH)), _const_spec((1, MEM_HEAD_DIM))],
        out_specs=[blk(), blk(), blk(), blk()],
        out_shape=[jax.ShapeDtypeStruct((b, m, MEM_WIDTH), F32), jax.ShapeDtypeStruct((b, m, MEM_WIDTH), F32),
                   jax.ShapeDtypeStruct((b, m, MEM_WIDTH), BF16), jax.ShapeDtypeStruct((b, m, MEM_WIDTH), BF16)],
        compiler_params=_params("parallel"),
        name="mem_kv",
    )(mem, w["g_src"], w["wk"], w["wv"], w["gk"])


def _mla_in_body(h_ref, g_ref, wcq_ref, gql_ref, wuq_ref, gq_ref, wckv_ref, gkv_ref, wkr_ref,
                 c_ref, s1_ref, s2_ref, q_ref, lat_ref, krp_ref):
    hb = _rms(h_ref[0], g_ref[...]).astype(BF16)
    c, s1, s2 = c_ref[...], s1_ref[...], s2_ref[...]
    cq = _rms(_dot(hb, wcq_ref[...]), gql_ref[...]).astype(BF16)
    q = _dot(cq, wuq_ref[...])
    for hd in range(MLA_HEADS):
        sl = slice(hd * LANES, (hd + 1) * LANES)
        qh = _rope128(q[:, sl], c, s1, s2)
        q_ref[0, :, sl] = _head_rms(qh, gq_ref[...], MLA_QK).astype(BF16)
    lat_ref[0] = _rms(_dot(hb, wckv_ref[...]), gkv_ref[...])
    krp_ref[0] = _rope128(_dot(hb, wkr_ref[...]), c, s1, s2)


def _mla_in(h, w, tables):
    b, t, d = h.shape
    tm = _row_tile(t, 512)
    row = lambda n: pl.BlockSpec((1, tm, n), lambda i, j: (i, j, 0))
    tab = lambda: pl.BlockSpec((tm, LANES), lambda i, j: (j, 0))
    return pl.pallas_call(
        _mla_in_body,
        grid=(b, t // tm),
        in_specs=[row(d), _const_spec((1, d)),
                  _const_spec((d, MLA_Q_LORA)), _const_spec((1, MLA_Q_LORA)),
                  _const_spec((MLA_Q_LORA, MLA_HEADS * LANES)), _const_spec((1, LANES)),
                  _const_spec((d, MLA_KV_LORA)), _const_spec((1, MLA_KV_LORA)), _const_spec((d, LANES)),
                  tab(), tab(), tab()],
        out_specs=[row(MLA_HEADS * LANES), row(MLA_KV_LORA), row(LANES)],
        out_shape=[jax.ShapeDtypeStruct((b, t, MLA_HEADS * LANES), BF16),
                   jax.ShapeDtypeStruct((b, t, MLA_KV_LORA), F32),
                   jax.ShapeDtypeStruct((b, t, LANES), F32)],
        compiler_params=_params("parallel", "parallel"),
        name="mla_in",
    )(h, w["g_mix"], w["wcq"], w["g_qlat"], w["wuq"], w["gq"], w["wckv"], w["g_kvlat"], w["wkr"], *tables)


def _mla_kv_body(lat_ref, krp_ref, wuk_ref, wuvt_ref, gk_ref, k_ref, vt_ref):
    lb = lat_ref[0].astype(BF16)
    krp = krp_ref[0]
    kn = _dot(lb, wuk_ref[...])
    for hd in range(MLA_HEADS):
        sl = slice(hd * LANES, (hd + 1) * LANES)
        k_ref[0, :, sl] = _head_rms(kn[:, sl] + krp, gk_ref[...], MLA_QK).astype(BF16)
    vt_ref[0] = _dot_nt(wuvt_ref[...], lb).astype(BF16)


def _mla_kv(lat, krp, w):
    b, l, _ = lat.shape
    tl = _row_tile(l, 512) if l % 512 == 0 else l
    row = lambda n: pl.BlockSpec((1, tl, n), lambda i, j: (i, j, 0))
    vw = MLA_HEADS * MLA_V
    return pl.pallas_call(
        _mla_kv_body,
        grid=(b, l // tl),
        in_specs=[row(MLA_KV_LORA), row(LANES), _const_spec((MLA_KV_LORA, MLA_HEADS * LANES)),
                  _const_spec((vw, MLA_KV_LORA)), _const_spec((1, LANES))],
        out_specs=[row(MLA_HEADS * LANES), pl.BlockSpec((1, vw, tl), lambda i, j: (i, 0, j))],
        out_shape=[jax.ShapeDtypeStruct((b, l, MLA_HEADS * LANES), BF16),
                   jax.ShapeDtypeStruct((b, vw, l), BF16)],
        compiler_params=_params("parallel", "parallel"),
        name="mla_kv",
    )(lat, krp, w["wuk"], w["wuvt"], w["gk"])


def _lru_body(h_ref, g_ref, wrec_ref, wgate_ref, cw_ref, cb_ref, wr_ref, wi_ref, br_ref, bi_ref, lam_ref,
              cprev_ref, h0_ref, y_ref, hl_ref, cl_ref, buf, a_s, b_s, hcar, *, tm):
    @pl.when(pl.program_id(1) == 0)
    def _():
        buf[0:HALO, :] = cprev_ref[0]
        hcar[...] = h0_ref[0]

    hb = _rms(h_ref[0], g_ref[...]).astype(BF16)
    xr = _dot(hb, wrec_ref[...])
    xg = _dot(hb, wgate_ref[...])
    buf[HALO:HALO + tm, :] = xr
    xc = cb_ref[...] + xr * cw_ref[CONV_WIDTH - 1:CONV_WIDTH, :]
    for j in range(CONV_WIDTH - 1):
        off = HALO - (CONV_WIDTH - 1) + j
        xc = xc + cw_ref[j:j + 1, :] * buf[off:off + tm, :]
    xcb = xc.astype(BF16)
    r = _sigmoid(_dot(xcb, wr_ref[...]) + br_ref[...])
    i = _sigmoid(_dot(xcb, wi_ref[...]) + bi_ref[...])
    log_a = (-LRU_C) * r * _softplus(-lam_ref[...])
    a = jnp.exp(log_a)
    b = jnp.sqrt(-jnp.tanh(log_a) * (a * a + 1.0)) * (i * xc)
    a_s[...] = a
    b_s[...] = b

    def step(t, hprev):
        hnew = a_s[pl.ds(t, 1), :] * hprev + b_s[pl.ds(t, 1), :]
        b_s[pl.ds(t, 1), :] = hnew
        return hnew

    hfin = lax.fori_loop(0, tm, step, hcar[...])
    hcar[...] = hfin
    y_ref[0] = (_gelu_tanh(xg) * b_s[...]).astype(BF16)
    hl_ref[0] = hfin
    tail = buf[tm:tm + HALO, :]
    buf[0:HALO, :] = tail
    cl_ref[0] = tail


def _lru(h, w, conv_prev8, h0):
    b, t, d = h.shape
    tm = _row_tile(t, 512)
    wd = LRU_WIDTH
    vec = lambda: _const_spec((1, wd))
    return pl.pallas_call(
        functools.partial(_lru_body, tm=tm),
        grid=(b, t // tm),
        in_specs=[pl.BlockSpec((1, tm, d), lambda i, j: (i, j, 0)), _const_spec((1, d)),
                  _const_spec((d, wd)), _const_spec((d, wd)), _const_spec((CONV_WIDTH, wd)), vec(),
                  _const_spec((wd, wd)), _const_spec((wd, wd)), vec(), vec(), vec(),
                  pl.BlockSpec((1, HALO, wd), lambda i, j: (i, 0, 0)),
                  pl.BlockSpec((1, 1, wd), lambda i, j: (i, 0, 0))],
        out_specs=[pl.BlockSpec((1, tm, wd), lambda i, j: (i, j, 0)),
                   pl.BlockSpec((1, 1, wd), lambda i, j: (i, 0, 0)),
                   pl.BlockSpec((1, HALO, wd), lambda i, j: (i, 0, 0))],
        out_shape=[jax.ShapeDtypeStruct((b, t, wd), BF16), jax.ShapeDtypeStruct((b, 1, wd), F32),
                   jax.ShapeDtypeStruct((b, HALO, wd), F32)],
        scratch_shapes=[pltpu.VMEM((tm + HALO, wd), F32), pltpu.VMEM((tm, wd), F32),
                        pltpu.VMEM((tm, wd), F32), pltpu.VMEM((1, wd), F32)],
        compiler_params=_params("parallel", "arbitrary"),
        name="lru",
    )(h, w["g_mix"], w["wrec"], w["wgate"], w["conv_w"], w["conv_b"], w["wr"], w["wi"], w["br"], w["bi"],
      w["lam"], conv_prev8, h0)


def _group_rms(x, e, et, g):
    hi, lo = _split_bf16(x * x, 2)
    ss = _dot(hi, e) + _dot(lo, e)
    inv = lax.rsqrt(ss * (1.0 / FOX_HEAD_DIM) + NORM_EPS)
    ih, il = _split_bf16(inv, 2)
    return x * (_dot(ih, et) + _dot(il, et)) * g


def _cumsum_rows(x, ltri):
    out = None
    for p in _split_bf16(x, 3):
        d = _dot(ltri, p)
        out = d if out is None else out + d
    return out


def _fox_aug(x, keep, bias):
    blocks = [x[:, (hd // 2) * LANES:(hd // 2 + 1) * LANES] for hd in range(FOX_HEADS)]
    return jnp.concatenate(blocks, axis=-1) * keep + bias


def _fox_key_bias(c, place):
    parts = jnp.concatenate(_split_bf16(c * (-LOG2E), 3), axis=-1)
    return _dot(parts, place)


def _fox_in_body(h_ref, g_ref, wq_ref, wk_ref, wv_ref, wf_ref, bf_ref, gq_ref, gk_ref, e_ref, et_ref,
                 keep_ref, ones_ref, place_ref, ltri_ref, c0_ref,
                 q_ref, k32_ref, v32_ref, ka_ref, vb_ref, lf_ref, c_ref, lf_s, carry, *, tm, tc):
    @pl.when(pl.program_id(1) == 0)
    def _():
        carry[...] = c0_ref[0]

    hb = _rms(h_ref[0], g_ref[...]).astype(BF16)
    e, et, keep = e_ref[...], et_ref[...], keep_ref[...]
    q = _group_rms(_dot(hb, wq_ref[...]), e, et, gq_ref[...])
    q_ref[0] = _fox_aug(q, keep, ones_ref[...]).astype(BF16)
    k = _group_rms(_dot(hb, wk_ref[...]), e, et, gk_ref[...])
    k32_ref[0] = k
    v = _dot(hb, wv_ref[...])
    v32_ref[0] = v
    vb_ref[0] = v.astype(BF16)
    logf = -_softplus(-(_dot(hb, wf_ref[...]) + bf_ref[...]))
    lf_ref[0] = logf
    if tc > tm:
        lf_s[...] = jnp.zeros_like(lf_s)
    lf_s[0:tm, :] = logf
    c = carry[...] + _cumsum_rows(lf_s[...], ltri_ref[...])[0:tm, :]
    c_ref[0] = c
    carry[...] = c[tm - 1:tm, :]
    ka_ref[0] = _fox_aug(k, keep, _fox_key_bias(c, place_ref[...])).astype(BF16)


def _fox_in(h, w, c0):
    b, t, d = h.shape
    tm = _row_tile(t, 512)
    tc = max(tm, LANES)
    ltri = jnp.tril(jnp.ones((tc, tc), F32)).astype(BF16)
    row = lambda n: pl.BlockSpec((1, tm, n), lambda i, j: (i, j, 0))
    fw, aw = FOX_WIDTH, FOX_HEADS * LANES
    return pl.pallas_call(
        functools.partial(_fox_in_body, tm=tm, tc=tc),
        grid=(b, t // tm),
        in_specs=[row(d), _const_spec((1, d)),
                  _const_spec((d, fw)), _const_spec((d, fw)), _const_spec((d, fw)), _const_spec((d, LANES)),
                  _const_spec((1, LANES)), _const_spec((1, fw)), _const_spec((1, fw)),
                  _const_spec((fw, LANES)), _const_spec((LANES, fw)),
                  _const_spec((1, aw)), _const_spec((1, aw)), _const_spec((3 * LANES, aw)),
                  _const_spec((tc, tc)), pl.BlockSpec((1, 1, LANES), lambda i, j: (i, 0, 0))],
        out_specs=[row(aw), row(fw), row(fw), row(aw), row(fw), row(LANES), row(LANES)],
        out_shape=[jax.ShapeDtypeStruct((b, t, aw), BF16), jax.ShapeDtypeStruct((b, t, fw), F32),
                   jax.ShapeDtypeStruct((b, t, fw), F32), jax.ShapeDtypeStruct((b, t, aw), BF16),
                   jax.ShapeDtypeStruct((b, t, fw), BF16), jax.ShapeDtypeStruct((b, t, LANES), F32),
                   jax.ShapeDtypeStruct((b, t, LANES), F32)],
        scratch_shapes=[pltpu.VMEM((tc, LANES), F32), pltpu.VMEM((1, LANES), F32)],
        compiler_params=_params("parallel", "arbitrary"),
        name="fox_in",
    )(h, w["g_mix"], w["wq"], w["wk"], w["wv"], w["wf"], w["bf"], w["gq"], w["gk"], w["e"], w["et"],
      w["keep"], w["ones"], w["place"], ltri, c0)


def _fox_past_body(k_ref, lf_ref, keep_ref, place_ref, ltri_ref, ka_ref, c_ref, carry):
    @pl.when(pl.program_id(1) == 0)
    def _():
        carry[...] = jnp.zeros_like(carry)

    c = carry[...] + _cumsum_rows(lf_ref[0], ltri_ref[...])
    c_ref[0] = c
    carry[...] = c[c.shape[0] - 1:, :]
    ka_ref[0] = _fox_aug(k_ref[0], keep_ref[...], _fox_key_bias(c, place_ref[...])).astype(BF16)


def _fox_past(past_k, past_logf, w):
    b, p, n = past_logf.shape
    tc = _row_tile(p, 512)
    ltri = jnp.tril(jnp.ones((tc, tc), F32)).astype(BF16)
    fw, aw = FOX_WIDTH, FOX_HEADS * LANES
    row = lambda m: pl.BlockSpec((1, tc, m), lambda i, j: (i, j, 0))
    return pl.pallas_call(
        _fox_past_body,
        grid=(b, p // tc),
        in_specs=[row(fw), row(n), _const_spec((1, aw)), _const_spec((3 * LANES, aw)), _const_spec((tc, tc))],
        out_specs=[row(aw), row(n)],
        out_shape=[jax.ShapeDtypeStruct((b, p, aw), BF16), jax.ShapeDtypeStruct((b, p, n), F32)],
        scratch_shapes=[pltpu.VMEM((1, n), F32)],
        compiler_params=_params("parallel", "arbitrary"),
        name="fox_past",
    )(past_k, past_logf, w["keep"], w["place"], ltri)


def _flash_body(q_ref, k_ref, vt_ref, o_ref, m_s, l_s, acc_s, sa_s, sb_s, *, tq, tk, tks, n_k, past, kv_len,
                chunk_causal, diag_aligned):
    q_start = past + pl.program_id(2) * tq
    q = q_ref[0]
    m_s[...] = jnp.full_like(m_s, NEG_INF)
    l_s[...] = jnp.zeros_like(l_s)
    acc_s[...] = jnp.zeros_like(acc_s)
    shift = int(math.log2(CHUNK))

    def block_kind(r0, rn, c0, cn):
        if chunk_causal:
            k_lo, k_hi, q_lo, q_hi = r0 >> shift, (r0 + rn - 1) >> shift, c0 >> shift, (c0 + cn - 1) >> shift
        else:
            k_lo, k_hi, q_lo, q_hi = r0, r0 + rn - 1, c0, c0 + cn - 1
        return "visible" if k_hi <= q_lo else ("hidden" if k_lo > q_hi else "partial")

    full_streams = [(hh, r0, 0, tq, "visible") for r0 in range(0, tk, tks) for hh in range(2)]
    if diag_aligned:
        cn = tks
        mask_streams = [(hh, r0, c0, cn, block_kind(r0, tks, c0, cn)) for r0 in range(0, tk, tks)
                        for hh in range(2) for c0 in range(0, tq, cn)]
        mask_streams = [st for st in mask_streams if st[4] != "hidden"]
    else:
        mask_streams = [(hh, r0, 0, tq, "partial") for hh, r0, _, _, _ in full_streams]

    def score(kblk, stream):
        hh, r0, c0, cn, _ = stream
        head = slice(hh * LANES, (hh + 1) * LANES)
        return _dot_nt(kblk[r0:r0 + tks, head], q[c0:c0 + cn, head])

    def key_block(kt):
        return k_ref[0, pl.ds(pl.multiple_of(jnp.minimum(kt, n_k - 1) * tk, tk), tk), :]

    def absorb(s, stream, kt):
        hh, r0, c0, cn, kind = stream
        cols = slice(c0, c0 + cn)
        k0 = pl.multiple_of(kt * tk, tk)
        if kind == "partial":
            kpos = k0 + r0 + lax.broadcasted_iota(jnp.int32, (tks, cn), 0)
            qpos = q_start + c0 + lax.broadcasted_iota(jnp.int32, (tks, cn), 1)
            if chunk_causal:
                vis = lax.shift_right_logical(kpos, shift) <= lax.shift_right_logical(qpos, shift)
            else:
                vis = kpos <= qpos
            s = jnp.where(jnp.logical_and(vis, kpos < kv_len), s, NEG_INF)
        m_old = m_s[hh, :, cols]
        m_new = jnp.maximum(m_old, jnp.max(s, axis=0, keepdims=True))
        alpha = jnp.exp2(m_old - m_new)
        p = jnp.exp2(s - m_new)
        l_s[hh, :, cols] = alpha * l_s[hh, :, cols] + jnp.sum(p, axis=0, keepdims=True)
        pv = _dot(vt_ref[0, :, pl.ds(pl.multiple_of(k0 + r0, LANES), tks)], p.astype(BF16))
        acc_s[hh, :, cols] = alpha * acc_s[hh, :, cols] + pv
        m_s[hh, :, cols] = m_new

    def single_tile(kt, streams):
        kblk = key_block(kt)
        ahead = 2
        pending = [score(kblk, st) for st in streams[:ahead]]
        for idx, st in enumerate(streams):
            s = pending.pop(0)
            if idx + ahead < len(streams):
                pending.append(score(kblk, streams[idx + ahead]))
            absorb(s, st, kt)

    def store_scores(buf, kt):
        kblk = key_block(kt)
        for hh, r0, _, _, _ in full_streams:
            buf[hh, r0:r0 + tks, :] = score(kblk, (hh, r0, 0, tq, "visible"))

    def absorb_stored(buf, kt):
        for st in full_streams:
            absorb(buf[st[0], st[1]:st[1] + tks, :], st, kt)

    def tile_pair(i, carry):
        kt = 2 * i
        store_scores(sb_s, kt + 1)
        absorb_stored(sa_s, kt)
        store_scores(sa_s, kt + 2)
        absorb_stored(sb_s, kt + 1)
        return carry

    def full_tile(kt, carry):
        single_tile(kt, full_streams)
        return carry

    def masked_tile(kt, carry):
        single_tile(kt, mask_streams)
        return carry

    n_full = jnp.minimum(q_start // tk, kv_len // tk)
    q_last = q_start + tq - 1
    k_hi = (q_last // CHUNK + 1) * CHUNK if chunk_causal else q_last + 1
    n_end = jnp.minimum((k_hi + tk - 1) // tk, n_k)
    n_pair = n_full // 2

    @pl.when(n_pair > 0)
    def _():
        store_scores(sa_s, 0)

    lax.fori_loop(0, n_pair, tile_pair, 0)
    lax.fori_loop(2 * n_pair, n_full, full_tile, 0)
    lax.fori_loop(n_full, n_end, masked_tile, 0)
    sub = lax.broadcasted_iota(jnp.int32, (LANES, tq), 0)
    out_t = jnp.where(sub < LANES // 2, acc_s[0] / l_s[0], acc_s[1] / l_s[1])
    o_ref[0] = out_t.T.astype(BF16)


def _flash(q, k, vt, *, past, kv_len, chunk_causal):
    b, t, w = q.shape
    lp = k.shape[1]
    n_pairs = w // (2 * LANES)
    t_pad = max(t, LANES)
    if t_pad > t:
        q = _pad_rows(q, t_pad)
    if past == 0 and t % 512 == 0 and lp % 512 == 0:
        tq = tk = 512
    elif t_pad == LANES:
        tq, tk = LANES, lp
    else:
        tq = tk = LANES
    assert t_pad % tq == 0 and lp % tk == 0, (t, lp, tq, tk)
    tks = 256 if tk % 256 == 0 and tq > LANES else tk
    out = pl.pallas_call(
        functools.partial(_flash_body, tq=tq, tk=tk, tks=tks, n_k=lp // tk, past=past, kv_len=kv_len,
                          chunk_causal=chunk_causal,
                          diag_aligned=(past == 0 and tq == tk and kv_len == lp and tks < tk)),
        grid=(b, n_pairs, t_pad // tq),
        in_specs=[pl.BlockSpec((1, tq, 2 * LANES), lambda i, j, s: (i, s, j)),
                  pl.BlockSpec((1, lp, 2 * LANES), lambda i, j, s: (i, 0, j)),
                  pl.BlockSpec((1, LANES, lp), lambda i, j, s: (i, j, 0))],
        out_specs=pl.BlockSpec((1, tq, LANES), lambda i, j, s: (i, s, j)),
        out_shape=jax.ShapeDtypeStruct((b, t_pad, n_pairs * LANES), BF16),
        scratch_shapes=[pltpu.VMEM((2, 1, tq), F32), pltpu.VMEM((2, 1, tq), F32),
                        pltpu.VMEM((2, LANES, tq), F32),
                        pltpu.VMEM((2, tk, tq), F32), pltpu.VMEM((2, tk, tq), F32)],
        compiler_params=_params("parallel", "parallel", "arbitrary"),
        name="flash_mla" if chunk_causal else "flash_fox",
    )(q, k, vt)
    return out[:, :t]


def _row(v):
    return v.reshape(1, -1).astype(F32)


def _pad_lanes(x, lo, total):
    pad = [(0, 0)] * (x.ndim - 1) + [(lo, total - lo - x.shape[-1])]
    return jnp.pad(x, pad)


def _ffn_weights(g, w_in, w_out):
    f = w_out.shape[0]
    return dict(g=_row(g), wg=w_in[:, :f].astype(BF16), wu=w_in[:, f:].astype(BF16), wo=w_out.astype(BF16))


def _mem_weights(g, g_src, w_q, w_kv, w_o, g_q, g_k):
    d = w_q.shape[0]
    kv = w_kv.reshape(d, MEM_HEADS, 2, MEM_HEAD_DIM)
    return dict(g=_row(g), g_src=_row(g_src), wq=w_q.astype(BF16), wo=w_o.astype(BF16),
                wk=kv[:, :, 0].reshape(d, MEM_WIDTH).astype(BF16),
                wv=kv[:, :, 1].reshape(d, MEM_WIDTH).astype(BF16),
                gq=_row(g_q) * (MEM_HEAD_DIM ** -0.5), gk=_row(g_k))


def _even_weights(g_mix, w_in, g_qlat, g_kvlat, w_uq, w_ukv, g_q, g_k, conv_w, conv_b, gate_w, gate_b, lam, w_out):
    d = w_in.shape[0]
    o1 = MLA_Q_LORA
    o2 = o1 + MLA_KV_LORA
    o3 = o2 + MLA_ROPE
    o4 = o3 + LRU_WIDTH
    uq = _pad_lanes(w_uq.reshape(MLA_Q_LORA, MLA_HEADS, MLA_QK), 0, LANES)
    ukv = w_ukv.reshape(MLA_KV_LORA, MLA_HEADS, MLA_NOPE + MLA_V)
    uk = _pad_lanes(ukv[:, :, :MLA_NOPE], 0, LANES)
    blk = LRU_WIDTH // LRU_BLOCKS
    eye = jnp.eye(LRU_BLOCKS, dtype=F32)
    wr = jnp.einsum("ncd,nm->ncmd", gate_w[:, :, :blk], eye).reshape(LRU_WIDTH, LRU_WIDTH)
    wi = jnp.einsum("ncd,nm->ncmd", gate_w[:, :, blk:], eye).reshape(LRU_WIDTH, LRU_WIDTH)
    return dict(
        g_mix=_row(g_mix), wcq=w_in[:, :o1].astype(BF16), g_qlat=_row(g_qlat),
        wuq=uq.reshape(MLA_Q_LORA, MLA_HEADS * LANES).astype(BF16),
        gq=_pad_lanes(_row(g_q), 0, LANES) * (MLA_QK ** -0.5 * LOG2E),
        wckv=w_in[:, o1:o2].astype(BF16), g_kvlat=_row(g_kvlat),
        wkr=_pad_lanes(w_in[:, o2:o3], MLA_NOPE, LANES).astype(BF16),
        wuk=uk.reshape(MLA_KV_LORA, MLA_HEADS * LANES).astype(BF16),
        wuvt=ukv[:, :, MLA_NOPE:].reshape(MLA_KV_LORA, MLA_HEADS * MLA_V).T.astype(BF16),
        gk=_pad_lanes(_row(g_k), 0, LANES),
        wrec=w_in[:, o3:o4].astype(BF16), wgate=w_in[:, o4:].astype(BF16),
        conv_w=conv_w.astype(F32), conv_b=_row(conv_b), wr=wr.astype(BF16), wi=wi.astype(BF16),
        br=_row(gate_b[:, :blk]), bi=_row(gate_b[:, blk:]), lam=_row(lam),
        wo_attn=w_out[:MLA_HEADS * MLA_V].astype(BF16), wo_rec=w_out[MLA_HEADS * MLA_V:].astype(BF16))


def _odd_weights(g_mix, w_in, b_f, g_q, g_k, w_out):
    fw = FOX_WIDTH
    head_of_lane = jnp.arange(fw) // FOX_HEAD_DIM
    e = (head_of_lane[:, None] == jnp.arange(LANES)[None, :]).astype(BF16)
    lane = jnp.arange(FOX_HEADS * LANES)
    hd, within = lane // LANES, lane % LANES
    own_low = hd % 2 == 0
    keep = jnp.where(own_low, within < FOX_HEAD_DIM, within >= FOX_HEAD_DIM)
    part = within - jnp.where(own_low, FOX_HEAD_DIM, 0)
    is_bias = (part >= 0) & (part < 3)
    src = part * LANES + hd
    place = ((jnp.arange(3 * LANES)[:, None] == src[None, :]) & is_bias[None, :]).astype(BF16)
    return dict(
        keep=keep.astype(F32)[None, :], ones=is_bias.astype(F32)[None, :], place=place,
        g_mix=_row(g_mix), wq=w_in[:, :fw].astype(BF16), wk=w_in[:, fw:2 * fw].astype(BF16),
        wv=w_in[:, 2 * fw:3 * fw].astype(BF16), wf=_pad_lanes(w_in[:, 3 * fw:], 0, LANES).astype(BF16),
        bf=_pad_lanes(_row(b_f), 0, LANES),
        gq=jnp.tile(_row(g_q), (1, FOX_HEADS)) * (FOX_HEAD_DIM ** -0.5 * LOG2E),
        gk=jnp.tile(_row(g_k), (1, FOX_HEADS)), e=e, et=e.T, wo=w_out.astype(BF16))


def _rope_tables(pos):
    half = MLA_ROPE // 2
    inv_freq = ROPE_THETA ** (-jnp.arange(half, dtype=F32) / half)
    ang = pos.astype(F32)[:, None] * inv_freq[None, :]
    cos, sin = jnp.cos(ang), jnp.sin(ang)
    zeros = jnp.zeros_like(sin)
    c = jnp.concatenate([jnp.ones((pos.shape[0], MLA_NOPE), F32), cos, cos,
                         jnp.ones((pos.shape[0], LANES - MLA_QK), F32)], axis=-1)
    s1 = _pad_lanes(jnp.concatenate([-sin, zeros], axis=-1), MLA_NOPE, LANES)
    s2 = _pad_lanes(jnp.concatenate([zeros, sin], axis=-1), MLA_NOPE, LANES)
    return c, s1, s2


def _pad_rows(x, total):
    return jnp.pad(x, [(0, 0), (0, total - x.shape[1])] + [(0, 0)] * (x.ndim - 2))


def _kv_pad_len(t, past):
    l = past + t
    if past == 0 and t % 512 == 0:
        return l
    return -(-l // LANES) * LANES


def _even_layer(h, past, w, state):
    b, t, d = h.shape
    past_lat, past_krope, h0, conv_prev = state
    q, lat_new, krp_new = _mla_in(h, w, _rope_tables(past + jnp.arange(t)))
    lp = _kv_pad_len(t, past)
    lat_all = _pad_rows(jnp.concatenate([past_lat, lat_new], axis=1), lp)
    krp_all = _pad_rows(jnp.concatenate([_pad_lanes(past_krope, MLA_NOPE, LANES), krp_new], axis=1), lp)
    k, vt = _mla_kv(lat_all, krp_all, w)
    attn = _flash(q, k, vt, past=past, kv_len=past + t, chunk_causal=True)
    conv_prev8 = jnp.pad(conv_prev, ((0, 0), (HALO - (CONV_WIDTH - 1), 0), (0, 0)))
    y_rec, h_last, conv_last = _lru(h, w, conv_prev8, h0[:, None, :])
    new = (lat_new, krp_new[:, :, MLA_NOPE:MLA_QK], h_last[:, 0], conv_last[:, HALO - (CONV_WIDTH - 1):])
    return [attn, y_rec], [w["wo_attn"], w["wo_rec"]], new


def _odd_layer(h, past, w, state):
    b, t, d = h.shape
    past_k, past_v, past_logf = state
    if past > 0:
        ka_past, c_past = _fox_past(past_k.reshape(b, past, FOX_WIDTH),
                                    _pad_lanes(past_logf.astype(F32), 0, LANES), w)
        c0 = c_past[:, past - 1:past, :]
    else:
        ka_past = jnp.zeros((b, 0, FOX_HEADS * LANES), BF16)
        c0 = jnp.zeros((b, 1, LANES), F32)
    q, k32, v32, ka_new, vb, logf, _ = _fox_in(h, w, c0)
    lp = _kv_pad_len(t, past)
    k_all = _pad_rows(jnp.concatenate([ka_past, ka_new], axis=1), lp)
    v_all = _pad_rows(jnp.concatenate([past_v.reshape(b, past, FOX_WIDTH).astype(BF16), vb], axis=1), lp)
    attn = _flash(q, k_all, jnp.swapaxes(v_all, 1, 2), past=past, kv_len=past + t, chunk_causal=False)
    new = (k32.reshape(b, t, FOX_HEADS, FOX_HEAD_DIM), v32.reshape(b, t, FOX_HEADS, FOX_HEAD_DIM),
           logf[:, :, :FOX_HEADS])
    return [attn], [w["wo"]], new


def _trunk(x, past, layers, mem_kvs, even_states, odd_states):
    b, t, d = x.shape
    even_new, odd_new = [], []
    for li, lw in enumerate(layers):
        h = _ffn(x.reshape(b * t, d), lw["ffn1"]).reshape(b, t, d)
        if li % 2 == 0:
            parts, w_parts, new = _even_layer(h, past, lw["mix"], even_states[li // 2])
            even_new.append(new)
        else:
            parts, w_parts, new = _odd_layer(h, past, lw["mix"], odd_states[li // 2])
            odd_new.append(new)
        x = _post_mixer(h, parts, w_parts, mem_kvs[li][0], mem_kvs[li][1], lw["mem"], lw["ffn2"])
    return x, even_new, odd_new


def kernel(x_prompt, x_sample, mem_prompt, cache_mla_latent, cache_mla_krope, state_lru_h, state_lru_conv, cache_fox_k, cache_fox_v, cache_fox_logf, cache_mem_k, cache_mem_v, norm_ffn1, ffn1_w_in, ffn1_w_out, norm_mix, norm_mem, norm_mem_src, mem_w_q, mem_w_kv, mem_w_o, mem_g_q, mem_g_k, norm_ffn2, ffn2_w_in, ffn2_w_out, ev_w_in, ev_g_qlat, ev_g_kvlat, ev_w_uq, ev_w_ukv, ev_g_q, ev_g_k, ev_conv_w, ev_conv_b, ev_gate_w, ev_gate_b, ev_lambda, ev_w_out, od_w_in, od_b_f, od_g_q, od_g_k, od_w_out):
    depth = norm_ffn1.shape[0]
    n_even, n_odd = (depth + 1) // 2, depth // 2
    b, _, _ = x_prompt.shape
    bs = x_sample.shape[0]
    past = cache_mla_latent.shape[2] if n_even else cache_fox_k.shape[2]

    layers = []
    for li in range(depth):
        j = li // 2
        if li % 2 == 0:
            mix = _even_weights(norm_mix[li], ev_w_in[j], ev_g_qlat[j], ev_g_kvlat[j], ev_w_uq[j], ev_w_ukv[j],
                                ev_g_q[j], ev_g_k[j], ev_conv_w[j], ev_conv_b[j], ev_gate_w[j], ev_gate_b[j],
                                ev_lambda[j], ev_w_out[j])
        else:
            mix = _odd_weights(norm_mix[li], od_w_in[j], od_b_f[j], od_g_q[j], od_g_k[j], od_w_out[j])
        layers.append(dict(
            ffn1=_ffn_weights(norm_ffn1[li], ffn1_w_in[li], ffn1_w_out[li]),
            ffn2=_ffn_weights(norm_ffn2[li], ffn2_w_in[li], ffn2_w_out[li]),
            mem=_mem_weights(norm_mem[li], norm_mem_src[li], mem_w_q[li], mem_w_kv[li], mem_w_o[li],
                             mem_g_q[li], mem_g_k[li]),
            mix=mix))

    mem_p = [_mem_kv(mem_prompt, lw["mem"]) for lw in layers]
    ev0 = [(jnp.zeros((b, 0, MLA_KV_LORA), F32), jnp.zeros((b, 0, MLA_ROPE), F32),
            jnp.zeros((b, LRU_WIDTH), F32), jnp.zeros((b, CONV_WIDTH - 1, LRU_WIDTH), F32))
           for _ in range(n_even)]
    od0 = [(jnp.zeros((b, 0, FOX_HEADS, FOX_HEAD_DIM), F32), jnp.zeros((b, 0, FOX_HEADS, FOX_HEAD_DIM), F32),
            jnp.zeros((b, 0, FOX_HEADS), F32)) for _ in range(n_odd)]
    y_prompt, ev_p, od_p = _trunk(x_prompt, 0, layers, [(m[2], m[3]) for m in mem_p], ev0, od0)

    m_tok = cache_mem_k.shape[2]
    mem_s = [(cache_mem_k[li].reshape(bs, m_tok, MEM_WIDTH).astype(BF16),
              cache_mem_v[li].reshape(bs, m_tok, MEM_WIDTH).astype(BF16)) for li in range(depth)]
    ev_s = [(cache_mla_latent[j], cache_mla_krope[j], state_lru_h[j], state_lru_conv[j]) for j in range(n_even)]
    od_s = [(cache_fox_k[j], cache_fox_v[j], cache_fox_logf[j]) for j in range(n_odd)]
    y_sample, ev_n, od_n = _trunk(x_sample, past, layers, mem_s, ev_s, od_s)

    mem_shape = (b, m_tok, MEM_HEADS, MEM_HEAD_DIM)
    p_even = [jnp.stack([s[f] for s in ev_p]) for f in range(4)]
    p_odd = [jnp.stack([s[f] for s in od_p]) for f in range(3)]
    p_mem_k = jnp.stack([m[0].reshape(mem_shape) for m in mem_p])
    p_mem_v = jnp.stack([m[1].reshape(mem_shape) for m in mem_p])
    s_even = [jnp.stack([s[f] for s in ev_n]) for f in range(4)]
    s_odd = [jnp.stack([s[f] for s in od_n]) for f in range(3)]
    return (y_prompt, y_sample, *p_even, *p_odd, p_mem_k, p_mem_v, *s_even, *s_odd)
```

```python
import functools
import math

import jax
import jax.numpy as jnp
from jax import lax
from jax.experimental import pallas as pl
from jax.experimental.pallas import tpu as pltpu

F32 = jnp.float32
BF16 = jnp.bfloat16

NORM_EPS = 1e-6
NEG_INF = -1e30
LOG2E = math.log2(math.e)
CHUNK = 64
LANES = 128
SUBLANES = 8
ROPE_THETA = 10000.0
LRU_C = 8.0
MLA_HEADS = 8
MLA_NOPE = 64
MLA_ROPE = 32
MLA_QK = MLA_NOPE + MLA_ROPE
MLA_V = 64
MLA_Q_LORA = 256
MLA_KV_LORA = 128
LRU_WIDTH = 512
LRU_BLOCKS = 8
CONV_WIDTH = 4
FOX_HEADS = 16
FOX_HEAD_DIM = 64
FOX_WIDTH = FOX_HEADS * FOX_HEAD_DIM
MEM_HEADS = 4
MEM_HEAD_DIM = 128
MEM_WIDTH = MEM_HEADS * MEM_HEAD_DIM
HALO = 8

VMEM_LIMIT = 56 * 1024 * 1024


def _dot(a, b):
    return jnp.dot(a, b, preferred_element_type=F32)


def _dot_nt(a, b):
    return lax.dot_general(a, b, (((1,), (1,)), ((), ())), preferred_element_type=F32)


def _rms(x, g):
    return x * lax.rsqrt(jnp.mean(x * x, axis=-1, keepdims=True) + NORM_EPS) * g


def _head_rms(x, g, n_live):
    ss = jnp.sum(x * x, axis=-1, keepdims=True) * (1.0 / n_live)
    return x * lax.rsqrt(ss + NORM_EPS) * g


def _sigmoid(x):
    return 1.0 / (1.0 + jnp.exp(-x))


def _log1p(y):
    u = 1.0 + y
    d = u - 1.0
    return jnp.where(d == 0.0, y, jnp.log(u) * (y / jnp.where(d == 0.0, 1.0, d)))


def _softplus(x):
    return jnp.maximum(x, 0.0) + _log1p(jnp.exp(-jnp.abs(x)))


def _gelu_tanh(x):
    return 0.5 * x * (1.0 + jnp.tanh(math.sqrt(2.0 / math.pi) * (x + 0.044715 * (x * x * x))))


def _split_bf16(x, parts):
    out = []
    r = x
    for _ in range(parts):
        p = r.astype(BF16)
        out.append(p)
        r = r - p.astype(F32)
    return out


def _const_spec(shape):
    nd = len(shape)
    return pl.BlockSpec(shape, lambda *_: (0,) * nd, pipeline_mode=pl.Buffered(1))


def _params(*sem):
    return pltpu.CompilerParams(dimension_semantics=sem, vmem_limit_bytes=VMEM_LIMIT)


def _row_tile(n, cap):
    t = min(n, cap)
    assert n % t == 0, (n, t)
    return t


FFN_CHUNKS = 2


def _swiglu_half_step(x, g_ref, wg_ref, wu_ref, wo_ref):
    hb = _rms(x, g_ref[...]).astype(BF16)
    fc = wg_ref.shape[1] // FFN_CHUNKS
    acc = jnp.zeros_like(x)
    for c in range(FFN_CHUNKS):
        sl = slice(c * fc, (c + 1) * fc)
        gate = _dot(hb, wg_ref[:, sl])
        up = _dot(hb, wu_ref[:, sl])
        act = (gate * _sigmoid(gate) * up).astype(BF16)
        acc = acc + _dot(act, wo_ref[sl, :])
    return x + 0.5 * acc


def _ffn_body(x_ref, g_ref, wg_ref, wu_ref, wo_ref, o_ref):
    o_ref[...] = _swiglu_half_step(x_ref[...], g_ref, wg_ref, wu_ref, wo_ref)


def _ffn(x2, w):
    n, d = x2.shape
    f = w["wg"].shape[1]
    tm = _row_tile(n, 512)
    return pl.pallas_call(
        _ffn_body,
        grid=(n // tm,),
        in_specs=[pl.BlockSpec((tm, d), lambda i: (i, 0)), _const_spec((1, d)),
                  _const_spec((d, f)), _const_spec((d, f)), _const_spec((f, d))],
        out_specs=pl.BlockSpec((tm, d), lambda i: (i, 0)),
        out_shape=jax.ShapeDtypeStruct((n, d), F32),
        compiler_params=_params("parallel"),
        name="ffn",
    )(x2, w["g"], w["wg"], w["wu"], w["wo"])


def _mem_cross_attention(h, g_ref, wq_ref, gq_ref, mk_ref, mv_ref, wo_ref):
    hb = _rms(h, g_ref[...]).astype(BF16)
    q = _dot(hb, wq_ref[...])
    outs = []
    for hd in range(MEM_HEADS):
        sl = slice(hd * MEM_HEAD_DIM, (hd + 1) * MEM_HEAD_DIM)
        qh = _head_rms(q[:, sl], gq_ref[...], MEM_HEAD_DIM).astype(BF16)
        s = _dot_nt(qh, mk_ref[0, :, sl])
        e = jnp.exp(s - jnp.max(s, axis=-1, keepdims=True))
        p = e / jnp.sum(e, axis=-1, keepdims=True)
        outs.append(_dot(p.astype(BF16), mv_ref[0, :, sl]).astype(BF16))
    return h + _dot(jnp.concatenate(outs, axis=-1), wo_ref[...])


def _post_mixer_body(*refs, n_parts):
    h_ref = refs[0]
    parts = refs[1:1 + n_parts]
    w_parts = refs[1 + n_parts:1 + 2 * n_parts]
    (mg_ref, mwq_ref, mgq_ref, mk_ref, mv_ref, mwo_ref,
     fg_ref, fwg_ref, fwu_ref, fwo_ref, o_ref) = refs[1 + 2 * n_parts:]
    h = h_ref[0]
    for p_ref, w_ref in zip(parts, w_parts):
        h = h + _dot(p_ref[0], w_ref[...])
    h = _mem_cross_attention(h, mg_ref, mwq_ref, mgq_ref, mk_ref, mv_ref, mwo_ref)
    o_ref[0] = _swiglu_half_step(h, fg_ref, fwg_ref, fwu_ref, fwo_ref)


def _post_mixer(h, parts, w_parts, mk, mv, wm, wf):
    b, t, d = h.shape
    m = mk.shape[1]
    f = wf["wg"].shape[1]
    tm = _row_tile(t, 512)
    row = lambda n: pl.BlockSpec((1, tm, n), lambda i, j: (i, j, 0))
    mem = lambda: pl.BlockSpec((1, m, MEM_WIDTH), lambda i, j: (i, 0, 0))
    in_specs = [row(d)] + [row(p.shape[2]) for p in parts] + [_const_spec(w.shape) for w in w_parts]
    in_specs += [_const_spec((1, d)), _const_spec((d, MEM_WIDTH)), _const_spec((1, MEM_HEAD_DIM)), mem(), mem(),
                 _const_spec((MEM_WIDTH, d)),
                 _const_spec((1, d)), _const_spec((d, f)), _const_spec((d, f)), _const_spec((f, d))]
    return pl.pallas_call(
        functools.partial(_post_mixer_body, n_parts=len(parts)),
        grid=(b, t // tm),
        in_specs=in_specs,
        out_specs=row(d),
        out_shape=jax.ShapeDtypeStruct((b, t, d), F32),
        compiler_params=_params("parallel", "parallel"),
        name="post_mixer",
    )(h, *parts, *w_parts, wm["g"], wm["wq"], wm["gq"], mk, mv, wm["wo"],
      wf["g"], wf["wg"], wf["wu"], wf["wo"])


def _mem_kv_body(m_ref, g_ref, wk_ref, wv_ref, gk_ref, k32_ref, v32_ref, kb_ref, vb_ref):
    hb = _rms(m_ref[0], g_ref[...]).astype(BF16)
    k = _dot(hb, wk_ref[...])
    v = _dot(hb, wv_ref[...])
    for hd in range(MEM_HEADS):
        sl = slice(hd * MEM_HEAD_DIM, (hd + 1) * MEM_HEAD_DIM)
        kh = _head_rms(k[:, sl], gk_ref[...], MEM_HEAD_DIM)
        k32_ref[0, :, sl] = kh
        kb_ref[0, :, sl] = kh.astype(BF16)
    v32_ref[0] = v
    vb_ref[0] = v.astype(BF16)


def _mem_kv(mem, w):
    b, m, d = mem.shape
    blk = lambda: pl.BlockSpec((1, m, MEM_WIDTH), lambda i: (i, 0, 0))
    return pl.pallas_call(
        _mem_kv_body,
        grid=(b,),
        in_specs=[pl.BlockSpec((1, m, d), lambda i: (i, 0, 0)), _const_spec((1, d)),
                  _const_spec((d, MEM_WIDTH)), _const_spec((d, MEM_WIDTH)), _const_spec((1, MEM_HEAD_DIM))],
        out_specs=[blk(), blk(), blk(), blk()],
        out_shape=[jax.ShapeDtypeStruct((b, m, MEM_WIDTH), F32), jax.ShapeDtypeStruct((b, m, MEM_WIDTH), F32),
                   jax.ShapeDtypeStruct((b, m, MEM_WIDTH), BF16), jax.ShapeDtypeStruct((b, m, MEM_WIDTH), BF16)],
        compiler_params=_params("parallel"),
        name="mem_kv",
    )(mem, w["g_src"], w["wk"], w["wv"], w["gk"])


def _mla_in_body(h_ref, g_ref, wcq_ref, gql_ref, wuq_ref, wuqs_ref, gq_ref, wckv_ref, gkv_ref, wkr_ref, wkrs_ref,
                 c_ref, s_ref, q_ref, lat_ref, krp_ref):
    hb = _rms(h_ref[0], g_ref[...]).astype(BF16)
    c, s = c_ref[...], s_ref[...]
    cq = _rms(_dot(hb, wcq_ref[...]), gql_ref[...]).astype(BF16)
    q = _dot(cq, wuq_ref[...])
    q_partner = _dot(cq, wuqs_ref[...])
    for hd in range(MLA_HEADS):
        sl = slice(hd * LANES, (hd + 1) * LANES)
        qh = q[:, sl] * c + q_partner[:, sl] * s
        q_ref[0, :, sl] = _head_rms(qh, gq_ref[...], MLA_QK).astype(BF16)
    lat_ref[0] = _rms(_dot(hb, wckv_ref[...]), gkv_ref[...])
    krp_ref[0] = _dot(hb, wkr_ref[...]) * c + _dot(hb, wkrs_ref[...]) * s


def _mla_in(h, w, tables):
    b, t, d = h.shape
    tm = _row_tile(t, 512)
    row = lambda n: pl.BlockSpec((1, tm, n), lambda i, j: (i, j, 0))
    tab = lambda: pl.BlockSpec((tm, LANES), lambda i, j: (j, 0))
    return pl.pallas_call(
        _mla_in_body,
        grid=(b, t // tm),
        in_specs=[row(d), _const_spec((1, d)),
                  _const_spec((d, MLA_Q_LORA)), _const_spec((1, MLA_Q_LORA)),
                  _const_spec((MLA_Q_LORA, MLA_HEADS * LANES)), _const_spec((MLA_Q_LORA, MLA_HEADS * LANES)),
                  _const_spec((1, LANES)),
                  _const_spec((d, MLA_KV_LORA)), _const_spec((1, MLA_KV_LORA)),
                  _const_spec((d, LANES)), _const_spec((d, LANES)),
                  tab(), tab()],
        out_specs=[row(MLA_HEADS * LANES), row(MLA_KV_LORA), row(LANES)],
        out_shape=[jax.ShapeDtypeStruct((b, t, MLA_HEADS * LANES), BF16),
                   jax.ShapeDtypeStruct((b, t, MLA_KV_LORA), F32),
                   jax.ShapeDtypeStruct((b, t, LANES), F32)],
        compiler_params=_params("parallel", "parallel"),
        name="mla_in",
    )(h, w["g_mix"], w["wcq"], w["g_qlat"], w["wuq"], w["wuq_swap"], w["gq"], w["wckv"], w["g_kvlat"],
      w["wkr"], w["wkr_swap"], *tables)


def _mla_kv_body(lat_ref, krp_ref, wuk_ref, wuvt_ref, gk_ref, k_ref, vt_ref):
    lb = lat_ref[0].astype(BF16)
    krp = krp_ref[0]
    kn = _dot(lb, wuk_ref[...])
    for hd in range(MLA_HEADS):
        sl = slice(hd * LANES, (hd + 1) * LANES)
        k_ref[0, :, sl] = _head_rms(kn[:, sl] + krp, gk_ref[...], MLA_QK).astype(BF16)
    vt_ref[0] = _dot_nt(wuvt_ref[...], lb).astype(BF16)


def _mla_kv(lat, krp, w):
    b, l, _ = lat.shape
    tl = _row_tile(l, 512) if l % 512 == 0 else l
    row = lambda n: pl.BlockSpec((1, tl, n), lambda i, j: (i, j, 0))
    vw = MLA_HEADS * MLA_V
    return pl.pallas_call(
        _mla_kv_body,
        grid=(b, l // tl),
        in_specs=[row(MLA_KV_LORA), row(LANES), _const_spec((MLA_KV_LORA, MLA_HEADS * LANES)),
                  _const_spec((vw, MLA_KV_LORA)), _const_spec((1, LANES))],
        out_specs=[row(MLA_HEADS * LANES), pl.BlockSpec((1, vw, tl), lambda i, j: (i, 0, j))],
        out_shape=[jax.ShapeDtypeStruct((b, l, MLA_HEADS * LANES), BF16),
                   jax.ShapeDtypeStruct((b, vw, l), BF16)],
        compiler_params=_params("parallel", "parallel"),
        name="mla_kv",
    )(lat, krp, w["wuk"], w["wuvt"], w["gk"])


def _lru_body(h_ref, g_ref, wrec_ref, wgate_ref, cw_ref, cb_ref, wr_ref, wi_ref, br_ref, bi_ref, lam_ref,
              cprev_ref, h0_ref, y_ref, hl_ref, cl_ref, buf, a_s, b_s, hcar, *, tm):
    @pl.when(pl.program_id(1) == 0)
    def _():
        buf[0:HALO, :] = cprev_ref[0]
        hcar[...] = h0_ref[0]

    hb = _rms(h_ref[0], g_ref[...]).astype(BF16)
    xr = _dot(hb, wrec_ref[...])
    xg = _dot(hb, wgate_ref[...])
    buf[HALO:HALO + tm, :] = xr
    xc = cb_ref[...] + xr * cw_ref[CONV_WIDTH - 1:CONV_WIDTH, :]
    for j in range(CONV_WIDTH - 1):
        off = HALO - (CONV_WIDTH - 1) + j
        xc = xc + cw_ref[j:j + 1, :] * buf[off:off + tm, :]
    xcb = xc.astype(BF16)
    r = _sigmoid(_dot(xcb, wr_ref[...]) + br_ref[...])
    i = _sigmoid(_dot(xcb, wi_ref[...]) + bi_ref[...])
    log_a = (-LRU_C) * r * _softplus(-lam_ref[...])
    a = jnp.exp(log_a)
    b = jnp.sqrt(-jnp.tanh(log_a) * (a * a + 1.0)) * (i * xc)
    a_s[...] = a
    b_s[...] = b

    row = lax.broadcasted_iota(jnp.int32, (SUBLANES, a.shape[1]), 0)

    def step(g, hprev):
        r0 = pl.multiple_of(g * SUBLANES, SUBLANES)
        ag = a_s[pl.ds(r0, SUBLANES), :]
        bg = b_s[pl.ds(r0, SUBLANES), :]
        d = 1
        while d < SUBLANES:
            keep = row >= d
            a_up = jnp.where(keep, pltpu.roll(ag, d, 0), 1.0)
            b_up = jnp.where(keep, pltpu.roll(bg, d, 0), 0.0)
            bg = ag * b_up + bg
            ag = ag * a_up
            d *= 2
        hg = ag * hprev + bg
        b_s[pl.ds(r0, SUBLANES), :] = hg
        return hg[SUBLANES - 1:, :]

    hfin = lax.fori_loop(0, tm // SUBLANES, step, hcar[...], unroll=2)
    hcar[...] = hfin
    y_ref[0] = (_gelu_tanh(xg) * b_s[...]).astype(BF16)
    hl_ref[0] = hfin
    tail = buf[tm:tm + HALO, :]
    buf[0:HALO, :] = tail
    cl_ref[0] = tail


def _lru(h, w, conv_prev8, h0):
    b, t, d = h.shape
    tm = _row_tile(t, 512)
    wd = LRU_WIDTH
    vec = lambda: _const_spec((1, wd))
    return pl.pallas_call(
        functools.partial(_lru_body, tm=tm),
        grid=(b, t // tm),
        in_specs=[pl.BlockSpec((1, tm, d), lambda i, j: (i, j, 0)), _const_spec((1, d)),
                  _const_spec((d, wd)), _const_spec((d, wd)), _const_spec((CONV_WIDTH, wd)), vec(),
                  _const_spec((wd, wd)), _const_spec((wd, wd)), vec(), vec(), vec(),
                  pl.BlockSpec((1, HALO, wd), lambda i, j: (i, 0, 0)),
                  pl.BlockSpec((1, 1, wd), lambda i, j: (i, 0, 0))],
        out_specs=[pl.BlockSpec((1, tm, wd), lambda i, j: (i, j, 0)),
                   pl.BlockSpec((1, 1, wd), lambda i, j: (i, 0, 0)),
                   pl.BlockSpec((1, HALO, wd), lambda i, j: (i, 0, 0))],
        out_shape=[jax.ShapeDtypeStruct((b, t, wd), BF16), jax.ShapeDtypeStruct((b, 1, wd), F32),
                   jax.ShapeDtypeStruct((b, HALO, wd), F32)],
        scratch_shapes=[pltpu.VMEM((tm + HALO, wd), F32), pltpu.VMEM((tm, wd), F32),
                        pltpu.VMEM((tm, wd), F32), pltpu.VMEM((1, wd), F32)],
        compiler_params=_params("parallel", "arbitrary"),
        name="lru",
    )(h, w["g_mix"], w["wrec"], w["wgate"], w["conv_w"], w["conv_b"], w["wr"], w["wi"], w["br"], w["bi"],
      w["lam"], conv_prev8, h0)


def _group_rms(x, e, et, g):
    hi, lo = _split_bf16(x * x, 2)
    ss = _dot(hi, e) + _dot(lo, e)
    inv = lax.rsqrt(ss * (1.0 / FOX_HEAD_DIM) + NORM_EPS)
    ih, il = _split_bf16(inv, 2)
    return x * (_dot(ih, et) + _dot(il, et)) * g


def _cumsum_rows(x, ltri):
    out = None
    for p in _split_bf16(x, 3):
        d = _dot(ltri, p)
        out = d if out is None else out + d
    return out


def _fox_aug(x, keep, bias):
    blocks = [x[:, (hd // 2) * LANES:(hd // 2 + 1) * LANES] for hd in range(FOX_HEADS)]
    return jnp.concatenate(blocks, axis=-1) * keep + bias


def _fox_key_bias(c, place):
    parts = jnp.concatenate(_split_bf16(c * (-LOG2E), 3), axis=-1)
    return _dot(parts, place)


def _fox_in_body(h_ref, g_ref, wq_ref, wk_ref, wv_ref, wf_ref, bf_ref, gq_ref, gk_ref, e_ref, et_ref,
                 keep_ref, ones_ref, place_ref, ltri_ref, c0_ref,
                 q_ref, k32_ref, v32_ref, ka_ref, vb_ref, lf_ref, c_ref, lf_s, carry, *, tm, tc):
    @pl.when(pl.program_id(1) == 0)
    def _():
        carry[...] = c0_ref[0]

    hb = _rms(h_ref[0], g_ref[...]).astype(BF16)
    e, et, keep = e_ref[...], et_ref[...], keep_ref[...]
    q = _group_rms(_dot(hb, wq_ref[...]), e, et, gq_ref[...])
    q_ref[0] = _fox_aug(q, keep, ones_ref[...]).astype(BF16)
    k = _group_rms(_dot(hb, wk_ref[...]), e, et, gk_ref[...])
    k32_ref[0] = k
    v = _dot(hb, wv_ref[...])
    v32_ref[0] = v
    vb_ref[0] = v.astype(BF16)
    logf = -_softplus(-(_dot(hb, wf_ref[...]) + bf_ref[...]))
    lf_ref[0] = logf
    if tc > tm:
        lf_s[...] = jnp.zeros_like(lf_s)
    lf_s[0:tm, :] = logf
    c = carry[...] + _cumsum_rows(lf_s[...], ltri_ref[...])[0:tm, :]
    c_ref[0] = c
    carry[...] = c[tm - 1:tm, :]
    ka_ref[0] = _fox_aug(k, keep, _fox_key_bias(c, place_ref[...])).astype(BF16)


def _fox_in(h, w, c0):
    b, t, d = h.shape
    tm = _row_tile(t, 512)
    tc = max(tm, LANES)
    ltri = jnp.tril(jnp.ones((tc, tc), F32)).astype(BF16)
    row = lambda n: pl.BlockSpec((1, tm, n), lambda i, j: (i, j, 0))
    fw, aw = FOX_WIDTH, FOX_HEADS * LANES
    return pl.pallas_call(
        functools.partial(_fox_in_body, tm=tm, tc=tc),
        grid=(b, t // tm),
        in_specs=[row(d), _const_spec((1, d)),
                  _const_spec((d, fw)), _const_spec((d, fw)), _const_spec((d, fw)), _const_spec((d, LANES)),
                  _const_spec((1, LANES)), _const_spec((1, fw)), _const_spec((1, fw)),
                  _const_spec((fw, LANES)), _const_spec((LANES, fw)),
                  _const_spec((1, aw)), _const_spec((1, aw)), _const_spec((3 * LANES, aw)),
                  _const_spec((tc, tc)), pl.BlockSpec((1, 1, LANES), lambda i, j: (i, 0, 0))],
        out_specs=[row(aw), row(fw), row(fw), row(aw), row(fw), row(LANES), row(LANES)],
        out_shape=[jax.ShapeDtypeStruct((b, t, aw), BF16), jax.ShapeDtypeStruct((b, t, fw), F32),
                   jax.ShapeDtypeStruct((b, t, fw), F32), jax.ShapeDtypeStruct((b, t, aw), BF16),
                   jax.ShapeDtypeStruct((b, t, fw), BF16), jax.ShapeDtypeStruct((b, t, LANES), F32),
                   jax.ShapeDtypeStruct((b, t, LANES), F32)],
        scratch_shapes=[pltpu.VMEM((tc, LANES), F32), pltpu.VMEM((1, LANES), F32)],
        compiler_params=_params("parallel", "arbitrary"),
        name="fox_in",
    )(h, w["g_mix"], w["wq"], w["wk"], w["wv"], w["wf"], w["bf"], w["gq"], w["gk"], w["e"], w["et"],
      w["keep"], w["ones"], w["place"], ltri, c0)


def _fox_past_body(k_ref, lf_ref, keep_ref, place_ref, ltri_ref, ka_ref, c_ref, carry):
    @pl.when(pl.program_id(1) == 0)
    def _():
        carry[...] = jnp.zeros_like(carry)

    c = carry[...] + _cumsum_rows(lf_ref[0], ltri_ref[...])
    c_ref[0] = c
    carry[...] = c[c.shape[0] - 1:, :]
    ka_ref[0] = _fox_aug(k_ref[0], keep_ref[...], _fox_key_bias(c, place_ref[...])).astype(BF16)


def _fox_past(past_k, past_logf, w):
    b, p, n = past_logf.shape
    tc = _row_tile(p, 512)
    ltri = jnp.tril(jnp.ones((tc, tc), F32)).astype(BF16)
    fw, aw = FOX_WIDTH, FOX_HEADS * LANES
    row = lambda m: pl.BlockSpec((1, tc, m), lambda i, j: (i, j, 0))
    return pl.pallas_call(
        _fox_past_body,
        grid=(b, p // tc),
        in_specs=[row(fw), row(n), _const_spec((1, aw)), _const_spec((3 * LANES, aw)), _const_spec((tc, tc))],
        out_specs=[row(aw), row(n)],
        out_shape=[jax.ShapeDtypeStruct((b, p, aw), BF16), jax.ShapeDtypeStruct((b, p, n), F32)],
        scratch_shapes=[pltpu.VMEM((1, n), F32)],
        compiler_params=_params("parallel", "arbitrary"),
        name="fox_past",
    )(past_k, past_logf, w["keep"], w["place"], ltri)


def _flash_body(q_ref, k_ref, vt_ref, o_ref, m_s, l_s, acc_s, sa_s, sb_s, *, tq, tk, tks, n_k, past, kv_len,
                chunk_causal, diag_aligned):
    q_start = past + pl.program_id(2) * tq
    q = q_ref[0]
    m_s[...] = jnp.full_like(m_s, NEG_INF)
    l_s[...] = jnp.zeros_like(l_s)
    acc_s[...] = jnp.zeros_like(acc_s)
    shift = int(math.log2(CHUNK))
    hv = LANES // 2

    def block_kind(r0, rn, c0, cn):
        if chunk_causal:
            k_lo, k_hi, q_lo, q_hi = r0 >> shift, (r0 + rn - 1) >> shift, c0 >> shift, (c0 + cn - 1) >> shift
        else:
            k_lo, k_hi, q_lo, q_hi = r0, r0 + rn - 1, c0, c0 + cn - 1
        return "visible" if k_hi <= q_lo else ("hidden" if k_lo > q_hi else "partial")

    full_streams = [(hh, r0, 0, tq, "visible") for r0 in range(0, tk, tks) for hh in range(2)]
    if diag_aligned:
        cn = tks
        mask_streams = [(hh, r0, c0, cn, block_kind(r0, tks, c0, cn)) for r0 in range(0, tk, tks)
                        for hh in range(2) for c0 in range(0, tq, cn)]
        mask_streams = [st for st in mask_streams if st[4] != "hidden"]
    else:
        mask_streams = [(hh, r0, 0, tq, "partial") for hh, r0, _, _, _ in full_streams]

    def score(kblk, stream):
        hh, r0, c0, cn, _ = stream
        head = slice(hh * LANES, (hh + 1) * LANES)
        return _dot_nt(kblk[r0:r0 + tks, head], q[c0:c0 + cn, head])

    def key_block(kt):
        return k_ref[0, pl.ds(pl.multiple_of(jnp.minimum(kt, n_k - 1) * tk, tk), tk), :]

    def absorb(s, stream, kt):
        hh, r0, c0, cn, kind = stream
        cols = slice(c0, c0 + cn)
        k0 = pl.multiple_of(kt * tk, tk)
        if kind == "partial":
            kpos = k0 + r0 + lax.broadcasted_iota(jnp.int32, (tks, cn), 0)
            qpos = q_start + c0 + lax.broadcasted_iota(jnp.int32, (tks, cn), 1)
            if chunk_causal:
                vis = lax.shift_right_logical(kpos, shift) <= lax.shift_right_logical(qpos, shift)
            else:
                vis = kpos <= qpos
            s = jnp.where(jnp.logical_and(vis, kpos < kv_len), s, NEG_INF)
        m_old = m_s[hh, :, cols]
        m_new = jnp.maximum(m_old, jnp.max(s, axis=0, keepdims=True))
        alpha = jnp.exp2(m_old - m_new)
        p = jnp.exp2(s - m_new)
        l_s[hh, :, cols] = alpha * l_s[hh, :, cols] + jnp.sum(p, axis=0, keepdims=True)
        vt = vt_ref[0, hh * hv:(hh + 1) * hv, pl.ds(pl.multiple_of(k0 + r0, LANES), tks)]
        acc_s[hh, :, cols] = alpha * acc_s[hh, :, cols] + _dot(vt, p.astype(BF16))
        m_s[hh, :, cols] = m_new

    def single_tile(kt, streams):
        kblk = key_block(kt)
        ahead = 2
        pending = [score(kblk, st) for st in streams[:ahead]]
        for idx, st in enumerate(streams):
            s = pending.pop(0)
            if idx + ahead < len(streams):
                pending.append(score(kblk, streams[idx + ahead]))
            absorb(s, st, kt)

    def store_scores(buf, kt):
        kblk = key_block(kt)
        for hh, r0, _, _, _ in full_streams:
            buf[hh, r0:r0 + tks, :] = score(kblk, (hh, r0, 0, tq, "visible"))

    def absorb_stored(buf, kt):
        for st in full_streams:
            absorb(buf[st[0], st[1]:st[1] + tks, :], st, kt)

    def tile_pair(i, carry):
        kt = 2 * i
        store_scores(sb_s, kt + 1)
        absorb_stored(sa_s, kt)
        store_scores(sa_s, kt + 2)
        absorb_stored(sb_s, kt + 1)
        return carry

    def full_tile(kt, carry):
        single_tile(kt, full_streams)
        return carry

    def masked_tile(kt, carry):
        single_tile(kt, mask_streams)
        return carry

    n_full = jnp.minimum(q_start // tk, kv_len // tk)
    q_last = q_start + tq - 1
    k_hi = (q_last // CHUNK + 1) * CHUNK if chunk_causal else q_last + 1
    n_end = jnp.minimum((k_hi + tk - 1) // tk, n_k)
    n_pair = n_full // 2

    @pl.when(n_pair > 0)
    def _():
        store_scores(sa_s, 0)

    lax.fori_loop(0, n_pair, tile_pair, 0)
    lax.fori_loop(2 * n_pair, n_full, full_tile, 0)
    lax.fori_loop(n_full, n_end, masked_tile, 0)
    out_t = jnp.concatenate([acc_s[0] / l_s[0], acc_s[1] / l_s[1]], axis=0)
    o_ref[0] = out_t.T.astype(BF16)


def _flash(q, k, vt, *, past, kv_len, chunk_causal):
    b, t, w = q.shape
    lp = k.shape[1]
    n_pairs = w // (2 * LANES)
    t_pad = max(t, LANES)
    if t_pad > t:
        q = _pad_rows(q, t_pad)
    if past == 0 and t % 512 == 0 and lp % 512 == 0:
        tq = tk = 512
    elif t_pad == LANES:
        tq, tk = LANES, lp
    else:
        tq = tk = LANES
    assert t_pad % tq == 0 and lp % tk == 0, (t, lp, tq, tk)
    tks = 256 if tk % 256 == 0 and tq > LANES else tk
    out = pl.pallas_call(
        functools.partial(_flash_body, tq=tq, tk=tk, tks=tks, n_k=lp // tk, past=past, kv_len=kv_len,
                          chunk_causal=chunk_causal,
                          diag_aligned=(past == 0 and tq == tk and kv_len == lp and tks < tk)),
        grid=(b, n_pairs, t_pad // tq),
        in_specs=[pl.BlockSpec((1, tq, 2 * LANES), lambda i, j, s: (i, s, j)),
                  pl.BlockSpec((1, lp, 2 * LANES), lambda i, j, s: (i, 0, j)),
                  pl.BlockSpec((1, LANES, lp), lambda i, j, s: (i, j, 0))],
        out_specs=pl.BlockSpec((1, tq, LANES), lambda i, j, s: (i, s, j)),
        out_shape=jax.ShapeDtypeStruct((b, t_pad, n_pairs * LANES), BF16),
        scratch_shapes=[pltpu.VMEM((2, 1, tq), F32), pltpu.VMEM((2, 1, tq), F32),
                        pltpu.VMEM((2, LANES // 2, tq), F32),
                        pltpu.VMEM((2, tk, tq), F32), pltpu.VMEM((2, tk, tq), F32)],
        compiler_params=_params("parallel", "parallel", "arbitrary"),
        name="flash_mla" if chunk_causal else "flash_fox",
    )(q, k, vt)
    return out[:, :t]


def _row(v):
    return v.reshape(1, -1).astype(F32)


def _pad_lanes(x, lo, total):
    pad = [(0, 0)] * (x.ndim - 1) + [(lo, total - lo - x.shape[-1])]
    return jnp.pad(x, pad)


def _ffn_weights(g, w_in, w_out):
    f = w_out.shape[0]
    return dict(g=_row(g), wg=w_in[:, :f].astype(BF16), wu=w_in[:, f:].astype(BF16), wo=w_out.astype(BF16))


def _mem_weights(g, g_src, w_q, w_kv, w_o, g_q, g_k):
    d = w_q.shape[0]
    kv = w_kv.reshape(d, MEM_HEADS, 2, MEM_HEAD_DIM)
    return dict(g=_row(g), g_src=_row(g_src), wq=w_q.astype(BF16), wo=w_o.astype(BF16),
                wk=kv[:, :, 0].reshape(d, MEM_WIDTH).astype(BF16),
                wv=kv[:, :, 1].reshape(d, MEM_WIDTH).astype(BF16),
                gq=_row(g_q) * (MEM_HEAD_DIM ** -0.5), gk=_row(g_k))


def _even_weights(g_mix, w_in, g_qlat, g_kvlat, w_uq, w_ukv, g_q, g_k, conv_w, conv_b, gate_w, gate_b, lam, w_out):
    d = w_in.shape[0]
    o1 = MLA_Q_LORA
    o2 = o1 + MLA_KV_LORA
    o3 = o2 + MLA_ROPE
    o4 = o3 + LRU_WIDTH
    half = MLA_ROPE // 2
    swap_halves = lambda r: jnp.concatenate([r[..., half:], r[..., :half]], axis=-1)
    uq3 = w_uq.reshape(MLA_Q_LORA, MLA_HEADS, MLA_QK)
    uq = _pad_lanes(uq3, 0, LANES)
    uq_swap = _pad_lanes(swap_halves(uq3[:, :, MLA_NOPE:]), MLA_NOPE, LANES)
    kr = w_in[:, o2:o3]
    ukv = w_ukv.reshape(MLA_KV_LORA, MLA_HEADS, MLA_NOPE + MLA_V)
    uk = _pad_lanes(ukv[:, :, :MLA_NOPE], 0, LANES)
    blk = LRU_WIDTH // LRU_BLOCKS
    eye = jnp.eye(LRU_BLOCKS, dtype=F32)
    wr = jnp.einsum("ncd,nm->ncmd", gate_w[:, :, :blk], eye).reshape(LRU_WIDTH, LRU_WIDTH)
    wi = jnp.einsum("ncd,nm->ncmd", gate_w[:, :, blk:], eye).reshape(LRU_WIDTH, LRU_WIDTH)
    return dict(
        g_mix=_row(g_mix), wcq=w_in[:, :o1].astype(BF16), g_qlat=_row(g_qlat),
        wuq=uq.reshape(MLA_Q_LORA, MLA_HEADS * LANES).astype(BF16),
        wuq_swap=uq_swap.reshape(MLA_Q_LORA, MLA_HEADS * LANES).astype(BF16),
        gq=_pad_lanes(_row(g_q), 0, LANES) * (MLA_QK ** -0.5 * LOG2E),
        wckv=w_in[:, o1:o2].astype(BF16), g_kvlat=_row(g_kvlat),
        wkr=_pad_lanes(kr, MLA_NOPE, LANES).astype(BF16),
        wkr_swap=_pad_lanes(swap_halves(kr), MLA_NOPE, LANES).astype(BF16),
        wuk=uk.reshape(MLA_KV_LORA, MLA_HEADS * LANES).astype(BF16),
        wuvt=ukv[:, :, MLA_NOPE:].reshape(MLA_KV_LORA, MLA_HEADS * MLA_V).T.astype(BF16),
        gk=_pad_lanes(_row(g_k), 0, LANES),
        wrec=w_in[:, o3:o4].astype(BF16), wgate=w_in[:, o4:].astype(BF16),
        conv_w=conv_w.astype(F32), conv_b=_row(conv_b), wr=wr.astype(BF16), wi=wi.astype(BF16),
        br=_row(gate_b[:, :blk]), bi=_row(gate_b[:, blk:]), lam=_row(lam),
        wo_attn=w_out[:MLA_HEADS * MLA_V].astype(BF16), wo_rec=w_out[MLA_HEADS * MLA_V:].astype(BF16))


def _odd_weights(g_mix, w_in, b_f, g_q, g_k, w_out):
    fw = FOX_WIDTH
    head_of_lane = jnp.arange(fw) // FOX_HEAD_DIM
    e = (head_of_lane[:, None] == jnp.arange(LANES)[None, :]).astype(BF16)
    lane = jnp.arange(FOX_HEADS * LANES)
    hd, within = lane // LANES, lane % LANES
    own_low = hd % 2 == 0
    keep = jnp.where(own_low, within < FOX_HEAD_DIM, within >= FOX_HEAD_DIM)
    part = within - jnp.where(own_low, FOX_HEAD_DIM, 0)
    is_bias = (part >= 0) & (part < 3)
    src = part * LANES + hd
    place = ((jnp.arange(3 * LANES)[:, None] == src[None, :]) & is_bias[None, :]).astype(BF16)
    return dict(
        keep=keep.astype(F32)[None, :], ones=is_bias.astype(F32)[None, :], place=place,
        g_mix=_row(g_mix), wq=w_in[:, :fw].astype(BF16), wk=w_in[:, fw:2 * fw].astype(BF16),
        wv=w_in[:, 2 * fw:3 * fw].astype(BF16), wf=_pad_lanes(w_in[:, 3 * fw:], 0, LANES).astype(BF16),
        bf=_pad_lanes(_row(b_f), 0, LANES),
        gq=jnp.tile(_row(g_q), (1, FOX_HEADS)) * (FOX_HEAD_DIM ** -0.5 * LOG2E),
        gk=jnp.tile(_row(g_k), (1, FOX_HEADS)), e=e, et=e.T, wo=w_out.astype(BF16))


def _rope_tables(pos):
    half = MLA_ROPE // 2
    inv_freq = ROPE_THETA ** (-jnp.arange(half, dtype=F32) / half)
    ang = pos.astype(F32)[:, None] * inv_freq[None, :]
    cos, sin = jnp.cos(ang), jnp.sin(ang)
    c = jnp.concatenate([jnp.ones((pos.shape[0], MLA_NOPE), F32), cos, cos,
                         jnp.ones((pos.shape[0], LANES - MLA_QK), F32)], axis=-1)
    s = _pad_lanes(jnp.concatenate([-sin, sin], axis=-1), MLA_NOPE, LANES)
    return c, s


def _pad_rows(x, total):
    return jnp.pad(x, [(0, 0), (0, total - x.shape[1])] + [(0, 0)] * (x.ndim - 2))


def _kv_pad_len(t, past):
    l = past + t
    if past == 0 and t % 512 == 0:
        return l
    return -(-l // LANES) * LANES


def _even_layer(h, past, w, state):
    b, t, d = h.shape
    past_lat, past_krope, h0, conv_prev = state
    q, lat_new, krp_new = _mla_in(h, w, _rope_tables(past + jnp.arange(t)))
    lp = _kv_pad_len(t, past)
    lat_all = _pad_rows(jnp.concatenate([past_lat, lat_new], axis=1), lp)
    krp_all = _pad_rows(jnp.concatenate([_pad_lanes(past_krope, MLA_NOPE, LANES), krp_new], axis=1), lp)
    k, vt = _mla_kv(lat_all, krp_all, w)
    attn = _flash(q, k, vt, past=past, kv_len=past + t, chunk_causal=True)
    conv_prev8 = jnp.pad(conv_prev, ((0, 0), (HALO - (CONV_WIDTH - 1), 0), (0, 0)))
    y_rec, h_last, conv_last = _lru(h, w, conv_prev8, h0[:, None, :])
    new = (lat_new, krp_new[:, :, MLA_NOPE:MLA_QK], h_last[:, 0], conv_last[:, HALO - (CONV_WIDTH - 1):])
    return [attn, y_rec], [w["wo_attn"], w["wo_rec"]], new


def _odd_layer(h, past, w, state):
    b, t, d = h.shape
    past_k, past_v, past_logf = state
    if past > 0:
        ka_past, c_past = _fox_past(past_k.reshape(b, past, FOX_WIDTH),
                                    _pad_lanes(past_logf.astype(F32), 0, LANES), w)
        c0 = c_past[:, past - 1:past, :]
    else:
        ka_past = jnp.zeros((b, 0, FOX_HEADS * LANES), BF16)
        c0 = jnp.zeros((b, 1, LANES), F32)
    q, k32, v32, ka_new, vb, logf, _ = _fox_in(h, w, c0)
    lp = _kv_pad_len(t, past)
    k_all = _pad_rows(jnp.concatenate([ka_past, ka_new], axis=1), lp)
    v_all = _pad_rows(jnp.concatenate([past_v.reshape(b, past, FOX_WIDTH).astype(BF16), vb], axis=1), lp)
    attn = _flash(q, k_all, jnp.swapaxes(v_all, 1, 2), past=past, kv_len=past + t, chunk_causal=False)
    new = (k32.reshape(b, t, FOX_HEADS, FOX_HEAD_DIM), v32.reshape(b, t, FOX_HEADS, FOX_HEAD_DIM),
           logf[:, :, :FOX_HEADS])
    return [attn], [w["wo"]], new


def _trunk(x, past, layers, mem_kvs, even_states, odd_states):
    b, t, d = x.shape
    even_new, odd_new = [], []
    for li, lw in enumerate(layers):
        h = _ffn(x.reshape(b * t, d), lw["ffn1"]).reshape(b, t, d)
        if li % 2 == 0:
            parts, w_parts, new = _even_layer(h, past, lw["mix"], even_states[li // 2])
            even_new.append(new)
        else:
            parts, w_parts, new = _odd_layer(h, past, lw["mix"], odd_states[li // 2])
            odd_new.append(new)
        x = _post_mixer(h, parts, w_parts, mem_kvs[li][0], mem_kvs[li][1], lw["mem"], lw["ffn2"])
    return x, even_new, odd_new


def kernel(x_prompt, x_sample, mem_prompt, cache_mla_latent, cache_mla_krope, state_lru_h, state_lru_conv, cache_fox_k, cache_fox_v, cache_fox_logf, cache_mem_k, cache_mem_v, norm_ffn1, ffn1_w_in, ffn1_w_out, norm_mix, norm_mem, norm_mem_src, mem_w_q, mem_w_kv, mem_w_o, mem_g_q, mem_g_k, norm_ffn2, ffn2_w_in, ffn2_w_out, ev_w_in, ev_g_qlat, ev_g_kvlat, ev_w_uq, ev_w_ukv, ev_g_q, ev_g_k, ev_conv_w, ev_conv_b, ev_gate_w, ev_gate_b, ev_lambda, ev_w_out, od_w_in, od_b_f, od_g_q, od_g_k, od_w_out):
    depth = norm_ffn1.shape[0]
    n_even, n_odd = (depth + 1) // 2, depth // 2
    b, _, _ = x_prompt.shape
    bs = x_sample.shape[0]
    past = cache_mla_latent.shape[2] if n_even else cache_fox_k.shape[2]

    layers = []
    for li in range(depth):
        j = li // 2
        if li % 2 == 0:
            mix = _even_weights(norm_mix[li], ev_w_in[j], ev_g_qlat[j], ev_g_kvlat[j], ev_w_uq[j], ev_w_ukv[j],
                                ev_g_q[j], ev_g_k[j], ev_conv_w[j], ev_conv_b[j], ev_gate_w[j], ev_gate_b[j],
                                ev_lambda[j], ev_w_out[j])
        else:
            mix = _odd_weights(norm_mix[li], od_w_in[j], od_b_f[j], od_g_q[j], od_g_k[j], od_w_out[j])
        layers.append(dict(
            ffn1=_ffn_weights(norm_ffn1[li], ffn1_w_in[li], ffn1_w_out[li]),
            ffn2=_ffn_weights(norm_ffn2[li], ffn2_w_in[li], ffn2_w_out[li]),
            mem=_mem_weights(norm_mem[li], norm_mem_src[li], mem_w_q[li], mem_w_kv[li], mem_w_o[li],
                             mem_g_q[li], mem_g_k[li]),
            mix=mix))

    mem_p = [_mem_kv(mem_prompt, lw["mem"]) for lw in layers]
    ev0 = [(jnp.zeros((b, 0, MLA_KV_LORA), F32), jnp.zeros((b, 0, MLA_ROPE), F32),
            jnp.zeros((b, LRU_WIDTH), F32), jnp.zeros((b, CONV_WIDTH - 1, LRU_WIDTH), F32))
           for _ in range(n_even)]
    od0 = [(jnp.zeros((b, 0, FOX_HEADS, FOX_HEAD_DIM), F32), jnp.zeros((b, 0, FOX_HEADS, FOX_HEAD_DIM), F32),
            jnp.zeros((b, 0, FOX_HEADS), F32)) for _ in range(n_odd)]
    y_prompt, ev_p, od_p = _trunk(x_prompt, 0, layers, [(m[2], m[3]) for m in mem_p], ev0, od0)

    m_tok = cache_mem_k.shape[2]
    mem_s = [(cache_mem_k[li].reshape(bs, m_tok, MEM_WIDTH).astype(BF16),
              cache_mem_v[li].reshape(bs, m_tok, MEM_WIDTH).astype(BF16)) for li in range(depth)]
    ev_s = [(cache_mla_latent[j], cache_mla_krope[j], state_lru_h[j], state_lru_conv[j]) for j in range(n_even)]
    od_s = [(cache_fox_k[j], cache_fox_v[j], cache_fox_logf[j]) for j in range(n_odd)]
    y_sample, ev_n, od_n = _trunk(x_sample, past, layers, mem_s, ev_s, od_s)

    mem_shape = (b, m_tok, MEM_HEADS, MEM_HEAD_DIM)
    p_even = [jnp.stack([s[f] for s in ev_p]) for f in range(4)]
    p_odd = [jnp.stack([s[f] for s in od_p]) for f in range(3)]
    p_mem_k = jnp.stack([m[0].reshape(mem_shape) for m in mem_p])
    p_mem_v = jnp.stack([m[1].reshape(mem_shape) for m in mem_p])
    s_even = [jnp.stack([s[f] for s in ev_n]) for f in range(4)]
    s_odd = [jnp.stack([s[f] for s in od_n]) for f in range(3)]
    return (y_prompt, y_sample, *p_even, *p_odd, p_mem_k, p_mem_v, *s_even, *s_odd)
```

```python
import functools
import math

import jax
import jax.numpy as jnp
from jax import lax
from jax.experimental import pallas as pl
from jax.experimental.pallas import tpu as pltpu

F32 = jnp.float32
BF16 = jnp.bfloat16

NORM_EPS = 1e-6
NEG_INF = -1e30
LOG2E = math.log2(math.e)
CHUNK = 64
LANES = 128
SUBLANES = 8
ROPE_THETA = 10000.0
LRU_C = 8.0
MLA_HEADS = 8
MLA_NOPE = 64
MLA_ROPE = 32
MLA_QK = MLA_NOPE + MLA_ROPE
MLA_V = 64
MLA_Q_LORA = 256
MLA_KV_LORA = 128
LRU_WIDTH = 512
LRU_BLOCKS = 8
CONV_WIDTH = 4
FOX_HEADS = 16
FOX_HEAD_DIM = 64
FOX_WIDTH = FOX_HEADS * FOX_HEAD_DIM
MEM_HEADS = 4
MEM_HEAD_DIM = 128
MEM_WIDTH = MEM_HEADS * MEM_HEAD_DIM
HALO = 8

VMEM_LIMIT = 56 * 1024 * 1024


def _dot(a, b):
    return jnp.dot(a, b, preferred_element_type=F32)


def _dot_nt(a, b):
    return lax.dot_general(a, b, (((1,), (1,)), ((), ())), preferred_element_type=F32)


def _rms(x, g):
    return x * lax.rsqrt(jnp.mean(x * x, axis=-1, keepdims=True) + NORM_EPS) * g


def _head_rms(x, g, n_live):
    ss = jnp.sum(x * x, axis=-1, keepdims=True) * (1.0 / n_live)
    return x * lax.rsqrt(ss + NORM_EPS) * g


def _sigmoid(x):
    return 1.0 / (1.0 + jnp.exp(-x))


def _log1p(y):
    u = 1.0 + y
    d = u - 1.0
    return jnp.where(d == 0.0, y, jnp.log(u) * (y / jnp.where(d == 0.0, 1.0, d)))


def _softplus(x):
    return jnp.maximum(x, 0.0) + _log1p(jnp.exp(-jnp.abs(x)))


def _gelu_tanh(x):
    return 0.5 * x * (1.0 + jnp.tanh(math.sqrt(2.0 / math.pi) * (x + 0.044715 * (x * x * x))))


def _split_bf16(x, parts):
    out = []
    r = x
    for _ in range(parts):
        p = r.astype(BF16)
        out.append(p)
        r = r - p.astype(F32)
    return out


def _const_spec(shape):
    nd = len(shape)
    return pl.BlockSpec(shape, lambda *_: (0,) * nd, pipeline_mode=pl.Buffered(1))


def _params(*sem):
    return pltpu.CompilerParams(dimension_semantics=sem, vmem_limit_bytes=VMEM_LIMIT)


def _row_tile(n, cap):
    t = min(n, cap)
    assert n % t == 0, (n, t)
    return t


FFN_CHUNKS = 2


def _swiglu_half_step(x, g_ref, wg_ref, wu_ref, wo_ref):
    hb = _rms(x, g_ref[...]).astype(BF16)
    fc = wg_ref.shape[1] // FFN_CHUNKS
    acc = jnp.zeros_like(x)
    for c in range(FFN_CHUNKS):
        sl = slice(c * fc, (c + 1) * fc)
        gate = _dot(hb, wg_ref[:, sl])
        up = _dot(hb, wu_ref[:, sl])
        act = (gate * _sigmoid(gate) * up).astype(BF16)
        acc = acc + _dot(act, wo_ref[sl, :])
    return x + 0.5 * acc


def _ffn_body(x_ref, g_ref, wg_ref, wu_ref, wo_ref, o_ref):
    o_ref[...] = _swiglu_half_step(x_ref[...], g_ref, wg_ref, wu_ref, wo_ref)


def _ffn(x2, w):
    n, d = x2.shape
    f = w["wg"].shape[1]
    tm = _row_tile(n, 512)
    return pl.pallas_call(
        _ffn_body,
        grid=(n // tm,),
        in_specs=[pl.BlockSpec((tm, d), lambda i: (i, 0)), _const_spec((1, d)),
                  _const_spec((d, f)), _const_spec((d, f)), _const_spec((f, d))],
        out_specs=pl.BlockSpec((tm, d), lambda i: (i, 0)),
        out_shape=jax.ShapeDtypeStruct((n, d), F32),
        compiler_params=_params("parallel"),
        name="ffn",
    )(x2, w["g"], w["wg"], w["wu"], w["wo"])


def _mem_cross_attention(h, tm, g_ref, wq_ref, gq_ref, mk_ref, mv_ref, wo_ref):
    hb = _rms(h, g_ref[...]).astype(BF16)
    q = _dot(hb, wq_ref[...])
    rows = []
    for i in range(h.shape[0] // tm):
        outs = []
        for hd in range(MEM_HEADS):
            sl = slice(hd * MEM_HEAD_DIM, (hd + 1) * MEM_HEAD_DIM)
            qh = _head_rms(q[i * tm:(i + 1) * tm, sl], gq_ref[...], MEM_HEAD_DIM).astype(BF16)
            s = _dot_nt(qh, mk_ref[i, :, sl])
            e = jnp.exp(s - jnp.max(s, axis=-1, keepdims=True))
            p = e / jnp.sum(e, axis=-1, keepdims=True)
            outs.append(_dot(p.astype(BF16), mv_ref[i, :, sl]).astype(BF16))
        rows.append(jnp.concatenate(outs, axis=-1))
    o = rows[0] if len(rows) == 1 else jnp.concatenate(rows, axis=0)
    return h + _dot(o, wo_ref[...])


def _post_mixer_body(*refs, n_parts):
    h_ref = refs[0]
    parts = refs[1:1 + n_parts]
    w_parts = refs[1 + n_parts:1 + 2 * n_parts]
    (mg_ref, mwq_ref, mgq_ref, mk_ref, mv_ref, mwo_ref,
     fg_ref, fwg_ref, fwu_ref, fwo_ref, o_ref) = refs[1 + 2 * n_parts:]
    bb, tm, d = h_ref.shape
    h = h_ref[...].reshape(bb * tm, d)
    for p_ref, w_ref in zip(parts, w_parts):
        h = h + _dot(p_ref[...].reshape(bb * tm, p_ref.shape[2]), w_ref[...])
    h = _mem_cross_attention(h, tm, mg_ref, mwq_ref, mgq_ref, mk_ref, mv_ref, mwo_ref)
    o_ref[...] = _swiglu_half_step(h, fg_ref, fwg_ref, fwu_ref, fwo_ref).reshape(bb, tm, d)


def _post_mixer(h, parts, w_parts, mk, mv, wm, wf):
    b, t, d = h.shape
    m = mk.shape[1]
    f = wf["wg"].shape[1]
    tm = _row_tile(t, 512)
    bb = b if b * t <= 512 and tm % SUBLANES == 0 else 1
    row = lambda n: pl.BlockSpec((bb, tm, n), lambda i, j: (i, j, 0))
    mem = lambda: pl.BlockSpec((bb, m, MEM_WIDTH), lambda i, j: (i, 0, 0))
    in_specs = [row(d)] + [row(p.shape[2]) for p in parts] + [_const_spec(w.shape) for w in w_parts]
    in_specs += [_const_spec((1, d)), _const_spec((d, MEM_WIDTH)), _const_spec((1, MEM_HEAD_DIM)), mem(), mem(),
                 _const_spec((MEM_WIDTH, d)),
                 _const_spec((1, d)), _const_spec((d, f)), _const_spec((d, f)), _const_spec((f, d))]
    return pl.pallas_call(
        functools.partial(_post_mixer_body, n_parts=len(parts)),
        grid=(b // bb, t // tm),
        in_specs=in_specs,
        out_specs=row(d),
        out_shape=jax.ShapeDtypeStruct((b, t, d), F32),
        compiler_params=_params("parallel", "parallel"),
        name="post_mixer",
    )(h, *parts, *w_parts, wm["g"], wm["wq"], wm["gq"], mk, mv, wm["wo"],
      wf["g"], wf["wg"], wf["wu"], wf["wo"])


def _mem_kv_body(m_ref, g_ref, wk_ref, wv_ref, gk_ref, k32_ref, v32_ref, kb_ref, vb_ref):
    hb = _rms(m_ref[0], g_ref[...]).astype(BF16)
    k = _dot(hb, wk_ref[...])
    v = _dot(hb, wv_ref[...])
    for hd in range(MEM_HEADS):
        sl = slice(hd * MEM_HEAD_DIM, (hd + 1) * MEM_HEAD_DIM)
        kh = _head_rms(k[:, sl], gk_ref[...], MEM_HEAD_DIM)
        k32_ref[0, :, sl] = kh
        kb_ref[0, :, sl] = kh.astype(BF16)
    v32_ref[0] = v
    vb_ref[0] = v.astype(BF16)


def _mem_kv(mem, w):
    b, m, d = mem.shape
    blk = lambda: pl.BlockSpec((1, m, MEM_WIDTH), lambda i: (i, 0, 0))
    return pl.pallas_call(
        _mem_kv_body,
        grid=(b,),
        in_specs=[pl.BlockSpec((1, m, d), lambda i: (i, 0, 0)), _const_spec((1, d)),
                  _const_spec((d, MEM_WIDTH)), _const_spec((d, MEM_WIDTH)), _const_spec((1, MEM_HEAD_DIM))],
        out_specs=[blk(), blk(), blk(), blk()],
        out_shape=[jax.ShapeDtypeStruct((b, m, MEM_WIDTH), F32), jax.ShapeDtypeStruct((b, m, MEM_WIDTH), F32),
                   jax.ShapeDtypeStruct((b, m, MEM_WIDTH), BF16), jax.ShapeDtypeStruct((b, m, MEM_WIDTH), BF16)],
        compiler_params=_params("parallel"),
        name="mem_kv",
    )(mem, w["g_src"], w["wk"], w["wv"], w["gk"])


def _mla_in_body(h_ref, g_ref, wcq_ref, gql_ref, wuq_ref, wuqs_ref, gq_ref, wckv_ref, gkv_ref, wkr_ref, wkrs_ref,
                 c_ref, s_ref, q_ref, lat_ref, krp_ref):
    hb = _rms(h_ref[0], g_ref[...]).astype(BF16)
    c, s = c_ref[...], s_ref[...]
    cq = _rms(_dot(hb, wcq_ref[...]), gql_ref[...]).astype(BF16)
    q = _dot(cq, wuq_ref[...])
    q_partner = _dot(cq, wuqs_ref[...])
    for hd in range(MLA_HEADS):
        sl = slice(hd * LANES, (hd + 1) * LANES)
        qh = q[:, sl] * c + q_partner[:, sl] * s
        q_ref[0, :, sl] = _head_rms(qh, gq_ref[...], MLA_QK).astype(BF16)
    lat_ref[0] = _rms(_dot(hb, wckv_ref[...]), gkv_ref[...])
    krp_ref[0] = _dot(hb, wkr_ref[...]) * c + _dot(hb, wkrs_ref[...]) * s


def _mla_in(h, w, tables):
    b, t, d = h.shape
    tm = _row_tile(t, 512)
    row = lambda n: pl.BlockSpec((1, tm, n), lambda i, j: (i, j, 0))
    tab = lambda: pl.BlockSpec((tm, LANES), lambda i, j: (j, 0))
    return pl.pallas_call(
        _mla_in_body,
        grid=(b, t // tm),
        in_specs=[row(d), _const_spec((1, d)),
                  _const_spec((d, MLA_Q_LORA)), _const_spec((1, MLA_Q_LORA)),
                  _const_spec((MLA_Q_LORA, MLA_HEADS * LANES)), _const_spec((MLA_Q_LORA, MLA_HEADS * LANES)),
                  _const_spec((1, LANES)),
                  _const_spec((d, MLA_KV_LORA)), _const_spec((1, MLA_KV_LORA)),
                  _const_spec((d, LANES)), _const_spec((d, LANES)),
                  tab(), tab()],
        out_specs=[row(MLA_HEADS * LANES), row(MLA_KV_LORA), row(LANES)],
        out_shape=[jax.ShapeDtypeStruct((b, t, MLA_HEADS * LANES), BF16),
                   jax.ShapeDtypeStruct((b, t, MLA_KV_LORA), F32),
                   jax.ShapeDtypeStruct((b, t, LANES), F32)],
        compiler_params=_params("parallel", "parallel"),
        name="mla_in",
    )(h, w["g_mix"], w["wcq"], w["g_qlat"], w["wuq"], w["wuq_swap"], w["gq"], w["wckv"], w["g_kvlat"],
      w["wkr"], w["wkr_swap"], *tables)


def _mla_kv_body(lat_ref, krp_ref, wuk_ref, wuvt_ref, gk_ref, k_ref, vt_ref):
    lb = lat_ref[0].astype(BF16)
    krp = krp_ref[0]
    kn = _dot(lb, wuk_ref[...])
    for hd in range(MLA_HEADS):
        sl = slice(hd * LANES, (hd + 1) * LANES)
        k_ref[0, :, sl] = _head_rms(kn[:, sl] + krp, gk_ref[...], MLA_QK).astype(BF16)
    vt_ref[0] = _dot_nt(wuvt_ref[...], lb).astype(BF16)


def _mla_kv(lat, krp, w):
    b, l, _ = lat.shape
    tl = _row_tile(l, 512) if l % 512 == 0 else l
    row = lambda n: pl.BlockSpec((1, tl, n), lambda i, j: (i, j, 0))
    vw = MLA_HEADS * MLA_V
    return pl.pallas_call(
        _mla_kv_body,
        grid=(b, l // tl),
        in_specs=[row(MLA_KV_LORA), row(LANES), _const_spec((MLA_KV_LORA, MLA_HEADS * LANES)),
                  _const_spec((vw, MLA_KV_LORA)), _const_spec((1, LANES))],
        out_specs=[row(MLA_HEADS * LANES), pl.BlockSpec((1, vw, tl), lambda i, j: (i, 0, j))],
        out_shape=[jax.ShapeDtypeStruct((b, l, MLA_HEADS * LANES), BF16),
                   jax.ShapeDtypeStruct((b, vw, l), BF16)],
        compiler_params=_params("parallel", "parallel"),
        name="mla_kv",
    )(lat, krp, w["wuk"], w["wuvt"], w["gk"])


def _lru_body(h_ref, g_ref, wrec_ref, wgate_ref, cw_ref, cb_ref, wr_ref, wi_ref, br_ref, bi_ref, lam_ref,
              cprev_ref, h0_ref, y_ref, hl_ref, cl_ref, buf, a_s, b_s, hcar, *, tm):
    @pl.when(pl.program_id(1) == 0)
    def _():
        buf[0:HALO, :] = cprev_ref[0]
        hcar[...] = h0_ref[0]

    hb = _rms(h_ref[0], g_ref[...]).astype(BF16)
    xr = _dot(hb, wrec_ref[...])
    xg = _dot(hb, wgate_ref[...])
    buf[HALO:HALO + tm, :] = xr
    xc = cb_ref[...] + xr * cw_ref[CONV_WIDTH - 1:CONV_WIDTH, :]
    for j in range(CONV_WIDTH - 1):
        off = HALO - (CONV_WIDTH - 1) + j
        xc = xc + cw_ref[j:j + 1, :] * buf[off:off + tm, :]
    xcb = xc.astype(BF16)
    r = _sigmoid(_dot(xcb, wr_ref[...]) + br_ref[...])
    i = _sigmoid(_dot(xcb, wi_ref[...]) + bi_ref[...])
    log_a = (-LRU_C) * r * _softplus(-lam_ref[...])
    a = jnp.exp(log_a)
    b = jnp.sqrt(-jnp.tanh(log_a) * (a * a + 1.0)) * (i * xc)
    a_s[...] = a
    b_s[...] = b

    row = lax.broadcasted_iota(jnp.int32, (SUBLANES, a.shape[1]), 0)

    def step(g, hprev):
        r0 = pl.multiple_of(g * SUBLANES, SUBLANES)
        ag = a_s[pl.ds(r0, SUBLANES), :]
        bg = b_s[pl.ds(r0, SUBLANES), :]
        d = 1
        while d < SUBLANES:
            keep = row >= d
            a_up = jnp.where(keep, pltpu.roll(ag, d, 0), 1.0)
            b_up = jnp.where(keep, pltpu.roll(bg, d, 0), 0.0)
            bg = ag * b_up + bg
            ag = ag * a_up
            d *= 2
        hg = ag * hprev + bg
        b_s[pl.ds(r0, SUBLANES), :] = hg
        return hg[SUBLANES - 1:, :]

    hfin = lax.fori_loop(0, tm // SUBLANES, step, hcar[...], unroll=2)
    hcar[...] = hfin
    y_ref[0] = (_gelu_tanh(xg) * b_s[...]).astype(BF16)
    hl_ref[0] = hfin
    tail = buf[tm:tm + HALO, :]
    buf[0:HALO, :] = tail
    cl_ref[0] = tail


def _lru(h, w, conv_prev8, h0):
    b, t, d = h.shape
    tm = _row_tile(t, 512)
    wd = LRU_WIDTH
    vec = lambda: _const_spec((1, wd))
    return pl.pallas_call(
        functools.partial(_lru_body, tm=tm),
        grid=(b, t // tm),
        in_specs=[pl.BlockSpec((1, tm, d), lambda i, j: (i, j, 0)), _const_spec((1, d)),
                  _const_spec((d, wd)), _const_spec((d, wd)), _const_spec((CONV_WIDTH, wd)), vec(),
                  _const_spec((wd, wd)), _const_spec((wd, wd)), vec(), vec(), vec(),
                  pl.BlockSpec((1, HALO, wd), lambda i, j: (i, 0, 0)),
                  pl.BlockSpec((1, 1, wd), lambda i, j: (i, 0, 0))],
        out_specs=[pl.BlockSpec((1, tm, wd), lambda i, j: (i, j, 0)),
                   pl.BlockSpec((1, 1, wd), lambda i, j: (i, 0, 0)),
                   pl.BlockSpec((1, HALO, wd), lambda i, j: (i, 0, 0))],
        out_shape=[jax.ShapeDtypeStruct((b, t, wd), BF16), jax.ShapeDtypeStruct((b, 1, wd), F32),
                   jax.ShapeDtypeStruct((b, HALO, wd), F32)],
        scratch_shapes=[pltpu.VMEM((tm + HALO, wd), F32), pltpu.VMEM((tm, wd), F32),
                        pltpu.VMEM((tm, wd), F32), pltpu.VMEM((1, wd), F32)],
        compiler_params=_params("parallel", "arbitrary"),
        name="lru",
    )(h, w["g_mix"], w["wrec"], w["wgate"], w["conv_w"], w["conv_b"], w["wr"], w["wi"], w["br"], w["bi"],
      w["lam"], conv_prev8, h0)


def _group_rms(x, e, et, g):
    hi, lo = _split_bf16(x * x, 2)
    ss = _dot(hi, e) + _dot(lo, e)
    inv = lax.rsqrt(ss * (1.0 / FOX_HEAD_DIM) + NORM_EPS)
    ih, il = _split_bf16(inv, 2)
    return x * (_dot(ih, et) + _dot(il, et)) * g


def _cumsum_rows(x, ltri):
    out = None
    for p in _split_bf16(x, 3):
        d = _dot(ltri, p)
        out = d if out is None else out + d
    return out


def _fox_aug(x, keep, bias):
    blocks = [x[:, (hd // 2) * LANES:(hd // 2 + 1) * LANES] for hd in range(FOX_HEADS)]
    return jnp.concatenate(blocks, axis=-1) * keep + bias


def _fox_key_bias(c, place):
    lane = lax.broadcasted_iota(jnp.int32, (1, LANES), 1)
    hi, mid, lo = _split_bf16(jnp.where(lane < FOX_HEADS, c * (-LOG2E), 0.0), 3)
    packed = (hi.astype(F32) + pltpu.roll(mid.astype(F32), FOX_HEADS, 1)
              + pltpu.roll(lo.astype(F32), 2 * FOX_HEADS, 1))
    return _dot(packed.astype(BF16), place)


def _fox_in_body(h_ref, g_ref, wq_ref, wk_ref, wv_ref, wf_ref, bf_ref, gq_ref, gk_ref, e_ref, et_ref,
                 keep_ref, ones_ref, place_ref, ltri_ref, c0_ref,
                 q_ref, k32_ref, v32_ref, ka_ref, vb_ref, lf_ref, c_ref, lf_s, carry, *, tm, tc):
    @pl.when(pl.program_id(1) == 0)
    def _():
        carry[...] = c0_ref[0]

    hb = _rms(h_ref[0], g_ref[...]).astype(BF16)
    e, et, keep = e_ref[...], et_ref[...], keep_ref[...]
    q = _group_rms(_dot(hb, wq_ref[...]), e, et, gq_ref[...])
    q_ref[0] = _fox_aug(q, keep, ones_ref[...]).astype(BF16)
    k = _group_rms(_dot(hb, wk_ref[...]), e, et, gk_ref[...])
    k32_ref[0] = k
    v = _dot(hb, wv_ref[...])
    v32_ref[0] = v
    vb_ref[0] = v.astype(BF16)
    logf = -_softplus(-(_dot(hb, wf_ref[...]) + bf_ref[...]))
    lf_ref[0] = logf
    if tc > tm:
        lf_s[...] = jnp.zeros_like(lf_s)
    lf_s[0:tm, :] = logf
    c = carry[...] + _cumsum_rows(lf_s[...], ltri_ref[...])[0:tm, :]
    c_ref[0] = c
    carry[...] = c[tm - 1:tm, :]
    ka_ref[0] = _fox_aug(k, keep, _fox_key_bias(c, place_ref[...])).astype(BF16)


def _fox_in(h, w, c0):
    b, t, d = h.shape
    tm = _row_tile(t, 512)
    tc = max(tm, LANES)
    ltri = jnp.tril(jnp.ones((tc, tc), F32)).astype(BF16)
    row = lambda n: pl.BlockSpec((1, tm, n), lambda i, j: (i, j, 0))
    fw, aw = FOX_WIDTH, FOX_HEADS * LANES
    return pl.pallas_call(
        functools.partial(_fox_in_body, tm=tm, tc=tc),
        grid=(b, t // tm),
        in_specs=[row(d), _const_spec((1, d)),
                  _const_spec((d, fw)), _const_spec((d, fw)), _const_spec((d, fw)), _const_spec((d, LANES)),
                  _const_spec((1, LANES)), _const_spec((1, fw)), _const_spec((1, fw)),
                  _const_spec((fw, LANES)), _const_spec((LANES, fw)),
                  _const_spec((1, aw)), _const_spec((1, aw)), _const_spec((LANES, aw)),
                  _const_spec((tc, tc)), pl.BlockSpec((1, 1, LANES), lambda i, j: (i, 0, 0))],
        out_specs=[row(aw), row(fw), row(fw), row(aw), row(fw), row(LANES), row(LANES)],
        out_shape=[jax.ShapeDtypeStruct((b, t, aw), BF16), jax.ShapeDtypeStruct((b, t, fw), F32),
                   jax.ShapeDtypeStruct((b, t, fw), F32), jax.ShapeDtypeStruct((b, t, aw), BF16),
                   jax.ShapeDtypeStruct((b, t, fw), BF16), jax.ShapeDtypeStruct((b, t, LANES), F32),
                   jax.ShapeDtypeStruct((b, t, LANES), F32)],
        scratch_shapes=[pltpu.VMEM((tc, LANES), F32), pltpu.VMEM((1, LANES), F32)],
        compiler_params=_params("parallel", "arbitrary"),
        name="fox_in",
    )(h, w["g_mix"], w["wq"], w["wk"], w["wv"], w["wf"], w["bf"], w["gq"], w["gk"], w["e"], w["et"],
      w["keep"], w["ones"], w["place"], ltri, c0)


def _fox_past_body(k_ref, lf_ref, keep_ref, place_ref, ltri_ref, ka_ref, c_ref, carry):
    @pl.when(pl.program_id(1) == 0)
    def _():
        carry[...] = jnp.zeros_like(carry)

    c = carry[...] + _cumsum_rows(lf_ref[0], ltri_ref[...])
    c_ref[0] = c
    carry[...] = c[c.shape[0] - 1:, :]
    ka_ref[0] = _fox_aug(k_ref[0], keep_ref[...], _fox_key_bias(c, place_ref[...])).astype(BF16)


def _fox_past(past_k, past_logf, w):
    b, p, n = past_logf.shape
    tc = _row_tile(p, 512)
    ltri = jnp.tril(jnp.ones((tc, tc), F32)).astype(BF16)
    fw, aw = FOX_WIDTH, FOX_HEADS * LANES
    row = lambda m: pl.BlockSpec((1, tc, m), lambda i, j: (i, j, 0))
    return pl.pallas_call(
        _fox_past_body,
        grid=(b, p // tc),
        in_specs=[row(fw), row(n), _const_spec((1, aw)), _const_spec((LANES, aw)), _const_spec((tc, tc))],
        out_specs=[row(aw), row(n)],
        out_shape=[jax.ShapeDtypeStruct((b, p, aw), BF16), jax.ShapeDtypeStruct((b, p, n), F32)],
        scratch_shapes=[pltpu.VMEM((1, n), F32)],
        compiler_params=_params("parallel", "arbitrary"),
        name="fox_past",
    )(past_k, past_logf, w["keep"], w["place"], ltri)


def _flash_body(q_ref, k_ref, vt_ref, o_ref, m_s, l_s, acc_s, sa_s, sb_s, *, tq, tqs, tk, tks, n_k, past, kv_len,
                chunk_causal, diag_aligned):
    q_start = past + pl.program_id(2) * tq
    q = q_ref[0]
    m_s[...] = jnp.full_like(m_s, NEG_INF)
    l_s[...] = jnp.zeros_like(l_s)
    acc_s[...] = jnp.zeros_like(acc_s)
    shift = int(math.log2(CHUNK))
    hv = LANES // 2

    def block_kind(r0, rn, c0, cn):
        if chunk_causal:
            k_lo, k_hi, q_lo, q_hi = r0 >> shift, (r0 + rn - 1) >> shift, c0 >> shift, (c0 + cn - 1) >> shift
        else:
            k_lo, k_hi, q_lo, q_hi = r0, r0 + rn - 1, c0, c0 + cn - 1
        return "visible" if k_hi <= q_lo else ("hidden" if k_lo > q_hi else "partial")

    def diag_streams(d):
        out = []
        for r0 in range(0, tk, tks):
            for hh in range(2):
                for c0 in range(0, tq, tqs):
                    halves = [(c, block_kind(d * tk + r0, tks, c, tks)) for c in range(c0, c0 + tqs, tks)]
                    if all(kind == "visible" for _, kind in halves):
                        out.append((hh, r0, c0, tqs, "visible"))
                    else:
                        out += [(hh, r0, c, tks, kind) for c, kind in halves if kind != "hidden"]
        return out

    full_streams = [(hh, r0, c0, tqs, "visible") for r0 in range(0, tk, tks) for hh in range(2)
                    for c0 in range(0, tq, tqs)]
    mask_streams = [st[:4] + ("partial",) for st in full_streams]

    def score(kblk, stream):
        hh, r0, c0, cn, _ = stream
        head = slice(hh * LANES, (hh + 1) * LANES)
        return _dot_nt(kblk[r0:r0 + tks, head], q[c0:c0 + cn, head])

    def key_block(kt):
        return k_ref[0, pl.ds(pl.multiple_of(jnp.minimum(kt, n_k - 1) * tk, tk), tk), :]

    def absorb(s, stream, kt):
        hh, r0, c0, cn, kind = stream
        cols = slice(c0, c0 + cn)
        k0 = pl.multiple_of(kt * tk, tk)
        if kind == "partial":
            kpos = k0 + r0 + lax.broadcasted_iota(jnp.int32, (tks, cn), 0)
            qpos = q_start + c0 + lax.broadcasted_iota(jnp.int32, (tks, cn), 1)
            if chunk_causal:
                vis = lax.shift_right_logical(kpos, shift) <= lax.shift_right_logical(qpos, shift)
            else:
                vis = kpos <= qpos
            s = jnp.where(jnp.logical_and(vis, kpos < kv_len), s, NEG_INF)
        m_old = m_s[hh, :, cols]
        m_new = jnp.maximum(m_old, jnp.max(s, axis=0, keepdims=True))
        alpha = jnp.exp2(m_old - m_new)
        p = jnp.exp2(s - m_new)
        l_s[hh, :, cols] = alpha * l_s[hh, :, cols] + jnp.sum(p, axis=0, keepdims=True)
        vt = vt_ref[0, hh * hv:(hh + 1) * hv, pl.ds(pl.multiple_of(k0 + r0, LANES), tks)]
        acc_s[hh, :, cols] = alpha * acc_s[hh, :, cols] + _dot(vt, p.astype(BF16))
        m_s[hh, :, cols] = m_new

    def single_tile(kt, streams, after_first_scores=None):
        kblk = key_block(kt)
        ahead = 2
        pending = [score(kblk, st) for st in streams[:ahead]]
        if after_first_scores is not None:
            after_first_scores()
        for idx, st in enumerate(streams):
            s = pending.pop(0)
            if idx + ahead < len(streams):
                pending.append(score(kblk, streams[idx + ahead]))
            absorb(s, st, kt)

    def store_scores(buf, kt):
        kblk = key_block(kt)
        for st in full_streams:
            hh, r0, c0, cn, _ = st
            buf[hh, r0:r0 + tks, c0:c0 + cn] = score(kblk, st)

    def absorb_stored(buf, kt):
        for st in full_streams:
            hh, r0, c0, cn, _ = st
            absorb(buf[hh, r0:r0 + tks, c0:c0 + cn], st, kt)

    def tile_pair(i, carry):
        kt = 2 * i
        store_scores(sb_s, kt + 1)
        absorb_stored(sa_s, kt)
        store_scores(sa_s, kt + 2)
        absorb_stored(sb_s, kt + 1)
        return carry

    def full_tile(kt, carry):
        single_tile(kt, full_streams)
        return carry

    def masked_tile(kt, carry):
        single_tile(kt, mask_streams)
        return carry

    n_full = jnp.minimum(q_start // tk, kv_len // tk)
    q_last = q_start + tq - 1
    k_hi = (q_last // CHUNK + 1) * CHUNK if chunk_causal else q_last + 1
    n_end = jnp.minimum((k_hi + tk - 1) // tk, n_k)
    n_pair = n_full // 2
    if diag_aligned:
        for d in range(tq // tk):
            single_tile(n_full + d, diag_streams(d),
                        after_first_scores=(lambda: store_scores(sa_s, 0)) if d == 0 else None)
        lax.fori_loop(0, n_pair, tile_pair, 0)
        lax.fori_loop(2 * n_pair, n_full, full_tile, 0)
    else:
        @pl.when(n_pair > 0)
        def _():
            store_scores(sa_s, 0)

        lax.fori_loop(0, n_pair, tile_pair, 0)
        lax.fori_loop(2 * n_pair, n_full, full_tile, 0)
        lax.fori_loop(n_full, n_end, masked_tile, 0)
    out_t = jnp.concatenate([acc_s[0] / l_s[0], acc_s[1] / l_s[1]], axis=0)
    o_ref[0] = out_t.T.astype(BF16)


def _flash(q, k, vt, *, past, kv_len, chunk_causal):
    b, t, w = q.shape
    lp = k.shape[1]
    n_pairs = w // (2 * LANES)
    t_pad = max(t, LANES)
    if t_pad > t:
        q = _pad_rows(q, t_pad)
    if past == 0 and t % 512 == 0 and lp % 512 == 0:
        tk = 512
        tq = 1024 if t % 1024 == 0 else 512
    elif t_pad == LANES:
        tq, tk = LANES, lp
    else:
        tq = tk = LANES
    assert t_pad % tq == 0 and lp % tk == 0, (t, lp, tq, tk)
    tks = 256 if tk % 256 == 0 and tq > LANES else tk
    tqs = min(tq, 512)
    out = pl.pallas_call(
        functools.partial(_flash_body, tq=tq, tqs=tqs, tk=tk, tks=tks, n_k=lp // tk, past=past, kv_len=kv_len,
                          chunk_causal=chunk_causal,
                          diag_aligned=(past == 0 and tq % tk == 0 and kv_len == lp and tks < tk)),
        grid=(b, n_pairs, t_pad // tq),
        in_specs=[pl.BlockSpec((1, tq, 2 * LANES), lambda i, j, s: (i, s, j)),
                  pl.BlockSpec((1, lp, 2 * LANES), lambda i, j, s: (i, 0, j)),
                  pl.BlockSpec((1, LANES, lp), lambda i, j, s: (i, j, 0))],
        out_specs=pl.BlockSpec((1, tq, LANES), lambda i, j, s: (i, s, j)),
        out_shape=jax.ShapeDtypeStruct((b, t_pad, n_pairs * LANES), BF16),
        scratch_shapes=[pltpu.VMEM((2, 1, tq), F32), pltpu.VMEM((2, 1, tq), F32),
                        pltpu.VMEM((2, LANES // 2, tq), F32),
                        pltpu.VMEM((2, tk, tq), F32), pltpu.VMEM((2, tk, tq), F32)],
        compiler_params=_params("parallel", "parallel", "arbitrary"),
        name="flash_mla" if chunk_causal else "flash_fox",
    )(q, k, vt)
    return out[:, :t]


def _row(v):
    return v.reshape(1, -1).astype(F32)


def _pad_lanes(x, lo, total):
    pad = [(0, 0)] * (x.ndim - 1) + [(lo, total - lo - x.shape[-1])]
    return jnp.pad(x, pad)


def _ffn_weights(g, w_in, w_out):
    f = w_out.shape[0]
    return dict(g=_row(g), wg=w_in[:, :f].astype(BF16), wu=w_in[:, f:].astype(BF16), wo=w_out.astype(BF16))


def _mem_weights(g, g_src, w_q, w_kv, w_o, g_q, g_k):
    d = w_q.shape[0]
    kv = w_kv.reshape(d, MEM_HEADS, 2, MEM_HEAD_DIM)
    return dict(g=_row(g), g_src=_row(g_src), wq=w_q.astype(BF16), wo=w_o.astype(BF16),
                wk=kv[:, :, 0].reshape(d, MEM_WIDTH).astype(BF16),
                wv=kv[:, :, 1].reshape(d, MEM_WIDTH).astype(BF16),
                gq=_row(g_q) * (MEM_HEAD_DIM ** -0.5), gk=_row(g_k))


def _even_weights(g_mix, w_in, g_qlat, g_kvlat, w_uq, w_ukv, g_q, g_k, conv_w, conv_b, gate_w, gate_b, lam, w_out):
    d = w_in.shape[0]
    o1 = MLA_Q_LORA
    o2 = o1 + MLA_KV_LORA
    o3 = o2 + MLA_ROPE
    o4 = o3 + LRU_WIDTH
    half = MLA_ROPE // 2
    swap_halves = lambda r: jnp.concatenate([r[..., half:], r[..., :half]], axis=-1)
    uq3 = w_uq.reshape(MLA_Q_LORA, MLA_HEADS, MLA_QK)
    uq = _pad_lanes(uq3, 0, LANES)
    uq_swap = _pad_lanes(swap_halves(uq3[:, :, MLA_NOPE:]), MLA_NOPE, LANES)
    kr = w_in[:, o2:o3]
    ukv = w_ukv.reshape(MLA_KV_LORA, MLA_HEADS, MLA_NOPE + MLA_V)
    uk = _pad_lanes(ukv[:, :, :MLA_NOPE], 0, LANES)
    blk = LRU_WIDTH // LRU_BLOCKS
    eye = jnp.eye(LRU_BLOCKS, dtype=F32)
    wr = jnp.einsum("ncd,nm->ncmd", gate_w[:, :, :blk], eye).reshape(LRU_WIDTH, LRU_WIDTH)
    wi = jnp.einsum("ncd,nm->ncmd", gate_w[:, :, blk:], eye).reshape(LRU_WIDTH, LRU_WIDTH)
    return dict(
        g_mix=_row(g_mix), wcq=w_in[:, :o1].astype(BF16), g_qlat=_row(g_qlat),
        wuq=uq.reshape(MLA_Q_LORA, MLA_HEADS * LANES).astype(BF16),
        wuq_swap=uq_swap.reshape(MLA_Q_LORA, MLA_HEADS * LANES).astype(BF16),
        gq=_pad_lanes(_row(g_q), 0, LANES) * (MLA_QK ** -0.5 * LOG2E),
        wckv=w_in[:, o1:o2].astype(BF16), g_kvlat=_row(g_kvlat),
        wkr=_pad_lanes(kr, MLA_NOPE, LANES).astype(BF16),
        wkr_swap=_pad_lanes(swap_halves(kr), MLA_NOPE, LANES).astype(BF16),
        wuk=uk.reshape(MLA_KV_LORA, MLA_HEADS * LANES).astype(BF16),
        wuvt=ukv[:, :, MLA_NOPE:].reshape(MLA_KV_LORA, MLA_HEADS * MLA_V).T.astype(BF16),
        gk=_pad_lanes(_row(g_k), 0, LANES),
        wrec=w_in[:, o3:o4].astype(BF16), wgate=w_in[:, o4:].astype(BF16),
        conv_w=conv_w.astype(F32), conv_b=_row(conv_b), wr=wr.astype(BF16), wi=wi.astype(BF16),
        br=_row(gate_b[:, :blk]), bi=_row(gate_b[:, blk:]), lam=_row(lam),
        wo_attn=w_out[:MLA_HEADS * MLA_V].astype(BF16), wo_rec=w_out[MLA_HEADS * MLA_V:].astype(BF16))


def _odd_weights(g_mix, w_in, b_f, g_q, g_k, w_out):
    fw = FOX_WIDTH
    head_of_lane = jnp.arange(fw) // FOX_HEAD_DIM
    e = (head_of_lane[:, None] == jnp.arange(LANES)[None, :]).astype(BF16)
    lane = jnp.arange(FOX_HEADS * LANES)
    hd, within = lane // LANES, lane % LANES
    own_low = hd % 2 == 0
    keep = jnp.where(own_low, within < FOX_HEAD_DIM, within >= FOX_HEAD_DIM)
    part = within - jnp.where(own_low, FOX_HEAD_DIM, 0)
    is_bias = (part >= 0) & (part < 3)
    src = part * FOX_HEADS + hd
    place = ((jnp.arange(LANES)[:, None] == src[None, :]) & is_bias[None, :]).astype(BF16)
    return dict(
        keep=keep.astype(F32)[None, :], ones=is_bias.astype(F32)[None, :], place=place,
        g_mix=_row(g_mix), wq=w_in[:, :fw].astype(BF16), wk=w_in[:, fw:2 * fw].astype(BF16),
        wv=w_in[:, 2 * fw:3 * fw].astype(BF16), wf=_pad_lanes(w_in[:, 3 * fw:], 0, LANES).astype(BF16),
        bf=_pad_lanes(_row(b_f), 0, LANES),
        gq=jnp.tile(_row(g_q), (1, FOX_HEADS)) * (FOX_HEAD_DIM ** -0.5 * LOG2E),
        gk=jnp.tile(_row(g_k), (1, FOX_HEADS)), e=e, et=e.T, wo=w_out.astype(BF16))


def _rope_tables(pos):
    half = MLA_ROPE // 2
    inv_freq = ROPE_THETA ** (-jnp.arange(half, dtype=F32) / half)
    ang = pos.astype(F32)[:, None] * inv_freq[None, :]
    cos, sin = jnp.cos(ang), jnp.sin(ang)
    c = jnp.concatenate([jnp.ones((pos.shape[0], MLA_NOPE), F32), cos, cos,
                         jnp.ones((pos.shape[0], LANES - MLA_QK), F32)], axis=-1)
    s = _pad_lanes(jnp.concatenate([-sin, sin], axis=-1), MLA_NOPE, LANES)
    return c, s


def _pad_rows(x, total):
    return jnp.pad(x, [(0, 0), (0, total - x.shape[1])] + [(0, 0)] * (x.ndim - 2))


def _kv_pad_len(t, past):
    l = past + t
    if past == 0 and t % 512 == 0:
        return l
    return -(-l // LANES) * LANES


def _even_layer(h, past, w, state):
    b, t, d = h.shape
    past_lat, past_krope, h0, conv_prev = state
    q, lat_new, krp_new = _mla_in(h, w, _rope_tables(past + jnp.arange(t)))
    lp = _kv_pad_len(t, past)
    lat_all = _pad_rows(jnp.concatenate([past_lat, lat_new], axis=1), lp)
    krp_all = _pad_rows(jnp.concatenate([_pad_lanes(past_krope, MLA_NOPE, LANES), krp_new], axis=1), lp)
    k, vt = _mla_kv(lat_all, krp_all, w)
    attn = _flash(q, k, vt, past=past, kv_len=past + t, chunk_causal=True)
    conv_prev8 = jnp.pad(conv_prev, ((0, 0), (HALO - (CONV_WIDTH - 1), 0), (0, 0)))
    y_rec, h_last, conv_last = _lru(h, w, conv_prev8, h0[:, None, :])
    new = (lat_new, krp_new[:, :, MLA_NOPE:MLA_QK], h_last[:, 0], conv_last[:, HALO - (CONV_WIDTH - 1):])
    return [attn, y_rec], [w["wo_attn"], w["wo_rec"]], new


def _odd_layer(h, past, w, state):
    b, t, d = h.shape
    past_k, past_v, past_logf = state
    if past > 0:
        ka_past, c_past = _fox_past(past_k.reshape(b, past, FOX_WIDTH),
                                    _pad_lanes(past_logf.astype(F32), 0, LANES), w)
        c0 = c_past[:, past - 1:past, :]
    else:
        ka_past = jnp.zeros((b, 0, FOX_HEADS * LANES), BF16)
        c0 = jnp.zeros((b, 1, LANES), F32)
    q, k32, v32, ka_new, vb, logf, _ = _fox_in(h, w, c0)
    lp = _kv_pad_len(t, past)
    k_all = _pad_rows(jnp.concatenate([ka_past, ka_new], axis=1), lp)
    v_all = _pad_rows(jnp.concatenate([past_v.reshape(b, past, FOX_WIDTH).astype(BF16), vb], axis=1), lp)
    attn = _flash(q, k_all, jnp.swapaxes(v_all, 1, 2), past=past, kv_len=past + t, chunk_causal=False)
    new = (k32.reshape(b, t, FOX_HEADS, FOX_HEAD_DIM), v32.reshape(b, t, FOX_HEADS, FOX_HEAD_DIM),
           logf[:, :, :FOX_HEADS])
    return [attn], [w["wo"]], new


def _trunk(x, past, layers, mem_kvs, even_states, odd_states):
    b, t, d = x.shape
    even_new, odd_new = [], []
    for li, lw in enumerate(layers):
        h = _ffn(x.reshape(b * t, d), lw["ffn1"]).reshape(b, t, d)
        if li % 2 == 0:
            parts, w_parts, new = _even_layer(h, past, lw["mix"], even_states[li // 2])
            even_new.append(new)
        else:
            parts, w_parts, new = _odd_layer(h, past, lw["mix"], odd_states[li // 2])
            odd_new.append(new)
        x = _post_mixer(h, parts, w_parts, mem_kvs[li][0], mem_kvs[li][1], lw["mem"], lw["ffn2"])
    return x, even_new, odd_new


def kernel(x_prompt, x_sample, mem_prompt, cache_mla_latent, cache_mla_krope, state_lru_h, state_lru_conv, cache_fox_k, cache_fox_v, cache_fox_logf, cache_mem_k, cache_mem_v, norm_ffn1, ffn1_w_in, ffn1_w_out, norm_mix, norm_mem, norm_mem_src, mem_w_q, mem_w_kv, mem_w_o, mem_g_q, mem_g_k, norm_ffn2, ffn2_w_in, ffn2_w_out, ev_w_in, ev_g_qlat, ev_g_kvlat, ev_w_uq, ev_w_ukv, ev_g_q, ev_g_k, ev_conv_w, ev_conv_b, ev_gate_w, ev_gate_b, ev_lambda, ev_w_out, od_w_in, od_b_f, od_g_q, od_g_k, od_w_out):
    depth = norm_ffn1.shape[0]
    n_even, n_odd = (depth + 1) // 2, depth // 2
    b, _, _ = x_prompt.shape
    bs = x_sample.shape[0]
    past = cache_mla_latent.shape[2] if n_even else cache_fox_k.shape[2]

    layers = []
    for li in range(depth):
        j = li // 2
        if li % 2 == 0:
            mix = _even_weights(norm_mix[li], ev_w_in[j], ev_g_qlat[j], ev_g_kvlat[j], ev_w_uq[j], ev_w_ukv[j],
                                ev_g_q[j], ev_g_k[j], ev_conv_w[j], ev_conv_b[j], ev_gate_w[j], ev_gate_b[j],
                                ev_lambda[j], ev_w_out[j])
        else:
            mix = _odd_weights(norm_mix[li], od_w_in[j], od_b_f[j], od_g_q[j], od_g_k[j], od_w_out[j])
        layers.append(dict(
            ffn1=_ffn_weights(norm_ffn1[li], ffn1_w_in[li], ffn1_w_out[li]),
            ffn2=_ffn_weights(norm_ffn2[li], ffn2_w_in[li], ffn2_w_out[li]),
            mem=_mem_weights(norm_mem[li], norm_mem_src[li], mem_w_q[li], mem_w_kv[li], mem_w_o[li],
                             mem_g_q[li], mem_g_k[li]),
            mix=mix))

    mem_p = [_mem_kv(mem_prompt, lw["mem"]) for lw in layers]
    ev0 = [(jnp.zeros((b, 0, MLA_KV_LORA), F32), jnp.zeros((b, 0, MLA_ROPE), F32),
            jnp.zeros((b, LRU_WIDTH), F32), jnp.zeros((b, CONV_WIDTH - 1, LRU_WIDTH), F32))
           for _ in range(n_even)]
    od0 = [(jnp.zeros((b, 0, FOX_HEADS, FOX_HEAD_DIM), F32), jnp.zeros((b, 0, FOX_HEADS, FOX_HEAD_DIM), F32),
            jnp.zeros((b, 0, FOX_HEADS), F32)) for _ in range(n_odd)]
    y_prompt, ev_p, od_p = _trunk(x_prompt, 0, layers, [(m[2], m[3]) for m in mem_p], ev0, od0)

    m_tok = cache_mem_k.shape[2]
    mem_s = [(cache_mem_k[li].reshape(bs, m_tok, MEM_WIDTH).astype(BF16),
              cache_mem_v[li].reshape(bs, m_tok, MEM_WIDTH).astype(BF16)) for li in range(depth)]
    ev_s = [(cache_mla_latent[j], cache_mla_krope[j], state_lru_h[j], state_lru_conv[j]) for j in range(n_even)]
    od_s = [(cache_fox_k[j], cache_fox_v[j], cache_fox_logf[j]) for j in range(n_odd)]
    y_sample, ev_n, od_n = _trunk(x_sample, past, layers, mem_s, ev_s, od_s)

    mem_shape = (b, m_tok, MEM_HEADS, MEM_HEAD_DIM)
    p_even = [jnp.stack([s[f] for s in ev_p]) for f in range(4)]
    p_odd = [jnp.stack([s[f] for s in od_p]) for f in range(3)]
    p_mem_k = jnp.stack([m[0].reshape(mem_shape) for m in mem_p])
    p_mem_v = jnp.stack([m[1].reshape(mem_shape) for m in mem_p])
    s_even = [jnp.stack([s[f] for s in ev_n]) for f in range(4)]
    s_odd = [jnp.stack([s[f] for s in od_n]) for f in range(3)]
    return (y_prompt, y_sample, *p_even, *p_odd, p_mem_k, p_mem_v, *s_even, *s_odd)
```

```python
import functools
import math

import jax
import jax.numpy as jnp
from jax import lax
from jax.experimental import pallas as pl
from jax.experimental.pallas import tpu as pltpu

F32 = jnp.float32
BF16 = jnp.bfloat16

NORM_EPS = 1e-6
NEG_INF = -1e30
LOG2E = math.log2(math.e)
CHUNK = 64
LANES = 128
SUBLANES = 8
ROPE_THETA = 10000.0
LRU_C = 8.0
MLA_HEADS = 8
MLA_NOPE = 64
MLA_ROPE = 32
MLA_QK = MLA_NOPE + MLA_ROPE
MLA_V = 64
MLA_Q_LORA = 256
MLA_KV_LORA = 128
LRU_WIDTH = 512
LRU_BLOCKS = 8
CONV_WIDTH = 4
FOX_HEADS = 16
FOX_HEAD_DIM = 64
FOX_WIDTH = FOX_HEADS * FOX_HEAD_DIM
MEM_HEADS = 4
MEM_HEAD_DIM = 128
MEM_WIDTH = MEM_HEADS * MEM_HEAD_DIM
HALO = 8

VMEM_LIMIT = 56 * 1024 * 1024


def _dot(a, b):
    return jnp.dot(a, b, preferred_element_type=F32)


def _dot_nt(a, b):
    return lax.dot_general(a, b, (((1,), (1,)), ((), ())), preferred_element_type=F32)


def _rms(x, g):
    return x * lax.rsqrt(jnp.mean(x * x, axis=-1, keepdims=True) + NORM_EPS) * g


def _head_rms(x, g, n_live):
    ss = jnp.sum(x * x, axis=-1, keepdims=True) * (1.0 / n_live)
    return x * lax.rsqrt(ss + NORM_EPS) * g


def _sigmoid(x):
    return 1.0 / (1.0 + jnp.exp(-x))


def _log1p(y):
    u = 1.0 + y
    d = u - 1.0
    return jnp.where(d == 0.0, y, jnp.log(u) * (y / jnp.where(d == 0.0, 1.0, d)))


def _softplus(x):
    return jnp.maximum(x, 0.0) + _log1p(jnp.exp(-jnp.abs(x)))


def _gelu_tanh(x):
    return 0.5 * x * (1.0 + jnp.tanh(math.sqrt(2.0 / math.pi) * (x + 0.044715 * (x * x * x))))


def _split_bf16(x, parts):
    out = []
    r = x
    for _ in range(parts):
        p = r.astype(BF16)
        out.append(p)
        r = r - p.astype(F32)
    return out


def _const_spec(shape):
    nd = len(shape)
    return pl.BlockSpec(shape, lambda *_: (0,) * nd, pipeline_mode=pl.Buffered(1))


def _params(*sem):
    return pltpu.CompilerParams(dimension_semantics=sem, vmem_limit_bytes=VMEM_LIMIT)


def _row_tile(n, cap):
    t = min(n, cap)
    assert n % t == 0, (n, t)
    return t


FFN_CHUNKS = 2
MXU_TILE = 256


def _ffn_chunk_bounds(f):
    tiles = -(-f // MXU_TILE)
    per = -(-tiles // FFN_CHUNKS) * MXU_TILE
    edges = [min(i * per, f) for i in range(FFN_CHUNKS + 1)]
    return [(lo, hi) for lo, hi in zip(edges[:-1], edges[1:]) if hi > lo]


def _swiglu_half_step(x, g_ref, wg_ref, wu_ref, wo_ref):
    hb = _rms(x, g_ref[...]).astype(BF16)
    acc = jnp.zeros_like(x)
    for lo, hi in _ffn_chunk_bounds(wg_ref.shape[1]):
        sl = slice(lo, hi)
        gate = _dot(hb, wg_ref[:, sl])
        up = _dot(hb, wu_ref[:, sl])
        act = (gate * _sigmoid(gate) * up).astype(BF16)
        acc = acc + _dot(act, wo_ref[sl, :])
    return x + 0.5 * acc


def _ffn_body(x_ref, g_ref, wg_ref, wu_ref, wo_ref, o_ref):
    o_ref[...] = _swiglu_half_step(x_ref[...], g_ref, wg_ref, wu_ref, wo_ref)


def _ffn(x2, w):
    n, d = x2.shape
    f = w["wg"].shape[1]
    tm = _row_tile(n, 512)
    return pl.pallas_call(
        _ffn_body,
        grid=(n // tm,),
        in_specs=[pl.BlockSpec((tm, d), lambda i: (i, 0)), _const_spec((1, d)),
                  _const_spec((d, f)), _const_spec((d, f)), _const_spec((f, d))],
        out_specs=pl.BlockSpec((tm, d), lambda i: (i, 0)),
        out_shape=jax.ShapeDtypeStruct((n, d), F32),
        compiler_params=_params("parallel"),
        name="ffn",
    )(x2, w["g"], w["wg"], w["wu"], w["wo"])


def _mem_cross_attention(h, tm, g_ref, wq_ref, gq_ref, mk_ref, mv_ref, wo_ref):
    hb = _rms(h, g_ref[...]).astype(BF16)
    q = _dot(hb, wq_ref[...])
    rows = []
    for i in range(h.shape[0] // tm):
        outs = []
        for hd in range(MEM_HEADS):
            sl = slice(hd * MEM_HEAD_DIM, (hd + 1) * MEM_HEAD_DIM)
            qh = _head_rms(q[i * tm:(i + 1) * tm, sl], gq_ref[...], MEM_HEAD_DIM).astype(BF16)
            s = _dot_nt(qh, mk_ref[i, :, sl])
            e = jnp.exp(s - jnp.max(s, axis=-1, keepdims=True))
            p = e / jnp.sum(e, axis=-1, keepdims=True)
            outs.append(_dot(p.astype(BF16), mv_ref[i, :, sl]).astype(BF16))
        rows.append(jnp.concatenate(outs, axis=-1))
    o = rows[0] if len(rows) == 1 else jnp.concatenate(rows, axis=0)
    return h + _dot(o, wo_ref[...])


def _post_mixer_body(*refs, n_parts):
    h_ref = refs[0]
    parts = refs[1:1 + n_parts]
    w_parts = refs[1 + n_parts:1 + 2 * n_parts]
    (mg_ref, mwq_ref, mgq_ref, mk_ref, mv_ref, mwo_ref,
     fg_ref, fwg_ref, fwu_ref, fwo_ref, o_ref) = refs[1 + 2 * n_parts:]
    bb, tm, d = h_ref.shape
    h = h_ref[...].reshape(bb * tm, d)
    for p_ref, w_ref in zip(parts, w_parts):
        h = h + _dot(p_ref[...].reshape(bb * tm, p_ref.shape[2]), w_ref[...])
    h = _mem_cross_attention(h, tm, mg_ref, mwq_ref, mgq_ref, mk_ref, mv_ref, mwo_ref)
    o_ref[...] = _swiglu_half_step(h, fg_ref, fwg_ref, fwu_ref, fwo_ref).reshape(bb, tm, d)


def _post_mixer(h, parts, w_parts, mk, mv, wm, wf):
    b, t, d = h.shape
    m = mk.shape[1]
    f = wf["wg"].shape[1]
    tm = _row_tile(t, 512)
    bb = b if b * t <= 512 and tm % SUBLANES == 0 else 1
    row = lambda n: pl.BlockSpec((bb, tm, n), lambda i, j: (i, j, 0))
    mem = lambda: pl.BlockSpec((bb, m, MEM_WIDTH), lambda i, j: (i, 0, 0))
    in_specs = [row(d)] + [row(p.shape[2]) for p in parts] + [_const_spec(w.shape) for w in w_parts]
    in_specs += [_const_spec((1, d)), _const_spec((d, MEM_WIDTH)), _const_spec((1, MEM_HEAD_DIM)), mem(), mem(),
                 _const_spec((MEM_WIDTH, d)),
                 _const_spec((1, d)), _const_spec((d, f)), _const_spec((d, f)), _const_spec((f, d))]
    return pl.pallas_call(
        functools.partial(_post_mixer_body, n_parts=len(parts)),
        grid=(b // bb, t // tm),
        in_specs=in_specs,
        out_specs=row(d),
        out_shape=jax.ShapeDtypeStruct((b, t, d), F32),
        compiler_params=_params("parallel", "parallel"),
        name="post_mixer",
    )(h, *parts, *w_parts, wm["g"], wm["wq"], wm["gq"], mk, mv, wm["wo"],
      wf["g"], wf["wg"], wf["wu"], wf["wo"])


def _mem_kv_body(m_ref, g_ref, wk_ref, wv_ref, gk_ref, k32_ref, v32_ref, kb_ref, vb_ref):
    hb = _rms(m_ref[0], g_ref[...]).astype(BF16)
    k = _dot(hb, wk_ref[...])
    v = _dot(hb, wv_ref[...])
    for hd in range(MEM_HEADS):
        sl = slice(hd * MEM_HEAD_DIM, (hd + 1) * MEM_HEAD_DIM)
        kh = _head_rms(k[:, sl], gk_ref[...], MEM_HEAD_DIM)
        k32_ref[0, :, sl] = kh
        kb_ref[0, :, sl] = kh.astype(BF16)
    v32_ref[0] = v
    vb_ref[0] = v.astype(BF16)


def _mem_kv(mem, w):
    b, m, d = mem.shape
    blk = lambda: pl.BlockSpec((1, m, MEM_WIDTH), lambda i: (i, 0, 0))
    return pl.pallas_call(
        _mem_kv_body,
        grid=(b,),
        in_specs=[pl.BlockSpec((1, m, d), lambda i: (i, 0, 0)), _const_spec((1, d)),
                  _const_spec((d, MEM_WIDTH)), _const_spec((d, MEM_WIDTH)), _const_spec((1, MEM_HEAD_DIM))],
        out_specs=[blk(), blk(), blk(), blk()],
        out_shape=[jax.ShapeDtypeStruct((b, m, MEM_WIDTH), F32), jax.ShapeDtypeStruct((b, m, MEM_WIDTH), F32),
                   jax.ShapeDtypeStruct((b, m, MEM_WIDTH), BF16), jax.ShapeDtypeStruct((b, m, MEM_WIDTH), BF16)],
        compiler_params=_params("parallel"),
        name="mem_kv",
    )(mem, w["g_src"], w["wk"], w["wv"], w["gk"])


def _mla_in_body(h_ref, g_ref, wcq_ref, gql_ref, wuq_ref, wuqs_ref, gq_ref, wckv_ref, gkv_ref, wkr_ref, wkrs_ref,
                 c_ref, s_ref, q_ref, lat_ref, krp_ref):
    hb = _rms(h_ref[0], g_ref[...]).astype(BF16)
    c, s = c_ref[...], s_ref[...]
    cq = _rms(_dot(hb, wcq_ref[...]), gql_ref[...]).astype(BF16)
    q = _dot(cq, wuq_ref[...])
    q_partner = _dot(cq, wuqs_ref[...])
    for hd in range(MLA_HEADS):
        sl = slice(hd * LANES, (hd + 1) * LANES)
        qh = q[:, sl] * c + q_partner[:, sl] * s
        q_ref[0, :, sl] = _head_rms(qh, gq_ref[...], MLA_QK).astype(BF16)
    lat_ref[0] = _rms(_dot(hb, wckv_ref[...]), gkv_ref[...])
    krp_ref[0] = _dot(hb, wkr_ref[...]) * c + _dot(hb, wkrs_ref[...]) * s


def _mla_in(h, w, tables):
    b, t, d = h.shape
    tm = _row_tile(t, 512)
    row = lambda n: pl.BlockSpec((1, tm, n), lambda i, j: (i, j, 0))
    tab = lambda: pl.BlockSpec((tm, LANES), lambda i, j: (j, 0))
    return pl.pallas_call(
        _mla_in_body,
        grid=(b, t // tm),
        in_specs=[row(d), _const_spec((1, d)),
                  _const_spec((d, MLA_Q_LORA)), _const_spec((1, MLA_Q_LORA)),
                  _const_spec((MLA_Q_LORA, MLA_HEADS * LANES)), _const_spec((MLA_Q_LORA, MLA_HEADS * LANES)),
                  _const_spec((1, LANES)),
                  _const_spec((d, MLA_KV_LORA)), _const_spec((1, MLA_KV_LORA)),
                  _const_spec((d, LANES)), _const_spec((d, LANES)),
                  tab(), tab()],
        out_specs=[row(MLA_HEADS * LANES), row(MLA_KV_LORA), row(LANES)],
        out_shape=[jax.ShapeDtypeStruct((b, t, MLA_HEADS * LANES), BF16),
                   jax.ShapeDtypeStruct((b, t, MLA_KV_LORA), F32),
                   jax.ShapeDtypeStruct((b, t, LANES), F32)],
        compiler_params=_params("parallel", "parallel"),
        name="mla_in",
    )(h, w["g_mix"], w["wcq"], w["g_qlat"], w["wuq"], w["wuq_swap"], w["gq"], w["wckv"], w["g_kvlat"],
      w["wkr"], w["wkr_swap"], *tables)


def _mla_kv_body(lat_ref, krp_ref, wuk_ref, wuvt_ref, gk_ref, k_ref, vt_ref):
    lb = lat_ref[0].astype(BF16)
    krp = krp_ref[0]
    kn = _dot(lb, wuk_ref[...])
    for hd in range(MLA_HEADS):
        sl = slice(hd * LANES, (hd + 1) * LANES)
        k_ref[0, :, sl] = _head_rms(kn[:, sl] + krp, gk_ref[...], MLA_QK).astype(BF16)
    vt_ref[0] = _dot_nt(wuvt_ref[...], lb).astype(BF16)


def _mla_kv(lat, krp, w):
    b, l, _ = lat.shape
    tl = _row_tile(l, 512) if l % 512 == 0 else l
    row = lambda n: pl.BlockSpec((1, tl, n), lambda i, j: (i, j, 0))
    vw = MLA_HEADS * MLA_V
    return pl.pallas_call(
        _mla_kv_body,
        grid=(b, l // tl),
        in_specs=[row(MLA_KV_LORA), row(LANES), _const_spec((MLA_KV_LORA, MLA_HEADS * LANES)),
                  _const_spec((vw, MLA_KV_LORA)), _const_spec((1, LANES))],
        out_specs=[row(MLA_HEADS * LANES), pl.BlockSpec((1, vw, tl), lambda i, j: (i, 0, j))],
        out_shape=[jax.ShapeDtypeStruct((b, l, MLA_HEADS * LANES), BF16),
                   jax.ShapeDtypeStruct((b, vw, l), BF16)],
        compiler_params=_params("parallel", "parallel"),
        name="mla_kv",
    )(lat, krp, w["wuk"], w["wuvt"], w["gk"])


def _lru_body(h_ref, g_ref, wrec_ref, wgate_ref, cw_ref, cb_ref, wr_ref, wi_ref, br_ref, bi_ref, lam_ref,
              cprev_ref, h0_ref, y_ref, hl_ref, cl_ref, buf, a_s, b_s, hcar, *, tm):
    @pl.when(pl.program_id(1) == 0)
    def _():
        buf[0:HALO, :] = cprev_ref[0]
        hcar[...] = h0_ref[0]

    hb = _rms(h_ref[0], g_ref[...]).astype(BF16)
    xr = _dot(hb, wrec_ref[...])
    xg = _dot(hb, wgate_ref[...])
    buf[HALO:HALO + tm, :] = xr
    xc = cb_ref[...] + xr * cw_ref[CONV_WIDTH - 1:CONV_WIDTH, :]
    for j in range(CONV_WIDTH - 1):
        off = HALO - (CONV_WIDTH - 1) + j
        xc = xc + cw_ref[j:j + 1, :] * buf[off:off + tm, :]
    xcb = xc.astype(BF16)
    r = _sigmoid(_dot(xcb, wr_ref[...]) + br_ref[...])
    i = _sigmoid(_dot(xcb, wi_ref[...]) + bi_ref[...])
    log_a = (-LRU_C) * r * _softplus(-lam_ref[...])
    a = jnp.exp(log_a)
    b = jnp.sqrt(-jnp.tanh(log_a) * (a * a + 1.0)) * (i * xc)
    a_s[...] = a
    b_s[...] = b

    row = lax.broadcasted_iota(jnp.int32, (SUBLANES, a.shape[1]), 0)

    def step(g, hprev):
        r0 = pl.multiple_of(g * SUBLANES, SUBLANES)
        ag = a_s[pl.ds(r0, SUBLANES), :]
        bg = b_s[pl.ds(r0, SUBLANES), :]
        d = 1
        while d < SUBLANES:
            keep = row >= d
            a_up = jnp.where(keep, pltpu.roll(ag, d, 0), 1.0)
            b_up = jnp.where(keep, pltpu.roll(bg, d, 0), 0.0)
            bg = ag * b_up + bg
            ag = ag * a_up
            d *= 2
        hg = ag * hprev + bg
        b_s[pl.ds(r0, SUBLANES), :] = hg
        return hg[SUBLANES - 1:, :]

    hfin = lax.fori_loop(0, tm // SUBLANES, step, hcar[...], unroll=2)
    hcar[...] = hfin
    y_ref[0] = (_gelu_tanh(xg) * b_s[...]).astype(BF16)
    hl_ref[0] = hfin
    tail = buf[tm:tm + HALO, :]
    buf[0:HALO, :] = tail
    cl_ref[0] = tail


def _lru(h, w, conv_prev8, h0):
    b, t, d = h.shape
    tm = _row_tile(t, 512)
    wd = LRU_WIDTH
    vec = lambda: _const_spec((1, wd))
    return pl.pallas_call(
        functools.partial(_lru_body, tm=tm),
        grid=(b, t // tm),
        in_specs=[pl.BlockSpec((1, tm, d), lambda i, j: (i, j, 0)), _const_spec((1, d)),
                  _const_spec((d, wd)), _const_spec((d, wd)), _const_spec((CONV_WIDTH, wd)), vec(),
                  _const_spec((wd, wd)), _const_spec((wd, wd)), vec(), vec(), vec(),
                  pl.BlockSpec((1, HALO, wd), lambda i, j: (i, 0, 0)),
                  pl.BlockSpec((1, 1, wd), lambda i, j: (i, 0, 0))],
        out_specs=[pl.BlockSpec((1, tm, wd), lambda i, j: (i, j, 0)),
                   pl.BlockSpec((1, 1, wd), lambda i, j: (i, 0, 0)),
                   pl.BlockSpec((1, HALO, wd), lambda i, j: (i, 0, 0))],
        out_shape=[jax.ShapeDtypeStruct((b, t, wd), BF16), jax.ShapeDtypeStruct((b, 1, wd), F32),
                   jax.ShapeDtypeStruct((b, HALO, wd), F32)],
        scratch_shapes=[pltpu.VMEM((tm + HALO, wd), F32), pltpu.VMEM((tm, wd), F32),
                        pltpu.VMEM((tm, wd), F32), pltpu.VMEM((1, wd), F32)],
        compiler_params=_params("parallel", "arbitrary"),
        name="lru",
    )(h, w["g_mix"], w["wrec"], w["wgate"], w["conv_w"], w["conv_b"], w["wr"], w["wi"], w["br"], w["bi"],
      w["lam"], conv_prev8, h0)


def _group_rms(x, e, et, g):
    hi, lo = _split_bf16(x * x, 2)
    ss = _dot(hi, e) + _dot(lo, e)
    inv = lax.rsqrt(ss * (1.0 / FOX_HEAD_DIM) + NORM_EPS)
    ih, il = _split_bf16(inv, 2)
    return x * (_dot(ih, et) + _dot(il, et)) * g


def _cumsum_rows(x, ltri):
    out = None
    for p in _split_bf16(x, 3):
        d = _dot(ltri, p)
        out = d if out is None else out + d
    return out


def _fox_aug(x, keep, bias):
    blocks = [x[:, (hd // 2) * LANES:(hd // 2 + 1) * LANES] for hd in range(FOX_HEADS)]
    return jnp.concatenate(blocks, axis=-1) * keep + bias


def _fox_key_bias(c, place):
    lane = lax.broadcasted_iota(jnp.int32, (1, LANES), 1)
    hi, mid, lo = _split_bf16(jnp.where(lane < FOX_HEADS, c * (-LOG2E), 0.0), 3)
    packed = (hi.astype(F32) + pltpu.roll(mid.astype(F32), FOX_HEADS, 1)
              + pltpu.roll(lo.astype(F32), 2 * FOX_HEADS, 1))
    return _dot(packed.astype(BF16), place)


def _fox_in_body(h_ref, g_ref, wq_ref, wk_ref, wv_ref, wf_ref, bf_ref, gq_ref, gk_ref, e_ref, et_ref,
                 keep_ref, ones_ref, place_ref, ltri_ref, c0_ref,
                 q_ref, k32_ref, v32_ref, ka_ref, vb_ref, lf_ref, c_ref, lf_s, carry, *, tm, tc):
    @pl.when(pl.program_id(1) == 0)
    def _():
        carry[...] = c0_ref[0]

    hb = _rms(h_ref[0], g_ref[...]).astype(BF16)
    e, et, keep = e_ref[...], et_ref[...], keep_ref[...]
    q = _group_rms(_dot(hb, wq_ref[...]), e, et, gq_ref[...])
    q_ref[0] = _fox_aug(q, keep, ones_ref[...]).astype(BF16)
    k = _group_rms(_dot(hb, wk_ref[...]), e, et, gk_ref[...])
    k32_ref[0] = k
    v = _dot(hb, wv_ref[...])
    v32_ref[0] = v
    vb_ref[0] = v.astype(BF16)
    logf = -_softplus(-(_dot(hb, wf_ref[...]) + bf_ref[...]))
    lf_ref[0] = logf
    if tc > tm:
        lf_s[...] = jnp.zeros_like(lf_s)
    lf_s[0:tm, :] = logf
    c = carry[...] + _cumsum_rows(lf_s[...], ltri_ref[...])[0:tm, :]
    c_ref[0] = c
    carry[...] = c[tm - 1:tm, :]
    ka_ref[0] = _fox_aug(k, keep, _fox_key_bias(c, place_ref[...])).astype(BF16)


def _fox_in(h, w, c0):
    b, t, d = h.shape
    tm = _row_tile(t, 512)
    tc = max(tm, LANES)
    ltri = jnp.tril(jnp.ones((tc, tc), F32)).astype(BF16)
    row = lambda n: pl.BlockSpec((1, tm, n), lambda i, j: (i, j, 0))
    fw, aw = FOX_WIDTH, FOX_HEADS * LANES
    return pl.pallas_call(
        functools.partial(_fox_in_body, tm=tm, tc=tc),
        grid=(b, t // tm),
        in_specs=[row(d), _const_spec((1, d)),
                  _const_spec((d, fw)), _const_spec((d, fw)), _const_spec((d, fw)), _const_spec((d, LANES)),
                  _const_spec((1, LANES)), _const_spec((1, fw)), _const_spec((1, fw)),
                  _const_spec((fw, LANES)), _const_spec((LANES, fw)),
                  _const_spec((1, aw)), _const_spec((1, aw)), _const_spec((LANES, aw)),
                  _const_spec((tc, tc)), pl.BlockSpec((1, 1, LANES), lambda i, j: (i, 0, 0))],
        out_specs=[row(aw), row(fw), row(fw), row(aw), row(fw), row(LANES), row(LANES)],
        out_shape=[jax.ShapeDtypeStruct((b, t, aw), BF16), jax.ShapeDtypeStruct((b, t, fw), F32),
                   jax.ShapeDtypeStruct((b, t, fw), F32), jax.ShapeDtypeStruct((b, t, aw), BF16),
                   jax.ShapeDtypeStruct((b, t, fw), BF16), jax.ShapeDtypeStruct((b, t, LANES), F32),
                   jax.ShapeDtypeStruct((b, t, LANES), F32)],
        scratch_shapes=[pltpu.VMEM((tc, LANES), F32), pltpu.VMEM((1, LANES), F32)],
        compiler_params=_params("parallel", "arbitrary"),
        name="fox_in",
    )(h, w["g_mix"], w["wq"], w["wk"], w["wv"], w["wf"], w["bf"], w["gq"], w["gk"], w["e"], w["et"],
      w["keep"], w["ones"], w["place"], ltri, c0)


def _fox_past_body(k_ref, lf_ref, keep_ref, place_ref, ltri_ref, ka_ref, c_ref, carry):
    @pl.when(pl.program_id(1) == 0)
    def _():
        carry[...] = jnp.zeros_like(carry)

    c = carry[...] + _cumsum_rows(lf_ref[0], ltri_ref[...])
    c_ref[0] = c
    carry[...] = c[c.shape[0] - 1:, :]
    ka_ref[0] = _fox_aug(k_ref[0], keep_ref[...], _fox_key_bias(c, place_ref[...])).astype(BF16)


def _fox_past(past_k, past_logf, w):
    b, p, n = past_logf.shape
    tc = _row_tile(p, 512)
    ltri = jnp.tril(jnp.ones((tc, tc), F32)).astype(BF16)
    fw, aw = FOX_WIDTH, FOX_HEADS * LANES
    row = lambda m: pl.BlockSpec((1, tc, m), lambda i, j: (i, j, 0))
    return pl.pallas_call(
        _fox_past_body,
        grid=(b, p // tc),
        in_specs=[row(fw), row(n), _const_spec((1, aw)), _const_spec((LANES, aw)), _const_spec((tc, tc))],
        out_specs=[row(aw), row(n)],
        out_shape=[jax.ShapeDtypeStruct((b, p, aw), BF16), jax.ShapeDtypeStruct((b, p, n), F32)],
        scratch_shapes=[pltpu.VMEM((1, n), F32)],
        compiler_params=_params("parallel", "arbitrary"),
        name="fox_past",
    )(past_k, past_logf, w["keep"], w["place"], ltri)


def _flash_body(q_ref, k_ref, vt_ref, o_ref, m_s, l_s, acc_s, sa_s, sb_s, *, tq, tqs, tk, tks, fr, fc, n_k, past,
                kv_len, chunk_causal, diag_aligned):
    q_start = past + pl.program_id(2) * tq
    q = q_ref[0]
    m_s[...] = jnp.full_like(m_s, NEG_INF)
    l_s[...] = jnp.zeros_like(l_s)
    acc_s[...] = jnp.zeros_like(acc_s)
    shift = int(math.log2(CHUNK))
    hv = LANES // 2

    def block_kind(r0, rn, c0, cn):
        if chunk_causal:
            k_lo, k_hi, q_lo, q_hi = r0 >> shift, (r0 + rn - 1) >> shift, c0 >> shift, (c0 + cn - 1) >> shift
        else:
            k_lo, k_hi, q_lo, q_hi = r0, r0 + rn - 1, c0, c0 + cn - 1
        return "visible" if k_hi <= q_lo else ("hidden" if k_lo > q_hi else "partial")

    def diag_streams(d):
        out = []
        for r0 in range(0, tk, tks):
            for hh in range(2):
                for c0 in range(0, tq, tqs):
                    halves = [(c, block_kind(d * tk + r0, tks, c, tks)) for c in range(c0, c0 + tqs, tks)]
                    if all(kind == "visible" for _, kind in halves):
                        out.append((hh, r0, tks, c0, tqs, "visible"))
                    else:
                        out += [(hh, r0, tks, c, tks, kind) for c, kind in halves if kind != "hidden"]
        return out

    full_streams = [(hh, r0, fr, c0, fc, "visible") for r0 in range(0, tk, fr) for hh in range(2)
                    for c0 in range(0, tq, fc)]
    mask_streams = [st[:5] + ("partial",) for st in full_streams]

    def score(kblk, stream):
        hh, r0, rn, c0, cn, _ = stream
        head = slice(hh * LANES, (hh + 1) * LANES)
        return _dot_nt(kblk[r0:r0 + rn, head], q[c0:c0 + cn, head])

    def key_block(kt):
        return k_ref[0, pl.ds(pl.multiple_of(jnp.minimum(kt, n_k - 1) * tk, tk), tk), :]

    def absorb(s, stream, kt):
        hh, r0, rn, c0, cn, kind = stream
        cols = slice(c0, c0 + cn)
        k0 = pl.multiple_of(kt * tk, tk)
        if kind == "partial":
            kpos = k0 + r0 + lax.broadcasted_iota(jnp.int32, (rn, cn), 0)
            qpos = q_start + c0 + lax.broadcasted_iota(jnp.int32, (rn, cn), 1)
            if chunk_causal:
                vis = lax.shift_right_logical(kpos, shift) <= lax.shift_right_logical(qpos, shift)
            else:
                vis = kpos <= qpos
            s = jnp.where(jnp.logical_and(vis, kpos < kv_len), s, NEG_INF)
        m_old = m_s[hh, :, cols]
        m_new = jnp.maximum(m_old, jnp.max(s, axis=0, keepdims=True))
        alpha = jnp.exp2(m_old - m_new)
        p = jnp.exp2(s - m_new)
        l_s[hh, :, cols] = alpha * l_s[hh, :, cols] + jnp.sum(p, axis=0, keepdims=True)
        vt = vt_ref[0, hh * hv:(hh + 1) * hv, pl.ds(pl.multiple_of(k0 + r0, LANES), rn)]
        acc_s[hh, :, cols] = alpha * acc_s[hh, :, cols] + _dot(vt, p.astype(BF16))
        m_s[hh, :, cols] = m_new

    def single_tile(kt, streams, after_first_scores=None):
        kblk = key_block(kt)
        ahead = 2
        pending = [score(kblk, st) for st in streams[:ahead]]
        if after_first_scores is not None:
            after_first_scores()
        for idx, st in enumerate(streams):
            s = pending.pop(0)
            if idx + ahead < len(streams):
                pending.append(score(kblk, streams[idx + ahead]))
            absorb(s, st, kt)

    def store_scores(buf, kt):
        kblk = key_block(kt)
        for st in full_streams:
            hh, r0, rn, c0, cn, _ = st
            buf[hh, r0:r0 + rn, c0:c0 + cn] = score(kblk, st)

    def absorb_stored(buf, kt):
        for st in full_streams:
            hh, r0, rn, c0, cn, _ = st
            absorb(buf[hh, r0:r0 + rn, c0:c0 + cn], st, kt)

    def tile_pair(i, carry):
        kt = 2 * i
        store_scores(sb_s, kt + 1)
        absorb_stored(sa_s, kt)
        store_scores(sa_s, kt + 2)
        absorb_stored(sb_s, kt + 1)
        return carry

    def full_tile(kt, carry):
        single_tile(kt, full_streams)
        return carry

    def masked_tile(kt, carry):
        single_tile(kt, mask_streams)
        return carry

    n_full = jnp.minimum(q_start // tk, kv_len // tk)
    q_last = q_start + tq - 1
    k_hi = (q_last // CHUNK + 1) * CHUNK if chunk_causal else q_last + 1
    n_end = jnp.minimum((k_hi + tk - 1) // tk, n_k)
    n_pair = n_full // 2
    if diag_aligned:
        for d in range(tq // tk):
            single_tile(n_full + d, diag_streams(d),
                        after_first_scores=(lambda: store_scores(sa_s, 0)) if d == 0 else None)
        lax.fori_loop(0, n_pair, tile_pair, 0)
        lax.fori_loop(2 * n_pair, n_full, full_tile, 0)
    else:
        @pl.when(n_pair > 0)
        def _():
            store_scores(sa_s, 0)

        lax.fori_loop(0, n_pair, tile_pair, 0)
        lax.fori_loop(2 * n_pair, n_full, full_tile, 0)
        lax.fori_loop(n_full, n_end, masked_tile, 0)
    out_t = jnp.concatenate([acc_s[0] / l_s[0], acc_s[1] / l_s[1]], axis=0)
    o_ref[0] = out_t.T.astype(BF16)


def _flash(q, k, vt, *, past, kv_len, chunk_causal):
    b, t, w = q.shape
    lp = k.shape[1]
    n_pairs = w // (2 * LANES)
    t_pad = max(t, LANES)
    if t_pad > t:
        q = _pad_rows(q, t_pad)
    if past == 0 and t % 512 == 0 and lp % 512 == 0:
        tk = 512
        tq = 1024 if t % 1024 == 0 else 512
    elif t_pad == LANES:
        tq, tk = LANES, lp
    else:
        tq = tk = LANES
    assert t_pad % tq == 0 and lp % tk == 0, (t, lp, tq, tk)
    tks = 256 if tk % 256 == 0 and tq > LANES else tk
    tqs = min(tq, 512)
    fr, fc = (tk, 256) if tks < tk and tq % 256 == 0 else (tks, tqs)
    out = pl.pallas_call(
        functools.partial(_flash_body, tq=tq, tqs=tqs, tk=tk, tks=tks, fr=fr, fc=fc, n_k=lp // tk, past=past,
                          kv_len=kv_len, chunk_causal=chunk_causal,
                          diag_aligned=(past == 0 and tq % tk == 0 and kv_len == lp and tks < tk)),
        grid=(b, n_pairs, t_pad // tq),
        in_specs=[pl.BlockSpec((1, tq, 2 * LANES), lambda i, j, s: (i, s, j)),
                  pl.BlockSpec((1, lp, 2 * LANES), lambda i, j, s: (i, 0, j)),
                  pl.BlockSpec((1, LANES, lp), lambda i, j, s: (i, j, 0))],
        out_specs=pl.BlockSpec((1, tq, LANES), lambda i, j, s: (i, s, j)),
        out_shape=jax.ShapeDtypeStruct((b, t_pad, n_pairs * LANES), BF16),
        scratch_shapes=[pltpu.VMEM((2, 1, tq), F32), pltpu.VMEM((2, 1, tq), F32),
                        pltpu.VMEM((2, LANES // 2, tq), F32),
                        pltpu.VMEM((2, tk, tq), F32), pltpu.VMEM((2, tk, tq), F32)],
        compiler_params=_params("parallel", "parallel", "arbitrary"),
        name="flash_mla" if chunk_causal else "flash_fox",
    )(q, k, vt)
    return out[:, :t]


def _row(v):
    return v.reshape(1, -1).astype(F32)


def _pad_lanes(x, lo, total):
    pad = [(0, 0)] * (x.ndim - 1) + [(lo, total - lo - x.shape[-1])]
    return jnp.pad(x, pad)


def _ffn_weights(g, w_in, w_out):
    f = w_out.shape[0]
    return dict(g=_row(g), wg=w_in[:, :f].astype(BF16), wu=w_in[:, f:].astype(BF16), wo=w_out.astype(BF16))


def _mem_weights(g, g_src, w_q, w_kv, w_o, g_q, g_k):
    d = w_q.shape[0]
    kv = w_kv.reshape(d, MEM_HEADS, 2, MEM_HEAD_DIM)
    return dict(g=_row(g), g_src=_row(g_src), wq=w_q.astype(BF16), wo=w_o.astype(BF16),
                wk=kv[:, :, 0].reshape(d, MEM_WIDTH).astype(BF16),
                wv=kv[:, :, 1].reshape(d, MEM_WIDTH).astype(BF16),
                gq=_row(g_q) * (MEM_HEAD_DIM ** -0.5), gk=_row(g_k))


def _even_weights(g_mix, w_in, g_qlat, g_kvlat, w_uq, w_ukv, g_q, g_k, conv_w, conv_b, gate_w, gate_b, lam, w_out):
    d = w_in.shape[0]
    o1 = MLA_Q_LORA
    o2 = o1 + MLA_KV_LORA
    o3 = o2 + MLA_ROPE
    o4 = o3 + LRU_WIDTH
    half = MLA_ROPE // 2
    swap_halves = lambda r: jnp.concatenate([r[..., half:], r[..., :half]], axis=-1)
    uq3 = w_uq.reshape(MLA_Q_LORA, MLA_HEADS, MLA_QK)
    uq = _pad_lanes(uq3, 0, LANES)
    uq_swap = _pad_lanes(swap_halves(uq3[:, :, MLA_NOPE:]), MLA_NOPE, LANES)
    kr = w_in[:, o2:o3]
    ukv = w_ukv.reshape(MLA_KV_LORA, MLA_HEADS, MLA_NOPE + MLA_V)
    uk = _pad_lanes(ukv[:, :, :MLA_NOPE], 0, LANES)
    blk = LRU_WIDTH // LRU_BLOCKS
    eye = jnp.eye(LRU_BLOCKS, dtype=F32)
    wr = jnp.einsum("ncd,nm->ncmd", gate_w[:, :, :blk], eye).reshape(LRU_WIDTH, LRU_WIDTH)
    wi = jnp.einsum("ncd,nm->ncmd", gate_w[:, :, blk:], eye).reshape(LRU_WIDTH, LRU_WIDTH)
    return dict(
        g_mix=_row(g_mix), wcq=w_in[:, :o1].astype(BF16), g_qlat=_row(g_qlat),
        wuq=uq.reshape(MLA_Q_LORA, MLA_HEADS * LANES).astype(BF16),
        wuq_swap=uq_swap.reshape(MLA_Q_LORA, MLA_HEADS * LANES).astype(BF16),
        gq=_pad_lanes(_row(g_q), 0, LANES) * (MLA_QK ** -0.5 * LOG2E),
        wckv=w_in[:, o1:o2].astype(BF16), g_kvlat=_row(g_kvlat),
        wkr=_pad_lanes(kr, MLA_NOPE, LANES).astype(BF16),
        wkr_swap=_pad_lanes(swap_halves(kr), MLA_NOPE, LANES).astype(BF16),
        wuk=uk.reshape(MLA_KV_LORA, MLA_HEADS * LANES).astype(BF16),
        wuvt=ukv[:, :, MLA_NOPE:].reshape(MLA_KV_LORA, MLA_HEADS * MLA_V).T.astype(BF16),
        gk=_pad_lanes(_row(g_k), 0, LANES),
        wrec=w_in[:, o3:o4].astype(BF16), wgate=w_in[:, o4:].astype(BF16),
        conv_w=conv_w.astype(F32), conv_b=_row(conv_b), wr=wr.astype(BF16), wi=wi.astype(BF16),
        br=_row(gate_b[:, :blk]), bi=_row(gate_b[:, blk:]), lam=_row(lam),
        wo_attn=w_out[:MLA_HEADS * MLA_V].astype(BF16), wo_rec=w_out[MLA_HEADS * MLA_V:].astype(BF16))


def _odd_weights(g_mix, w_in, b_f, g_q, g_k, w_out):
    fw = FOX_WIDTH
    head_of_lane = jnp.arange(fw) // FOX_HEAD_DIM
    e = (head_of_lane[:, None] == jnp.arange(LANES)[None, :]).astype(BF16)
    lane = jnp.arange(FOX_HEADS * LANES)
    hd, within = lane // LANES, lane % LANES
    own_low = hd % 2 == 0
    keep = jnp.where(own_low, within < FOX_HEAD_DIM, within >= FOX_HEAD_DIM)
    part = within - jnp.where(own_low, FOX_HEAD_DIM, 0)
    is_bias = (part >= 0) & (part < 3)
    src = part * FOX_HEADS + hd
    place = ((jnp.arange(LANES)[:, None] == src[None, :]) & is_bias[None, :]).astype(BF16)
    return dict(
        keep=keep.astype(F32)[None, :], ones=is_bias.astype(F32)[None, :], place=place,
        g_mix=_row(g_mix), wq=w_in[:, :fw].astype(BF16), wk=w_in[:, fw:2 * fw].astype(BF16),
        wv=w_in[:, 2 * fw:3 * fw].astype(BF16), wf=_pad_lanes(w_in[:, 3 * fw:], 0, LANES).astype(BF16),
        bf=_pad_lanes(_row(b_f), 0, LANES),
        gq=jnp.tile(_row(g_q), (1, FOX_HEADS)) * (FOX_HEAD_DIM ** -0.5 * LOG2E),
        gk=jnp.tile(_row(g_k), (1, FOX_HEADS)), e=e, et=e.T, wo=w_out.astype(BF16))


def _rope_tables(pos):
    half = MLA_ROPE // 2
    inv_freq = ROPE_THETA ** (-jnp.arange(half, dtype=F32) / half)
    ang = pos.astype(F32)[:, None] * inv_freq[None, :]
    cos, sin = jnp.cos(ang), jnp.sin(ang)
    c = jnp.concatenate([jnp.ones((pos.shape[0], MLA_NOPE), F32), cos, cos,
                         jnp.ones((pos.shape[0], LANES - MLA_QK), F32)], axis=-1)
    s = _pad_lanes(jnp.concatenate([-sin, sin], axis=-1), MLA_NOPE, LANES)
    return c, s


def _pad_rows(x, total):
    return jnp.pad(x, [(0, 0), (0, total - x.shape[1])] + [(0, 0)] * (x.ndim - 2))


def _kv_pad_len(t, past):
    l = past + t
    if past == 0 and t % 512 == 0:
        return l
    return -(-l // LANES) * LANES


def _even_layer(h, past, w, state):
    b, t, d = h.shape
    past_lat, past_krope, h0, conv_prev = state
    q, lat_new, krp_new = _mla_in(h, w, _rope_tables(past + jnp.arange(t)))
    lp = _kv_pad_len(t, past)
    lat_all = _pad_rows(jnp.concatenate([past_lat, lat_new], axis=1), lp)
    krp_all = _pad_rows(jnp.concatenate([_pad_lanes(past_krope, MLA_NOPE, LANES), krp_new], axis=1), lp)
    k, vt = _mla_kv(lat_all, krp_all, w)
    attn = _flash(q, k, vt, past=past, kv_len=past + t, chunk_causal=True)
    conv_prev8 = jnp.pad(conv_prev, ((0, 0), (HALO - (CONV_WIDTH - 1), 0), (0, 0)))
    y_rec, h_last, conv_last = _lru(h, w, conv_prev8, h0[:, None, :])
    new = (lat_new, krp_new[:, :, MLA_NOPE:MLA_QK], h_last[:, 0], conv_last[:, HALO - (CONV_WIDTH - 1):])
    return [attn, y_rec], [w["wo_attn"], w["wo_rec"]], new


def _odd_layer(h, past, w, state):
    b, t, d = h.shape
    past_k, past_v, past_logf = state
    if past > 0:
        ka_past, c_past = _fox_past(past_k.reshape(b, past, FOX_WIDTH),
                                    _pad_lanes(past_logf.astype(F32), 0, LANES), w)
        c0 = c_past[:, past - 1:past, :]
    else:
        ka_past = jnp.zeros((b, 0, FOX_HEADS * LANES), BF16)
        c0 = jnp.zeros((b, 1, LANES), F32)
    q, k32, v32, ka_new, vb, logf, _ = _fox_in(h, w, c0)
    lp = _kv_pad_len(t, past)
    k_all = _pad_rows(jnp.concatenate([ka_past, ka_new], axis=1), lp)
    v_all = _pad_rows(jnp.concatenate([past_v.reshape(b, past, FOX_WIDTH).astype(BF16), vb], axis=1), lp)
    attn = _flash(q, k_all, jnp.swapaxes(v_all, 1, 2), past=past, kv_len=past + t, chunk_causal=False)
    new = (k32.reshape(b, t, FOX_HEADS, FOX_HEAD_DIM), v32.reshape(b, t, FOX_HEADS, FOX_HEAD_DIM),
           logf[:, :, :FOX_HEADS])
    return [attn], [w["wo"]], new


def _trunk(x, past, layers, mem_kvs, even_states, odd_states):
    b, t, d = x.shape
    even_new, odd_new = [], []
    for li, lw in enumerate(layers):
        h = _ffn(x.reshape(b * t, d), lw["ffn1"]).reshape(b, t, d)
        if li % 2 == 0:
            parts, w_parts, new = _even_layer(h, past, lw["mix"], even_states[li // 2])
            even_new.append(new)
        else:
            parts, w_parts, new = _odd_layer(h, past, lw["mix"], odd_states[li // 2])
            odd_new.append(new)
        x = _post_mixer(h, parts, w_parts, mem_kvs[li][0], mem_kvs[li][1], lw["mem"], lw["ffn2"])
    return x, even_new, odd_new


def kernel(x_prompt, x_sample, mem_prompt, cache_mla_latent, cache_mla_krope, state_lru_h, state_lru_conv, cache_fox_k, cache_fox_v, cache_fox_logf, cache_mem_k, cache_mem_v, norm_ffn1, ffn1_w_in, ffn1_w_out, norm_mix, norm_mem, norm_mem_src, mem_w_q, mem_w_kv, mem_w_o, mem_g_q, mem_g_k, norm_ffn2, ffn2_w_in, ffn2_w_out, ev_w_in, ev_g_qlat, ev_g_kvlat, ev_w_uq, ev_w_ukv, ev_g_q, ev_g_k, ev_conv_w, ev_conv_b, ev_gate_w, ev_gate_b, ev_lambda, ev_w_out, od_w_in, od_b_f, od_g_q, od_g_k, od_w_out):
    depth = norm_ffn1.shape[0]
    n_even, n_odd = (depth + 1) // 2, depth // 2
    b, _, _ = x_prompt.shape
    bs = x_sample.shape[0]
    past = cache_mla_latent.shape[2] if n_even else cache_fox_k.shape[2]

    layers = []
    for li in range(depth):
        j = li // 2
        if li % 2 == 0:
            mix = _even_weights(norm_mix[li], ev_w_in[j], ev_g_qlat[j], ev_g_kvlat[j], ev_w_uq[j], ev_w_ukv[j],
                                ev_g_q[j], ev_g_k[j], ev_conv_w[j], ev_conv_b[j], ev_gate_w[j], ev_gate_b[j],
                                ev_lambda[j], ev_w_out[j])
        else:
            mix = _odd_weights(norm_mix[li], od_w_in[j], od_b_f[j], od_g_q[j], od_g_k[j], od_w_out[j])
        layers.append(dict(
            ffn1=_ffn_weights(norm_ffn1[li], ffn1_w_in[li], ffn1_w_out[li]),
            ffn2=_ffn_weights(norm_ffn2[li], ffn2_w_in[li], ffn2_w_out[li]),
            mem=_mem_weights(norm_mem[li], norm_mem_src[li], mem_w_q[li], mem_w_kv[li], mem_w_o[li],
                             mem_g_q[li], mem_g_k[li]),
            mix=mix))

    mem_p = [_mem_kv(mem_prompt, lw["mem"]) for lw in layers]
    ev0 = [(jnp.zeros((b, 0, MLA_KV_LORA), F32), jnp.zeros((b, 0, MLA_ROPE), F32),
            jnp.zeros((b, LRU_WIDTH), F32), jnp.zeros((b, CONV_WIDTH - 1, LRU_WIDTH), F32))
           for _ in range(n_even)]
    od0 = [(jnp.zeros((b, 0, FOX_HEADS, FOX_HEAD_DIM), F32), jnp.zeros((b, 0, FOX_HEADS, FOX_HEAD_DIM), F32),
            jnp.zeros((b, 0, FOX_HEADS), F32)) for _ in range(n_odd)]
    y_prompt, ev_p, od_p = _trunk(x_prompt, 0, layers, [(m[2], m[3]) for m in mem_p], ev0, od0)

    m_tok = cache_mem_k.shape[2]
    mem_s = [(cache_mem_k[li].reshape(bs, m_tok, MEM_WIDTH).astype(BF16),
              cache_mem_v[li].reshape(bs, m_tok, MEM_WIDTH).astype(BF16)) for li in range(depth)]
    ev_s = [(cache_mla_latent[j], cache_mla_krope[j], state_lru_h[j], state_lru_conv[j]) for j in range(n_even)]
    od_s = [(cache_fox_k[j], cache_fox_v[j], cache_fox_logf[j]) for j in range(n_odd)]
    y_sample, ev_n, od_n = _trunk(x_sample, past, layers, mem_s, ev_s, od_s)

    mem_shape = (b, m_tok, MEM_HEADS, MEM_HEAD_DIM)
    p_even = [jnp.stack([s[f] for s in ev_p]) for f in range(4)]
    p_odd = [jnp.stack([s[f] for s in od_p]) for f in range(3)]
    p_mem_k = jnp.stack([m[0].reshape(mem_shape) for m in mem_p])
    p_mem_v = jnp.stack([m[1].reshape(mem_shape) for m in mem_p])
    s_even = [jnp.stack([s[f] for s in ev_n]) for f in range(4)]
    s_odd = [jnp.stack([s[f] for s in od_n]) for f in range(3)]
    return (y_prompt, y_sample, *p_even, *p_odd, p_mem_k, p_mem_v, *s_even, *s_odd)
```

```python
import functools
import math

import jax
import jax.numpy as jnp
from jax import lax
from jax.experimental import pallas as pl
from jax.experimental.pallas import tpu as pltpu

F32 = jnp.float32
BF16 = jnp.bfloat16

NORM_EPS = 1e-6
NEG_INF = -1e30
LOG2E = math.log2(math.e)
CHUNK = 64
LANES = 128
SUBLANES = 8
ROPE_THETA = 10000.0
LRU_C = 8.0
MLA_HEADS = 8
MLA_NOPE = 64
MLA_ROPE = 32
MLA_QK = MLA_NOPE + MLA_ROPE
MLA_V = 64
MLA_Q_LORA = 256
MLA_KV_LORA = 128
LRU_WIDTH = 512
LRU_BLOCKS = 8
CONV_WIDTH = 4
FOX_HEADS = 16
FOX_HEAD_DIM = 64
FOX_WIDTH = FOX_HEADS * FOX_HEAD_DIM
MEM_HEADS = 4
MEM_HEAD_DIM = 128
MEM_WIDTH = MEM_HEADS * MEM_HEAD_DIM
HALO = 8

VMEM_LIMIT = 56 * 1024 * 1024


def _dot(a, b):
    return jnp.dot(a, b, preferred_element_type=F32)


def _dot_nt(a, b):
    return lax.dot_general(a, b, (((1,), (1,)), ((), ())), preferred_element_type=F32)


def _rms(x, g):
    return x * lax.rsqrt(jnp.mean(x * x, axis=-1, keepdims=True) + NORM_EPS) * g


def _head_rms(x, g, n_live):
    ss = jnp.sum(x * x, axis=-1, keepdims=True) * (1.0 / n_live)
    return x * lax.rsqrt(ss + NORM_EPS) * g


def _sigmoid(x):
    return 1.0 / (1.0 + jnp.exp(-x))


def _log1p(y):
    u = 1.0 + y
    d = u - 1.0
    return jnp.where(d == 0.0, y, jnp.log(u) * (y / jnp.where(d == 0.0, 1.0, d)))


def _softplus(x):
    return jnp.maximum(x, 0.0) + _log1p(jnp.exp(-jnp.abs(x)))


def _gelu_tanh(x):
    return 0.5 * x * (1.0 + jnp.tanh(math.sqrt(2.0 / math.pi) * (x + 0.044715 * (x * x * x))))


def _split_bf16(x, parts):
    out = []
    r = x
    for _ in range(parts):
        p = r.astype(BF16)
        out.append(p)
        r = r - p.astype(F32)
    return out


def _const_spec(shape):
    nd = len(shape)
    return pl.BlockSpec(shape, lambda *_: (0,) * nd, pipeline_mode=pl.Buffered(1))


def _params(*sem):
    return pltpu.CompilerParams(dimension_semantics=sem, vmem_limit_bytes=VMEM_LIMIT)


def _row_tile(n, cap):
    t = min(n, cap)
    assert n % t == 0, (n, t)
    return t


FFN_CHUNKS = 2
MXU_TILE = 256


def _ffn_chunk_bounds(f):
    tiles = -(-f // MXU_TILE)
    per = -(-tiles // FFN_CHUNKS) * MXU_TILE
    edges = [min(i * per, f) for i in range(FFN_CHUNKS + 1)]
    return [(lo, hi) for lo, hi in zip(edges[:-1], edges[1:]) if hi > lo]


def _swiglu_half_step(x, g_ref, wg_ref, wu_ref, wo_ref):
    hb = _rms(x, g_ref[...]).astype(BF16)
    acc = jnp.zeros_like(x)
    for lo, hi in _ffn_chunk_bounds(wg_ref.shape[1]):
        sl = slice(lo, hi)
        gate = _dot(hb, wg_ref[:, sl])
        up = _dot(hb, wu_ref[:, sl])
        act = (gate * _sigmoid(gate) * up).astype(BF16)
        acc = acc + _dot(act, wo_ref[sl, :])
    return x + 0.5 * acc


def _ffn_body(x_ref, g_ref, wg_ref, wu_ref, wo_ref, o_ref):
    o_ref[...] = _swiglu_half_step(x_ref[...], g_ref, wg_ref, wu_ref, wo_ref)


def _ffn(x2, w):
    n, d = x2.shape
    f = w["wg"].shape[1]
    tm = _row_tile(n, 512)
    return pl.pallas_call(
        _ffn_body,
        grid=(n // tm,),
        in_specs=[pl.BlockSpec((tm, d), lambda i: (i, 0)), _const_spec((1, d)),
                  _const_spec((d, f)), _const_spec((d, f)), _const_spec((f, d))],
        out_specs=pl.BlockSpec((tm, d), lambda i: (i, 0)),
        out_shape=jax.ShapeDtypeStruct((n, d), F32),
        compiler_params=_params("parallel"),
        name="ffn",
    )(x2, w["g"], w["wg"], w["wu"], w["wo"])


def _mem_cross_attention(h, tm, g_ref, wq_ref, gq_ref, mk_ref, mv_ref, wo_ref):
    hb = _rms(h, g_ref[...]).astype(BF16)
    q = _dot(hb, wq_ref[...])
    rows = []
    for i in range(h.shape[0] // tm):
        outs = []
        for hd in range(MEM_HEADS):
            sl = slice(hd * MEM_HEAD_DIM, (hd + 1) * MEM_HEAD_DIM)
            qh = _head_rms(q[i * tm:(i + 1) * tm, sl], gq_ref[...], MEM_HEAD_DIM).astype(BF16)
            s = _dot_nt(qh, mk_ref[i, :, sl])
            e = jnp.exp(s - jnp.max(s, axis=-1, keepdims=True))
            p = e / jnp.sum(e, axis=-1, keepdims=True)
            outs.append(_dot(p.astype(BF16), mv_ref[i, :, sl]).astype(BF16))
        rows.append(jnp.concatenate(outs, axis=-1))
    o = rows[0] if len(rows) == 1 else jnp.concatenate(rows, axis=0)
    return h + _dot(o, wo_ref[...])


def _post_mixer_body(*refs, n_parts):
    h_ref = refs[0]
    parts = refs[1:1 + n_parts]
    w_parts = refs[1 + n_parts:1 + 2 * n_parts]
    (mg_ref, mwq_ref, mgq_ref, mk_ref, mv_ref, mwo_ref,
     fg_ref, fwg_ref, fwu_ref, fwo_ref, o_ref) = refs[1 + 2 * n_parts:]
    bb, tm, d = h_ref.shape
    h = h_ref[...].reshape(bb * tm, d)
    for p_ref, w_ref in zip(parts, w_parts):
        h = h + _dot(p_ref[...].reshape(bb * tm, p_ref.shape[2]), w_ref[...])
    h = _mem_cross_attention(h, tm, mg_ref, mwq_ref, mgq_ref, mk_ref, mv_ref, mwo_ref)
    o_ref[...] = _swiglu_half_step(h, fg_ref, fwg_ref, fwu_ref, fwo_ref).reshape(bb, tm, d)


def _post_mixer(h, parts, w_parts, mk, mv, wm, wf):
    b, t, d = h.shape
    m = mk.shape[1]
    f = wf["wg"].shape[1]
    tm = _row_tile(t, 512)
    bb = b if b * t <= 512 and tm % SUBLANES == 0 else 1
    row = lambda n: pl.BlockSpec((bb, tm, n), lambda i, j: (i, j, 0))
    mem = lambda: pl.BlockSpec((bb, m, MEM_WIDTH), lambda i, j: (i, 0, 0))
    in_specs = [row(d)] + [row(p.shape[2]) for p in parts] + [_const_spec(w.shape) for w in w_parts]
    in_specs += [_const_spec((1, d)), _const_spec((d, MEM_WIDTH)), _const_spec((1, MEM_HEAD_DIM)), mem(), mem(),
                 _const_spec((MEM_WIDTH, d)),
                 _const_spec((1, d)), _const_spec((d, f)), _const_spec((d, f)), _const_spec((f, d))]
    return pl.pallas_call(
        functools.partial(_post_mixer_body, n_parts=len(parts)),
        grid=(b // bb, t // tm),
        in_specs=in_specs,
        out_specs=row(d),
        out_shape=jax.ShapeDtypeStruct((b, t, d), F32),
        compiler_params=_params("parallel", "parallel"),
        name="post_mixer",
    )(h, *parts, *w_parts, wm["g"], wm["wq"], wm["gq"], mk, mv, wm["wo"],
      wf["g"], wf["wg"], wf["wu"], wf["wo"])


def _mem_kv_body(m_ref, g_ref, wk_ref, wv_ref, gk_ref, k32_ref, v32_ref, kb_ref, vb_ref):
    hb = _rms(m_ref[0], g_ref[...]).astype(BF16)
    k = _dot(hb, wk_ref[...])
    v = _dot(hb, wv_ref[...])
    for hd in range(MEM_HEADS):
        sl = slice(hd * MEM_HEAD_DIM, (hd + 1) * MEM_HEAD_DIM)
        kh = _head_rms(k[:, sl], gk_ref[...], MEM_HEAD_DIM)
        k32_ref[0, :, sl] = kh
        kb_ref[0, :, sl] = kh.astype(BF16)
    v32_ref[0] = v
    vb_ref[0] = v.astype(BF16)


def _mem_kv(mem, w):
    b, m, d = mem.shape
    blk = lambda: pl.BlockSpec((1, m, MEM_WIDTH), lambda i: (i, 0, 0))
    return pl.pallas_call(
        _mem_kv_body,
        grid=(b,),
        in_specs=[pl.BlockSpec((1, m, d), lambda i: (i, 0, 0)), _const_spec((1, d)),
                  _const_spec((d, MEM_WIDTH)), _const_spec((d, MEM_WIDTH)), _const_spec((1, MEM_HEAD_DIM))],
        out_specs=[blk(), blk(), blk(), blk()],
        out_shape=[jax.ShapeDtypeStruct((b, m, MEM_WIDTH), F32), jax.ShapeDtypeStruct((b, m, MEM_WIDTH), F32),
                   jax.ShapeDtypeStruct((b, m, MEM_WIDTH), BF16), jax.ShapeDtypeStruct((b, m, MEM_WIDTH), BF16)],
        compiler_params=_params("parallel"),
        name="mem_kv",
    )(mem, w["g_src"], w["wk"], w["wv"], w["gk"])


def _mla_keys_values(lat, krp, wuk_ref, wuvt_ref, gk_ref, k_ref, vt_ref):
    lb = lat.astype(BF16)
    kn = _dot(lb, wuk_ref[...])
    for hd in range(MLA_HEADS):
        sl = slice(hd * LANES, (hd + 1) * LANES)
        k_ref[0, :, sl] = _head_rms(kn[:, sl] + krp, gk_ref[...], MLA_QK).astype(BF16)
    vt_ref[0] = _dot_nt(wuvt_ref[...], lb).astype(BF16)


def _mla_in_body(h_ref, g_ref, wcq_ref, gql_ref, wuq_ref, wuqs_ref, gq_ref, wckv_ref, gkv_ref, wkr_ref, wkrs_ref,
                 c_ref, s_ref, *rest):
    q_ref, lat_ref, krp_ref = rest[-5:-2] if len(rest) > 3 else rest
    hb = _rms(h_ref[0], g_ref[...]).astype(BF16)
    c, s = c_ref[...], s_ref[...]
    cq = _rms(_dot(hb, wcq_ref[...]), gql_ref[...]).astype(BF16)
    q = _dot(cq, wuq_ref[...])
    q_partner = _dot(cq, wuqs_ref[...])
    for hd in range(MLA_HEADS):
        sl = slice(hd * LANES, (hd + 1) * LANES)
        qh = q[:, sl] * c + q_partner[:, sl] * s
        q_ref[0, :, sl] = _head_rms(qh, gq_ref[...], MLA_QK).astype(BF16)
    lat = _rms(_dot(hb, wckv_ref[...]), gkv_ref[...])
    krp = _dot(hb, wkr_ref[...]) * c + _dot(hb, wkrs_ref[...]) * s
    lat_ref[0] = lat
    krp_ref[0] = krp
    if len(rest) > 3:
        wuk_ref, wuvt_ref, gk_ref = rest[:3]
        _mla_keys_values(lat, krp, wuk_ref, wuvt_ref, gk_ref, *rest[-2:])


def _mla_in(h, w, tables, with_kv):
    b, t, d = h.shape
    tm = _row_tile(t, 512)
    row = lambda n: pl.BlockSpec((1, tm, n), lambda i, j: (i, j, 0))
    tab = lambda: pl.BlockSpec((tm, LANES), lambda i, j: (j, 0))
    vw = MLA_HEADS * MLA_V
    in_specs = [row(d), _const_spec((1, d)),
                _const_spec((d, MLA_Q_LORA)), _const_spec((1, MLA_Q_LORA)),
                _const_spec((MLA_Q_LORA, MLA_HEADS * LANES)), _const_spec((MLA_Q_LORA, MLA_HEADS * LANES)),
                _const_spec((1, LANES)),
                _const_spec((d, MLA_KV_LORA)), _const_spec((1, MLA_KV_LORA)),
                _const_spec((d, LANES)), _const_spec((d, LANES)),
                tab(), tab()]
    args = [h, w["g_mix"], w["wcq"], w["g_qlat"], w["wuq"], w["wuq_swap"], w["gq"], w["wckv"], w["g_kvlat"],
            w["wkr"], w["wkr_swap"], *tables]
    out_specs = [row(MLA_HEADS * LANES), row(MLA_KV_LORA), row(LANES)]
    out_shape = [jax.ShapeDtypeStruct((b, t, MLA_HEADS * LANES), BF16),
                 jax.ShapeDtypeStruct((b, t, MLA_KV_LORA), F32),
                 jax.ShapeDtypeStruct((b, t, LANES), F32)]
    if with_kv:
        in_specs += [_const_spec((MLA_KV_LORA, MLA_HEADS * LANES)), _const_spec((vw, MLA_KV_LORA)),
                     _const_spec((1, LANES))]
        args += [w["wuk"], w["wuvt"], w["gk"]]
        out_specs += [row(MLA_HEADS * LANES), pl.BlockSpec((1, vw, tm), lambda i, j: (i, 0, j))]
        out_shape += [jax.ShapeDtypeStruct((b, t, MLA_HEADS * LANES), BF16),
                      jax.ShapeDtypeStruct((b, vw, t), BF16)]
    return pl.pallas_call(
        _mla_in_body,
        grid=(b, t // tm),
        in_specs=in_specs,
        out_specs=out_specs,
        out_shape=out_shape,
        compiler_params=_params("parallel", "parallel"),
        name="mla_in",
    )(*args)


def _mla_kv_body(lat_ref, krp_ref, wuk_ref, wuvt_ref, gk_ref, k_ref, vt_ref):
    _mla_keys_values(lat_ref[0], krp_ref[0], wuk_ref, wuvt_ref, gk_ref, k_ref, vt_ref)


def _mla_kv(lat, krp, w):
    b, l, _ = lat.shape
    tl = _row_tile(l, 512) if l % 512 == 0 else l
    row = lambda n: pl.BlockSpec((1, tl, n), lambda i, j: (i, j, 0))
    vw = MLA_HEADS * MLA_V
    return pl.pallas_call(
        _mla_kv_body,
        grid=(b, l // tl),
        in_specs=[row(MLA_KV_LORA), row(LANES), _const_spec((MLA_KV_LORA, MLA_HEADS * LANES)),
                  _const_spec((vw, MLA_KV_LORA)), _const_spec((1, LANES))],
        out_specs=[row(MLA_HEADS * LANES), pl.BlockSpec((1, vw, tl), lambda i, j: (i, 0, j))],
        out_shape=[jax.ShapeDtypeStruct((b, l, MLA_HEADS * LANES), BF16),
                   jax.ShapeDtypeStruct((b, vw, l), BF16)],
        compiler_params=_params("parallel", "parallel"),
        name="mla_kv",
    )(lat, krp, w["wuk"], w["wuvt"], w["gk"])


def _lru_body(h_ref, g_ref, wrec_ref, wgate_ref, cw_ref, cb_ref, wr_ref, wi_ref, br_ref, bi_ref, lam_ref,
              cprev_ref, h0_ref, y_ref, hl_ref, cl_ref, buf, a_s, b_s, hcar, *, tm):
    @pl.when(pl.program_id(1) == 0)
    def _():
        buf[0:HALO, :] = cprev_ref[0]
        hcar[...] = h0_ref[0]

    hb = _rms(h_ref[0], g_ref[...]).astype(BF16)
    xr = _dot(hb, wrec_ref[...])
    xg = _dot(hb, wgate_ref[...])
    buf[HALO:HALO + tm, :] = xr
    xc = cb_ref[...] + xr * cw_ref[CONV_WIDTH - 1:CONV_WIDTH, :]
    for j in range(CONV_WIDTH - 1):
        off = HALO - (CONV_WIDTH - 1) + j
        xc = xc + cw_ref[j:j + 1, :] * buf[off:off + tm, :]
    xcb = xc.astype(BF16)
    r = _sigmoid(_dot(xcb, wr_ref[...]) + br_ref[...])
    i = _sigmoid(_dot(xcb, wi_ref[...]) + bi_ref[...])
    log_a = (-LRU_C) * r * _softplus(-lam_ref[...])
    a = jnp.exp(log_a)
    b = jnp.sqrt(-jnp.tanh(log_a) * (a * a + 1.0)) * (i * xc)
    a_s[...] = a
    b_s[...] = b

    row = lax.broadcasted_iota(jnp.int32, (SUBLANES, a.shape[1]), 0)

    def step(g, hprev):
        r0 = pl.multiple_of(g * SUBLANES, SUBLANES)
        ag = a_s[pl.ds(r0, SUBLANES), :]
        bg = b_s[pl.ds(r0, SUBLANES), :]
        d = 1
        while d < SUBLANES:
            keep = row >= d
            a_up = jnp.where(keep, pltpu.roll(ag, d, 0), 1.0)
            b_up = jnp.where(keep, pltpu.roll(bg, d, 0), 0.0)
            bg = ag * b_up + bg
            ag = ag * a_up
            d *= 2
        hg = ag * hprev + bg
        b_s[pl.ds(r0, SUBLANES), :] = hg
        return hg[SUBLANES - 1:, :]

    hfin = lax.fori_loop(0, tm // SUBLANES, step, hcar[...], unroll=2)
    hcar[...] = hfin
    y_ref[0] = (_gelu_tanh(xg) * b_s[...]).astype(BF16)
    hl_ref[0] = hfin
    tail = buf[tm:tm + HALO, :]
    buf[0:HALO, :] = tail
    cl_ref[0] = tail


def _lru(h, w, conv_prev8, h0):
    b, t, d = h.shape
    tm = _row_tile(t, 512)
    wd = LRU_WIDTH
    vec = lambda: _const_spec((1, wd))
    return pl.pallas_call(
        functools.partial(_lru_body, tm=tm),
        grid=(b, t // tm),
        in_specs=[pl.BlockSpec((1, tm, d), lambda i, j: (i, j, 0)), _const_spec((1, d)),
                  _const_spec((d, wd)), _const_spec((d, wd)), _const_spec((CONV_WIDTH, wd)), vec(),
                  _const_spec((wd, wd)), _const_spec((wd, wd)), vec(), vec(), vec(),
                  pl.BlockSpec((1, HALO, wd), lambda i, j: (i, 0, 0)),
                  pl.BlockSpec((1, 1, wd), lambda i, j: (i, 0, 0))],
        out_specs=[pl.BlockSpec((1, tm, wd), lambda i, j: (i, j, 0)),
                   pl.BlockSpec((1, 1, wd), lambda i, j: (i, 0, 0)),
                   pl.BlockSpec((1, HALO, wd), lambda i, j: (i, 0, 0))],
        out_shape=[jax.ShapeDtypeStruct((b, t, wd), BF16), jax.ShapeDtypeStruct((b, 1, wd), F32),
                   jax.ShapeDtypeStruct((b, HALO, wd), F32)],
        scratch_shapes=[pltpu.VMEM((tm + HALO, wd), F32), pltpu.VMEM((tm, wd), F32),
                        pltpu.VMEM((tm, wd), F32), pltpu.VMEM((1, wd), F32)],
        compiler_params=_params("parallel", "arbitrary"),
        name="lru",
    )(h, w["g_mix"], w["wrec"], w["wgate"], w["conv_w"], w["conv_b"], w["wr"], w["wi"], w["br"], w["bi"],
      w["lam"], conv_prev8, h0)


def _group_rms(x, g):
    low = lax.broadcasted_iota(jnp.int32, (1, LANES), 1) < FOX_HEAD_DIM
    out = []
    for j in range(x.shape[1] // LANES):
        xb = x[:, j * LANES:(j + 1) * LANES]
        sq = xb * xb
        s_low = jnp.sum(jnp.where(low, sq, 0.0), axis=-1, keepdims=True)
        s_high = jnp.sum(jnp.where(low, 0.0, sq), axis=-1, keepdims=True)
        ss = jnp.where(low, s_low, s_high)
        out.append(xb * lax.rsqrt(ss * (1.0 / FOX_HEAD_DIM) + NORM_EPS))
    return jnp.concatenate(out, axis=-1) * g


def _cumsum_rows(x, ltri):
    out = None
    for p in _split_bf16(x, 3):
        d = _dot(ltri, p)
        out = d if out is None else out + d
    return out


def _fox_aug(x, keep, bias):
    blocks = [x[:, (hd // 2) * LANES:(hd // 2 + 1) * LANES] for hd in range(FOX_HEADS)]
    return jnp.concatenate(blocks, axis=-1) * keep + bias


def _fox_key_bias(c, place):
    lane = lax.broadcasted_iota(jnp.int32, (1, LANES), 1)
    hi, mid, lo = _split_bf16(jnp.where(lane < FOX_HEADS, c * (-LOG2E), 0.0), 3)
    packed = (hi.astype(F32) + pltpu.roll(mid.astype(F32), FOX_HEADS, 1)
              + pltpu.roll(lo.astype(F32), 2 * FOX_HEADS, 1))
    return _dot(packed.astype(BF16), place)


def _fox_in_body(h_ref, g_ref, wq_ref, wk_ref, wv_ref, wf_ref, bf_ref, gq_ref, gk_ref,
                 keep_ref, ones_ref, place_ref, ltri_ref, c0_ref,
                 q_ref, k32_ref, v32_ref, ka_ref, vb_ref, lf_ref, c_ref, lf_s, carry, *, tm, tc):
    @pl.when(pl.program_id(1) == 0)
    def _():
        carry[...] = c0_ref[0]

    hb = _rms(h_ref[0], g_ref[...]).astype(BF16)
    keep = keep_ref[...]
    q = _group_rms(_dot(hb, wq_ref[...]), gq_ref[...])
    q_ref[0] = _fox_aug(q, keep, ones_ref[...]).astype(BF16)
    k = _group_rms(_dot(hb, wk_ref[...]), gk_ref[...])
    k32_ref[0] = k
    v = _dot(hb, wv_ref[...])
    v32_ref[0] = v
    vb_ref[0] = v.astype(BF16)
    logf = -_softplus(-(_dot(hb, wf_ref[...]) + bf_ref[...]))
    lf_ref[0] = logf
    if tc > tm:
        lf_s[...] = jnp.zeros_like(lf_s)
    lf_s[0:tm, :] = logf
    c = carry[...] + _cumsum_rows(lf_s[...], ltri_ref[...])[0:tm, :]
    c_ref[0] = c
    carry[...] = c[tm - 1:tm, :]
    ka_ref[0] = _fox_aug(k, keep, _fox_key_bias(c, place_ref[...])).astype(BF16)


def _fox_in(h, w, c0):
    b, t, d = h.shape
    tm = _row_tile(t, 512)
    tc = max(tm, LANES)
    ltri = jnp.tril(jnp.ones((tc, tc), F32)).astype(BF16)
    row = lambda n: pl.BlockSpec((1, tm, n), lambda i, j: (i, j, 0))
    fw, aw = FOX_WIDTH, FOX_HEADS * LANES
    return pl.pallas_call(
        functools.partial(_fox_in_body, tm=tm, tc=tc),
        grid=(b, t // tm),
        in_specs=[row(d), _const_spec((1, d)),
                  _const_spec((d, fw)), _const_spec((d, fw)), _const_spec((d, fw)), _const_spec((d, LANES)),
                  _const_spec((1, LANES)), _const_spec((1, fw)), _const_spec((1, fw)),
                  _const_spec((1, aw)), _const_spec((1, aw)), _const_spec((LANES, aw)),
                  _const_spec((tc, tc)), pl.BlockSpec((1, 1, LANES), lambda i, j: (i, 0, 0))],
        out_specs=[row(aw), row(fw), row(fw), row(aw), row(fw), row(LANES), row(LANES)],
        out_shape=[jax.ShapeDtypeStruct((b, t, aw), BF16), jax.ShapeDtypeStruct((b, t, fw), F32),
                   jax.ShapeDtypeStruct((b, t, fw), F32), jax.ShapeDtypeStruct((b, t, aw), BF16),
                   jax.ShapeDtypeStruct((b, t, fw), BF16), jax.ShapeDtypeStruct((b, t, LANES), F32),
                   jax.ShapeDtypeStruct((b, t, LANES), F32)],
        scratch_shapes=[pltpu.VMEM((tc, LANES), F32), pltpu.VMEM((1, LANES), F32)],
        compiler_params=_params("parallel", "arbitrary"),
        name="fox_in",
    )(h, w["g_mix"], w["wq"], w["wk"], w["wv"], w["wf"], w["bf"], w["gq"], w["gk"],
      w["keep"], w["ones"], w["place"], ltri, c0)


def _fox_past_body(k_ref, lf_ref, keep_ref, place_ref, ltri_ref, ka_ref, c_ref, carry):
    @pl.when(pl.program_id(1) == 0)
    def _():
        carry[...] = jnp.zeros_like(carry)

    c = carry[...] + _cumsum_rows(lf_ref[0], ltri_ref[...])
    c_ref[0] = c
    carry[...] = c[c.shape[0] - 1:, :]
    ka_ref[0] = _fox_aug(k_ref[0], keep_ref[...], _fox_key_bias(c, place_ref[...])).astype(BF16)


def _fox_past(past_k, past_logf, w):
    b, p, n = past_logf.shape
    tc = _row_tile(p, 512)
    ltri = jnp.tril(jnp.ones((tc, tc), F32)).astype(BF16)
    fw, aw = FOX_WIDTH, FOX_HEADS * LANES
    row = lambda m: pl.BlockSpec((1, tc, m), lambda i, j: (i, j, 0))
    return pl.pallas_call(
        _fox_past_body,
        grid=(b, p // tc),
        in_specs=[row(fw), row(n), _const_spec((1, aw)), _const_spec((LANES, aw)), _const_spec((tc, tc))],
        out_specs=[row(aw), row(n)],
        out_shape=[jax.ShapeDtypeStruct((b, p, aw), BF16), jax.ShapeDtypeStruct((b, p, n), F32)],
        scratch_shapes=[pltpu.VMEM((1, n), F32)],
        compiler_params=_params("parallel", "arbitrary"),
        name="fox_past",
    )(past_k, past_logf, w["keep"], w["place"], ltri)


def _flash_body(q_ref, k_ref, vt_ref, o_ref, m_s, l_s, acc_s, sa_s, sb_s, *, tq, tqs, tk, tks, fr, fc, n_k, past,
                kv_len, chunk_causal, diag_aligned):
    q_start = past + pl.program_id(2) * tq
    q = q_ref[0]
    m_s[...] = jnp.full_like(m_s, NEG_INF)
    l_s[...] = jnp.zeros_like(l_s)
    acc_s[...] = jnp.zeros_like(acc_s)
    shift = int(math.log2(CHUNK))
    hv = LANES // 2

    def block_kind(r0, rn, c0, cn):
        if chunk_causal:
            k_lo, k_hi, q_lo, q_hi = r0 >> shift, (r0 + rn - 1) >> shift, c0 >> shift, (c0 + cn - 1) >> shift
        else:
            k_lo, k_hi, q_lo, q_hi = r0, r0 + rn - 1, c0, c0 + cn - 1
        return "visible" if k_hi <= q_lo else ("hidden" if k_lo > q_hi else "partial")

    def diag_streams(d):
        out = []
        for r0 in range(0, tk, tks):
            for hh in range(2):
                for c0 in range(0, tq, tqs):
                    halves = [(c, block_kind(d * tk + r0, tks, c, tks)) for c in range(c0, c0 + tqs, tks)]
                    if all(kind == "visible" for _, kind in halves):
                        out.append((hh, r0, tks, c0, tqs, "visible"))
                    else:
                        out += [(hh, r0, tks, c, tks, kind) for c, kind in halves if kind != "hidden"]
        return out

    full_streams = [(hh, r0, fr, c0, fc, "visible") for r0 in range(0, tk, fr) for hh in range(2)
                    for c0 in range(0, tq, fc)]
    mask_streams = [st[:5] + ("partial",) for st in full_streams]

    def score(kblk, stream):
        hh, r0, rn, c0, cn, _ = stream
        head = slice(hh * LANES, (hh + 1) * LANES)
        return _dot_nt(kblk[r0:r0 + rn, head], q[c0:c0 + cn, head])

    def key_block(kt):
        return k_ref[0, pl.ds(pl.multiple_of(jnp.minimum(kt, n_k - 1) * tk, tk), tk), :]

    def absorb(s, stream, kt):
        hh, r0, rn, c0, cn, kind = stream
        cols = slice(c0, c0 + cn)
        k0 = pl.multiple_of(kt * tk, tk)
        if kind == "partial":
            kpos = k0 + r0 + lax.broadcasted_iota(jnp.int32, (rn, cn), 0)
            qpos = q_start + c0 + lax.broadcasted_iota(jnp.int32, (rn, cn), 1)
            if chunk_causal:
                vis = lax.shift_right_logical(kpos, shift) <= lax.shift_right_logical(qpos, shift)
            else:
                vis = kpos <= qpos
            s = jnp.where(jnp.logical_and(vis, kpos < kv_len), s, NEG_INF)
        m_old = m_s[hh, :, cols]
        m_new = jnp.maximum(m_old, jnp.max(s, axis=0, keepdims=True))
        alpha = jnp.exp2(m_old - m_new)
        p = jnp.exp2(s - m_new)
        l_s[hh, :, cols] = alpha * l_s[hh, :, cols] + jnp.sum(p, axis=0, keepdims=True)
        vt = vt_ref[0, hh * hv:(hh + 1) * hv, pl.ds(pl.multiple_of(k0 + r0, LANES), rn)]
        acc_s[hh, :, cols] = alpha * acc_s[hh, :, cols] + _dot(vt, p.astype(BF16))
        m_s[hh, :, cols] = m_new

    def single_tile(kt, streams, after_first_scores=None):
        kblk = key_block(kt)
        ahead = 2
        pending = [score(kblk, st) for st in streams[:ahead]]
        if after_first_scores is not None:
            after_first_scores()
        for idx, st in enumerate(streams):
            s = pending.pop(0)
            if idx + ahead < len(streams):
                pending.append(score(kblk, streams[idx + ahead]))
            absorb(s, st, kt)

    def store_scores(buf, kt):
        kblk = key_block(kt)
        for st in full_streams:
            hh, r0, rn, c0, cn, _ = st
            buf[hh, r0:r0 + rn, c0:c0 + cn] = score(kblk, st)

    def absorb_stored(buf, kt):
        for st in full_streams:
            hh, r0, rn, c0, cn, _ = st
            absorb(buf[hh, r0:r0 + rn, c0:c0 + cn], st, kt)

    def tile_pair(i, carry):
        kt = 2 * i
        store_scores(sb_s, kt + 1)
        absorb_stored(sa_s, kt)
        store_scores(sa_s, kt + 2)
        absorb_stored(sb_s, kt + 1)
        return carry

    def full_tile(kt, carry):
        single_tile(kt, full_streams)
        return carry

    def masked_tile(kt, carry):
        single_tile(kt, mask_streams)
        return carry

    n_full = jnp.minimum(q_start // tk, kv_len // tk)
    q_last = q_start + tq - 1
    k_hi = (q_last // CHUNK + 1) * CHUNK if chunk_causal else q_last + 1
    n_end = jnp.minimum((k_hi + tk - 1) // tk, n_k)
    n_pair = n_full // 2
    if diag_aligned:
        for d in range(tq // tk):
            single_tile(n_full + d, diag_streams(d),
                        after_first_scores=(lambda: store_scores(sa_s, 0)) if d == 0 else None)
        lax.fori_loop(0, n_pair, tile_pair, 0)
        lax.fori_loop(2 * n_pair, n_full, full_tile, 0)
    else:
        @pl.when(n_pair > 0)
        def _():
            store_scores(sa_s, 0)

        lax.fori_loop(0, n_pair, tile_pair, 0)
        lax.fori_loop(2 * n_pair, n_full, full_tile, 0)
        lax.fori_loop(n_full, n_end, masked_tile, 0)
    out_t = jnp.concatenate([acc_s[0] / l_s[0], acc_s[1] / l_s[1]], axis=0)
    o_ref[0] = out_t.T.astype(BF16)


def _flash(q, k, vt, *, past, kv_len, chunk_causal):
    b, t, w = q.shape
    lp = k.shape[1]
    n_pairs = w // (2 * LANES)
    t_pad = max(t, LANES)
    if t_pad > t:
        q = _pad_rows(q, t_pad)
    if past == 0 and t % 512 == 0 and lp % 512 == 0:
        tk = 512
        tq = 1024 if t % 1024 == 0 else 512
    elif t_pad == LANES:
        tq, tk = LANES, lp
    else:
        tq = tk = LANES
    assert t_pad % tq == 0 and lp % tk == 0, (t, lp, tq, tk)
    tks = 256 if tk % 256 == 0 and tq > LANES else tk
    tqs = min(tq, 512)
    fr, fc = (tk, 256) if tks < tk and tq % 256 == 0 else (tks, tqs)
    out = pl.pallas_call(
        functools.partial(_flash_body, tq=tq, tqs=tqs, tk=tk, tks=tks, fr=fr, fc=fc, n_k=lp // tk, past=past,
                          kv_len=kv_len, chunk_causal=chunk_causal,
                          diag_aligned=(past == 0 and tq % tk == 0 and kv_len == lp and tks < tk)),
        grid=(b, n_pairs, t_pad // tq),
        in_specs=[pl.BlockSpec((1, tq, 2 * LANES), lambda i, j, s: (i, s, j)),
                  pl.BlockSpec((1, lp, 2 * LANES), lambda i, j, s: (i, 0, j)),
                  pl.BlockSpec((1, LANES, lp), lambda i, j, s: (i, j, 0))],
        out_specs=pl.BlockSpec((1, tq, LANES), lambda i, j, s: (i, s, j)),
        out_shape=jax.ShapeDtypeStruct((b, t_pad, n_pairs * LANES), BF16),
        scratch_shapes=[pltpu.VMEM((2, 1, tq), F32), pltpu.VMEM((2, 1, tq), F32),
                        pltpu.VMEM((2, LANES // 2, tq), F32),
                        pltpu.VMEM((2, tk, tq), F32), pltpu.VMEM((2, tk, tq), F32)],
        compiler_params=_params("parallel", "parallel", "arbitrary"),
        name="flash_mla" if chunk_causal else "flash_fox",
    )(q, k, vt)
    return out[:, :t]


def _row(v):
    return v.reshape(1, -1).astype(F32)


def _pad_lanes(x, lo, total):
    pad = [(0, 0)] * (x.ndim - 1) + [(lo, total - lo - x.shape[-1])]
    return jnp.pad(x, pad)


def _ffn_weights(g, w_in, w_out):
    f = w_out.shape[0]
    return dict(g=_row(g), wg=w_in[:, :f].astype(BF16), wu=w_in[:, f:].astype(BF16), wo=w_out.astype(BF16))


def _mem_weights(g, g_src, w_q, w_kv, w_o, g_q, g_k):
    d = w_q.shape[0]
    kv = w_kv.reshape(d, MEM_HEADS, 2, MEM_HEAD_DIM)
    return dict(g=_row(g), g_src=_row(g_src), wq=w_q.astype(BF16), wo=w_o.astype(BF16),
                wk=kv[:, :, 0].reshape(d, MEM_WIDTH).astype(BF16),
                wv=kv[:, :, 1].reshape(d, MEM_WIDTH).astype(BF16),
                gq=_row(g_q) * (MEM_HEAD_DIM ** -0.5), gk=_row(g_k))


def _even_weights(g_mix, w_in, g_qlat, g_kvlat, w_uq, w_ukv, g_q, g_k, conv_w, conv_b, gate_w, gate_b, lam, w_out):
    d = w_in.shape[0]
    o1 = MLA_Q_LORA
    o2 = o1 + MLA_KV_LORA
    o3 = o2 + MLA_ROPE
    o4 = o3 + LRU_WIDTH
    half = MLA_ROPE // 2
    swap_halves = lambda r: jnp.concatenate([r[..., half:], r[..., :half]], axis=-1)
    uq3 = w_uq.reshape(MLA_Q_LORA, MLA_HEADS, MLA_QK)
    uq = _pad_lanes(uq3, 0, LANES)
    uq_swap = _pad_lanes(swap_halves(uq3[:, :, MLA_NOPE:]), MLA_NOPE, LANES)
    kr = w_in[:, o2:o3]
    ukv = w_ukv.reshape(MLA_KV_LORA, MLA_HEADS, MLA_NOPE + MLA_V)
    uk = _pad_lanes(ukv[:, :, :MLA_NOPE], 0, LANES)
    blk = LRU_WIDTH // LRU_BLOCKS
    eye = jnp.eye(LRU_BLOCKS, dtype=F32)
    wr = jnp.einsum("ncd,nm->ncmd", gate_w[:, :, :blk], eye).reshape(LRU_WIDTH, LRU_WIDTH)
    wi = jnp.einsum("ncd,nm->ncmd", gate_w[:, :, blk:], eye).reshape(LRU_WIDTH, LRU_WIDTH)
    return dict(
        g_mix=_row(g_mix), wcq=w_in[:, :o1].astype(BF16), g_qlat=_row(g_qlat),
        wuq=uq.reshape(MLA_Q_LORA, MLA_HEADS * LANES).astype(BF16),
        wuq_swap=uq_swap.reshape(MLA_Q_LORA, MLA_HEADS * LANES).astype(BF16),
        gq=_pad_lanes(_row(g_q), 0, LANES) * (MLA_QK ** -0.5 * LOG2E),
        wckv=w_in[:, o1:o2].astype(BF16), g_kvlat=_row(g_kvlat),
        wkr=_pad_lanes(kr, MLA_NOPE, LANES).astype(BF16),
        wkr_swap=_pad_lanes(swap_halves(kr), MLA_NOPE, LANES).astype(BF16),
        wuk=uk.reshape(MLA_KV_LORA, MLA_HEADS * LANES).astype(BF16),
        wuvt=ukv[:, :, MLA_NOPE:].reshape(MLA_KV_LORA, MLA_HEADS * MLA_V).T.astype(BF16),
        gk=_pad_lanes(_row(g_k), 0, LANES),
        wrec=w_in[:, o3:o4].astype(BF16), wgate=w_in[:, o4:].astype(BF16),
        conv_w=conv_w.astype(F32), conv_b=_row(conv_b), wr=wr.astype(BF16), wi=wi.astype(BF16),
        br=_row(gate_b[:, :blk]), bi=_row(gate_b[:, blk:]), lam=_row(lam),
        wo_attn=w_out[:MLA_HEADS * MLA_V].astype(BF16), wo_rec=w_out[MLA_HEADS * MLA_V:].astype(BF16))


def _odd_weights(g_mix, w_in, b_f, g_q, g_k, w_out):
    fw = FOX_WIDTH
    lane = jnp.arange(FOX_HEADS * LANES)
    hd, within = lane // LANES, lane % LANES
    own_low = hd % 2 == 0
    keep = jnp.where(own_low, within < FOX_HEAD_DIM, within >= FOX_HEAD_DIM)
    part = within - jnp.where(own_low, FOX_HEAD_DIM, 0)
    is_bias = (part >= 0) & (part < 3)
    src = part * FOX_HEADS + hd
    place = ((jnp.arange(LANES)[:, None] == src[None, :]) & is_bias[None, :]).astype(BF16)
    return dict(
        keep=keep.astype(F32)[None, :], ones=is_bias.astype(F32)[None, :], place=place,
        g_mix=_row(g_mix), wq=w_in[:, :fw].astype(BF16), wk=w_in[:, fw:2 * fw].astype(BF16),
        wv=w_in[:, 2 * fw:3 * fw].astype(BF16), wf=_pad_lanes(w_in[:, 3 * fw:], 0, LANES).astype(BF16),
        bf=_pad_lanes(_row(b_f), 0, LANES),
        gq=jnp.tile(_row(g_q), (1, FOX_HEADS)) * (FOX_HEAD_DIM ** -0.5 * LOG2E),
        gk=jnp.tile(_row(g_k), (1, FOX_HEADS)), wo=w_out.astype(BF16))


def _rope_tables(pos):
    half = MLA_ROPE // 2
    inv_freq = ROPE_THETA ** (-jnp.arange(half, dtype=F32) / half)
    ang = pos.astype(F32)[:, None] * inv_freq[None, :]
    cos, sin = jnp.cos(ang), jnp.sin(ang)
    c = jnp.concatenate([jnp.ones((pos.shape[0], MLA_NOPE), F32), cos, cos,
                         jnp.ones((pos.shape[0], LANES - MLA_QK), F32)], axis=-1)
    s = _pad_lanes(jnp.concatenate([-sin, sin], axis=-1), MLA_NOPE, LANES)
    return c, s


def _pad_rows(x, total):
    return jnp.pad(x, [(0, 0), (0, total - x.shape[1])] + [(0, 0)] * (x.ndim - 2))


def _kv_pad_len(t, past):
    l = past + t
    if past == 0 and t % 512 == 0:
        return l
    return -(-l // LANES) * LANES


def _even_layer(h, past, w, state):
    b, t, d = h.shape
    past_lat, past_krope, h0, conv_prev = state
    lp = _kv_pad_len(t, past)
    tables = _rope_tables(past + jnp.arange(t))
    if past == 0 and lp == t and t % LANES == 0:
        q, lat_new, krp_new, k, vt = _mla_in(h, w, tables, with_kv=True)
    else:
        q, lat_new, krp_new = _mla_in(h, w, tables, with_kv=False)
        lat_all = _pad_rows(jnp.concatenate([past_lat, lat_new], axis=1), lp)
        krp_all = _pad_rows(jnp.concatenate([_pad_lanes(past_krope, MLA_NOPE, LANES), krp_new], axis=1), lp)
        k, vt = _mla_kv(lat_all, krp_all, w)
    attn = _flash(q, k, vt, past=past, kv_len=past + t, chunk_causal=True)
    conv_prev8 = jnp.pad(conv_prev, ((0, 0), (HALO - (CONV_WIDTH - 1), 0), (0, 0)))
    y_rec, h_last, conv_last = _lru(h, w, conv_prev8, h0[:, None, :])
    new = (lat_new, krp_new[:, :, MLA_NOPE:MLA_QK], h_last[:, 0], conv_last[:, HALO - (CONV_WIDTH - 1):])
    return [attn, y_rec], [w["wo_attn"], w["wo_rec"]], new


def _odd_layer(h, past, w, state):
    b, t, d = h.shape
    past_k, past_v, past_logf = state
    if past > 0:
        ka_past, c_past = _fox_past(past_k.reshape(b, past, FOX_WIDTH),
                                    _pad_lanes(past_logf.astype(F32), 0, LANES), w)
        c0 = c_past[:, past - 1:past, :]
    else:
        ka_past = jnp.zeros((b, 0, FOX_HEADS * LANES), BF16)
        c0 = jnp.zeros((b, 1, LANES), F32)
    q, k32, v32, ka_new, vb, logf, _ = _fox_in(h, w, c0)
    lp = _kv_pad_len(t, past)
    k_all = _pad_rows(jnp.concatenate([ka_past, ka_new], axis=1), lp)
    v_all = _pad_rows(jnp.concatenate([past_v.reshape(b, past, FOX_WIDTH).astype(BF16), vb], axis=1), lp)
    attn = _flash(q, k_all, jnp.swapaxes(v_all, 1, 2), past=past, kv_len=past + t, chunk_causal=False)
    new = (k32.reshape(b, t, FOX_HEADS, FOX_HEAD_DIM), v32.reshape(b, t, FOX_HEADS, FOX_HEAD_DIM),
           logf[:, :, :FOX_HEADS])
    return [attn], [w["wo"]], new


def _trunk(x, past, layers, mem_kvs, even_states, odd_states):
    b, t, d = x.shape
    even_new, odd_new = [], []
    for li, lw in enumerate(layers):
        h = _ffn(x.reshape(b * t, d), lw["ffn1"]).reshape(b, t, d)
        if li % 2 == 0:
            parts, w_parts, new = _even_layer(h, past, lw["mix"], even_states[li // 2])
            even_new.append(new)
        else:
            parts, w_parts, new = _odd_layer(h, past, lw["mix"], odd_states[li // 2])
            odd_new.append(new)
        x = _post_mixer(h, parts, w_parts, mem_kvs[li][0], mem_kvs[li][1], lw["mem"], lw["ffn2"])
    return x, even_new, odd_new


def kernel(x_prompt, x_sample, mem_prompt, cache_mla_latent, cache_mla_krope, state_lru_h, state_lru_conv, cache_fox_k, cache_fox_v, cache_fox_logf, cache_mem_k, cache_mem_v, norm_ffn1, ffn1_w_in, ffn1_w_out, norm_mix, norm_mem, norm_mem_src, mem_w_q, mem_w_kv, mem_w_o, mem_g_q, mem_g_k, norm_ffn2, ffn2_w_in, ffn2_w_out, ev_w_in, ev_g_qlat, ev_g_kvlat, ev_w_uq, ev_w_ukv, ev_g_q, ev_g_k, ev_conv_w, ev_conv_b, ev_gate_w, ev_gate_b, ev_lambda, ev_w_out, od_w_in, od_b_f, od_g_q, od_g_k, od_w_out):
    depth = norm_ffn1.shape[0]
    n_even, n_odd = (depth + 1) // 2, depth // 2
    b, _, _ = x_prompt.shape
    bs = x_sample.shape[0]
    past = cache_mla_latent.shape[2] if n_even else cache_fox_k.shape[2]

    layers = []
    for li in range(depth):
        j = li // 2
        if li % 2 == 0:
            mix = _even_weights(norm_mix[li], ev_w_in[j], ev_g_qlat[j], ev_g_kvlat[j], ev_w_uq[j], ev_w_ukv[j],
                                ev_g_q[j], ev_g_k[j], ev_conv_w[j], ev_conv_b[j], ev_gate_w[j], ev_gate_b[j],
                                ev_lambda[j], ev_w_out[j])
        else:
            mix = _odd_weights(norm_mix[li], od_w_in[j], od_b_f[j], od_g_q[j], od_g_k[j], od_w_out[j])
        layers.append(dict(
            ffn1=_ffn_weights(norm_ffn1[li], ffn1_w_in[li], ffn1_w_out[li]),
            ffn2=_ffn_weights(norm_ffn2[li], ffn2_w_in[li], ffn2_w_out[li]),
            mem=_mem_weights(norm_mem[li], norm_mem_src[li], mem_w_q[li], mem_w_kv[li], mem_w_o[li],
                             mem_g_q[li], mem_g_k[li]),
            mix=mix))

    mem_p = [_mem_kv(mem_prompt, lw["mem"]) for lw in layers]
    ev0 = [(jnp.zeros((b, 0, MLA_KV_LORA), F32), jnp.zeros((b, 0, MLA_ROPE), F32),
            jnp.zeros((b, LRU_WIDTH), F32), jnp.zeros((b, CONV_WIDTH - 1, LRU_WIDTH), F32))
           for _ in range(n_even)]
    od0 = [(jnp.zeros((b, 0, FOX_HEADS, FOX_HEAD_DIM), F32), jnp.zeros((b, 0, FOX_HEADS, FOX_HEAD_DIM), F32),
            jnp.zeros((b, 0, FOX_HEADS), F32)) for _ in range(n_odd)]
    y_prompt, ev_p, od_p = _trunk(x_prompt, 0, layers, [(m[2], m[3]) for m in mem_p], ev0, od0)

    m_tok = cache_mem_k.shape[2]
    mem_s = [(cache_mem_k[li].reshape(bs, m_tok, MEM_WIDTH).astype(BF16),
              cache_mem_v[li].reshape(bs, m_tok, MEM_WIDTH).astype(BF16)) for li in range(depth)]
    ev_s = [(cache_mla_latent[j], cache_mla_krope[j], state_lru_h[j], state_lru_conv[j]) for j in range(n_even)]
    od_s = [(cache_fox_k[j], cache_fox_v[j], cache_fox_logf[j]) for j in range(n_odd)]
    y_sample, ev_n, od_n = _trunk(x_sample, past, layers, mem_s, ev_s, od_s)

    mem_shape = (b, m_tok, MEM_HEADS, MEM_HEAD_DIM)
    p_even = [jnp.stack([s[f] for s in ev_p]) for f in range(4)]
    p_odd = [jnp.stack([s[f] for s in od_p]) for f in range(3)]
    p_mem_k = jnp.stack([m[0].reshape(mem_shape) for m in mem_p])
    p_mem_v = jnp.stack([m[1].reshape(mem_shape) for m in mem_p])
    s_even = [jnp.stack([s[f] for s in ev_n]) for f in range(4)]
    s_odd = [jnp.stack([s[f] for s in od_n]) for f in range(3)]
    return (y_prompt, y_sample, *p_even, *p_odd, p_mem_k, p_mem_v, *s_even, *s_odd)
```

```python
import functools
import math

import jax
import jax.numpy as jnp
from jax import lax
from jax.experimental import pallas as pl
from jax.experimental.pallas import tpu as pltpu

F32 = jnp.float32
BF16 = jnp.bfloat16

NORM_EPS = 1e-6
NEG_INF = -1e30
LOG2E = math.log2(math.e)
CHUNK = 64
LANES = 128
SUBLANES = 8
ROPE_THETA = 10000.0
LRU_C = 8.0
MLA_HEADS = 8
MLA_NOPE = 64
MLA_ROPE = 32
MLA_QK = MLA_NOPE + MLA_ROPE
MLA_V = 64
MLA_Q_LORA = 256
MLA_KV_LORA = 128
LRU_WIDTH = 512
LRU_BLOCKS = 8
CONV_WIDTH = 4
FOX_HEADS = 16
FOX_HEAD_DIM = 64
FOX_WIDTH = FOX_HEADS * FOX_HEAD_DIM
MEM_HEADS = 4
MEM_HEAD_DIM = 128
MEM_WIDTH = MEM_HEADS * MEM_HEAD_DIM
HALO = 8

MXU_TILE = 256
ROW_TILE = 512
KEY_TILE = 512
QUERY_TILE = 1024

VMEM_LIMIT = 56 * 1024 * 1024


def _dot(a, b):
    return jnp.dot(a, b, preferred_element_type=F32)


def _dot_nt(a, b):
    return lax.dot_general(a, b, (((1,), (1,)), ((), ())), preferred_element_type=F32)


def _rms(x, g):
    return x * lax.rsqrt(jnp.mean(x * x, axis=-1, keepdims=True) + NORM_EPS) * g


def _head_rms(x, g, n_live):
    ss = jnp.sum(x * x, axis=-1, keepdims=True) * (1.0 / n_live)
    return x * lax.rsqrt(ss + NORM_EPS) * g


def _sigmoid(x):
    return 1.0 / (1.0 + jnp.exp(-x))


def _log1p(y):
    u = 1.0 + y
    d = u - 1.0
    return jnp.where(d == 0.0, y, jnp.log(u) * (y / jnp.where(d == 0.0, 1.0, d)))


def _softplus(x):
    return jnp.maximum(x, 0.0) + _log1p(jnp.exp(-jnp.abs(x)))


def _gelu_tanh(x):
    return 0.5 * x * (1.0 + jnp.tanh(math.sqrt(2.0 / math.pi) * (x + 0.044715 * (x * x * x))))


def _split_bf16(x, parts):
    out = []
    r = x
    for _ in range(parts):
        p = r.astype(BF16)
        out.append(p)
        r = r - p.astype(F32)
    return out


def _const_spec(shape):
    nd = len(shape)
    return pl.BlockSpec(shape, lambda *_: (0,) * nd, pipeline_mode=pl.Buffered(1))


def _params(*sem):
    return pltpu.CompilerParams(dimension_semantics=sem, vmem_limit_bytes=VMEM_LIMIT)


def _row_tile(n, cap):
    t = min(n, cap)
    assert n % t == 0, (n, t)
    return t


FFN_CHUNKS = 2


def _ffn_chunk_bounds(f):
    tiles = -(-f // MXU_TILE)
    per = -(-tiles // FFN_CHUNKS) * MXU_TILE
    edges = [min(i * per, f) for i in range(FFN_CHUNKS + 1)]
    return [(lo, hi) for lo, hi in zip(edges[:-1], edges[1:]) if hi > lo]


def _swiglu_half_step(x, g_ref, wg_ref, wu_ref, wo_ref):
    hb = _rms(x, g_ref[...]).astype(BF16)
    acc = jnp.zeros_like(x)
    for lo, hi in _ffn_chunk_bounds(wg_ref.shape[1]):
        sl = slice(lo, hi)
        gate = _dot(hb, wg_ref[:, sl])
        up = _dot(hb, wu_ref[:, sl])
        act = (gate * _sigmoid(gate) * up).astype(BF16)
        acc = acc + _dot(act, wo_ref[sl, :])
    return x + 0.5 * acc


def _ffn_body(x_ref, g_ref, wg_ref, wu_ref, wo_ref, o_ref):
    o_ref[...] = _swiglu_half_step(x_ref[...], g_ref, wg_ref, wu_ref, wo_ref)


def _ffn(x2, w):
    n, d = x2.shape
    f = w["wg"].shape[1]
    tm = _row_tile(n, ROW_TILE)
    return pl.pallas_call(
        _ffn_body,
        grid=(n // tm,),
        in_specs=[pl.BlockSpec((tm, d), lambda i: (i, 0)), _const_spec((1, d)),
                  _const_spec((d, f)), _const_spec((d, f)), _const_spec((f, d))],
        out_specs=pl.BlockSpec((tm, d), lambda i: (i, 0)),
        out_shape=jax.ShapeDtypeStruct((n, d), F32),
        compiler_params=_params("parallel"),
        name="ffn",
    )(x2, w["g"], w["wg"], w["wu"], w["wo"])


def _mem_cross_attention(h, tm, g_ref, wq_ref, gq_ref, mk_ref, mv_ref, wo_ref):
    hb = _rms(h, g_ref[...]).astype(BF16)
    q = _dot(hb, wq_ref[...])
    rows = []
    for i in range(h.shape[0] // tm):
        outs = []
        for hd in range(MEM_HEADS):
            sl = slice(hd * MEM_HEAD_DIM, (hd + 1) * MEM_HEAD_DIM)
            qh = _head_rms(q[i * tm:(i + 1) * tm, sl], gq_ref[...], MEM_HEAD_DIM).astype(BF16)
            s = _dot_nt(qh, mk_ref[i, :, sl])
            e = jnp.exp(s - jnp.max(s, axis=-1, keepdims=True))
            p = e / jnp.sum(e, axis=-1, keepdims=True)
            outs.append(_dot(p.astype(BF16), mv_ref[i, :, sl]).astype(BF16))
        rows.append(jnp.concatenate(outs, axis=-1))
    o = rows[0] if len(rows) == 1 else jnp.concatenate(rows, axis=0)
    return h + _dot(o, wo_ref[...])


def _post_mixer_body(*refs, n_parts):
    h_ref = refs[0]
    parts = refs[1:1 + n_parts]
    w_parts = refs[1 + n_parts:1 + 2 * n_parts]
    (mg_ref, mwq_ref, mgq_ref, mk_ref, mv_ref, mwo_ref,
     fg_ref, fwg_ref, fwu_ref, fwo_ref, o_ref) = refs[1 + 2 * n_parts:]
    bb, tm, d = h_ref.shape
    h = h_ref[...].reshape(bb * tm, d)
    for p_ref, w_ref in zip(parts, w_parts):
        h = h + _dot(p_ref[...].reshape(bb * tm, p_ref.shape[2]), w_ref[...])
    h = _mem_cross_attention(h, tm, mg_ref, mwq_ref, mgq_ref, mk_ref, mv_ref, mwo_ref)
    o_ref[...] = _swiglu_half_step(h, fg_ref, fwg_ref, fwu_ref, fwo_ref).reshape(bb, tm, d)


def _post_mixer(h, parts, w_parts, mk, mv, wm, wf):
    b, t, d = h.shape
    m = mk.shape[1]
    f = wf["wg"].shape[1]
    tm = _row_tile(t, ROW_TILE)
    bb = b if b * t <= ROW_TILE and tm % SUBLANES == 0 else 1
    row = lambda n: pl.BlockSpec((bb, tm, n), lambda i, j: (i, j, 0))
    mem = lambda: pl.BlockSpec((bb, m, MEM_WIDTH), lambda i, j: (i, 0, 0))
    in_specs = [row(d)] + [row(p.shape[2]) for p in parts] + [_const_spec(w.shape) for w in w_parts]
    in_specs += [_const_spec((1, d)), _const_spec((d, MEM_WIDTH)), _const_spec((1, MEM_HEAD_DIM)), mem(), mem(),
                 _const_spec((MEM_WIDTH, d)),
                 _const_spec((1, d)), _const_spec((d, f)), _const_spec((d, f)), _const_spec((f, d))]
    return pl.pallas_call(
        functools.partial(_post_mixer_body, n_parts=len(parts)),
        grid=(b // bb, t // tm),
        in_specs=in_specs,
        out_specs=row(d),
        out_shape=jax.ShapeDtypeStruct((b, t, d), F32),
        compiler_params=_params("parallel", "parallel"),
        name="post_mixer",
    )(h, *parts, *w_parts, wm["g"], wm["wq"], wm["gq"], mk, mv, wm["wo"],
      wf["g"], wf["wg"], wf["wu"], wf["wo"])


def _mem_kv_body(m_ref, g_ref, wk_ref, wv_ref, gk_ref, k32_ref, v32_ref, kb_ref, vb_ref):
    hb = _rms(m_ref[0], g_ref[...]).astype(BF16)
    k = _dot(hb, wk_ref[...])
    v = _dot(hb, wv_ref[...])
    for hd in range(MEM_HEADS):
        sl = slice(hd * MEM_HEAD_DIM, (hd + 1) * MEM_HEAD_DIM)
        kh = _head_rms(k[:, sl], gk_ref[...], MEM_HEAD_DIM)
        k32_ref[0, :, sl] = kh
        kb_ref[0, :, sl] = kh.astype(BF16)
    v32_ref[0] = v
    vb_ref[0] = v.astype(BF16)


def _mem_kv(mem, w):
    b, m, d = mem.shape
    blk = lambda: pl.BlockSpec((1, m, MEM_WIDTH), lambda i: (i, 0, 0))
    return pl.pallas_call(
        _mem_kv_body,
        grid=(b,),
        in_specs=[pl.BlockSpec((1, m, d), lambda i: (i, 0, 0)), _const_spec((1, d)),
                  _const_spec((d, MEM_WIDTH)), _const_spec((d, MEM_WIDTH)), _const_spec((1, MEM_HEAD_DIM))],
        out_specs=[blk(), blk(), blk(), blk()],
        out_shape=[jax.ShapeDtypeStruct((b, m, MEM_WIDTH), F32), jax.ShapeDtypeStruct((b, m, MEM_WIDTH), F32),
                   jax.ShapeDtypeStruct((b, m, MEM_WIDTH), BF16), jax.ShapeDtypeStruct((b, m, MEM_WIDTH), BF16)],
        compiler_params=_params("parallel"),
        name="mem_kv",
    )(mem, w["g_src"], w["wk"], w["wv"], w["gk"])


def _mla_keys_values(lat, krp, wuk_ref, wuvt_ref, gk_ref, k_ref, vt_ref):
    lb = lat.astype(BF16)
    kn = _dot(lb, wuk_ref[...])
    for hd in range(MLA_HEADS):
        sl = slice(hd * LANES, (hd + 1) * LANES)
        k_ref[0, :, sl] = _head_rms(kn[:, sl] + krp, gk_ref[...], MLA_QK).astype(BF16)
    vt_ref[0] = _dot_nt(wuvt_ref[...], lb).astype(BF16)


def _mla_in_body(h_ref, g_ref, wcq_ref, gql_ref, wuq_ref, wuqs_ref, gq_ref, wckv_ref, gkv_ref, wkr_ref, wkrs_ref,
                 c_ref, s_ref, *rest):
    q_ref, lat_ref, krp_ref = rest[-5:-2] if len(rest) > 3 else rest
    hb = _rms(h_ref[0], g_ref[...]).astype(BF16)
    c, s = c_ref[...], s_ref[...]
    cq = _rms(_dot(hb, wcq_ref[...]), gql_ref[...]).astype(BF16)
    q = _dot(cq, wuq_ref[...])
    q_partner = _dot(cq, wuqs_ref[...])
    for hd in range(MLA_HEADS):
        sl = slice(hd * LANES, (hd + 1) * LANES)
        qh = q[:, sl] * c + q_partner[:, sl] * s
        q_ref[0, :, sl] = _head_rms(qh, gq_ref[...], MLA_QK).astype(BF16)
    lat = _rms(_dot(hb, wckv_ref[...]), gkv_ref[...])
    krp = _dot(hb, wkr_ref[...]) * c + _dot(hb, wkrs_ref[...]) * s
    lat_ref[0] = lat
    krp_ref[0] = krp
    if len(rest) > 3:
        wuk_ref, wuvt_ref, gk_ref = rest[:3]
        _mla_keys_values(lat, krp, wuk_ref, wuvt_ref, gk_ref, *rest[-2:])


def _mla_in(h, w, tables, with_kv):
    b, t, d = h.shape
    tm = _row_tile(t, ROW_TILE)
    row = lambda n: pl.BlockSpec((1, tm, n), lambda i, j: (i, j, 0))
    tab = lambda: pl.BlockSpec((tm, LANES), lambda i, j: (j, 0))
    vw = MLA_HEADS * MLA_V
    in_specs = [row(d), _const_spec((1, d)),
                _const_spec((d, MLA_Q_LORA)), _const_spec((1, MLA_Q_LORA)),
                _const_spec((MLA_Q_LORA, MLA_HEADS * LANES)), _const_spec((MLA_Q_LORA, MLA_HEADS * LANES)),
                _const_spec((1, LANES)),
                _const_spec((d, MLA_KV_LORA)), _const_spec((1, MLA_KV_LORA)),
                _const_spec((d, LANES)), _const_spec((d, LANES)),
                tab(), tab()]
    args = [h, w["g_mix"], w["wcq"], w["g_qlat"], w["wuq"], w["wuq_swap"], w["gq"], w["wckv"], w["g_kvlat"],
            w["wkr"], w["wkr_swap"], *tables]
    out_specs = [row(MLA_HEADS * LANES), row(MLA_KV_LORA), row(LANES)]
    out_shape = [jax.ShapeDtypeStruct((b, t, MLA_HEADS * LANES), BF16),
                 jax.ShapeDtypeStruct((b, t, MLA_KV_LORA), F32),
                 jax.ShapeDtypeStruct((b, t, LANES), F32)]
    if with_kv:
        in_specs += [_const_spec((MLA_KV_LORA, MLA_HEADS * LANES)), _const_spec((vw, MLA_KV_LORA)),
                     _const_spec((1, LANES))]
        args += [w["wuk"], w["wuvt"], w["gk"]]
        out_specs += [row(MLA_HEADS * LANES), pl.BlockSpec((1, vw, tm), lambda i, j: (i, 0, j))]
        out_shape += [jax.ShapeDtypeStruct((b, t, MLA_HEADS * LANES), BF16),
                      jax.ShapeDtypeStruct((b, vw, t), BF16)]
    return pl.pallas_call(
        _mla_in_body,
        grid=(b, t // tm),
        in_specs=in_specs,
        out_specs=out_specs,
        out_shape=out_shape,
        compiler_params=_params("parallel", "parallel"),
        name="mla_in",
    )(*args)


def _mla_kv_body(lat_ref, krp_ref, wuk_ref, wuvt_ref, gk_ref, k_ref, vt_ref):
    _mla_keys_values(lat_ref[0], krp_ref[0], wuk_ref, wuvt_ref, gk_ref, k_ref, vt_ref)


def _mla_kv(lat, krp, w):
    b, l, _ = lat.shape
    tl = ROW_TILE if l % ROW_TILE == 0 else l
    row = lambda n: pl.BlockSpec((1, tl, n), lambda i, j: (i, j, 0))
    vw = MLA_HEADS * MLA_V
    return pl.pallas_call(
        _mla_kv_body,
        grid=(b, l // tl),
        in_specs=[row(MLA_KV_LORA), row(LANES), _const_spec((MLA_KV_LORA, MLA_HEADS * LANES)),
                  _const_spec((vw, MLA_KV_LORA)), _const_spec((1, LANES))],
        out_specs=[row(MLA_HEADS * LANES), pl.BlockSpec((1, vw, tl), lambda i, j: (i, 0, j))],
        out_shape=[jax.ShapeDtypeStruct((b, l, MLA_HEADS * LANES), BF16),
                   jax.ShapeDtypeStruct((b, vw, l), BF16)],
        compiler_params=_params("parallel", "parallel"),
        name="mla_kv",
    )(lat, krp, w["wuk"], w["wuvt"], w["gk"])


def _lru_body(h_ref, g_ref, wrec_ref, wgate_ref, cw_ref, cb_ref, wr_ref, wi_ref, br_ref, bi_ref, lam_ref,
              cprev_ref, h0_ref, y_ref, hl_ref, cl_ref, buf, a_s, b_s, hcar, *, tm):
    @pl.when(pl.program_id(1) == 0)
    def _():
        buf[0:HALO, :] = cprev_ref[0]
        hcar[...] = h0_ref[0]

    hb = _rms(h_ref[0], g_ref[...]).astype(BF16)
    xr = _dot(hb, wrec_ref[...])
    xg = _dot(hb, wgate_ref[...])
    buf[HALO:HALO + tm, :] = xr
    xc = cb_ref[...] + xr * cw_ref[CONV_WIDTH - 1:CONV_WIDTH, :]
    for j in range(CONV_WIDTH - 1):
        off = HALO - (CONV_WIDTH - 1) + j
        xc = xc + cw_ref[j:j + 1, :] * buf[off:off + tm, :]
    xcb = xc.astype(BF16)
    r = _sigmoid(_dot(xcb, wr_ref[...]) + br_ref[...])
    i = _sigmoid(_dot(xcb, wi_ref[...]) + bi_ref[...])
    log_a = (-LRU_C) * r * _softplus(-lam_ref[...])
    a = jnp.exp(log_a)
    z = -jnp.tanh(log_a) * (a * a + 1.0)
    b = jnp.where(z > 0.0, z * lax.rsqrt(z), 0.0) * (i * xc)
    a_s[...] = a
    b_s[...] = b

    row = lax.broadcasted_iota(jnp.int32, (SUBLANES, a.shape[1]), 0)

    def step(g, hprev):
        r0 = pl.multiple_of(g * SUBLANES, SUBLANES)
        ag = a_s[pl.ds(r0, SUBLANES), :]
        bg = b_s[pl.ds(r0, SUBLANES), :]
        d = 1
        while d < SUBLANES:
            keep = row >= d
            a_up = jnp.where(keep, pltpu.roll(ag, d, 0), 1.0)
            b_up = jnp.where(keep, pltpu.roll(bg, d, 0), 0.0)
            bg = ag * b_up + bg
            ag = ag * a_up
            d *= 2
        hg = ag * hprev + bg
        b_s[pl.ds(r0, SUBLANES), :] = hg
        return hg[SUBLANES - 1:, :]

    hfin = lax.fori_loop(0, tm // SUBLANES, step, hcar[...], unroll=2)
    hcar[...] = hfin
    y_ref[0] = (_gelu_tanh(xg) * b_s[...]).astype(BF16)
    hl_ref[0] = hfin
    tail = buf[tm:tm + HALO, :]
    buf[0:HALO, :] = tail
    cl_ref[0] = tail


def _lru(h, w, conv_prev8, h0):
    b, t, d = h.shape
    tm = _row_tile(t, ROW_TILE)
    wd = LRU_WIDTH
    vec = lambda: _const_spec((1, wd))
    return pl.pallas_call(
        functools.partial(_lru_body, tm=tm),
        grid=(b, t // tm),
        in_specs=[pl.BlockSpec((1, tm, d), lambda i, j: (i, j, 0)), _const_spec((1, d)),
                  _const_spec((d, wd)), _const_spec((d, wd)), _const_spec((CONV_WIDTH, wd)), vec(),
                  _const_spec((wd, wd)), _const_spec((wd, wd)), vec(), vec(), vec(),
                  pl.BlockSpec((1, HALO, wd), lambda i, j: (i, 0, 0)),
                  pl.BlockSpec((1, 1, wd), lambda i, j: (i, 0, 0))],
        out_specs=[pl.BlockSpec((1, tm, wd), lambda i, j: (i, j, 0)),
                   pl.BlockSpec((1, 1, wd), lambda i, j: (i, 0, 0)),
                   pl.BlockSpec((1, HALO, wd), lambda i, j: (i, 0, 0))],
        out_shape=[jax.ShapeDtypeStruct((b, t, wd), BF16), jax.ShapeDtypeStruct((b, 1, wd), F32),
                   jax.ShapeDtypeStruct((b, HALO, wd), F32)],
        scratch_shapes=[pltpu.VMEM((tm + HALO, wd), F32), pltpu.VMEM((tm, wd), F32),
                        pltpu.VMEM((tm, wd), F32), pltpu.VMEM((1, wd), F32)],
        compiler_params=_params("parallel", "arbitrary"),
        name="lru",
    )(h, w["g_mix"], w["wrec"], w["wgate"], w["conv_w"], w["conv_b"], w["wr"], w["wi"], w["br"], w["bi"],
      w["lam"], conv_prev8, h0)


def _group_rms(x, g):
    low = lax.broadcasted_iota(jnp.int32, (1, LANES), 1) < FOX_HEAD_DIM
    out = []
    for j in range(x.shape[1] // LANES):
        xb = x[:, j * LANES:(j + 1) * LANES]
        sq = xb * xb
        s_low = jnp.sum(jnp.where(low, sq, 0.0), axis=-1, keepdims=True)
        s_high = jnp.sum(jnp.where(low, 0.0, sq), axis=-1, keepdims=True)
        ss = jnp.where(low, s_low, s_high)
        out.append(xb * lax.rsqrt(ss * (1.0 / FOX_HEAD_DIM) + NORM_EPS))
    return jnp.concatenate(out, axis=-1) * g


def _cumsum_rows(x, ltri):
    out = None
    for p in _split_bf16(x, 3):
        d = _dot(ltri, p)
        out = d if out is None else out + d
    return out


def _fox_aug(x, keep, bias):
    blocks = [x[:, (hd // 2) * LANES:(hd // 2 + 1) * LANES] for hd in range(FOX_HEADS)]
    return jnp.concatenate(blocks, axis=-1) * keep + bias


def _fox_key_bias(c, place):
    lane = lax.broadcasted_iota(jnp.int32, (1, LANES), 1)
    hi, mid, lo = _split_bf16(jnp.where(lane < FOX_HEADS, c * (-LOG2E), 0.0), 3)
    packed = (hi.astype(F32) + pltpu.roll(mid.astype(F32), FOX_HEADS, 1)
              + pltpu.roll(lo.astype(F32), 2 * FOX_HEADS, 1))
    return _dot(packed.astype(BF16), place)


def _fox_in_body(h_ref, g_ref, wq_ref, wk_ref, wv_ref, wf_ref, bf_ref, gq_ref, gk_ref,
                 keep_ref, ones_ref, place_ref, ltri_ref, c0_ref,
                 q_ref, k32_ref, v32_ref, ka_ref, vb_ref, lf_ref, c_ref, lf_s, carry, *, tm, tc):
    @pl.when(pl.program_id(1) == 0)
    def _():
        carry[...] = c0_ref[0]

    hb = _rms(h_ref[0], g_ref[...]).astype(BF16)
    keep = keep_ref[...]
    q = _group_rms(_dot(hb, wq_ref[...]), gq_ref[...])
    q_ref[0] = _fox_aug(q, keep, ones_ref[...]).astype(BF16)
    k = _group_rms(_dot(hb, wk_ref[...]), gk_ref[...])
    k32_ref[0] = k
    v = _dot(hb, wv_ref[...])
    v32_ref[0] = v
    vb_ref[0] = v.astype(BF16)
    logf = -_softplus(-(_dot(hb, wf_ref[...]) + bf_ref[...]))
    lf_ref[0] = logf
    if tc > tm:
        lf_s[...] = jnp.zeros_like(lf_s)
    lf_s[0:tm, :] = logf
    c = carry[...] + _cumsum_rows(lf_s[...], ltri_ref[...])[0:tm, :]
    c_ref[0] = c
    carry[...] = c[tm - 1:tm, :]
    ka_ref[0] = _fox_aug(k, keep, _fox_key_bias(c, place_ref[...])).astype(BF16)


def _fox_in(h, w, c0):
    b, t, d = h.shape
    tm = _row_tile(t, ROW_TILE)
    tc = max(tm, LANES)
    ltri = jnp.tril(jnp.ones((tc, tc), F32)).astype(BF16)
    row = lambda n: pl.BlockSpec((1, tm, n), lambda i, j: (i, j, 0))
    fw, aw = FOX_WIDTH, FOX_HEADS * LANES
    return pl.pallas_call(
        functools.partial(_fox_in_body, tm=tm, tc=tc),
        grid=(b, t // tm),
        in_specs=[row(d), _const_spec((1, d)),
                  _const_spec((d, fw)), _const_spec((d, fw)), _const_spec((d, fw)), _const_spec((d, LANES)),
                  _const_spec((1, LANES)), _const_spec((1, fw)), _const_spec((1, fw)),
                  _const_spec((1, aw)), _const_spec((1, aw)), _const_spec((LANES, aw)),
                  _const_spec((tc, tc)), pl.BlockSpec((1, 1, LANES), lambda i, j: (i, 0, 0))],
        out_specs=[row(aw), row(fw), row(fw), row(aw), row(fw), row(LANES), row(LANES)],
        out_shape=[jax.ShapeDtypeStruct((b, t, aw), BF16), jax.ShapeDtypeStruct((b, t, fw), F32),
                   jax.ShapeDtypeStruct((b, t, fw), F32), jax.ShapeDtypeStruct((b, t, aw), BF16),
                   jax.ShapeDtypeStruct((b, t, fw), BF16), jax.ShapeDtypeStruct((b, t, LANES), F32),
                   jax.ShapeDtypeStruct((b, t, LANES), F32)],
        scratch_shapes=[pltpu.VMEM((tc, LANES), F32), pltpu.VMEM((1, LANES), F32)],
        compiler_params=_params("parallel", "arbitrary"),
        name="fox_in",
    )(h, w["g_mix"], w["wq"], w["wk"], w["wv"], w["wf"], w["bf"], w["gq"], w["gk"],
      w["keep"], w["ones"], w["place"], ltri, c0)


def _fox_past_body(k_ref, lf_ref, keep_ref, place_ref, ltri_ref, ka_ref, c_ref, carry):
    @pl.when(pl.program_id(1) == 0)
    def _():
        carry[...] = jnp.zeros_like(carry)

    c = carry[...] + _cumsum_rows(lf_ref[0], ltri_ref[...])
    c_ref[0] = c
    carry[...] = c[c.shape[0] - 1:, :]
    ka_ref[0] = _fox_aug(k_ref[0], keep_ref[...], _fox_key_bias(c, place_ref[...])).astype(BF16)


def _fox_past(past_k, past_logf, w):
    b, p, n = past_logf.shape
    tc = _row_tile(p, ROW_TILE)
    ltri = jnp.tril(jnp.ones((tc, tc), F32)).astype(BF16)
    fw, aw = FOX_WIDTH, FOX_HEADS * LANES
    row = lambda m: pl.BlockSpec((1, tc, m), lambda i, j: (i, j, 0))
    return pl.pallas_call(
        _fox_past_body,
        grid=(b, p // tc),
        in_specs=[row(fw), row(n), _const_spec((1, aw)), _const_spec((LANES, aw)), _const_spec((tc, tc))],
        out_specs=[row(aw), row(n)],
        out_shape=[jax.ShapeDtypeStruct((b, p, aw), BF16), jax.ShapeDtypeStruct((b, p, n), F32)],
        scratch_shapes=[pltpu.VMEM((1, n), F32)],
        compiler_params=_params("parallel", "arbitrary"),
        name="fox_past",
    )(past_k, past_logf, w["keep"], w["place"], ltri)


def _flash_body(q_ref, k_ref, vt_ref, o_ref, m_s, l_s, acc_s, sa_s, sb_s, *, tq, tqs, tk, tks, fr, fc, n_k, past,
                kv_len, chunk_causal, diag_aligned):
    q_start = past + pl.program_id(2) * tq
    q = q_ref[0]
    m_s[...] = jnp.full_like(m_s, NEG_INF)
    l_s[...] = jnp.zeros_like(l_s)
    acc_s[...] = jnp.zeros_like(acc_s)
    shift = int(math.log2(CHUNK))
    hv = LANES // 2

    def block_kind(r0, rn, c0, cn):
        if chunk_causal:
            k_lo, k_hi, q_lo, q_hi = r0 >> shift, (r0 + rn - 1) >> shift, c0 >> shift, (c0 + cn - 1) >> shift
        else:
            k_lo, k_hi, q_lo, q_hi = r0, r0 + rn - 1, c0, c0 + cn - 1
        return "visible" if k_hi <= q_lo else ("hidden" if k_lo > q_hi else "partial")

    def diag_streams(d):
        out = []
        for r0 in range(0, tk, tks):
            for hh in range(2):
                for c0 in range(0, tq, tqs):
                    halves = [(c, block_kind(d * tk + r0, tks, c, tks)) for c in range(c0, c0 + tqs, tks)]
                    if all(kind == "visible" for _, kind in halves):
                        out.append((hh, r0, tks, c0, tqs, "visible"))
                    else:
                        out += [(hh, r0, tks, c, tks, kind) for c, kind in halves if kind != "hidden"]
        return out

    full_streams = [(hh, r0, fr, c0, fc, "visible") for r0 in range(0, tk, fr) for hh in range(2)
                    for c0 in range(0, tq, fc)]
    mask_streams = [st[:5] + ("partial",) for st in full_streams]

    def score(kblk, stream):
        hh, r0, rn, c0, cn, _ = stream
        head = slice(hh * LANES, (hh + 1) * LANES)
        return _dot_nt(kblk[r0:r0 + rn, head], q[c0:c0 + cn, head])

    def key_block(kt):
        return k_ref[0, pl.ds(pl.multiple_of(jnp.minimum(kt, n_k - 1) * tk, tk), tk), :]

    def absorb(s, stream, kt):
        hh, r0, rn, c0, cn, kind = stream
        cols = slice(c0, c0 + cn)
        k0 = pl.multiple_of(kt * tk, tk)
        if kind == "partial":
            kpos = k0 + r0 + lax.broadcasted_iota(jnp.int32, (rn, cn), 0)
            qpos = q_start + c0 + lax.broadcasted_iota(jnp.int32, (rn, cn), 1)
            if chunk_causal:
                vis = lax.shift_right_logical(kpos, shift) <= lax.shift_right_logical(qpos, shift)
            else:
                vis = kpos <= qpos
            s = jnp.where(jnp.logical_and(vis, kpos < kv_len), s, NEG_INF)
        m_old = m_s[hh, :, cols]
        m_new = jnp.maximum(m_old, jnp.max(s, axis=0, keepdims=True))
        alpha = jnp.exp2(m_old - m_new)
        p = jnp.exp2(s - m_new)
        l_s[hh, :, cols] = alpha * l_s[hh, :, cols] + jnp.sum(p, axis=0, keepdims=True)
        vt = vt_ref[0, hh * hv:(hh + 1) * hv, pl.ds(pl.multiple_of(k0 + r0, LANES), rn)]
        acc_s[hh, :, cols] = alpha * acc_s[hh, :, cols] + _dot(vt, p.astype(BF16))
        m_s[hh, :, cols] = m_new

    def single_tile(kt, streams, after_first_scores=None):
        kblk = key_block(kt)
        ahead = 2
        pending = [score(kblk, st) for st in streams[:ahead]]
        if after_first_scores is not None:
            after_first_scores()
        for idx, st in enumerate(streams):
            s = pending.pop(0)
            if idx + ahead < len(streams):
                pending.append(score(kblk, streams[idx + ahead]))
            absorb(s, st, kt)

    def store_scores(buf, kt):
        kblk = key_block(kt)
        for st in full_streams:
            hh, r0, rn, c0, cn, _ = st
            buf[hh, r0:r0 + rn, c0:c0 + cn] = score(kblk, st)

    def absorb_stored(buf, kt):
        for st in full_streams:
            hh, r0, rn, c0, cn, _ = st
            absorb(buf[hh, r0:r0 + rn, c0:c0 + cn], st, kt)

    def tile_pair(i, carry):
        kt = 2 * i
        store_scores(sb_s, kt + 1)
        absorb_stored(sa_s, kt)
        store_scores(sa_s, kt + 2)
        absorb_stored(sb_s, kt + 1)
        return carry

    def full_tile(kt, carry):
        single_tile(kt, full_streams)
        return carry

    def masked_tile(kt, carry):
        single_tile(kt, mask_streams)
        return carry

    n_full = jnp.minimum(q_start // tk, kv_len // tk)
    q_last = q_start + tq - 1
    k_hi = (q_last // CHUNK + 1) * CHUNK if chunk_causal else q_last + 1
    n_end = jnp.minimum((k_hi + tk - 1) // tk, n_k)
    n_pair = n_full // 2
    if diag_aligned:
        for d in range(tq // tk):
            single_tile(n_full + d, diag_streams(d),
                        after_first_scores=(lambda: store_scores(sa_s, 0)) if d == 0 else None)
        lax.fori_loop(0, n_pair, tile_pair, 0)
        lax.fori_loop(2 * n_pair, n_full, full_tile, 0)
    else:
        @pl.when(n_pair > 0)
        def _():
            store_scores(sa_s, 0)

        lax.fori_loop(0, n_pair, tile_pair, 0)
        lax.fori_loop(2 * n_pair, n_full, full_tile, 0)
        lax.fori_loop(n_full, n_end, masked_tile, 0)
    out_t = jnp.concatenate([acc_s[0] / l_s[0], acc_s[1] / l_s[1]], axis=0)
    o_ref[0] = out_t.T.astype(BF16)


def _flash(q, k, vt, *, past, kv_len, chunk_causal):
    b, t, w = q.shape
    lp = k.shape[1]
    n_pairs = w // (2 * LANES)
    t_pad = max(t, LANES)
    if t_pad > t:
        q = _pad_rows(q, t_pad)
    if past == 0 and t % KEY_TILE == 0 and lp % KEY_TILE == 0:
        tk = KEY_TILE
        tq = QUERY_TILE if t % QUERY_TILE == 0 else KEY_TILE
    elif t_pad == LANES:
        tq, tk = LANES, lp
    else:
        tq = tk = LANES
    assert t_pad % tq == 0 and lp % tk == 0, (t, lp, tq, tk)
    tks = MXU_TILE if tk % MXU_TILE == 0 and tq > LANES else tk
    tqs = min(tq, 2 * MXU_TILE)
    fr, fc = (tk, MXU_TILE) if tks < tk and tq % MXU_TILE == 0 else (tks, tqs)
    out = pl.pallas_call(
        functools.partial(_flash_body, tq=tq, tqs=tqs, tk=tk, tks=tks, fr=fr, fc=fc, n_k=lp // tk, past=past,
                          kv_len=kv_len, chunk_causal=chunk_causal,
                          diag_aligned=(past == 0 and tq % tk == 0 and kv_len == lp and tks < tk)),
        grid=(b, n_pairs, t_pad // tq),
        in_specs=[pl.BlockSpec((1, tq, 2 * LANES), lambda i, j, s: (i, s, j)),
                  pl.BlockSpec((1, lp, 2 * LANES), lambda i, j, s: (i, 0, j)),
                  pl.BlockSpec((1, LANES, lp), lambda i, j, s: (i, j, 0))],
        out_specs=pl.BlockSpec((1, tq, LANES), lambda i, j, s: (i, s, j)),
        out_shape=jax.ShapeDtypeStruct((b, t_pad, n_pairs * LANES), BF16),
        scratch_shapes=[pltpu.VMEM((2, 1, tq), F32), pltpu.VMEM((2, 1, tq), F32),
                        pltpu.VMEM((2, LANES // 2, tq), F32),
                        pltpu.VMEM((2, tk, tq), F32), pltpu.VMEM((2, tk, tq), F32)],
        compiler_params=_params("parallel", "parallel", "arbitrary"),
        name="flash_mla" if chunk_causal else "flash_fox",
    )(q, k, vt)
    return out[:, :t]


def _row(v):
    return v.reshape(1, -1).astype(F32)


def _pad_lanes(x, lo, total):
    pad = [(0, 0)] * (x.ndim - 1) + [(lo, total - lo - x.shape[-1])]
    return jnp.pad(x, pad)


def _ffn_weights(g, w_in, w_out):
    f = w_out.shape[0]
    return dict(g=_row(g), wg=w_in[:, :f].astype(BF16), wu=w_in[:, f:].astype(BF16), wo=w_out.astype(BF16))


def _mem_weights(g, g_src, w_q, w_kv, w_o, g_q, g_k):
    d = w_q.shape[0]
    kv = w_kv.reshape(d, MEM_HEADS, 2, MEM_HEAD_DIM)
    return dict(g=_row(g), g_src=_row(g_src), wq=w_q.astype(BF16), wo=w_o.astype(BF16),
                wk=kv[:, :, 0].reshape(d, MEM_WIDTH).astype(BF16),
                wv=kv[:, :, 1].reshape(d, MEM_WIDTH).astype(BF16),
                gq=_row(g_q) * (MEM_HEAD_DIM ** -0.5), gk=_row(g_k))


def _even_weights(g_mix, w_in, g_qlat, g_kvlat, w_uq, w_ukv, g_q, g_k, conv_w, conv_b, gate_w, gate_b, lam, w_out):
    d = w_in.shape[0]
    o1 = MLA_Q_LORA
    o2 = o1 + MLA_KV_LORA
    o3 = o2 + MLA_ROPE
    o4 = o3 + LRU_WIDTH
    half = MLA_ROPE // 2
    swap_halves = lambda r: jnp.concatenate([r[..., half:], r[..., :half]], axis=-1)
    uq3 = w_uq.reshape(MLA_Q_LORA, MLA_HEADS, MLA_QK)
    uq = _pad_lanes(uq3, 0, LANES)
    uq_swap = _pad_lanes(swap_halves(uq3[:, :, MLA_NOPE:]), MLA_NOPE, LANES)
    kr = w_in[:, o2:o3]
    ukv = w_ukv.reshape(MLA_KV_LORA, MLA_HEADS, MLA_NOPE + MLA_V)
    uk = _pad_lanes(ukv[:, :, :MLA_NOPE], 0, LANES)
    blk = LRU_WIDTH // LRU_BLOCKS
    eye = jnp.eye(LRU_BLOCKS, dtype=F32)
    wr = jnp.einsum("ncd,nm->ncmd", gate_w[:, :, :blk], eye).reshape(LRU_WIDTH, LRU_WIDTH)
    wi = jnp.einsum("ncd,nm->ncmd", gate_w[:, :, blk:], eye).reshape(LRU_WIDTH, LRU_WIDTH)
    return dict(
        g_mix=_row(g_mix), wcq=w_in[:, :o1].astype(BF16), g_qlat=_row(g_qlat),
        wuq=uq.reshape(MLA_Q_LORA, MLA_HEADS * LANES).astype(BF16),
        wuq_swap=uq_swap.reshape(MLA_Q_LORA, MLA_HEADS * LANES).astype(BF16),
        gq=_pad_lanes(_row(g_q), 0, LANES) * (MLA_QK ** -0.5 * LOG2E),
        wckv=w_in[:, o1:o2].astype(BF16), g_kvlat=_row(g_kvlat),
        wkr=_pad_lanes(kr, MLA_NOPE, LANES).astype(BF16),
        wkr_swap=_pad_lanes(swap_halves(kr), MLA_NOPE, LANES).astype(BF16),
        wuk=uk.reshape(MLA_KV_LORA, MLA_HEADS * LANES).astype(BF16),
        wuvt=ukv[:, :, MLA_NOPE:].reshape(MLA_KV_LORA, MLA_HEADS * MLA_V).T.astype(BF16),
        gk=_pad_lanes(_row(g_k), 0, LANES),
        wrec=w_in[:, o3:o4].astype(BF16), wgate=w_in[:, o4:].astype(BF16),
        conv_w=conv_w.astype(F32), conv_b=_row(conv_b), wr=wr.astype(BF16), wi=wi.astype(BF16),
        br=_row(gate_b[:, :blk]), bi=_row(gate_b[:, blk:]), lam=_row(lam),
        wo_attn=w_out[:MLA_HEADS * MLA_V].astype(BF16), wo_rec=w_out[MLA_HEADS * MLA_V:].astype(BF16))


def _odd_weights(g_mix, w_in, b_f, g_q, g_k, w_out):
    fw = FOX_WIDTH
    lane = jnp.arange(FOX_HEADS * LANES)
    hd, within = lane // LANES, lane % LANES
    own_low = hd % 2 == 0
    keep = jnp.where(own_low, within < FOX_HEAD_DIM, within >= FOX_HEAD_DIM)
    part = within - jnp.where(own_low, FOX_HEAD_DIM, 0)
    is_bias = (part >= 0) & (part < 3)
    src = part * FOX_HEADS + hd
    place = ((jnp.arange(LANES)[:, None] == src[None, :]) & is_bias[None, :]).astype(BF16)
    return dict(
        keep=keep.astype(F32)[None, :], ones=is_bias.astype(F32)[None, :], place=place,
        g_mix=_row(g_mix), wq=w_in[:, :fw].astype(BF16), wk=w_in[:, fw:2 * fw].astype(BF16),
        wv=w_in[:, 2 * fw:3 * fw].astype(BF16), wf=_pad_lanes(w_in[:, 3 * fw:], 0, LANES).astype(BF16),
        bf=_pad_lanes(_row(b_f), 0, LANES),
        gq=jnp.tile(_row(g_q), (1, FOX_HEADS)) * (FOX_HEAD_DIM ** -0.5 * LOG2E),
        gk=jnp.tile(_row(g_k), (1, FOX_HEADS)), wo=w_out.astype(BF16))


def _rope_tables(pos):
    half = MLA_ROPE // 2
    inv_freq = ROPE_THETA ** (-jnp.arange(half, dtype=F32) / half)
    ang = pos.astype(F32)[:, None] * inv_freq[None, :]
    cos, sin = jnp.cos(ang), jnp.sin(ang)
    c = jnp.concatenate([jnp.ones((pos.shape[0], MLA_NOPE), F32), cos, cos,
                         jnp.ones((pos.shape[0], LANES - MLA_QK), F32)], axis=-1)
    s = _pad_lanes(jnp.concatenate([-sin, sin], axis=-1), MLA_NOPE, LANES)
    return c, s


def _pad_rows(x, total):
    return jnp.pad(x, [(0, 0), (0, total - x.shape[1])] + [(0, 0)] * (x.ndim - 2))


def _kv_pad_len(t, past):
    l = past + t
    if past == 0 and t % KEY_TILE == 0:
        return l
    return -(-l // LANES) * LANES


def _even_layer(h, past, w, state):
    b, t, d = h.shape
    past_lat, past_krope, h0, conv_prev = state
    lp = _kv_pad_len(t, past)
    tables = _rope_tables(past + jnp.arange(t))
    if past == 0 and lp == t and t % LANES == 0:
        q, lat_new, krp_new, k, vt = _mla_in(h, w, tables, with_kv=True)
    else:
        q, lat_new, krp_new = _mla_in(h, w, tables, with_kv=False)
        lat_all = _pad_rows(jnp.concatenate([past_lat, lat_new], axis=1), lp)
        krp_all = _pad_rows(jnp.concatenate([_pad_lanes(past_krope, MLA_NOPE, LANES), krp_new], axis=1), lp)
        k, vt = _mla_kv(lat_all, krp_all, w)
    attn = _flash(q, k, vt, past=past, kv_len=past + t, chunk_causal=True)
    conv_prev8 = jnp.pad(conv_prev, ((0, 0), (HALO - (CONV_WIDTH - 1), 0), (0, 0)))
    y_rec, h_last, conv_last = _lru(h, w, conv_prev8, h0[:, None, :])
    new = (lat_new, krp_new[:, :, MLA_NOPE:MLA_QK], h_last[:, 0], conv_last[:, HALO - (CONV_WIDTH - 1):])
    return [attn, y_rec], [w["wo_attn"], w["wo_rec"]], new


def _odd_layer(h, past, w, state):
    b, t, d = h.shape
    past_k, past_v, past_logf = state
    if past > 0:
        ka_past, c_past = _fox_past(past_k.reshape(b, past, FOX_WIDTH),
                                    _pad_lanes(past_logf.astype(F32), 0, LANES), w)
        c0 = c_past[:, past - 1:past, :]
    else:
        ka_past = jnp.zeros((b, 0, FOX_HEADS * LANES), BF16)
        c0 = jnp.zeros((b, 1, LANES), F32)
    q, k32, v32, ka_new, vb, logf, _ = _fox_in(h, w, c0)
    lp = _kv_pad_len(t, past)
    k_all = _pad_rows(jnp.concatenate([ka_past, ka_new], axis=1), lp)
    v_all = _pad_rows(jnp.concatenate([past_v.reshape(b, past, FOX_WIDTH).astype(BF16), vb], axis=1), lp)
    attn = _flash(q, k_all, jnp.swapaxes(v_all, 1, 2), past=past, kv_len=past + t, chunk_causal=False)
    new = (k32.reshape(b, t, FOX_HEADS, FOX_HEAD_DIM), v32.reshape(b, t, FOX_HEADS, FOX_HEAD_DIM),
           logf[:, :, :FOX_HEADS])
    return [attn], [w["wo"]], new


def _trunk(x, past, layers, mem_kvs, even_states, odd_states):
    b, t, d = x.shape
    even_new, odd_new = [], []
    for li, lw in enumerate(layers):
        h = _ffn(x.reshape(b * t, d), lw["ffn1"]).reshape(b, t, d)
        if li % 2 == 0:
            parts, w_parts, new = _even_layer(h, past, lw["mix"], even_states[li // 2])
            even_new.append(new)
        else:
            parts, w_parts, new = _odd_layer(h, past, lw["mix"], odd_states[li // 2])
            odd_new.append(new)
        x = _post_mixer(h, parts, w_parts, mem_kvs[li][0], mem_kvs[li][1], lw["mem"], lw["ffn2"])
    return x, even_new, odd_new


def kernel(x_prompt, x_sample, mem_prompt, cache_mla_latent, cache_mla_krope, state_lru_h, state_lru_conv, cache_fox_k, cache_fox_v, cache_fox_logf, cache_mem_k, cache_mem_v, norm_ffn1, ffn1_w_in, ffn1_w_out, norm_mix, norm_mem, norm_mem_src, mem_w_q, mem_w_kv, mem_w_o, mem_g_q, mem_g_k, norm_ffn2, ffn2_w_in, ffn2_w_out, ev_w_in, ev_g_qlat, ev_g_kvlat, ev_w_uq, ev_w_ukv, ev_g_q, ev_g_k, ev_conv_w, ev_conv_b, ev_gate_w, ev_gate_b, ev_lambda, ev_w_out, od_w_in, od_b_f, od_g_q, od_g_k, od_w_out):
    depth = norm_ffn1.shape[0]
    n_even, n_odd = (depth + 1) // 2, depth // 2
    b, _, _ = x_prompt.shape
    bs = x_sample.shape[0]
    past = cache_mla_latent.shape[2] if n_even else cache_fox_k.shape[2]

    layers = []
    for li in range(depth):
        j = li // 2
        if li % 2 == 0:
            mix = _even_weights(norm_mix[li], ev_w_in[j], ev_g_qlat[j], ev_g_kvlat[j], ev_w_uq[j], ev_w_ukv[j],
                                ev_g_q[j], ev_g_k[j], ev_conv_w[j], ev_conv_b[j], ev_gate_w[j], ev_gate_b[j],
                                ev_lambda[j], ev_w_out[j])
        else:
            mix = _odd_weights(norm_mix[li], od_w_in[j], od_b_f[j], od_g_q[j], od_g_k[j], od_w_out[j])
        layers.append(dict(
            ffn1=_ffn_weights(norm_ffn1[li], ffn1_w_in[li], ffn1_w_out[li]),
            ffn2=_ffn_weights(norm_ffn2[li], ffn2_w_in[li], ffn2_w_out[li]),
            mem=_mem_weights(norm_mem[li], norm_mem_src[li], mem_w_q[li], mem_w_kv[li], mem_w_o[li],
                             mem_g_q[li], mem_g_k[li]),
            mix=mix))

    mem_p = [_mem_kv(mem_prompt, lw["mem"]) for lw in layers]
    ev0 = [(jnp.zeros((b, 0, MLA_KV_LORA), F32), jnp.zeros((b, 0, MLA_ROPE), F32),
            jnp.zeros((b, LRU_WIDTH), F32), jnp.zeros((b, CONV_WIDTH - 1, LRU_WIDTH), F32))
           for _ in range(n_even)]
    od0 = [(jnp.zeros((b, 0, FOX_HEADS, FOX_HEAD_DIM), F32), jnp.zeros((b, 0, FOX_HEADS, FOX_HEAD_DIM), F32),
            jnp.zeros((b, 0, FOX_HEADS), F32)) for _ in range(n_odd)]
    y_prompt, ev_p, od_p = _trunk(x_prompt, 0, layers, [(m[2], m[3]) for m in mem_p], ev0, od0)

    m_tok = cache_mem_k.shape[2]
    mem_s = [(cache_mem_k[li].reshape(bs, m_tok, MEM_WIDTH).astype(BF16),
              cache_mem_v[li].reshape(bs, m_tok, MEM_WIDTH).astype(BF16)) for li in range(depth)]
    ev_s = [(cache_mla_latent[j], cache_mla_krope[j], state_lru_h[j], state_lru_conv[j]) for j in range(n_even)]
    od_s = [(cache_fox_k[j], cache_fox_v[j], cache_fox_logf[j]) for j in range(n_odd)]
    y_sample, ev_n, od_n = _trunk(x_sample, past, layers, mem_s, ev_s, od_s)

    mem_shape = (b, m_tok, MEM_HEADS, MEM_HEAD_DIM)
    p_even = [jnp.stack([s[f] for s in ev_p]) for f in range(4)]
    p_odd = [jnp.stack([s[f] for s in od_p]) for f in range(3)]
    p_mem_k = jnp.stack([m[0].reshape(mem_shape) for m in mem_p])
    p_mem_v = jnp.stack([m[1].reshape(mem_shape) for m in mem_p])
    s_even = [jnp.stack([s[f] for s in ev_n]) for f in range(4)]
    s_odd = [jnp.stack([s[f] for s in od_n]) for f in range(3)]
    return (y_prompt, y_sample, *p_even, *p_odd, p_mem_k, p_mem_v, *s_even, *s_odd)
```

```python
import functools
import math

import jax
import jax.numpy as jnp
from jax import lax
from jax.experimental import pallas as pl
from jax.experimental.pallas import tpu as pltpu

F32 = jnp.float32
BF16 = jnp.bfloat16

NORM_EPS = 1e-6
NEG_INF = -1e30
LOG2E = math.log2(math.e)
CHUNK = 64
LANES = 128
SUBLANES = 8
ROPE_THETA = 10000.0
LRU_C = 8.0
MLA_HEADS = 8
MLA_NOPE = 64
MLA_ROPE = 32
MLA_QK = MLA_NOPE + MLA_ROPE
MLA_V = 64
MLA_Q_LORA = 256
MLA_KV_LORA = 128
LRU_WIDTH = 512
LRU_BLOCKS = 8
CONV_WIDTH = 4
FOX_HEADS = 16
FOX_HEAD_DIM = 64
FOX_WIDTH = FOX_HEADS * FOX_HEAD_DIM
MEM_HEADS = 4
MEM_HEAD_DIM = 128
MEM_WIDTH = MEM_HEADS * MEM_HEAD_DIM
HALO = 8

MXU_TILE = 256
ROW_TILE = 512
KEY_TILE = 512
QUERY_TILE = 1024

VMEM_LIMIT = 56 * 1024 * 1024


def _dot(a, b):
    return jnp.dot(a, b, preferred_element_type=F32)


def _dot_nt(a, b):
    return lax.dot_general(a, b, (((1,), (1,)), ((), ())), preferred_element_type=F32)


def _rms(x, g):
    return x * lax.rsqrt(jnp.mean(x * x, axis=-1, keepdims=True) + NORM_EPS) * g


def _head_rms(x, g, n_live):
    ss = jnp.sum(x * x, axis=-1, keepdims=True) * (1.0 / n_live)
    return x * lax.rsqrt(ss + NORM_EPS) * g


def _sigmoid(x):
    return 1.0 / (1.0 + jnp.exp(-x))


def _log1p(y):
    u = 1.0 + y
    d = u - 1.0
    return jnp.where(d == 0.0, y, jnp.log(u) * (y / jnp.where(d == 0.0, 1.0, d)))


def _softplus(x):
    return jnp.maximum(x, 0.0) + _log1p(jnp.exp(-jnp.abs(x)))


def _gelu_tanh(x):
    return 0.5 * x * (1.0 + jnp.tanh(math.sqrt(2.0 / math.pi) * (x + 0.044715 * (x * x * x))))


def _split_bf16(x, parts):
    out = []
    r = x
    for _ in range(parts):
        p = r.astype(BF16)
        out.append(p)
        r = r - p.astype(F32)
    return out


def _const_spec(shape):
    nd = len(shape)
    return pl.BlockSpec(shape, lambda *_: (0,) * nd, pipeline_mode=pl.Buffered(1))


def _params(*sem):
    return pltpu.CompilerParams(dimension_semantics=sem, vmem_limit_bytes=VMEM_LIMIT)


def _row_tile(n, cap):
    t = min(n, cap)
    assert n % t == 0, (n, t)
    return t


FFN_CHUNKS = 2


def _ffn_chunk_bounds(f):
    tiles = -(-f // MXU_TILE)
    per = -(-tiles // FFN_CHUNKS) * MXU_TILE
    edges = [min(i * per, f) for i in range(FFN_CHUNKS + 1)]
    return [(lo, hi) for lo, hi in zip(edges[:-1], edges[1:]) if hi > lo]


def _swiglu_half_step(x, g_ref, wg_ref, wu_ref, wo_ref):
    hb = _rms(x, g_ref[...]).astype(BF16)
    acc = jnp.zeros_like(x)
    for lo, hi in _ffn_chunk_bounds(wg_ref.shape[1]):
        sl = slice(lo, hi)
        gate = _dot(hb, wg_ref[:, sl])
        up = _dot(hb, wu_ref[:, sl])
        act = (gate * _sigmoid(gate) * up).astype(BF16)
        acc = acc + _dot(act, wo_ref[sl, :])
    return x + 0.5 * acc


def _ffn_body(x_ref, g_ref, wg_ref, wu_ref, wo_ref, o_ref):
    o_ref[...] = _swiglu_half_step(x_ref[...], g_ref, wg_ref, wu_ref, wo_ref)


def _ffn(x2, w):
    n, d = x2.shape
    f = w["wg"].shape[1]
    tm = _row_tile(n, ROW_TILE)
    return pl.pallas_call(
        _ffn_body,
        grid=(n // tm,),
        in_specs=[pl.BlockSpec((tm, d), lambda i: (i, 0)), _const_spec((1, d)),
                  _const_spec((d, f)), _const_spec((d, f)), _const_spec((f, d))],
        out_specs=pl.BlockSpec((tm, d), lambda i: (i, 0)),
        out_shape=jax.ShapeDtypeStruct((n, d), F32),
        compiler_params=_params("parallel"),
        name="ffn",
    )(x2, w["g"], w["wg"], w["wu"], w["wo"])


def _mem_cross_attention(h, tm, g_ref, wq_ref, gq_ref, mk_ref, mv_ref, wo_ref):
    hb = _rms(h, g_ref[...]).astype(BF16)
    q = _dot(hb, wq_ref[...])
    rows = []
    for i in range(h.shape[0] // tm):
        outs = []
        for hd in range(MEM_HEADS):
            sl = slice(hd * MEM_HEAD_DIM, (hd + 1) * MEM_HEAD_DIM)
            qh = _head_rms(q[i * tm:(i + 1) * tm, sl], gq_ref[...], MEM_HEAD_DIM).astype(BF16)
            s = _dot_nt(qh, mk_ref[i, :, sl])
            e = jnp.exp(s - jnp.max(s, axis=-1, keepdims=True))
            p = e / jnp.sum(e, axis=-1, keepdims=True)
            outs.append(_dot(p.astype(BF16), mv_ref[i, :, sl]).astype(BF16))
        rows.append(jnp.concatenate(outs, axis=-1))
    o = rows[0] if len(rows) == 1 else jnp.concatenate(rows, axis=0)
    return h + _dot(o, wo_ref[...])


def _post_mixer_body(*refs, n_parts):
    h_ref = refs[0]
    parts = refs[1:1 + n_parts]
    w_parts = refs[1 + n_parts:1 + 2 * n_parts]
    (mg_ref, mwq_ref, mgq_ref, mk_ref, mv_ref, mwo_ref,
     fg_ref, fwg_ref, fwu_ref, fwo_ref, o_ref) = refs[1 + 2 * n_parts:]
    bb, tm, d = h_ref.shape
    h = h_ref[...].reshape(bb * tm, d)
    for p_ref, w_ref in zip(parts, w_parts):
        h = h + _dot(p_ref[...].reshape(bb * tm, p_ref.shape[2]), w_ref[...])
    h = _mem_cross_attention(h, tm, mg_ref, mwq_ref, mgq_ref, mk_ref, mv_ref, mwo_ref)
    o_ref[...] = _swiglu_half_step(h, fg_ref, fwg_ref, fwu_ref, fwo_ref).reshape(bb, tm, d)


def _post_mixer(h, parts, w_parts, mk, mv, wm, wf):
    b, t, d = h.shape
    m = mk.shape[1]
    f = wf["wg"].shape[1]
    tm = _row_tile(t, ROW_TILE)
    bb = b if b * t <= ROW_TILE and tm % SUBLANES == 0 else 1
    row = lambda n: pl.BlockSpec((bb, tm, n), lambda i, j: (i, j, 0))
    mem = lambda: pl.BlockSpec((bb, m, MEM_WIDTH), lambda i, j: (i, 0, 0))
    in_specs = [row(d)] + [row(p.shape[2]) for p in parts] + [_const_spec(w.shape) for w in w_parts]
    in_specs += [_const_spec((1, d)), _const_spec((d, MEM_WIDTH)), _const_spec((1, MEM_HEAD_DIM)), mem(), mem(),
                 _const_spec((MEM_WIDTH, d)),
                 _const_spec((1, d)), _const_spec((d, f)), _const_spec((d, f)), _const_spec((f, d))]
    return pl.pallas_call(
        functools.partial(_post_mixer_body, n_parts=len(parts)),
        grid=(b // bb, t // tm),
        in_specs=in_specs,
        out_specs=row(d),
        out_shape=jax.ShapeDtypeStruct((b, t, d), F32),
        compiler_params=_params("parallel", "parallel"),
        name="post_mixer",
    )(h, *parts, *w_parts, wm["g"], wm["wq"], wm["gq"], mk, mv, wm["wo"],
      wf["g"], wf["wg"], wf["wu"], wf["wo"])


def _mem_kv_body(m_ref, g_ref, wk_ref, wv_ref, gk_ref, k32_ref, v32_ref, kb_ref, vb_ref):
    hb = _rms(m_ref[0], g_ref[0]).astype(BF16)
    k = _dot(hb, wk_ref[0])
    v = _dot(hb, wv_ref[0])
    for hd in range(MEM_HEADS):
        sl = slice(hd * MEM_HEAD_DIM, (hd + 1) * MEM_HEAD_DIM)
        kh = _head_rms(k[:, sl], gk_ref[0], MEM_HEAD_DIM)
        k32_ref[0, 0, :, hd, :] = kh
        v32_ref[0, 0, :, hd, :] = v[:, sl]
        kb_ref[0, 0, :, sl] = kh.astype(BF16)
    vb_ref[0, 0] = v.astype(BF16)


def _mem_kv(mem, ws):
    b, m, d = mem.shape
    n = len(ws)
    stack = lambda name: jnp.stack([w[name] for w in ws])
    per_layer = lambda *shape: pl.BlockSpec((1,) + shape, lambda l, i: (l,) + (0,) * len(shape))
    blk = lambda: pl.BlockSpec((1, 1, m, MEM_WIDTH), lambda l, i: (l, i, 0, 0))
    blk5 = lambda: pl.BlockSpec((1, 1, m, MEM_HEADS, MEM_HEAD_DIM), lambda l, i: (l, i, 0, 0, 0))
    shape5 = (n, b, m, MEM_HEADS, MEM_HEAD_DIM)
    return pl.pallas_call(
        _mem_kv_body,
        grid=(n, b),
        in_specs=[pl.BlockSpec((1, m, d), lambda l, i: (i, 0, 0)), per_layer(1, d),
                  per_layer(d, MEM_WIDTH), per_layer(d, MEM_WIDTH), per_layer(1, MEM_HEAD_DIM)],
        out_specs=[blk5(), blk5(), blk(), blk()],
        out_shape=[jax.ShapeDtypeStruct(shape5, F32), jax.ShapeDtypeStruct(shape5, F32),
                   jax.ShapeDtypeStruct((n, b, m, MEM_WIDTH), BF16),
                   jax.ShapeDtypeStruct((n, b, m, MEM_WIDTH), BF16)],
        compiler_params=_params("parallel", "parallel"),
        name="mem_kv",
    )(mem, stack("g_src"), stack("wk"), stack("wv"), stack("gk"))


def _mla_keys_values(lat, krp, wuk_ref, wuvt_ref, gk_ref, k_ref, vt_ref):
    lb = lat.astype(BF16)
    kn = _dot(lb, wuk_ref[...])
    for hd in range(MLA_HEADS):
        sl = slice(hd * LANES, (hd + 1) * LANES)
        k_ref[0, :, sl] = _head_rms(kn[:, sl] + krp, gk_ref[...], MLA_QK).astype(BF16)
    vt_ref[0] = _dot_nt(wuvt_ref[...], lb).astype(BF16)


def _mla_in_body(h_ref, g_ref, wcq_ref, gql_ref, wuq_ref, wuqs_ref, gq_ref, wckv_ref, gkv_ref, wkr_ref, wkrs_ref,
                 c_ref, s_ref, *rest):
    q_ref, lat_ref, krp_ref = rest[-5:-2] if len(rest) > 3 else rest
    hb = _rms(h_ref[0], g_ref[...]).astype(BF16)
    c, s = c_ref[...], s_ref[...]
    cq = _rms(_dot(hb, wcq_ref[...]), gql_ref[...]).astype(BF16)
    q = _dot(cq, wuq_ref[...])
    q_partner = _dot(cq, wuqs_ref[...])
    for hd in range(MLA_HEADS):
        sl = slice(hd * LANES, (hd + 1) * LANES)
        qh = q[:, sl] * c + q_partner[:, sl] * s
        q_ref[0, :, sl] = _head_rms(qh, gq_ref[...], MLA_QK).astype(BF16)
    lat = _rms(_dot(hb, wckv_ref[...]), gkv_ref[...])
    krp = _dot(hb, wkr_ref[...]) * c + _dot(hb, wkrs_ref[...]) * s
    lat_ref[0] = lat
    if len(rest) > 3:
        krp_ref[0] = krp[:, MLA_NOPE:MLA_QK]
        wuk_ref, wuvt_ref, gk_ref = rest[:3]
        _mla_keys_values(lat, krp, wuk_ref, wuvt_ref, gk_ref, *rest[-2:])
    else:
        krp_ref[0] = krp


def _mla_in(h, w, tables, with_kv):
    b, t, d = h.shape
    tm = _row_tile(t, ROW_TILE)
    row = lambda n: pl.BlockSpec((1, tm, n), lambda i, j: (i, j, 0))
    tab = lambda: pl.BlockSpec((tm, LANES), lambda i, j: (j, 0))
    vw = MLA_HEADS * MLA_V
    in_specs = [row(d), _const_spec((1, d)),
                _const_spec((d, MLA_Q_LORA)), _const_spec((1, MLA_Q_LORA)),
                _const_spec((MLA_Q_LORA, MLA_HEADS * LANES)), _const_spec((MLA_Q_LORA, MLA_HEADS * LANES)),
                _const_spec((1, LANES)),
                _const_spec((d, MLA_KV_LORA)), _const_spec((1, MLA_KV_LORA)),
                _const_spec((d, LANES)), _const_spec((d, LANES)),
                tab(), tab()]
    args = [h, w["g_mix"], w["wcq"], w["g_qlat"], w["wuq"], w["wuq_swap"], w["gq"], w["wckv"], w["g_kvlat"],
            w["wkr"], w["wkr_swap"], *tables]
    kr_w = MLA_ROPE if with_kv else LANES
    out_specs = [row(MLA_HEADS * LANES), row(MLA_KV_LORA), row(kr_w)]
    out_shape = [jax.ShapeDtypeStruct((b, t, MLA_HEADS * LANES), BF16),
                 jax.ShapeDtypeStruct((b, t, MLA_KV_LORA), F32),
                 jax.ShapeDtypeStruct((b, t, kr_w), F32)]
    if with_kv:
        in_specs += [_const_spec((MLA_KV_LORA, MLA_HEADS * LANES)), _const_spec((vw, MLA_KV_LORA)),
                     _const_spec((1, LANES))]
        args += [w["wuk"], w["wuvt"], w["gk"]]
        out_specs += [row(MLA_HEADS * LANES), pl.BlockSpec((1, vw, tm), lambda i, j: (i, 0, j))]
        out_shape += [jax.ShapeDtypeStruct((b, t, MLA_HEADS * LANES), BF16),
                      jax.ShapeDtypeStruct((b, vw, t), BF16)]
    return pl.pallas_call(
        _mla_in_body,
        grid=(b, t // tm),
        in_specs=in_specs,
        out_specs=out_specs,
        out_shape=out_shape,
        compiler_params=_params("parallel", "parallel"),
        name="mla_in",
    )(*args)


def _mla_kv_body(lat_ref, krp_ref, wuk_ref, wuvt_ref, gk_ref, k_ref, vt_ref):
    _mla_keys_values(lat_ref[0], krp_ref[0], wuk_ref, wuvt_ref, gk_ref, k_ref, vt_ref)


def _mla_kv(lat, krp, w):
    b, l, _ = lat.shape
    tl = ROW_TILE if l % ROW_TILE == 0 else l
    row = lambda n: pl.BlockSpec((1, tl, n), lambda i, j: (i, j, 0))
    vw = MLA_HEADS * MLA_V
    return pl.pallas_call(
        _mla_kv_body,
        grid=(b, l // tl),
        in_specs=[row(MLA_KV_LORA), row(LANES), _const_spec((MLA_KV_LORA, MLA_HEADS * LANES)),
                  _const_spec((vw, MLA_KV_LORA)), _const_spec((1, LANES))],
        out_specs=[row(MLA_HEADS * LANES), pl.BlockSpec((1, vw, tl), lambda i, j: (i, 0, j))],
        out_shape=[jax.ShapeDtypeStruct((b, l, MLA_HEADS * LANES), BF16),
                   jax.ShapeDtypeStruct((b, vw, l), BF16)],
        compiler_params=_params("parallel", "parallel"),
        name="mla_kv",
    )(lat, krp, w["wuk"], w["wuvt"], w["gk"])


def _lru_body(h_ref, g_ref, wrec_ref, wgate_ref, cw_ref, cb_ref, wr_ref, wi_ref, br_ref, bi_ref, lam_ref,
              cprev_ref, h0_ref, y_ref, hl_ref, cl_ref, buf, a_s, b_s, hcar, *, tm):
    @pl.when(pl.program_id(1) == 0)
    def _():
        buf[0:HALO, :] = cprev_ref[0]
        hcar[...] = h0_ref[0]

    hb = _rms(h_ref[0], g_ref[...]).astype(BF16)
    xr = _dot(hb, wrec_ref[...])
    xg = _dot(hb, wgate_ref[...])
    buf[HALO:HALO + tm, :] = xr
    xc = cb_ref[...] + xr * cw_ref[CONV_WIDTH - 1:CONV_WIDTH, :]
    for j in range(CONV_WIDTH - 1):
        off = HALO - (CONV_WIDTH - 1) + j
        xc = xc + cw_ref[j:j + 1, :] * buf[off:off + tm, :]
    xcb = xc.astype(BF16)
    r = _sigmoid(_dot(xcb, wr_ref[...]) + br_ref[...])
    i = _sigmoid(_dot(xcb, wi_ref[...]) + bi_ref[...])
    log_a = (-LRU_C) * r * _softplus(-lam_ref[...])
    a = jnp.exp(log_a)
    z = -jnp.tanh(log_a) * (a * a + 1.0)
    b = jnp.where(z > 0.0, z * lax.rsqrt(z), 0.0) * (i * xc)
    a_s[...] = a
    b_s[...] = b

    row = lax.broadcasted_iota(jnp.int32, (SUBLANES, a.shape[1]), 0)

    def step(g, hprev):
        r0 = pl.multiple_of(g * SUBLANES, SUBLANES)
        ag = a_s[pl.ds(r0, SUBLANES), :]
        bg = b_s[pl.ds(r0, SUBLANES), :]
        d = 1
        while d < SUBLANES:
            keep = row >= d
            a_up = jnp.where(keep, pltpu.roll(ag, d, 0), 1.0)
            b_up = jnp.where(keep, pltpu.roll(bg, d, 0), 0.0)
            bg = ag * b_up + bg
            ag = ag * a_up
            d *= 2
        hg = ag * hprev + bg
        b_s[pl.ds(r0, SUBLANES), :] = hg
        return hg[SUBLANES - 1:, :]

    hfin = lax.fori_loop(0, tm // SUBLANES, step, hcar[...], unroll=2)
    hcar[...] = hfin
    y_ref[0] = (_gelu_tanh(xg) * b_s[...]).astype(BF16)
    hl_ref[0] = hfin
    tail = buf[tm:tm + HALO, :]
    buf[0:HALO, :] = tail
    cl_ref[0] = tail


def _lru(h, w, conv_prev8, h0):
    b, t, d = h.shape
    tm = _row_tile(t, ROW_TILE)
    wd = LRU_WIDTH
    vec = lambda: _const_spec((1, wd))
    return pl.pallas_call(
        functools.partial(_lru_body, tm=tm),
        grid=(b, t // tm),
        in_specs=[pl.BlockSpec((1, tm, d), lambda i, j: (i, j, 0)), _const_spec((1, d)),
                  _const_spec((d, wd)), _const_spec((d, wd)), _const_spec((CONV_WIDTH, wd)), vec(),
                  _const_spec((wd, wd)), _const_spec((wd, wd)), vec(), vec(), vec(),
                  pl.BlockSpec((1, HALO, wd), lambda i, j: (i, 0, 0)),
                  pl.BlockSpec((1, 1, wd), lambda i, j: (i, 0, 0))],
        out_specs=[pl.BlockSpec((1, tm, wd), lambda i, j: (i, j, 0)),
                   pl.BlockSpec((1, 1, wd), lambda i, j: (i, 0, 0)),
                   pl.BlockSpec((1, HALO, wd), lambda i, j: (i, 0, 0))],
        out_shape=[jax.ShapeDtypeStruct((b, t, wd), BF16), jax.ShapeDtypeStruct((b, 1, wd), F32),
                   jax.ShapeDtypeStruct((b, HALO, wd), F32)],
        scratch_shapes=[pltpu.VMEM((tm + HALO, wd), F32), pltpu.VMEM((tm, wd), F32),
                        pltpu.VMEM((tm, wd), F32), pltpu.VMEM((1, wd), F32)],
        compiler_params=_params("parallel", "arbitrary"),
        name="lru",
    )(h, w["g_mix"], w["wrec"], w["wgate"], w["conv_w"], w["conv_b"], w["wr"], w["wi"], w["br"], w["bi"],
      w["lam"], conv_prev8, h0)


def _group_rms(x, g):
    low = lax.broadcasted_iota(jnp.int32, (1, LANES), 1) < FOX_HEAD_DIM
    out = []
    for j in range(x.shape[1] // LANES):
        xb = x[:, j * LANES:(j + 1) * LANES]
        sq = xb * xb
        s_low = jnp.sum(jnp.where(low, sq, 0.0), axis=-1, keepdims=True)
        s_high = jnp.sum(jnp.where(low, 0.0, sq), axis=-1, keepdims=True)
        ss = jnp.where(low, s_low, s_high)
        out.append(xb * lax.rsqrt(ss * (1.0 / FOX_HEAD_DIM) + NORM_EPS))
    return jnp.concatenate(out, axis=-1) * g


def _cumsum_rows(x, ltri):
    out = None
    for p in _split_bf16(x, 3):
        d = _dot(ltri, p)
        out = d if out is None else out + d
    return out


def _fox_aug(x, keep, bias):
    blocks = [x[:, (hd // 2) * LANES:(hd // 2 + 1) * LANES] for hd in range(FOX_HEADS)]
    return jnp.concatenate(blocks, axis=-1) * keep + bias


def _fox_key_bias(c, place):
    lane = lax.broadcasted_iota(jnp.int32, (1, LANES), 1)
    hi, mid, lo = _split_bf16(jnp.where(lane < FOX_HEADS, c * (-LOG2E), 0.0), 3)
    packed = (hi.astype(F32) + pltpu.roll(mid.astype(F32), FOX_HEADS, 1)
              + pltpu.roll(lo.astype(F32), 2 * FOX_HEADS, 1))
    return _dot(packed.astype(BF16), place)


def _fox_in_body(h_ref, g_ref, wq_ref, wk_ref, wv_ref, wf_ref, bf_ref, gq_ref, gk_ref,
                 keep_ref, ones_ref, place_ref, ltri_ref, c0_ref,
                 q_ref, k32_ref, v32_ref, ka_ref, vb_ref, lf_ref, lf_s, carry, *, tm, tc, values_transposed):
    @pl.when(pl.program_id(1) == 0)
    def _():
        carry[...] = c0_ref[0]

    hb = _rms(h_ref[0], g_ref[...]).astype(BF16)
    keep = keep_ref[...]
    q = _group_rms(_dot(hb, wq_ref[...]), gq_ref[...])
    q_ref[0] = _fox_aug(q, keep, ones_ref[...]).astype(BF16)
    k = _group_rms(_dot(hb, wk_ref[...]), gk_ref[...])
    k32_ref[0] = k
    v = _dot(hb, wv_ref[...])
    v32_ref[0] = v
    vb_ref[0] = (v.T if values_transposed else v).astype(BF16)
    logf =-_softplus(-(_dot(hb, wf_ref[...]) + bf_ref[...]))
    lf_ref[0] = logf[:, :FOX_HEADS]
    if tc > tm:
        lf_s[...] = jnp.zeros_like(lf_s)
    lf_s[0:tm, :] = logf
    c = carry[...] + _cumsum_rows(lf_s[...], ltri_ref[...])[0:tm, :]
    carry[...] = c[tm - 1:tm, :]
    ka_ref[0] = _fox_aug(k, keep, _fox_key_bias(c, place_ref[...])).astype(BF16)


def _fox_in(h, w, c0):
    b, t, d = h.shape
    tm = _row_tile(t, ROW_TILE)
    tc = max(tm, LANES)
    ltri = jnp.tril(jnp.ones((tc, tc), F32)).astype(BF16)
    row = lambda n: pl.BlockSpec((1, tm, n), lambda i, j: (i, j, 0))
    fw, aw = FOX_WIDTH, FOX_HEADS * LANES
    values_transposed = tm % LANES == 0
    v_spec = pl.BlockSpec((1, fw, tm), lambda i, j: (i, 0, j)) if values_transposed else row(fw)
    v_shape = (b, fw, t) if values_transposed else (b, t, fw)
    outs = pl.pallas_call(
        functools.partial(_fox_in_body, tm=tm, tc=tc, values_transposed=values_transposed),
        grid=(b, t // tm),
        in_specs=[row(d), _const_spec((1, d)),
                  _const_spec((d, fw)), _const_spec((d, fw)), _const_spec((d, fw)), _const_spec((d, LANES)),
                  _const_spec((1, LANES)), _const_spec((1, fw)), _const_spec((1, fw)),
                  _const_spec((1, aw)), _const_spec((1, aw)), _const_spec((LANES, aw)),
                  _const_spec((tc, tc)), pl.BlockSpec((1, 1, LANES), lambda i, j: (i, 0, 0))],
        out_specs=[row(aw), row(fw), row(fw), row(aw), v_spec, row(FOX_HEADS)],
        out_shape=[jax.ShapeDtypeStruct((b, t, aw), BF16), jax.ShapeDtypeStruct((b, t, fw), F32),
                   jax.ShapeDtypeStruct((b, t, fw), F32), jax.ShapeDtypeStruct((b, t, aw), BF16),
                   jax.ShapeDtypeStruct(v_shape, BF16), jax.ShapeDtypeStruct((b, t, FOX_HEADS), F32)],
        scratch_shapes=[pltpu.VMEM((tc, LANES), F32), pltpu.VMEM((1, LANES), F32)],
        compiler_params=_params("parallel", "arbitrary"),
        name="fox_in",
    )(h, w["g_mix"], w["wq"], w["wk"], w["wv"], w["wf"], w["bf"], w["gq"], w["gk"],
      w["keep"], w["ones"], w["place"], ltri, c0)
    q, k32, v32, ka, vb, logf = outs
    return q, k32, v32, ka, (vb if values_transposed else jnp.swapaxes(vb, 1, 2)), logf


def _fox_past_body(k_ref, lf_ref, keep_ref, place_ref, ltri_ref, ka_ref, c_ref, carry):
    @pl.when(pl.program_id(1) == 0)
    def _():
        carry[...] = jnp.zeros_like(carry)

    c = carry[...] + _cumsum_rows(lf_ref[0], ltri_ref[...])
    c_ref[0] = c
    carry[...] = c[c.shape[0] - 1:, :]
    ka_ref[0] = _fox_aug(k_ref[0], keep_ref[...], _fox_key_bias(c, place_ref[...])).astype(BF16)


def _fox_past(past_k, past_logf, w):
    b, p, n = past_logf.shape
    tc = _row_tile(p, ROW_TILE)
    ltri = jnp.tril(jnp.ones((tc, tc), F32)).astype(BF16)
    fw, aw = FOX_WIDTH, FOX_HEADS * LANES
    row = lambda m: pl.BlockSpec((1, tc, m), lambda i, j: (i, j, 0))
    return pl.pallas_call(
        _fox_past_body,
        grid=(b, p // tc),
        in_specs=[row(fw), row(n), _const_spec((1, aw)), _const_spec((LANES, aw)), _const_spec((tc, tc))],
        out_specs=[row(aw), row(n)],
        out_shape=[jax.ShapeDtypeStruct((b, p, aw), BF16), jax.ShapeDtypeStruct((b, p, n), F32)],
        scratch_shapes=[pltpu.VMEM((1, n), F32)],
        compiler_params=_params("parallel", "arbitrary"),
        name="fox_past",
    )(past_k, past_logf, w["keep"], w["place"], ltri)


def _flash_body(q_ref, k_ref, vt_ref, o_ref, m_s, l_s, acc_s, sa_s, sb_s, *, tq, tqs, tk, tks, fr, fc, n_k, past,
                kv_len, chunk_causal, diag_aligned):
    q_start = past + pl.program_id(2) * tq
    q = q_ref[0]
    m_s[...] = jnp.full_like(m_s, NEG_INF)
    l_s[...] = jnp.zeros_like(l_s)
    acc_s[...] = jnp.zeros_like(acc_s)
    shift = int(math.log2(CHUNK))
    hv = LANES // 2

    def block_kind(r0, rn, c0, cn):
        if chunk_causal:
            k_lo, k_hi, q_lo, q_hi = r0 >> shift, (r0 + rn - 1) >> shift, c0 >> shift, (c0 + cn - 1) >> shift
        else:
            k_lo, k_hi, q_lo, q_hi = r0, r0 + rn - 1, c0, c0 + cn - 1
        return "visible" if k_hi <= q_lo else ("hidden" if k_lo > q_hi else "partial")

    def diag_streams(d):
        out = []
        for r0 in range(0, tk, tks):
            for hh in range(2):
                for c0 in range(0, tq, tqs):
                    halves = [(c, block_kind(d * tk + r0, tks, c, tks)) for c in range(c0, c0 + tqs, tks)]
                    if all(kind == "visible" for _, kind in halves):
                        out.append((hh, r0, tks, c0, tqs, "visible"))
                    else:
                        out += [(hh, r0, tks, c, tks, kind) for c, kind in halves if kind != "hidden"]
        return out

    full_streams = [(hh, r0, fr, c0, fc, "visible") for r0 in range(0, tk, fr) for hh in range(2)
                    for c0 in range(0, tq, fc)]
    mask_streams = [st[:5] + ("partial",) for st in full_streams]

    def score(kblk, stream):
        hh, r0, rn, c0, cn, _ = stream
        head = slice(hh * LANES, (hh + 1) * LANES)
        return _dot_nt(kblk[r0:r0 + rn, head], q[c0:c0 + cn, head])

    def key_block(kt):
        return k_ref[0, pl.ds(pl.multiple_of(jnp.minimum(kt, n_k - 1) * tk, tk), tk), :]

    def absorb(s, stream, kt):
        hh, r0, rn, c0, cn, kind = stream
        cols = slice(c0, c0 + cn)
        k0 = pl.multiple_of(kt * tk, tk)
        if kind == "partial":
            kpos = k0 + r0 + lax.broadcasted_iota(jnp.int32, (rn, cn), 0)
            qpos = q_start + c0 + lax.broadcasted_iota(jnp.int32, (rn, cn), 1)
            if chunk_causal:
                vis = lax.shift_right_logical(kpos, shift) <= lax.shift_right_logical(qpos, shift)
            else:
                vis = kpos <= qpos
            s = jnp.where(jnp.logical_and(vis, kpos < kv_len), s, NEG_INF)
        m_old = m_s[hh, :, cols]
        m_new = jnp.maximum(m_old, jnp.max(s, axis=0, keepdims=True))
        alpha = jnp.exp2(m_old - m_new)
        p = jnp.exp2(s - m_new)
        l_s[hh, :, cols] = alpha * l_s[hh, :, cols] + jnp.sum(p, axis=0, keepdims=True)
        vt = vt_ref[0, hh * hv:(hh + 1) * hv, pl.ds(pl.multiple_of(k0 + r0, LANES), rn)]
        acc_s[hh, :, cols] = alpha * acc_s[hh, :, cols] + _dot(vt, p.astype(BF16))
        m_s[hh, :, cols] = m_new

    def single_tile(kt, streams, after_first_scores=None):
        kblk = key_block(kt)
        ahead = 2
        pending = [score(kblk, st) for st in streams[:ahead]]
        if after_first_scores is not None:
            after_first_scores()
        for idx, st in enumerate(streams):
            s = pending.pop(0)
            if idx + ahead < len(streams):
                pending.append(score(kblk, streams[idx + ahead]))
            absorb(s, st, kt)

    def store_scores(buf, kt):
        kblk = key_block(kt)
        for st in full_streams:
            hh, r0, rn, c0, cn, _ = st
            buf[hh, r0:r0 + rn, c0:c0 + cn] = score(kblk, st)

    def absorb_stored(buf, kt):
        for st in full_streams:
            hh, r0, rn, c0, cn, _ = st
            absorb(buf[hh, r0:r0 + rn, c0:c0 + cn], st, kt)

    def tile_pair(i, carry):
        kt = 2 * i
        store_scores(sb_s, kt + 1)
        absorb_stored(sa_s, kt)
        store_scores(sa_s, kt + 2)
        absorb_stored(sb_s, kt + 1)
        return carry

    def full_tile(kt, carry):
        single_tile(kt, full_streams)
        return carry

    def masked_tile(kt, carry):
        single_tile(kt, mask_streams)
        return carry

    n_full = jnp.minimum(q_start // tk, kv_len // tk)
    q_last = q_start + tq - 1
    k_hi = (q_last // CHUNK + 1) * CHUNK if chunk_causal else q_last + 1
    n_end = jnp.minimum((k_hi + tk - 1) // tk, n_k)
    n_pair = n_full // 2
    if diag_aligned:
        for d in range(tq // tk):
            single_tile(n_full + d, diag_streams(d),
                        after_first_scores=(lambda: store_scores(sa_s, 0)) if d == 0 else None)
        lax.fori_loop(0, n_pair, tile_pair, 0)
        lax.fori_loop(2 * n_pair, n_full, full_tile, 0)
    else:
        @pl.when(n_pair > 0)
        def _():
            store_scores(sa_s, 0)

        lax.fori_loop(0, n_pair, tile_pair, 0)
        lax.fori_loop(2 * n_pair, n_full, full_tile, 0)
        lax.fori_loop(n_full, n_end, masked_tile, 0)
    out_t = jnp.concatenate([acc_s[0] / l_s[0], acc_s[1] / l_s[1]], axis=0)
    o_ref[0] = out_t.T.astype(BF16)


def _flash(q, k, vt, *, past, kv_len, chunk_causal):
    b, t, w = q.shape
    lp = k.shape[1]
    n_pairs = w // (2 * LANES)
    t_pad = max(t, LANES)
    if t_pad > t:
        q = _pad_rows(q, t_pad)
    if past == 0 and t % KEY_TILE == 0 and lp % KEY_TILE == 0:
        tk = KEY_TILE
        tq = QUERY_TILE if t % QUERY_TILE == 0 else KEY_TILE
    elif t_pad == LANES:
        tq, tk = LANES, lp
    else:
        tq = tk = LANES
    assert t_pad % tq == 0 and lp % tk == 0, (t, lp, tq, tk)
    tks = MXU_TILE if tk % MXU_TILE == 0 and tq > LANES else tk
    tqs = min(tq, 2 * MXU_TILE)
    fr, fc = (tk, MXU_TILE) if tks < tk and tq % MXU_TILE == 0 else (tks, tqs)
    out = pl.pallas_call(
        functools.partial(_flash_body, tq=tq, tqs=tqs, tk=tk, tks=tks, fr=fr, fc=fc, n_k=lp // tk, past=past,
                          kv_len=kv_len, chunk_causal=chunk_causal,
                          diag_aligned=(past == 0 and tq % tk == 0 and kv_len == lp and tks < tk)),
        grid=(b, n_pairs, t_pad // tq),
        in_specs=[pl.BlockSpec((1, tq, 2 * LANES), lambda i, j, s: (i, s, j)),
                  pl.BlockSpec((1, lp, 2 * LANES), lambda i, j, s: (i, 0, j)),
                  pl.BlockSpec((1, LANES, lp), lambda i, j, s: (i, j, 0))],
        out_specs=pl.BlockSpec((1, tq, LANES), lambda i, j, s: (i, s, j)),
        out_shape=jax.ShapeDtypeStruct((b, t_pad, n_pairs * LANES), BF16),
        scratch_shapes=[pltpu.VMEM((2, 1, tq), F32), pltpu.VMEM((2, 1, tq), F32),
                        pltpu.VMEM((2, LANES // 2, tq), F32),
                        pltpu.VMEM((2, tk, tq), F32), pltpu.VMEM((2, tk, tq), F32)],
        compiler_params=_params("parallel", "parallel", "arbitrary"),
        name="flash_mla" if chunk_causal else "flash_fox",
    )(q, k, vt)
    return out[:, :t]


def _row(v):
    return v.reshape(1, -1).astype(F32)


def _pad_lanes(x, lo, total):
    pad = [(0, 0)] * (x.ndim - 1) + [(lo, total - lo - x.shape[-1])]
    return jnp.pad(x, pad)


def _ffn_weights(g, w_in, w_out):
    f = w_out.shape[0]
    return dict(g=_row(g), wg=w_in[:, :f].astype(BF16), wu=w_in[:, f:].astype(BF16), wo=w_out.astype(BF16))


def _mem_weights(g, g_src, w_q, w_kv, w_o, g_q, g_k):
    d = w_q.shape[0]
    kv = w_kv.reshape(d, MEM_HEADS, 2, MEM_HEAD_DIM)
    return dict(g=_row(g), g_src=_row(g_src), wq=w_q.astype(BF16), wo=w_o.astype(BF16),
                wk=kv[:, :, 0].reshape(d, MEM_WIDTH).astype(BF16),
                wv=kv[:, :, 1].reshape(d, MEM_WIDTH).astype(BF16),
                gq=_row(g_q) * (MEM_HEAD_DIM ** -0.5), gk=_row(g_k))


def _even_weights(g_mix, w_in, g_qlat, g_kvlat, w_uq, w_ukv, g_q, g_k, conv_w, conv_b, gate_w, gate_b, lam, w_out):
    d = w_in.shape[0]
    o1 = MLA_Q_LORA
    o2 = o1 + MLA_KV_LORA
    o3 = o2 + MLA_ROPE
    o4 = o3 + LRU_WIDTH
    half = MLA_ROPE // 2
    swap_halves = lambda r: jnp.concatenate([r[..., half:], r[..., :half]], axis=-1)
    uq3 = w_uq.reshape(MLA_Q_LORA, MLA_HEADS, MLA_QK)
    uq = _pad_lanes(uq3, 0, LANES)
    uq_swap = _pad_lanes(swap_halves(uq3[:, :, MLA_NOPE:]), MLA_NOPE, LANES)
    kr = w_in[:, o2:o3]
    ukv = w_ukv.reshape(MLA_KV_LORA, MLA_HEADS, MLA_NOPE + MLA_V)
    uk = _pad_lanes(ukv[:, :, :MLA_NOPE], 0, LANES)
    blk = LRU_WIDTH // LRU_BLOCKS
    eye = jnp.eye(LRU_BLOCKS, dtype=F32)
    wr = jnp.einsum("ncd,nm->ncmd", gate_w[:, :, :blk], eye).reshape(LRU_WIDTH, LRU_WIDTH)
    wi = jnp.einsum("ncd,nm->ncmd", gate_w[:, :, blk:], eye).reshape(LRU_WIDTH, LRU_WIDTH)
    return dict(
        g_mix=_row(g_mix), wcq=w_in[:, :o1].astype(BF16), g_qlat=_row(g_qlat),
        wuq=uq.reshape(MLA_Q_LORA, MLA_HEADS * LANES).astype(BF16),
        wuq_swap=uq_swap.reshape(MLA_Q_LORA, MLA_HEADS * LANES).astype(BF16),
        gq=_pad_lanes(_row(g_q), 0, LANES) * (MLA_QK ** -0.5 * LOG2E),
        wckv=w_in[:, o1:o2].astype(BF16), g_kvlat=_row(g_kvlat),
        wkr=_pad_lanes(kr, MLA_NOPE, LANES).astype(BF16),
        wkr_swap=_pad_lanes(swap_halves(kr), MLA_NOPE, LANES).astype(BF16),
        wuk=uk.reshape(MLA_KV_LORA, MLA_HEADS * LANES).astype(BF16),
        wuvt=ukv[:, :, MLA_NOPE:].reshape(MLA_KV_LORA, MLA_HEADS * MLA_V).T.astype(BF16),
        gk=_pad_lanes(_row(g_k), 0, LANES),
        wrec=w_in[:, o3:o4].astype(BF16), wgate=w_in[:, o4:].astype(BF16),
        conv_w=conv_w.astype(F32), conv_b=_row(conv_b), wr=wr.astype(BF16), wi=wi.astype(BF16),
        br=_row(gate_b[:, :blk]), bi=_row(gate_b[:, blk:]), lam=_row(lam),
        wo_attn=w_out[:MLA_HEADS * MLA_V].astype(BF16), wo_rec=w_out[MLA_HEADS * MLA_V:].astype(BF16))


def _odd_weights(g_mix, w_in, b_f, g_q, g_k, w_out):
    fw = FOX_WIDTH
    lane = jnp.arange(FOX_HEADS * LANES)
    hd, within = lane // LANES, lane % LANES
    own_low = hd % 2 == 0
    keep = jnp.where(own_low, within < FOX_HEAD_DIM, within >= FOX_HEAD_DIM)
    part = within - jnp.where(own_low, FOX_HEAD_DIM, 0)
    is_bias = (part >= 0) & (part < 3)
    src = part * FOX_HEADS + hd
    place = ((jnp.arange(LANES)[:, None] == src[None, :]) & is_bias[None, :]).astype(BF16)
    return dict(
        keep=keep.astype(F32)[None, :], ones=is_bias.astype(F32)[None, :], place=place,
        g_mix=_row(g_mix), wq=w_in[:, :fw].astype(BF16), wk=w_in[:, fw:2 * fw].astype(BF16),
        wv=w_in[:, 2 * fw:3 * fw].astype(BF16), wf=_pad_lanes(w_in[:, 3 * fw:], 0, LANES).astype(BF16),
        bf=_pad_lanes(_row(b_f), 0, LANES),
        gq=jnp.tile(_row(g_q), (1, FOX_HEADS)) * (FOX_HEAD_DIM ** -0.5 * LOG2E),
        gk=jnp.tile(_row(g_k), (1, FOX_HEADS)), wo=w_out.astype(BF16))


def _rope_tables(pos):
    half = MLA_ROPE // 2
    inv_freq = ROPE_THETA ** (-jnp.arange(half, dtype=F32) / half)
    ang = pos.astype(F32)[:, None] * inv_freq[None, :]
    cos, sin = jnp.cos(ang), jnp.sin(ang)
    c = jnp.concatenate([jnp.ones((pos.shape[0], MLA_NOPE), F32), cos, cos,
                         jnp.ones((pos.shape[0], LANES - MLA_QK), F32)], axis=-1)
    s = _pad_lanes(jnp.concatenate([-sin, sin], axis=-1), MLA_NOPE, LANES)
    return c, s


def _pad_rows(x, total):
    return jnp.pad(x, [(0, 0), (0, total - x.shape[1])] + [(0, 0)] * (x.ndim - 2))


def _kv_pad_len(t, past):
    l = past + t
    if past == 0 and t % KEY_TILE == 0:
        return l
    return -(-l // LANES) * LANES


def _even_layer(h, past, w, state):
    b, t, d = h.shape
    past_lat, past_krope, h0, conv_prev = state
    lp = _kv_pad_len(t, past)
    tables = _rope_tables(past + jnp.arange(t))
    if past == 0 and lp == t and t % LANES == 0:
        q, lat_new, krope_new, k, vt = _mla_in(h, w, tables, with_kv=True)
    else:
        q, lat_new, krp_new = _mla_in(h, w, tables, with_kv=False)
        krope_new = krp_new[:, :, MLA_NOPE:MLA_QK]
        lat_all = _pad_rows(jnp.concatenate([past_lat, lat_new], axis=1), lp)
        krp_all = _pad_rows(jnp.concatenate([_pad_lanes(past_krope, MLA_NOPE, LANES), krp_new], axis=1), lp)
        k, vt = _mla_kv(lat_all, krp_all, w)
    attn = _flash(q, k, vt, past=past, kv_len=past + t, chunk_causal=True)
    conv_prev8 = jnp.pad(conv_prev, ((0, 0), (HALO - (CONV_WIDTH - 1), 0), (0, 0)))
    y_rec, h_last, conv_last = _lru(h, w, conv_prev8, h0[:, None, :])
    new = (lat_new, krope_new, h_last[:, 0], conv_last[:, HALO - (CONV_WIDTH - 1):])
    return [attn, y_rec], [w["wo_attn"], w["wo_rec"]], new


def _odd_layer(h, past, w, state):
    b, t, d = h.shape
    past_k, past_v, past_logf = state
    if past > 0:
        ka_past, c_past = _fox_past(past_k.reshape(b, past, FOX_WIDTH),
                                    _pad_lanes(past_logf.astype(F32), 0, LANES), w)
        c0 = c_past[:, past - 1:past, :]
    else:
        ka_past = jnp.zeros((b, 0, FOX_HEADS * LANES), BF16)
        c0 = jnp.zeros((b, 1, LANES), F32)
    q, k32, v32, ka_new, vt_new, logf = _fox_in(h, w, c0)
    lp = _kv_pad_len(t, past)
    k_all = _pad_rows(jnp.concatenate([ka_past, ka_new], axis=1), lp)
    vt_past = jnp.swapaxes(past_v.reshape(b, past, FOX_WIDTH).astype(BF16), 1, 2)
    vt_all = jnp.pad(jnp.concatenate([vt_past, vt_new], axis=2), ((0, 0), (0, 0), (0, lp - past - t)))
    attn = _flash(q, k_all, vt_all, past=past, kv_len=past + t, chunk_causal=False)
    new = (k32.reshape(b, t, FOX_HEADS, FOX_HEAD_DIM), v32.reshape(b, t, FOX_HEADS, FOX_HEAD_DIM),
           logf)
    return [attn], [w["wo"]], new


def _trunk(x, past, layers, mem_kvs, even_states, odd_states):
    b, t, d = x.shape
    even_new, odd_new = [], []
    for li, lw in enumerate(layers):
        h = _ffn(x.reshape(b * t, d), lw["ffn1"]).reshape(b, t, d)
        if li % 2 == 0:
            parts, w_parts, new = _even_layer(h, past, lw["mix"], even_states[li // 2])
            even_new.append(new)
        else:
            parts, w_parts, new = _odd_layer(h, past, lw["mix"], odd_states[li // 2])
            odd_new.append(new)
        x = _post_mixer(h, parts, w_parts, mem_kvs[li][0], mem_kvs[li][1], lw["mem"], lw["ffn2"])
    return x, even_new, odd_new


def kernel(x_prompt, x_sample, mem_prompt, cache_mla_latent, cache_mla_krope, state_lru_h, state_lru_conv, cache_fox_k, cache_fox_v, cache_fox_logf, cache_mem_k, cache_mem_v, norm_ffn1, ffn1_w_in, ffn1_w_out, norm_mix, norm_mem, norm_mem_src, mem_w_q, mem_w_kv, mem_w_o, mem_g_q, mem_g_k, norm_ffn2, ffn2_w_in, ffn2_w_out, ev_w_in, ev_g_qlat, ev_g_kvlat, ev_w_uq, ev_w_ukv, ev_g_q, ev_g_k, ev_conv_w, ev_conv_b, ev_gate_w, ev_gate_b, ev_lambda, ev_w_out, od_w_in, od_b_f, od_g_q, od_g_k, od_w_out):
    depth = norm_ffn1.shape[0]
    n_even, n_odd = (depth + 1) // 2, depth // 2
    b, _, _ = x_prompt.shape
    bs = x_sample.shape[0]
    past = cache_mla_latent.shape[2] if n_even else cache_fox_k.shape[2]

    layers = []
    for li in range(depth):
        j = li // 2
        if li % 2 == 0:
            mix = _even_weights(norm_mix[li], ev_w_in[j], ev_g_qlat[j], ev_g_kvlat[j], ev_w_uq[j], ev_w_ukv[j],
                                ev_g_q[j], ev_g_k[j], ev_conv_w[j], ev_conv_b[j], ev_gate_w[j], ev_gate_b[j],
                                ev_lambda[j], ev_w_out[j])
        else:
            mix = _odd_weights(norm_mix[li], od_w_in[j], od_b_f[j], od_g_q[j], od_g_k[j], od_w_out[j])
        layers.append(dict(
            ffn1=_ffn_weights(norm_ffn1[li], ffn1_w_in[li], ffn1_w_out[li]),
            ffn2=_ffn_weights(norm_ffn2[li], ffn2_w_in[li], ffn2_w_out[li]),
            mem=_mem_weights(norm_mem[li], norm_mem_src[li], mem_w_q[li], mem_w_kv[li], mem_w_o[li],
                             mem_g_q[li], mem_g_k[li]),
            mix=mix))

    p_mem_k, p_mem_v, mem_kb, mem_vb = _mem_kv(mem_prompt, [lw["mem"] for lw in layers])
    ev0 = [(jnp.zeros((b, 0, MLA_KV_LORA), F32), jnp.zeros((b, 0, MLA_ROPE), F32),
            jnp.zeros((b, LRU_WIDTH), F32), jnp.zeros((b, CONV_WIDTH - 1, LRU_WIDTH), F32))
           for _ in range(n_even)]
    od0 = [(jnp.zeros((b, 0, FOX_HEADS, FOX_HEAD_DIM), F32), jnp.zeros((b, 0, FOX_HEADS, FOX_HEAD_DIM), F32),
            jnp.zeros((b, 0, FOX_HEADS), F32)) for _ in range(n_odd)]
    y_prompt, ev_p, od_p = _trunk(x_prompt, 0, layers, [(mem_kb[li], mem_vb[li]) for li in range(depth)],
                                  ev0, od0)

    m_tok = cache_mem_k.shape[2]
    mem_s = [(cache_mem_k[li].reshape(bs, m_tok, MEM_WIDTH).astype(BF16),
              cache_mem_v[li].reshape(bs, m_tok, MEM_WIDTH).astype(BF16)) for li in range(depth)]
    ev_s = [(cache_mla_latent[j], cache_mla_krope[j], state_lru_h[j], state_lru_conv[j]) for j in range(n_even)]
    od_s = [(cache_fox_k[j], cache_fox_v[j], cache_fox_logf[j]) for j in range(n_odd)]
    y_sample, ev_n, od_n = _trunk(x_sample, past, layers, mem_s, ev_s, od_s)

    p_even = [jnp.stack([s[f] for s in ev_p]) for f in range(4)]
    p_odd = [jnp.stack([s[f] for s in od_p]) for f in range(3)]
    s_even = [jnp.stack([s[f] for s in ev_n]) for f in range(4)]
    s_odd = [jnp.stack([s[f] for s in od_n]) for f in range(3)]
    return (y_prompt, y_sample, *p_even, *p_odd, p_mem_k, p_mem_v, *s_even, *s_odd)
```

```python
import functools
import math

import jax
import jax.numpy as jnp
from jax import lax
from jax.experimental import pallas as pl
from jax.experimental.pallas import tpu as pltpu

F32 = jnp.float32
BF16 = jnp.bfloat16

NORM_EPS = 1e-6
NEG_INF = -1e30
LOG2E = math.log2(math.e)
CHUNK = 64
LANES = 128
SUBLANES = 8
ROPE_THETA = 10000.0
LRU_C = 8.0
MLA_HEADS = 8
MLA_NOPE = 64
MLA_ROPE = 32
MLA_QK = MLA_NOPE + MLA_ROPE
MLA_V = 64
MLA_Q_LORA = 256
MLA_KV_LORA = 128
LRU_WIDTH = 512
LRU_BLOCKS = 8
CONV_WIDTH = 4
FOX_HEADS = 16
FOX_HEAD_DIM = 64
FOX_WIDTH = FOX_HEADS * FOX_HEAD_DIM
MEM_HEADS = 4
MEM_HEAD_DIM = 128
MEM_WIDTH = MEM_HEADS * MEM_HEAD_DIM
HALO = 8

MXU_TILE = 256
ROW_TILE = 512
KEY_TILE = 512
QUERY_TILE = 2048

VMEM_LIMIT = 56 * 1024 * 1024


def _dot(a, b):
    return jnp.dot(a, b, preferred_element_type=F32)


def _dot_nt(a, b):
    return lax.dot_general(a, b, (((1,), (1,)), ((), ())), preferred_element_type=F32)


def _rms(x, g):
    return x * lax.rsqrt(jnp.mean(x * x, axis=-1, keepdims=True) + NORM_EPS) * g


def _head_rms(x, g, n_live):
    ss = jnp.sum(x * x, axis=-1, keepdims=True) * (1.0 / n_live)
    return x * lax.rsqrt(ss + NORM_EPS) * g


def _sigmoid(x):
    return 1.0 / (1.0 + jnp.exp(-x))


def _log1p(y):
    u = 1.0 + y
    d = u - 1.0
    return jnp.where(d == 0.0, y, jnp.log(u) * (y / jnp.where(d == 0.0, 1.0, d)))


def _softplus(x):
    return jnp.maximum(x, 0.0) + _log1p(jnp.exp(-jnp.abs(x)))


def _gelu_tanh(x):
    return 0.5 * x * (1.0 + jnp.tanh(math.sqrt(2.0 / math.pi) * (x + 0.044715 * (x * x * x))))


def _split_bf16(x, parts):
    out = []
    r = x
    for _ in range(parts):
        p = r.astype(BF16)
        out.append(p)
        r = r - p.astype(F32)
    return out


def _const_spec(shape):
    nd = len(shape)
    return pl.BlockSpec(shape, lambda *_: (0,) * nd, pipeline_mode=pl.Buffered(1))


def _params(*sem):
    return pltpu.CompilerParams(dimension_semantics=sem, vmem_limit_bytes=VMEM_LIMIT)


def _row_tile(n, cap):
    t = min(n, cap)
    assert n % t == 0, (n, t)
    return t


FFN_CHUNKS = 2


def _ffn_chunk_bounds(f):
    tiles = -(-f // MXU_TILE)
    per = -(-tiles // FFN_CHUNKS) * MXU_TILE
    edges = [min(i * per, f) for i in range(FFN_CHUNKS + 1)]
    return [(lo, hi) for lo, hi in zip(edges[:-1], edges[1:]) if hi > lo]


def _swiglu_half_step(x, g_ref, wg_ref, wu_ref, wo_ref):
    hb = _rms(x, g_ref[...]).astype(BF16)
    acc = jnp.zeros_like(x)
    for lo, hi in _ffn_chunk_bounds(wg_ref.shape[1]):
        sl = slice(lo, hi)
        gate = _dot(hb, wg_ref[:, sl])
        up = _dot(hb, wu_ref[:, sl])
        act = (gate * _sigmoid(gate) * up).astype(BF16)
        acc = acc + _dot(act, wo_ref[sl, :])
    return x + 0.5 * acc


def _ffn_body(x_ref, g_ref, wg_ref, wu_ref, wo_ref, o_ref):
    o_ref[...] = _swiglu_half_step(x_ref[...], g_ref, wg_ref, wu_ref, wo_ref)


def _ffn(x2, w):
    n, d = x2.shape
    f = w["wg"].shape[1]
    tm = _row_tile(n, ROW_TILE)
    return pl.pallas_call(
        _ffn_body,
        grid=(n // tm,),
        in_specs=[pl.BlockSpec((tm, d), lambda i: (i, 0)), _const_spec((1, d)),
                  _const_spec((d, f)), _const_spec((d, f)), _const_spec((f, d))],
        out_specs=pl.BlockSpec((tm, d), lambda i: (i, 0)),
        out_shape=jax.ShapeDtypeStruct((n, d), F32),
        compiler_params=_params("parallel"),
        name="ffn",
    )(x2, w["g"], w["wg"], w["wu"], w["wo"])


def _mem_cross_attention(h, tm, g_ref, wq_ref, gq_ref, mk_ref, mv_ref, wo_ref):
    hb = _rms(h, g_ref[...]).astype(BF16)
    q = _dot(hb, wq_ref[...])
    rows = []
    for i in range(h.shape[0] // tm):
        outs = []
        for hd in range(MEM_HEADS):
            sl = slice(hd * MEM_HEAD_DIM, (hd + 1) * MEM_HEAD_DIM)
            qh = _head_rms(q[i * tm:(i + 1) * tm, sl], gq_ref[...], MEM_HEAD_DIM).astype(BF16)
            s = _dot_nt(qh, mk_ref[i, :, sl])
            e = jnp.exp(s - jnp.max(s, axis=-1, keepdims=True))
            p = e / jnp.sum(e, axis=-1, keepdims=True)
            outs.append(_dot(p.astype(BF16), mv_ref[i, :, sl]).astype(BF16))
        rows.append(jnp.concatenate(outs, axis=-1))
    o = rows[0] if len(rows) == 1 else jnp.concatenate(rows, axis=0)
    return h + _dot(o, wo_ref[...])


def _post_mixer_body(*refs, n_parts):
    h_ref = refs[0]
    parts = refs[1:1 + n_parts]
    w_parts = refs[1 + n_parts:1 + 2 * n_parts]
    (mg_ref, mwq_ref, mgq_ref, mk_ref, mv_ref, mwo_ref,
     fg_ref, fwg_ref, fwu_ref, fwo_ref, o_ref) = refs[1 + 2 * n_parts:]
    bb, tm, d = h_ref.shape
    h = h_ref[...].reshape(bb * tm, d)
    for p_ref, w_ref in zip(parts, w_parts):
        h = h + _dot(p_ref[...].reshape(bb * tm, p_ref.shape[2]), w_ref[...])
    h = _mem_cross_attention(h, tm, mg_ref, mwq_ref, mgq_ref, mk_ref, mv_ref, mwo_ref)
    o_ref[...] = _swiglu_half_step(h, fg_ref, fwg_ref, fwu_ref, fwo_ref).reshape(bb, tm, d)


def _post_mixer(h, parts, w_parts, mk, mv, wm, wf):
    b, t, d = h.shape
    m = mk.shape[1]
    f = wf["wg"].shape[1]
    tm = _row_tile(t, ROW_TILE)
    bb = b if b * t <= ROW_TILE and tm % SUBLANES == 0 else 1
    row = lambda n: pl.BlockSpec((bb, tm, n), lambda i, j: (i, j, 0))
    mem = lambda: pl.BlockSpec((bb, m, MEM_WIDTH), lambda i, j: (i, 0, 0))
    in_specs = [row(d)] + [row(p.shape[2]) for p in parts] + [_const_spec(w.shape) for w in w_parts]
    in_specs += [_const_spec((1, d)), _const_spec((d, MEM_WIDTH)), _const_spec((1, MEM_HEAD_DIM)), mem(), mem(),
                 _const_spec((MEM_WIDTH, d)),
                 _const_spec((1, d)), _const_spec((d, f)), _const_spec((d, f)), _const_spec((f, d))]
    return pl.pallas_call(
        functools.partial(_post_mixer_body, n_parts=len(parts)),
        grid=(b // bb, t // tm),
        in_specs=in_specs,
        out_specs=row(d),
        out_shape=jax.ShapeDtypeStruct((b, t, d), F32),
        compiler_params=_params("parallel", "parallel"),
        name="post_mixer",
    )(h, *parts, *w_parts, wm["g"], wm["wq"], wm["gq"], mk, mv, wm["wo"],
      wf["g"], wf["wg"], wf["wu"], wf["wo"])


def _mem_kv_body(m_ref, g_ref, wk_ref, wv_ref, gk_ref, k32_ref, v32_ref, kb_ref, vb_ref):
    hb = _rms(m_ref[0], g_ref[0]).astype(BF16)
    k = _dot(hb, wk_ref[0])
    v = _dot(hb, wv_ref[0])
    for hd in range(MEM_HEADS):
        sl = slice(hd * MEM_HEAD_DIM, (hd + 1) * MEM_HEAD_DIM)
        kh = _head_rms(k[:, sl], gk_ref[0], MEM_HEAD_DIM)
        k32_ref[0, 0, :, hd, :] = kh
        v32_ref[0, 0, :, hd, :] = v[:, sl]
        kb_ref[0, 0, :, sl] = kh.astype(BF16)
    vb_ref[0, 0] = v.astype(BF16)


def _mem_kv(mem, ws):
    b, m, d = mem.shape
    n = len(ws)
    stack = lambda name: jnp.stack([w[name] for w in ws])
    per_layer = lambda *shape: pl.BlockSpec((1,) + shape, lambda l, i: (l,) + (0,) * len(shape))
    blk = lambda: pl.BlockSpec((1, 1, m, MEM_WIDTH), lambda l, i: (l, i, 0, 0))
    blk5 = lambda: pl.BlockSpec((1, 1, m, MEM_HEADS, MEM_HEAD_DIM), lambda l, i: (l, i, 0, 0, 0))
    shape5 = (n, b, m, MEM_HEADS, MEM_HEAD_DIM)
    return pl.pallas_call(
        _mem_kv_body,
        grid=(n, b),
        in_specs=[pl.BlockSpec((1, m, d), lambda l, i: (i, 0, 0)), per_layer(1, d),
                  per_layer(d, MEM_WIDTH), per_layer(d, MEM_WIDTH), per_layer(1, MEM_HEAD_DIM)],
        out_specs=[blk5(), blk5(), blk(), blk()],
        out_shape=[jax.ShapeDtypeStruct(shape5, F32), jax.ShapeDtypeStruct(shape5, F32),
                   jax.ShapeDtypeStruct((n, b, m, MEM_WIDTH), BF16),
                   jax.ShapeDtypeStruct((n, b, m, MEM_WIDTH), BF16)],
        compiler_params=_params("parallel", "parallel"),
        name="mem_kv",
    )(mem, stack("g_src"), stack("wk"), stack("wv"), stack("gk"))


def _mla_keys_values(lat, krp, wuk_ref, wuvt_ref, gk_ref, k_ref, vt_ref):
    lb = lat.astype(BF16)
    kn = _dot(lb, wuk_ref[...])
    for hd in range(MLA_HEADS):
        sl = slice(hd * LANES, (hd + 1) * LANES)
        k_ref[0, :, sl] = _head_rms(kn[:, sl] + krp, gk_ref[...], MLA_QK).astype(BF16)
    vt_ref[0] = _dot_nt(wuvt_ref[...], lb).astype(BF16)


def _mla_in_body(h_ref, g_ref, wcq_ref, gql_ref, wuq_ref, wuqs_ref, gq_ref, wckv_ref, gkv_ref, wkr_ref, wkrs_ref,
                 c_ref, s_ref, *rest):
    q_ref, lat_ref, krp_ref = rest[-5:-2] if len(rest) > 3 else rest
    hb = _rms(h_ref[0], g_ref[...]).astype(BF16)
    c, s = c_ref[...], s_ref[...]
    cq = _rms(_dot(hb, wcq_ref[...]), gql_ref[...]).astype(BF16)
    q = _dot(cq, wuq_ref[...])
    q_partner = _dot(cq, wuqs_ref[...])
    for hd in range(MLA_HEADS):
        sl = slice(hd * LANES, (hd + 1) * LANES)
        qh = q[:, sl] * c + q_partner[:, sl] * s
        q_ref[0, :, sl] = _head_rms(qh, gq_ref[...], MLA_QK).astype(BF16)
    lat = _rms(_dot(hb, wckv_ref[...]), gkv_ref[...])
    krp = _dot(hb, wkr_ref[...]) * c + _dot(hb, wkrs_ref[...]) * s
    lat_ref[0] = lat
    if len(rest) > 3:
        krp_ref[0] = krp[:, MLA_NOPE:MLA_QK]
        wuk_ref, wuvt_ref, gk_ref = rest[:3]
        _mla_keys_values(lat, krp, wuk_ref, wuvt_ref, gk_ref, *rest[-2:])
    else:
        krp_ref[0] = krp


def _mla_in(h, w, tables, with_kv):
    b, t, d = h.shape
    tm = _row_tile(t, ROW_TILE)
    row = lambda n: pl.BlockSpec((1, tm, n), lambda i, j: (i, j, 0))
    tab = lambda: pl.BlockSpec((tm, LANES), lambda i, j: (j, 0))
    vw = MLA_HEADS * MLA_V
    in_specs = [row(d), _const_spec((1, d)),
                _const_spec((d, MLA_Q_LORA)), _const_spec((1, MLA_Q_LORA)),
                _const_spec((MLA_Q_LORA, MLA_HEADS * LANES)), _const_spec((MLA_Q_LORA, MLA_HEADS * LANES)),
                _const_spec((1, LANES)),
                _const_spec((d, MLA_KV_LORA)), _const_spec((1, MLA_KV_LORA)),
                _const_spec((d, LANES)), _const_spec((d, LANES)),
                tab(), tab()]
    args = [h, w["g_mix"], w["wcq"], w["g_qlat"], w["wuq"], w["wuq_swap"], w["gq"], w["wckv"], w["g_kvlat"],
            w["wkr"], w["wkr_swap"], *tables]
    kr_w = MLA_ROPE if with_kv else LANES
    out_specs = [row(MLA_HEADS * LANES), row(MLA_KV_LORA), row(kr_w)]
    out_shape = [jax.ShapeDtypeStruct((b, t, MLA_HEADS * LANES), BF16),
                 jax.ShapeDtypeStruct((b, t, MLA_KV_LORA), F32),
                 jax.ShapeDtypeStruct((b, t, kr_w), F32)]
    if with_kv:
        in_specs += [_const_spec((MLA_KV_LORA, MLA_HEADS * LANES)), _const_spec((vw, MLA_KV_LORA)),
                     _const_spec((1, LANES))]
        args += [w["wuk"], w["wuvt"], w["gk"]]
        out_specs += [row(MLA_HEADS * LANES), pl.BlockSpec((1, vw, tm), lambda i, j: (i, 0, j))]
        out_shape += [jax.ShapeDtypeStruct((b, t, MLA_HEADS * LANES), BF16),
                      jax.ShapeDtypeStruct((b, vw, t), BF16)]
    return pl.pallas_call(
        _mla_in_body,
        grid=(b, t // tm),
        in_specs=in_specs,
        out_specs=out_specs,
        out_shape=out_shape,
        compiler_params=_params("parallel", "parallel"),
        name="mla_in",
    )(*args)


def _mla_kv_body(lat_ref, krp_ref, wuk_ref, wuvt_ref, gk_ref, k_ref, vt_ref):
    _mla_keys_values(lat_ref[0], krp_ref[0], wuk_ref, wuvt_ref, gk_ref, k_ref, vt_ref)


def _mla_kv(lat, krp, w):
    b, l, _ = lat.shape
    tl = ROW_TILE if l % ROW_TILE == 0 else l
    row = lambda n: pl.BlockSpec((1, tl, n), lambda i, j: (i, j, 0))
    vw = MLA_HEADS * MLA_V
    return pl.pallas_call(
        _mla_kv_body,
        grid=(b, l // tl),
        in_specs=[row(MLA_KV_LORA), row(LANES), _const_spec((MLA_KV_LORA, MLA_HEADS * LANES)),
                  _const_spec((vw, MLA_KV_LORA)), _const_spec((1, LANES))],
        out_specs=[row(MLA_HEADS * LANES), pl.BlockSpec((1, vw, tl), lambda i, j: (i, 0, j))],
        out_shape=[jax.ShapeDtypeStruct((b, l, MLA_HEADS * LANES), BF16),
                   jax.ShapeDtypeStruct((b, vw, l), BF16)],
        compiler_params=_params("parallel", "parallel"),
        name="mla_kv",
    )(lat, krp, w["wuk"], w["wuvt"], w["gk"])


def _lru_body(h_ref, g_ref, wrec_ref, wgate_ref, cw_ref, cb_ref, wr_ref, wi_ref, br_ref, bi_ref, lam_ref,
              cprev_ref, h0_ref, y_ref, hl_ref, cl_ref, buf, a_s, b_s, hcar, *, tm):
    @pl.when(pl.program_id(1) == 0)
    def _():
        buf[0:HALO, :] = cprev_ref[0]
        hcar[...] = h0_ref[0]

    hb = _rms(h_ref[0], g_ref[...]).astype(BF16)
    xr = _dot(hb, wrec_ref[...])
    xg = _dot(hb, wgate_ref[...])
    buf[HALO:HALO + tm, :] = xr
    xc = cb_ref[...] + xr * cw_ref[CONV_WIDTH - 1:CONV_WIDTH, :]
    for j in range(CONV_WIDTH - 1):
        off = HALO - (CONV_WIDTH - 1) + j
        xc = xc + cw_ref[j:j + 1, :] * buf[off:off + tm, :]
    xcb = xc.astype(BF16)
    r = _sigmoid(_dot(xcb, wr_ref[...]) + br_ref[...])
    i = _sigmoid(_dot(xcb, wi_ref[...]) + bi_ref[...])
    log_a = (-LRU_C) * r * _softplus(-lam_ref[...])
    a = jnp.exp(log_a)
    z = -jnp.tanh(log_a) * (a * a + 1.0)
    b = jnp.where(z > 0.0, z * lax.rsqrt(z), 0.0) * (i * xc)
    a_s[...] = a
    b_s[...] = b

    row = lax.broadcasted_iota(jnp.int32, (SUBLANES, a.shape[1]), 0)

    def step(g, hprev):
        r0 = pl.multiple_of(g * SUBLANES, SUBLANES)
        ag = a_s[pl.ds(r0, SUBLANES), :]
        bg = b_s[pl.ds(r0, SUBLANES), :]
        d = 1
        while d < SUBLANES:
            keep = row >= d
            a_up = jnp.where(keep, pltpu.roll(ag, d, 0), 1.0)
            b_up = jnp.where(keep, pltpu.roll(bg, d, 0), 0.0)
            bg = ag * b_up + bg
            ag = ag * a_up
            d *= 2
        hg = ag * hprev + bg
        b_s[pl.ds(r0, SUBLANES), :] = hg
        return hg[SUBLANES - 1:, :]

    hfin = lax.fori_loop(0, tm // SUBLANES, step, hcar[...], unroll=2)
    hcar[...] = hfin
    y_ref[0] = (_gelu_tanh(xg) * b_s[...]).astype(BF16)
    hl_ref[0] = hfin
    tail = buf[tm:tm + HALO, :]
    buf[0:HALO, :] = tail
    cl_ref[0] = tail


def _lru(h, w, conv_prev8, h0):
    b, t, d = h.shape
    tm = _row_tile(t, ROW_TILE)
    wd = LRU_WIDTH
    vec = lambda: _const_spec((1, wd))
    return pl.pallas_call(
        functools.partial(_lru_body, tm=tm),
        grid=(b, t // tm),
        in_specs=[pl.BlockSpec((1, tm, d), lambda i, j: (i, j, 0)), _const_spec((1, d)),
                  _const_spec((d, wd)), _const_spec((d, wd)), _const_spec((CONV_WIDTH, wd)), vec(),
                  _const_spec((wd, wd)), _const_spec((wd, wd)), vec(), vec(), vec(),
                  pl.BlockSpec((1, HALO, wd), lambda i, j: (i, 0, 0)),
                  pl.BlockSpec((1, 1, wd), lambda i, j: (i, 0, 0))],
        out_specs=[pl.BlockSpec((1, tm, wd), lambda i, j: (i, j, 0)),
                   pl.BlockSpec((1, 1, wd), lambda i, j: (i, 0, 0)),
                   pl.BlockSpec((1, HALO, wd), lambda i, j: (i, 0, 0))],
        out_shape=[jax.ShapeDtypeStruct((b, t, wd), BF16), jax.ShapeDtypeStruct((b, 1, wd), F32),
                   jax.ShapeDtypeStruct((b, HALO, wd), F32)],
        scratch_shapes=[pltpu.VMEM((tm + HALO, wd), F32), pltpu.VMEM((tm, wd), F32),
                        pltpu.VMEM((tm, wd), F32), pltpu.VMEM((1, wd), F32)],
        compiler_params=_params("parallel", "arbitrary"),
        name="lru",
    )(h, w["g_mix"], w["wrec"], w["wgate"], w["conv_w"], w["conv_b"], w["wr"], w["wi"], w["br"], w["bi"],
      w["lam"], conv_prev8, h0)


def _group_rms(x, g):
    low = lax.broadcasted_iota(jnp.int32, (1, LANES), 1) < FOX_HEAD_DIM
    out = []
    for j in range(x.shape[1] // LANES):
        xb = x[:, j * LANES:(j + 1) * LANES]
        sq = xb * xb
        s_low = jnp.sum(jnp.where(low, sq, 0.0), axis=-1, keepdims=True)
        s_high = jnp.sum(jnp.where(low, 0.0, sq), axis=-1, keepdims=True)
        ss = jnp.where(low, s_low, s_high)
        out.append(xb * lax.rsqrt(ss * (1.0 / FOX_HEAD_DIM) + NORM_EPS))
    return jnp.concatenate(out, axis=-1) * g


def _cumsum_rows(x, ltri):
    out = None
    for p in _split_bf16(x, 3):
        d = _dot(ltri, p)
        out = d if out is None else out + d
    return out


def _fox_aug(x, keep, bias):
    blocks = [x[:, (hd // 2) * LANES:(hd // 2 + 1) * LANES] for hd in range(FOX_HEADS)]
    return jnp.concatenate(blocks, axis=-1) * keep + bias


def _fox_key_bias(c, place):
    lane = lax.broadcasted_iota(jnp.int32, (1, LANES), 1)
    hi, mid, lo = _split_bf16(jnp.where(lane < FOX_HEADS, c * (-LOG2E), 0.0), 3)
    packed = (hi.astype(F32) + pltpu.roll(mid.astype(F32), FOX_HEADS, 1)
              + pltpu.roll(lo.astype(F32), 2 * FOX_HEADS, 1))
    return _dot(packed.astype(BF16), place)


def _fox_in_body(h_ref, g_ref, wq_ref, wk_ref, wv_ref, wf_ref, bf_ref, gq_ref, gk_ref,
                 keep_ref, ones_ref, place_ref, ltri_ref, c0_ref,
                 q_ref, k32_ref, v32_ref, ka_ref, vb_ref, lf_ref, lf_s, carry, *, tm, tc, values_transposed):
    @pl.when(pl.program_id(1) == 0)
    def _():
        carry[...] = c0_ref[0]

    hb = _rms(h_ref[0], g_ref[...]).astype(BF16)
    keep = keep_ref[...]
    q = _group_rms(_dot(hb, wq_ref[...]), gq_ref[...])
    q_ref[0] = _fox_aug(q, keep, ones_ref[...]).astype(BF16)
    k = _group_rms(_dot(hb, wk_ref[...]), gk_ref[...])
    k32_ref[0] = k
    v = _dot(hb, wv_ref[...])
    v32_ref[0] = v
    vb_ref[0] = (v.T if values_transposed else v).astype(BF16)
    logf =-_softplus(-(_dot(hb, wf_ref[...]) + bf_ref[...]))
    lf_ref[0] = logf[:, :FOX_HEADS]
    if tc > tm:
        lf_s[...] = jnp.zeros_like(lf_s)
    lf_s[0:tm, :] = logf
    c = carry[...] + _cumsum_rows(lf_s[...], ltri_ref[...])[0:tm, :]
    carry[...] = c[tm - 1:tm, :]
    ka_ref[0] = _fox_aug(k, keep, _fox_key_bias(c, place_ref[...])).astype(BF16)


def _fox_in(h, w, c0):
    b, t, d = h.shape
    tm = _row_tile(t, ROW_TILE)
    tc = max(tm, LANES)
    ltri = jnp.tril(jnp.ones((tc, tc), F32)).astype(BF16)
    row = lambda n: pl.BlockSpec((1, tm, n), lambda i, j: (i, j, 0))
    fw, aw = FOX_WIDTH, FOX_HEADS * LANES
    values_transposed = tm % LANES == 0
    v_spec = pl.BlockSpec((1, fw, tm), lambda i, j: (i, 0, j)) if values_transposed else row(fw)
    v_shape = (b, fw, t) if values_transposed else (b, t, fw)
    outs = pl.pallas_call(
        functools.partial(_fox_in_body, tm=tm, tc=tc, values_transposed=values_transposed),
        grid=(b, t // tm),
        in_specs=[row(d), _const_spec((1, d)),
                  _const_spec((d, fw)), _const_spec((d, fw)), _const_spec((d, fw)), _const_spec((d, LANES)),
                  _const_spec((1, LANES)), _const_spec((1, fw)), _const_spec((1, fw)),
                  _const_spec((1, aw)), _const_spec((1, aw)), _const_spec((LANES, aw)),
                  _const_spec((tc, tc)), pl.BlockSpec((1, 1, LANES), lambda i, j: (i, 0, 0))],
        out_specs=[row(aw), row(fw), row(fw), row(aw), v_spec, row(FOX_HEADS)],
        out_shape=[jax.ShapeDtypeStruct((b, t, aw), BF16), jax.ShapeDtypeStruct((b, t, fw), F32),
                   jax.ShapeDtypeStruct((b, t, fw), F32), jax.ShapeDtypeStruct((b, t, aw), BF16),
                   jax.ShapeDtypeStruct(v_shape, BF16), jax.ShapeDtypeStruct((b, t, FOX_HEADS), F32)],
        scratch_shapes=[pltpu.VMEM((tc, LANES), F32), pltpu.VMEM((1, LANES), F32)],
        compiler_params=_params("parallel", "arbitrary"),
        name="fox_in",
    )(h, w["g_mix"], w["wq"], w["wk"], w["wv"], w["wf"], w["bf"], w["gq"], w["gk"],
      w["keep"], w["ones"], w["place"], ltri, c0)
    q, k32, v32, ka, vb, logf = outs
    return q, k32, v32, ka, (vb if values_transposed else jnp.swapaxes(vb, 1, 2)), logf


def _fox_past_body(k_ref, lf_ref, keep_ref, place_ref, ltri_ref, ka_ref, c_ref, carry):
    @pl.when(pl.program_id(1) == 0)
    def _():
        carry[...] = jnp.zeros_like(carry)

    c = carry[...] + _cumsum_rows(lf_ref[0], ltri_ref[...])
    c_ref[0] = c
    carry[...] = c[c.shape[0] - 1:, :]
    ka_ref[0] = _fox_aug(k_ref[0], keep_ref[...], _fox_key_bias(c, place_ref[...])).astype(BF16)


def _fox_past(past_k, past_logf, w):
    b, p, n = past_logf.shape
    tc = _row_tile(p, ROW_TILE)
    ltri = jnp.tril(jnp.ones((tc, tc), F32)).astype(BF16)
    fw, aw = FOX_WIDTH, FOX_HEADS * LANES
    row = lambda m: pl.BlockSpec((1, tc, m), lambda i, j: (i, j, 0))
    return pl.pallas_call(
        _fox_past_body,
        grid=(b, p // tc),
        in_specs=[row(fw), row(n), _const_spec((1, aw)), _const_spec((LANES, aw)), _const_spec((tc, tc))],
        out_specs=[row(aw), row(n)],
        out_shape=[jax.ShapeDtypeStruct((b, p, aw), BF16), jax.ShapeDtypeStruct((b, p, n), F32)],
        scratch_shapes=[pltpu.VMEM((1, n), F32)],
        compiler_params=_params("parallel", "arbitrary"),
        name="fox_past",
    )(past_k, past_logf, w["keep"], w["place"], ltri)


def _flash_body(q_ref, k_ref, vt_ref, o_ref, m_s, l_s, acc_s, sa_s, sb_s, *, tq, tqs, tk, tks, fr, fc, n_k, past,
                kv_len, chunk_causal, diag_aligned):
    q_start = past + pl.program_id(2) * tq
    q = q_ref[0]
    m_s[...] = jnp.full_like(m_s, NEG_INF)
    l_s[...] = jnp.zeros_like(l_s)
    acc_s[...] = jnp.zeros_like(acc_s)
    shift = int(math.log2(CHUNK))
    hv = LANES // 2

    def block_kind(r0, rn, c0, cn):
        if chunk_causal:
            k_lo, k_hi, q_lo, q_hi = r0 >> shift, (r0 + rn - 1) >> shift, c0 >> shift, (c0 + cn - 1) >> shift
        else:
            k_lo, k_hi, q_lo, q_hi = r0, r0 + rn - 1, c0, c0 + cn - 1
        return "visible" if k_hi <= q_lo else ("hidden" if k_lo > q_hi else "partial")

    def diag_streams(d):
        out = []
        for r0 in range(0, tk, tks):
            for hh in range(2):
                for c0 in range(0, tq, tqs):
                    halves = [(c, block_kind(d * tk + r0, tks, c, tks)) for c in range(c0, c0 + tqs, tks)]
                    if all(kind == "visible" for _, kind in halves):
                        out.append((hh, r0, tks, c0, tqs, "visible"))
                    else:
                        out += [(hh, r0, tks, c, tks, kind) for c, kind in halves if kind != "hidden"]
        return out

    full_streams = [(hh, r0, fr, c0, fc, "visible") for r0 in range(0, tk, fr) for hh in range(2)
                    for c0 in range(0, tq, fc)]
    mask_streams = [st[:5] + ("partial",) for st in full_streams]

    def score(kblk, stream):
        hh, r0, rn, c0, cn, _ = stream
        head = slice(hh * LANES, (hh + 1) * LANES)
        return _dot_nt(kblk[r0:r0 + rn, head], q[c0:c0 + cn, head])

    def key_block(kt):
        return k_ref[0, pl.ds(pl.multiple_of(jnp.minimum(kt, n_k - 1) * tk, tk), tk), :]

    def absorb(s, stream, kt):
        hh, r0, rn, c0, cn, kind = stream
        cols = slice(c0, c0 + cn)
        k0 = pl.multiple_of(kt * tk, tk)
        if kind == "partial":
            kpos = k0 + r0 + lax.broadcasted_iota(jnp.int32, (rn, cn), 0)
            qpos = q_start + c0 + lax.broadcasted_iota(jnp.int32, (rn, cn), 1)
            if chunk_causal:
                vis = lax.shift_right_logical(kpos, shift) <= lax.shift_right_logical(qpos, shift)
            else:
                vis = kpos <= qpos
            s = jnp.where(jnp.logical_and(vis, kpos < kv_len), s, NEG_INF)
        m_old = m_s[hh, :, cols]
        m_new = jnp.maximum(m_old, jnp.max(s, axis=0, keepdims=True))
        alpha = jnp.exp2(m_old - m_new)
        p = jnp.exp2(s - m_new)
        l_s[hh, :, cols] = alpha * l_s[hh, :, cols] + jnp.sum(p, axis=0, keepdims=True)
        vt = vt_ref[0, hh * hv:(hh + 1) * hv, pl.ds(pl.multiple_of(k0 + r0, LANES), rn)]
        acc_s[hh, :, cols] = alpha * acc_s[hh, :, cols] + _dot(vt, p.astype(BF16))
        m_s[hh, :, cols] = m_new

    def single_tile(kt, streams, after_first_scores=None):
        kblk = key_block(kt)
        ahead = 2
        pending = [score(kblk, st) for st in streams[:ahead]]
        if after_first_scores is not None:
            after_first_scores()
        for idx, st in enumerate(streams):
            s = pending.pop(0)
            if idx + ahead < len(streams):
                pending.append(score(kblk, streams[idx + ahead]))
            absorb(s, st, kt)

    def store_scores(buf, kt):
        kblk = key_block(kt)
        for st in full_streams:
            hh, r0, rn, c0, cn, _ = st
            buf[hh, r0:r0 + rn, c0:c0 + cn] = score(kblk, st)

    def absorb_stored(buf, kt):
        for st in full_streams:
            hh, r0, rn, c0, cn, _ = st
            absorb(buf[hh, r0:r0 + rn, c0:c0 + cn], st, kt)

    def tile_pair(i, carry):
        kt = 2 * i
        store_scores(sb_s, kt + 1)
        absorb_stored(sa_s, kt)
        store_scores(sa_s, kt + 2)
        absorb_stored(sb_s, kt + 1)
        return carry

    def full_tile(kt, carry):
        single_tile(kt, full_streams)
        return carry

    def masked_tile(kt, carry):
        single_tile(kt, mask_streams)
        return carry

    n_full = jnp.minimum(q_start // tk, kv_len // tk)
    q_last = q_start + tq - 1
    k_hi = (q_last // CHUNK + 1) * CHUNK if chunk_causal else q_last + 1
    n_end = jnp.minimum((k_hi + tk - 1) // tk, n_k)
    n_pair = n_full // 2
    if diag_aligned:
        for d in range(tq // tk):
            single_tile(n_full + d, diag_streams(d),
                        after_first_scores=(lambda: store_scores(sa_s, 0)) if d == 0 else None)
        lax.fori_loop(0, n_pair, tile_pair, 0)
        lax.fori_loop(2 * n_pair, n_full, full_tile, 0)
    else:
        @pl.when(n_pair > 0)
        def _():
            store_scores(sa_s, 0)

        lax.fori_loop(0, n_pair, tile_pair, 0)
        lax.fori_loop(2 * n_pair, n_full, full_tile, 0)
        lax.fori_loop(n_full, n_end, masked_tile, 0)
    out_t = jnp.concatenate([acc_s[0] / l_s[0], acc_s[1] / l_s[1]], axis=0)
    o_ref[0] = out_t.T.astype(BF16)


def _flash(q, k, vt, *, past, kv_len, chunk_causal):
    b, t, w = q.shape
    lp = k.shape[1]
    n_pairs = w // (2 * LANES)
    t_pad = max(t, LANES)
    if t_pad > t:
        q = _pad_rows(q, t_pad)
    if past == 0 and t % KEY_TILE == 0 and lp % KEY_TILE == 0:
        tk = KEY_TILE
        tq = QUERY_TILE if t % QUERY_TILE == 0 else KEY_TILE
    elif t_pad == LANES:
        tq, tk = LANES, lp
    else:
        tq = tk = LANES
    assert t_pad % tq == 0 and lp % tk == 0, (t, lp, tq, tk)
    tks = MXU_TILE if tk % MXU_TILE == 0 and tq > LANES else tk
    tqs = min(tq, 2 * MXU_TILE)
    fr, fc = (tk, MXU_TILE) if tks < tk and tq % MXU_TILE == 0 else (tks, tqs)
    out = pl.pallas_call(
        functools.partial(_flash_body, tq=tq, tqs=tqs, tk=tk, tks=tks, fr=fr, fc=fc, n_k=lp // tk, past=past,
                          kv_len=kv_len, chunk_causal=chunk_causal,
                          diag_aligned=(past == 0 and tq % tk == 0 and kv_len == lp and tks < tk)),
        grid=(b, n_pairs, t_pad // tq),
        in_specs=[pl.BlockSpec((1, tq, 2 * LANES), lambda i, j, s: (i, s, j)),
                  pl.BlockSpec((1, lp, 2 * LANES), lambda i, j, s: (i, 0, j)),
                  pl.BlockSpec((1, LANES, lp), lambda i, j, s: (i, j, 0))],
        out_specs=pl.BlockSpec((1, tq, LANES), lambda i, j, s: (i, s, j)),
        out_shape=jax.ShapeDtypeStruct((b, t_pad, n_pairs * LANES), BF16),
        scratch_shapes=[pltpu.VMEM((2, 1, tq), F32), pltpu.VMEM((2, 1, tq), F32),
                        pltpu.VMEM((2, LANES // 2, tq), F32),
                        pltpu.VMEM((2, tk, tq), F32), pltpu.VMEM((2, tk, tq), F32)],
        compiler_params=_params("parallel", "parallel", "arbitrary"),
        name="flash_mla" if chunk_causal else "flash_fox",
    )(q, k, vt)
    return out[:, :t]


def _row(v):
    return v.reshape(1, -1).astype(F32)


def _pad_lanes(x, lo, total):
    pad = [(0, 0)] * (x.ndim - 1) + [(lo, total - lo - x.shape[-1])]
    return jnp.pad(x, pad)


def _ffn_weights(g, w_in, w_out):
    f = w_out.shape[0]
    return dict(g=_row(g), wg=w_in[:, :f].astype(BF16), wu=w_in[:, f:].astype(BF16), wo=w_out.astype(BF16))


def _mem_weights(g, g_src, w_q, w_kv, w_o, g_q, g_k):
    d = w_q.shape[0]
    kv = w_kv.reshape(d, MEM_HEADS, 2, MEM_HEAD_DIM)
    return dict(g=_row(g), g_src=_row(g_src), wq=w_q.astype(BF16), wo=w_o.astype(BF16),
                wk=kv[:, :, 0].reshape(d, MEM_WIDTH).astype(BF16),
                wv=kv[:, :, 1].reshape(d, MEM_WIDTH).astype(BF16),
                gq=_row(g_q) * (MEM_HEAD_DIM ** -0.5), gk=_row(g_k))


def _even_weights(g_mix, w_in, g_qlat, g_kvlat, w_uq, w_ukv, g_q, g_k, conv_w, conv_b, gate_w, gate_b, lam, w_out):
    d = w_in.shape[0]
    o1 = MLA_Q_LORA
    o2 = o1 + MLA_KV_LORA
    o3 = o2 + MLA_ROPE
    o4 = o3 + LRU_WIDTH
    half = MLA_ROPE // 2
    swap_halves = lambda r: jnp.concatenate([r[..., half:], r[..., :half]], axis=-1)
    uq3 = w_uq.reshape(MLA_Q_LORA, MLA_HEADS, MLA_QK)
    uq = _pad_lanes(uq3, 0, LANES)
    uq_swap = _pad_lanes(swap_halves(uq3[:, :, MLA_NOPE:]), MLA_NOPE, LANES)
    kr = w_in[:, o2:o3]
    ukv = w_ukv.reshape(MLA_KV_LORA, MLA_HEADS, MLA_NOPE + MLA_V)
    uk = _pad_lanes(ukv[:, :, :MLA_NOPE], 0, LANES)
    blk = LRU_WIDTH // LRU_BLOCKS
    eye = jnp.eye(LRU_BLOCKS, dtype=F32)
    wr = jnp.einsum("ncd,nm->ncmd", gate_w[:, :, :blk], eye).reshape(LRU_WIDTH, LRU_WIDTH)
    wi = jnp.einsum("ncd,nm->ncmd", gate_w[:, :, blk:], eye).reshape(LRU_WIDTH, LRU_WIDTH)
    return dict(
        g_mix=_row(g_mix), wcq=w_in[:, :o1].astype(BF16), g_qlat=_row(g_qlat),
        wuq=uq.reshape(MLA_Q_LORA, MLA_HEADS * LANES).astype(BF16),
        wuq_swap=uq_swap.reshape(MLA_Q_LORA, MLA_HEADS * LANES).astype(BF16),
        gq=_pad_lanes(_row(g_q), 0, LANES) * (MLA_QK ** -0.5 * LOG2E),
        wckv=w_in[:, o1:o2].astype(BF16), g_kvlat=_row(g_kvlat),
        wkr=_pad_lanes(kr, MLA_NOPE, LANES).astype(BF16),
        wkr_swap=_pad_lanes(swap_halves(kr), MLA_NOPE, LANES).astype(BF16),
        wuk=uk.reshape(MLA_KV_LORA, MLA_HEADS * LANES).astype(BF16),
        wuvt=ukv[:, :, MLA_NOPE:].reshape(MLA_KV_LORA, MLA_HEADS * MLA_V).T.astype(BF16),
        gk=_pad_lanes(_row(g_k), 0, LANES),
        wrec=w_in[:, o3:o4].astype(BF16), wgate=w_in[:, o4:].astype(BF16),
        conv_w=conv_w.astype(F32), conv_b=_row(conv_b), wr=wr.astype(BF16), wi=wi.astype(BF16),
        br=_row(gate_b[:, :blk]), bi=_row(gate_b[:, blk:]), lam=_row(lam),
        wo_attn=w_out[:MLA_HEADS * MLA_V].astype(BF16), wo_rec=w_out[MLA_HEADS * MLA_V:].astype(BF16))


def _odd_weights(g_mix, w_in, b_f, g_q, g_k, w_out):
    fw = FOX_WIDTH
    lane = jnp.arange(FOX_HEADS * LANES)
    hd, within = lane // LANES, lane % LANES
    own_low = hd % 2 == 0
    keep = jnp.where(own_low, within < FOX_HEAD_DIM, within >= FOX_HEAD_DIM)
    part = within - jnp.where(own_low, FOX_HEAD_DIM, 0)
    is_bias = (part >= 0) & (part < 3)
    src = part * FOX_HEADS + hd
    place = ((jnp.arange(LANES)[:, None] == src[None, :]) & is_bias[None, :]).astype(BF16)
    return dict(
        keep=keep.astype(F32)[None, :], ones=is_bias.astype(F32)[None, :], place=place,
        g_mix=_row(g_mix), wq=w_in[:, :fw].astype(BF16), wk=w_in[:, fw:2 * fw].astype(BF16),
        wv=w_in[:, 2 * fw:3 * fw].astype(BF16), wf=_pad_lanes(w_in[:, 3 * fw:], 0, LANES).astype(BF16),
        bf=_pad_lanes(_row(b_f), 0, LANES),
        gq=jnp.tile(_row(g_q), (1, FOX_HEADS)) * (FOX_HEAD_DIM ** -0.5 * LOG2E),
        gk=jnp.tile(_row(g_k), (1, FOX_HEADS)), wo=w_out.astype(BF16))


def _rope_tables(pos):
    half = MLA_ROPE // 2
    inv_freq = ROPE_THETA ** (-jnp.arange(half, dtype=F32) / half)
    ang = pos.astype(F32)[:, None] * inv_freq[None, :]
    cos, sin = jnp.cos(ang), jnp.sin(ang)
    c = jnp.concatenate([jnp.ones((pos.shape[0], MLA_NOPE), F32), cos, cos,
                         jnp.ones((pos.shape[0], LANES - MLA_QK), F32)], axis=-1)
    s = _pad_lanes(jnp.concatenate([-sin, sin], axis=-1), MLA_NOPE, LANES)
    return c, s


def _pad_rows(x, total):
    return jnp.pad(x, [(0, 0), (0, total - x.shape[1])] + [(0, 0)] * (x.ndim - 2))


def _kv_pad_len(t, past):
    l = past + t
    if past == 0 and t % KEY_TILE == 0:
        return l
    return -(-l // LANES) * LANES


def _even_layer(h, past, w, state):
    b, t, d = h.shape
    past_lat, past_krope, h0, conv_prev = state
    lp = _kv_pad_len(t, past)
    tables = _rope_tables(past + jnp.arange(t))
    if past == 0 and lp == t and t % LANES == 0:
        q, lat_new, krope_new, k, vt = _mla_in(h, w, tables, with_kv=True)
    else:
        q, lat_new, krp_new = _mla_in(h, w, tables, with_kv=False)
        krope_new = krp_new[:, :, MLA_NOPE:MLA_QK]
        lat_all = _pad_rows(jnp.concatenate([past_lat, lat_new], axis=1), lp)
        krp_all = _pad_rows(jnp.concatenate([_pad_lanes(past_krope, MLA_NOPE, LANES), krp_new], axis=1), lp)
        k, vt = _mla_kv(lat_all, krp_all, w)
    attn = _flash(q, k, vt, past=past, kv_len=past + t, chunk_causal=True)
    conv_prev8 = jnp.pad(conv_prev, ((0, 0), (HALO - (CONV_WIDTH - 1), 0), (0, 0)))
    y_rec, h_last, conv_last = _lru(h, w, conv_prev8, h0[:, None, :])
    new = (lat_new, krope_new, h_last[:, 0], conv_last[:, HALO - (CONV_WIDTH - 1):])
    return [attn, y_rec], [w["wo_attn"], w["wo_rec"]], new


def _odd_layer(h, past, w, state):
    b, t, d = h.shape
    past_k, past_v, past_logf = state
    if past > 0:
        ka_past, c_past = _fox_past(past_k.reshape(b, past, FOX_WIDTH),
                                    _pad_lanes(past_logf.astype(F32), 0, LANES), w)
        c0 = c_past[:, past - 1:past, :]
    else:
        ka_past = jnp.zeros((b, 0, FOX_HEADS * LANES), BF16)
        c0 = jnp.zeros((b, 1, LANES), F32)
    q, k32, v32, ka_new, vt_new, logf = _fox_in(h, w, c0)
    lp = _kv_pad_len(t, past)
    k_all = _pad_rows(jnp.concatenate([ka_past, ka_new], axis=1), lp)
    vt_past = jnp.swapaxes(past_v.reshape(b, past, FOX_WIDTH).astype(BF16), 1, 2)
    vt_all = jnp.pad(jnp.concatenate([vt_past, vt_new], axis=2), ((0, 0), (0, 0), (0, lp - past - t)))
    attn = _flash(q, k_all, vt_all, past=past, kv_len=past + t, chunk_causal=False)
    new = (k32.reshape(b, t, FOX_HEADS, FOX_HEAD_DIM), v32.reshape(b, t, FOX_HEADS, FOX_HEAD_DIM),
           logf)
    return [attn], [w["wo"]], new


def _trunk(x, past, layers, mem_kvs, even_states, odd_states):
    b, t, d = x.shape
    even_new, odd_new = [], []
    for li, lw in enumerate(layers):
        h = _ffn(x.reshape(b * t, d), lw["ffn1"]).reshape(b, t, d)
        if li % 2 == 0:
            parts, w_parts, new = _even_layer(h, past, lw["mix"], even_states[li // 2])
            even_new.append(new)
        else:
            parts, w_parts, new = _odd_layer(h, past, lw["mix"], odd_states[li // 2])
            odd_new.append(new)
        x = _post_mixer(h, parts, w_parts, mem_kvs[li][0], mem_kvs[li][1], lw["mem"], lw["ffn2"])
    return x, even_new, odd_new


def kernel(x_prompt, x_sample, mem_prompt, cache_mla_latent, cache_mla_krope, state_lru_h, state_lru_conv, cache_fox_k, cache_fox_v, cache_fox_logf, cache_mem_k, cache_mem_v, norm_ffn1, ffn1_w_in, ffn1_w_out, norm_mix, norm_mem, norm_mem_src, mem_w_q, mem_w_kv, mem_w_o, mem_g_q, mem_g_k, norm_ffn2, ffn2_w_in, ffn2_w_out, ev_w_in, ev_g_qlat, ev_g_kvlat, ev_w_uq, ev_w_ukv, ev_g_q, ev_g_k, ev_conv_w, ev_conv_b, ev_gate_w, ev_gate_b, ev_lambda, ev_w_out, od_w_in, od_b_f, od_g_q, od_g_k, od_w_out):
    depth = norm_ffn1.shape[0]
    n_even, n_odd = (depth + 1) // 2, depth // 2
    b, _, _ = x_prompt.shape
    bs = x_sample.shape[0]
    past = cache_mla_latent.shape[2] if n_even else cache_fox_k.shape[2]

    layers = []
    for li in range(depth):
        j = li // 2
        if li % 2 == 0:
            mix = _even_weights(norm_mix[li], ev_w_in[j], ev_g_qlat[j], ev_g_kvlat[j], ev_w_uq[j], ev_w_ukv[j],
                                ev_g_q[j], ev_g_k[j], ev_conv_w[j], ev_conv_b[j], ev_gate_w[j], ev_gate_b[j],
                                ev_lambda[j], ev_w_out[j])
        else:
            mix = _odd_weights(norm_mix[li], od_w_in[j], od_b_f[j], od_g_q[j], od_g_k[j], od_w_out[j])
        layers.append(dict(
            ffn1=_ffn_weights(norm_ffn1[li], ffn1_w_in[li], ffn1_w_out[li]),
            ffn2=_ffn_weights(norm_ffn2[li], ffn2_w_in[li], ffn2_w_out[li]),
            mem=_mem_weights(norm_mem[li], norm_mem_src[li], mem_w_q[li], mem_w_kv[li], mem_w_o[li],
                             mem_g_q[li], mem_g_k[li]),
            mix=mix))

    p_mem_k, p_mem_v, mem_kb, mem_vb = _mem_kv(mem_prompt, [lw["mem"] for lw in layers])
    ev0 = [(jnp.zeros((b, 0, MLA_KV_LORA), F32), jnp.zeros((b, 0, MLA_ROPE), F32),
            jnp.zeros((b, LRU_WIDTH), F32), jnp.zeros((b, CONV_WIDTH - 1, LRU_WIDTH), F32))
           for _ in range(n_even)]
    od0 = [(jnp.zeros((b, 0, FOX_HEADS, FOX_HEAD_DIM), F32), jnp.zeros((b, 0, FOX_HEADS, FOX_HEAD_DIM), F32),
            jnp.zeros((b, 0, FOX_HEADS), F32)) for _ in range(n_odd)]
    y_prompt, ev_p, od_p = _trunk(x_prompt, 0, layers, [(mem_kb[li], mem_vb[li]) for li in range(depth)],
                                  ev0, od0)

    m_tok = cache_mem_k.shape[2]
    mem_s = [(cache_mem_k[li].reshape(bs, m_tok, MEM_WIDTH).astype(BF16),
              cache_mem_v[li].reshape(bs, m_tok, MEM_WIDTH).astype(BF16)) for li in range(depth)]
    ev_s = [(cache_mla_latent[j], cache_mla_krope[j], state_lru_h[j], state_lru_conv[j]) for j in range(n_even)]
    od_s = [(cache_fox_k[j], cache_fox_v[j], cache_fox_logf[j]) for j in range(n_odd)]
    y_sample, ev_n, od_n = _trunk(x_sample, past, layers, mem_s, ev_s, od_s)

    p_even = [jnp.stack([s[f] for s in ev_p]) for f in range(4)]
    p_odd = [jnp.stack([s[f] for s in od_p]) for f in range(3)]
    s_even = [jnp.stack([s[f] for s in ev_n]) for f in range(4)]
    s_odd = [jnp.stack([s[f] for s in od_n]) for f in range(3)]
    return (y_prompt, y_sample, *p_even, *p_odd, p_mem_k, p_mem_v, *s_even, *s_odd)
```

```python
import functools
import math

import jax
import jax.numpy as jnp
from jax import lax
from jax.experimental import pallas as pl
from jax.experimental.pallas import tpu as pltpu

F32 = jnp.float32
BF16 = jnp.bfloat16

NORM_EPS = 1e-6
NEG_INF = -1e30
LOG2E = math.log2(math.e)
CHUNK = 64
LANES = 128
SUBLANES = 8
ROPE_THETA = 10000.0
LRU_C = 8.0
MLA_HEADS = 8
MLA_NOPE = 64
MLA_ROPE = 32
MLA_QK = MLA_NOPE + MLA_ROPE
MLA_V = 64
MLA_Q_LORA = 256
MLA_KV_LORA = 128
LRU_WIDTH = 512
LRU_BLOCKS = 8
CONV_WIDTH = 4
FOX_HEADS = 16
FOX_HEAD_DIM = 64
FOX_WIDTH = FOX_HEADS * FOX_HEAD_DIM
MEM_HEADS = 4
MEM_HEAD_DIM = 128
MEM_WIDTH = MEM_HEADS * MEM_HEAD_DIM
HALO = 8

MXU_TILE = 256
ROW_TILE = 512
KEY_TILE = 512
QUERY_TILE = 2048

VMEM_LIMIT = 56 * 1024 * 1024


def _dot(a, b):
    return jnp.dot(a, b, preferred_element_type=F32)


def _dot_nt(a, b):
    return lax.dot_general(a, b, (((1,), (1,)), ((), ())), preferred_element_type=F32)


def _rms(x, g):
    return x * lax.rsqrt(jnp.mean(x * x, axis=-1, keepdims=True) + NORM_EPS) * g


def _head_rms(x, g, n_live):
    ss = jnp.sum(x * x, axis=-1, keepdims=True) * (1.0 / n_live)
    return x * lax.rsqrt(ss + NORM_EPS) * g


def _sigmoid(x):
    return 1.0 / (1.0 + jnp.exp(-x))


def _log1p(y):
    u = 1.0 + y
    d = u - 1.0
    return jnp.where(d == 0.0, y, jnp.log(u) * (y / jnp.where(d == 0.0, 1.0, d)))


def _softplus(x):
    return jnp.maximum(x, 0.0) + _log1p(jnp.exp(-jnp.abs(x)))


def _gelu_tanh(x):
    return 0.5 * x * (1.0 + jnp.tanh(math.sqrt(2.0 / math.pi) * (x + 0.044715 * (x * x * x))))


def _split_bf16(x, parts):
    out = []
    r = x
    for _ in range(parts):
        p = r.astype(BF16)
        out.append(p)
        r = r - p.astype(F32)
    return out


def _const_spec(shape):
    nd = len(shape)
    return pl.BlockSpec(shape, lambda *_: (0,) * nd, pipeline_mode=pl.Buffered(1))


def _params(*sem):
    return pltpu.CompilerParams(dimension_semantics=sem, vmem_limit_bytes=VMEM_LIMIT)


def _row_tile(n, cap):
    t = min(n, cap)
    assert n % t == 0, (n, t)
    return t


FFN_CHUNKS = 2


def _ffn_chunk_bounds(f):
    tiles = -(-f // MXU_TILE)
    per = -(-tiles // FFN_CHUNKS) * MXU_TILE
    edges = [min(i * per, f) for i in range(FFN_CHUNKS + 1)]
    return [(lo, hi) for lo, hi in zip(edges[:-1], edges[1:]) if hi > lo]


def _swiglu_half_step(x, g_ref, wg_ref, wu_ref, wo_ref):
    hb = _rms(x, g_ref[...]).astype(BF16)
    acc = jnp.zeros_like(x)
    for lo, hi in _ffn_chunk_bounds(wg_ref.shape[1]):
        sl = slice(lo, hi)
        gate = _dot(hb, wg_ref[:, sl])
        up = _dot(hb, wu_ref[:, sl])
        act = (gate * _sigmoid(gate) * up).astype(BF16)
        acc = acc + _dot(act, wo_ref[sl, :])
    return x + 0.5 * acc


def _ffn_body(x_ref, g_ref, wg_ref, wu_ref, wo_ref, o_ref):
    o_ref[...] = _swiglu_half_step(x_ref[...], g_ref, wg_ref, wu_ref, wo_ref)


def _ffn(x2, w):
    n, d = x2.shape
    f = w["wg"].shape[1]
    tm = _row_tile(n, ROW_TILE)
    return pl.pallas_call(
        _ffn_body,
        grid=(n // tm,),
        in_specs=[pl.BlockSpec((tm, d), lambda i: (i, 0)), _const_spec((1, d)),
                  _const_spec((d, f)), _const_spec((d, f)), _const_spec((f, d))],
        out_specs=pl.BlockSpec((tm, d), lambda i: (i, 0)),
        out_shape=jax.ShapeDtypeStruct((n, d), F32),
        compiler_params=_params("parallel"),
        name="ffn",
    )(x2, w["g"], w["wg"], w["wu"], w["wo"])


def _mem_cross_attention(h, tm, g_ref, wq_ref, gq_ref, mk_ref, mv_ref, wo_ref):
    hb = _rms(h, g_ref[...]).astype(BF16)
    q = _dot(hb, wq_ref[...])
    rows = []
    for i in range(h.shape[0] // tm):
        outs = []
        for hd in range(MEM_HEADS):
            sl = slice(hd * MEM_HEAD_DIM, (hd + 1) * MEM_HEAD_DIM)
            qh = _head_rms(q[i * tm:(i + 1) * tm, sl], gq_ref[...], MEM_HEAD_DIM).astype(BF16)
            s = _dot_nt(qh, mk_ref[i, :, sl])
            e = jnp.exp(s - jnp.max(s, axis=-1, keepdims=True))
            p = e / jnp.sum(e, axis=-1, keepdims=True)
            outs.append(_dot(p.astype(BF16), mv_ref[i, :, sl]).astype(BF16))
        rows.append(jnp.concatenate(outs, axis=-1))
    o = rows[0] if len(rows) == 1 else jnp.concatenate(rows, axis=0)
    return h + _dot(o, wo_ref[...])


def _post_mixer_body(*refs, n_parts):
    h_ref = refs[0]
    parts = refs[1:1 + n_parts]
    w_parts = refs[1 + n_parts:1 + 2 * n_parts]
    (mg_ref, mwq_ref, mgq_ref, mk_ref, mv_ref, mwo_ref,
     fg_ref, fwg_ref, fwu_ref, fwo_ref, o_ref) = refs[1 + 2 * n_parts:]
    bb, tm, d = h_ref.shape
    h = h_ref[...].reshape(bb * tm, d)
    for p_ref, w_ref in zip(parts, w_parts):
        h = h + _dot(p_ref[...].reshape(bb * tm, p_ref.shape[2]), w_ref[...])
    h = _mem_cross_attention(h, tm, mg_ref, mwq_ref, mgq_ref, mk_ref, mv_ref, mwo_ref)
    o_ref[...] = _swiglu_half_step(h, fg_ref, fwg_ref, fwu_ref, fwo_ref).reshape(bb, tm, d)


def _post_mixer(h, parts, w_parts, mk, mv, wm, wf):
    b, t, d = h.shape
    m = mk.shape[1]
    f = wf["wg"].shape[1]
    tm = _row_tile(t, ROW_TILE)
    bb = b if b * t <= ROW_TILE and tm % SUBLANES == 0 else 1
    row = lambda n: pl.BlockSpec((bb, tm, n), lambda i, j: (i, j, 0))
    mem = lambda: pl.BlockSpec((bb, m, MEM_WIDTH), lambda i, j: (i, 0, 0))
    in_specs = [row(d)] + [row(p.shape[2]) for p in parts] + [_const_spec(w.shape) for w in w_parts]
    in_specs += [_const_spec((1, d)), _const_spec((d, MEM_WIDTH)), _const_spec((1, MEM_HEAD_DIM)), mem(), mem(),
                 _const_spec((MEM_WIDTH, d)),
                 _const_spec((1, d)), _const_spec((d, f)), _const_spec((d, f)), _const_spec((f, d))]
    return pl.pallas_call(
        functools.partial(_post_mixer_body, n_parts=len(parts)),
        grid=(b // bb, t // tm),
        in_specs=in_specs,
        out_specs=row(d),
        out_shape=jax.ShapeDtypeStruct((b, t, d), F32),
        compiler_params=_params("parallel", "parallel"),
        name="post_mixer",
    )(h, *parts, *w_parts, wm["g"], wm["wq"], wm["gq"], mk, mv, wm["wo"],
      wf["g"], wf["wg"], wf["wu"], wf["wo"])


def _mem_kv_body(m_ref, g_ref, wk_ref, wv_ref, gk_ref, k32_ref, v32_ref, kb_ref, vb_ref):
    hb = _rms(m_ref[0], g_ref[0]).astype(BF16)
    k = _dot(hb, wk_ref[0])
    v = _dot(hb, wv_ref[0])
    for hd in range(MEM_HEADS):
        sl = slice(hd * MEM_HEAD_DIM, (hd + 1) * MEM_HEAD_DIM)
        kh = _head_rms(k[:, sl], gk_ref[0], MEM_HEAD_DIM)
        k32_ref[0, 0, :, hd, :] = kh
        v32_ref[0, 0, :, hd, :] = v[:, sl]
        kb_ref[0, 0, :, sl] = kh.astype(BF16)
    vb_ref[0, 0] = v.astype(BF16)


def _mem_kv(mem, ws):
    b, m, d = mem.shape
    n = len(ws)
    stack = lambda name: jnp.stack([w[name] for w in ws])
    per_layer = lambda *shape: pl.BlockSpec((1,) + shape, lambda l, i: (l,) + (0,) * len(shape))
    blk = lambda: pl.BlockSpec((1, 1, m, MEM_WIDTH), lambda l, i: (l, i, 0, 0))
    blk5 = lambda: pl.BlockSpec((1, 1, m, MEM_HEADS, MEM_HEAD_DIM), lambda l, i: (l, i, 0, 0, 0))
    shape5 = (n, b, m, MEM_HEADS, MEM_HEAD_DIM)
    return pl.pallas_call(
        _mem_kv_body,
        grid=(n, b),
        in_specs=[pl.BlockSpec((1, m, d), lambda l, i: (i, 0, 0)), per_layer(1, d),
                  per_layer(d, MEM_WIDTH), per_layer(d, MEM_WIDTH), per_layer(1, MEM_HEAD_DIM)],
        out_specs=[blk5(), blk5(), blk(), blk()],
        out_shape=[jax.ShapeDtypeStruct(shape5, F32), jax.ShapeDtypeStruct(shape5, F32),
                   jax.ShapeDtypeStruct((n, b, m, MEM_WIDTH), BF16),
                   jax.ShapeDtypeStruct((n, b, m, MEM_WIDTH), BF16)],
        compiler_params=_params("parallel", "parallel"),
        name="mem_kv",
    )(mem, stack("g_src"), stack("wk"), stack("wv"), stack("gk"))


def _mla_keys_values(lat, krp, wuk_ref, wuvt_ref, gk_ref, k_ref, vt_ref):
    lb = lat.astype(BF16)
    kn = _dot(lb, wuk_ref[...])
    for hd in range(MLA_HEADS):
        sl = slice(hd * LANES, (hd + 1) * LANES)
        k_ref[0, :, sl] = _head_rms(kn[:, sl] + krp, gk_ref[...], MLA_QK).astype(BF16)
    vt_ref[0] = _dot_nt(wuvt_ref[...], lb).astype(BF16)


def _mla_in_body(h_ref, g_ref, wcq_ref, gql_ref, wuq_ref, wuqs_ref, gq_ref, wckv_ref, gkv_ref, wkr_ref, wkrs_ref,
                 c_ref, s_ref, *rest):
    q_ref, lat_ref, krp_ref = rest[-5:-2] if len(rest) > 3 else rest
    hb = _rms(h_ref[0], g_ref[...]).astype(BF16)
    c, s = c_ref[...], s_ref[...]
    cq = _rms(_dot(hb, wcq_ref[...]), gql_ref[...]).astype(BF16)
    q = _dot(cq, wuq_ref[...])
    q_partner = _dot(cq, wuqs_ref[...])
    for hd in range(MLA_HEADS):
        sl = slice(hd * LANES, (hd + 1) * LANES)
        qh = q[:, sl] * c + q_partner[:, sl] * s
        q_ref[0, :, sl] = _head_rms(qh, gq_ref[...], MLA_QK).astype(BF16)
    lat = _rms(_dot(hb, wckv_ref[...]), gkv_ref[...])
    krp = _dot(hb, wkr_ref[...]) * c + _dot(hb, wkrs_ref[...]) * s
    lat_ref[0] = lat
    if len(rest) > 3:
        krp_ref[0] = krp[:, MLA_NOPE:MLA_QK]
        wuk_ref, wuvt_ref, gk_ref = rest[:3]
        _mla_keys_values(lat, krp, wuk_ref, wuvt_ref, gk_ref, *rest[-2:])
    else:
        krp_ref[0] = krp


def _mla_in(h, w, tables, with_kv):
    b, t, d = h.shape
    tm = _row_tile(t, ROW_TILE)
    row = lambda n: pl.BlockSpec((1, tm, n), lambda i, j: (i, j, 0))
    tab = lambda: pl.BlockSpec((tm, LANES), lambda i, j: (j, 0))
    vw = MLA_HEADS * MLA_V
    in_specs = [row(d), _const_spec((1, d)),
                _const_spec((d, MLA_Q_LORA)), _const_spec((1, MLA_Q_LORA)),
                _const_spec((MLA_Q_LORA, MLA_HEADS * LANES)), _const_spec((MLA_Q_LORA, MLA_HEADS * LANES)),
                _const_spec((1, LANES)),
                _const_spec((d, MLA_KV_LORA)), _const_spec((1, MLA_KV_LORA)),
                _const_spec((d, LANES)), _const_spec((d, LANES)),
                tab(), tab()]
    args = [h, w["g_mix"], w["wcq"], w["g_qlat"], w["wuq"], w["wuq_swap"], w["gq"], w["wckv"], w["g_kvlat"],
            w["wkr"], w["wkr_swap"], *tables]
    kr_w = MLA_ROPE if with_kv else LANES
    out_specs = [row(MLA_HEADS * LANES), row(MLA_KV_LORA), row(kr_w)]
    out_shape = [jax.ShapeDtypeStruct((b, t, MLA_HEADS * LANES), BF16),
                 jax.ShapeDtypeStruct((b, t, MLA_KV_LORA), F32),
                 jax.ShapeDtypeStruct((b, t, kr_w), F32)]
    if with_kv:
        in_specs += [_const_spec((MLA_KV_LORA, MLA_HEADS * LANES)), _const_spec((vw, MLA_KV_LORA)),
                     _const_spec((1, LANES))]
        args += [w["wuk"], w["wuvt"], w["gk"]]
        out_specs += [row(MLA_HEADS * LANES), pl.BlockSpec((1, vw, tm), lambda i, j: (i, 0, j))]
        out_shape += [jax.ShapeDtypeStruct((b, t, MLA_HEADS * LANES), BF16),
                      jax.ShapeDtypeStruct((b, vw, t), BF16)]
    return pl.pallas_call(
        _mla_in_body,
        grid=(b, t // tm),
        in_specs=in_specs,
        out_specs=out_specs,
        out_shape=out_shape,
        compiler_params=_params("parallel", "parallel"),
        name="mla_in",
    )(*args)


def _mla_kv_body(lat_ref, krp_ref, wuk_ref, wuvt_ref, gk_ref, k_ref, vt_ref):
    _mla_keys_values(lat_ref[0], krp_ref[0], wuk_ref, wuvt_ref, gk_ref, k_ref, vt_ref)


def _mla_kv(lat, krp, w):
    b, l, _ = lat.shape
    tl = ROW_TILE if l % ROW_TILE == 0 else l
    row = lambda n: pl.BlockSpec((1, tl, n), lambda i, j: (i, j, 0))
    vw = MLA_HEADS * MLA_V
    return pl.pallas_call(
        _mla_kv_body,
        grid=(b, l // tl),
        in_specs=[row(MLA_KV_LORA), row(LANES), _const_spec((MLA_KV_LORA, MLA_HEADS * LANES)),
                  _const_spec((vw, MLA_KV_LORA)), _const_spec((1, LANES))],
        out_specs=[row(MLA_HEADS * LANES), pl.BlockSpec((1, vw, tl), lambda i, j: (i, 0, j))],
        out_shape=[jax.ShapeDtypeStruct((b, l, MLA_HEADS * LANES), BF16),
                   jax.ShapeDtypeStruct((b, vw, l), BF16)],
        compiler_params=_params("parallel", "parallel"),
        name="mla_kv",
    )(lat, krp, w["wuk"], w["wuvt"], w["gk"])


def _lru_body(h_ref, g_ref, wrec_ref, wgate_ref, cw_ref, cb_ref, wr_ref, wi_ref, br_ref, bi_ref, lam_ref,
              cprev_ref, h0_ref, y_ref, hl_ref, cl_ref, buf, a_s, b_s, hcar, *, tm):
    @pl.when(pl.program_id(1) == 0)
    def _():
        buf[0:HALO, :] = cprev_ref[0]
        hcar[...] = h0_ref[0]

    hb = _rms(h_ref[0], g_ref[...]).astype(BF16)
    xr = _dot(hb, wrec_ref[...])
    xg = _dot(hb, wgate_ref[...])
    buf[HALO:HALO + tm, :] = xr
    xc = cb_ref[...] + xr * cw_ref[CONV_WIDTH - 1:CONV_WIDTH, :]
    for j in range(CONV_WIDTH - 1):
        off = HALO - (CONV_WIDTH - 1) + j
        xc = xc + cw_ref[j:j + 1, :] * buf[off:off + tm, :]
    xcb = xc.astype(BF16)
    r = _sigmoid(_dot(xcb, wr_ref[...]) + br_ref[...])
    i = _sigmoid(_dot(xcb, wi_ref[...]) + bi_ref[...])
    log_a = (-LRU_C) * r * _softplus(-lam_ref[...])
    a = jnp.exp(log_a)
    z = -jnp.tanh(log_a) * (a * a + 1.0)
    b = jnp.where(z > 0.0, z * lax.rsqrt(z), 0.0) * (i * xc)
    a_s[...] = a
    b_s[...] = b

    row = lax.broadcasted_iota(jnp.int32, (SUBLANES, a.shape[1]), 0)

    def step(g, hprev):
        r0 = pl.multiple_of(g * SUBLANES, SUBLANES)
        ag = a_s[pl.ds(r0, SUBLANES), :]
        bg = b_s[pl.ds(r0, SUBLANES), :]
        d = 1
        while d < SUBLANES:
            keep = row >= d
            a_up = jnp.where(keep, pltpu.roll(ag, d, 0), 1.0)
            b_up = jnp.where(keep, pltpu.roll(bg, d, 0), 0.0)
            bg = ag * b_up + bg
            ag = ag * a_up
            d *= 2
        hg = ag * hprev + bg
        b_s[pl.ds(r0, SUBLANES), :] = hg
        return hg[SUBLANES - 1:, :]

    hfin = lax.fori_loop(0, tm // SUBLANES, step, hcar[...], unroll=2)
    hcar[...] = hfin
    y_ref[0] = (_gelu_tanh(xg) * b_s[...]).astype(BF16)
    hl_ref[0] = hfin
    tail = buf[tm:tm + HALO, :]
    buf[0:HALO, :] = tail
    cl_ref[0] = tail


def _lru(h, w, conv_prev8, h0):
    b, t, d = h.shape
    tm = _row_tile(t, ROW_TILE)
    wd = LRU_WIDTH
    vec = lambda: _const_spec((1, wd))
    return pl.pallas_call(
        functools.partial(_lru_body, tm=tm),
        grid=(b, t // tm),
        in_specs=[pl.BlockSpec((1, tm, d), lambda i, j: (i, j, 0)), _const_spec((1, d)),
                  _const_spec((d, wd)), _const_spec((d, wd)), _const_spec((CONV_WIDTH, wd)), vec(),
                  _const_spec((wd, wd)), _const_spec((wd, wd)), vec(), vec(), vec(),
                  pl.BlockSpec((1, HALO, wd), lambda i, j: (i, 0, 0)),
                  pl.BlockSpec((1, 1, wd), lambda i, j: (i, 0, 0))],
        out_specs=[pl.BlockSpec((1, tm, wd), lambda i, j: (i, j, 0)),
                   pl.BlockSpec((1, 1, wd), lambda i, j: (i, 0, 0)),
                   pl.BlockSpec((1, HALO, wd), lambda i, j: (i, 0, 0))],
        out_shape=[jax.ShapeDtypeStruct((b, t, wd), BF16), jax.ShapeDtypeStruct((b, 1, wd), F32),
                   jax.ShapeDtypeStruct((b, HALO, wd), F32)],
        scratch_shapes=[pltpu.VMEM((tm + HALO, wd), F32), pltpu.VMEM((tm, wd), F32),
                        pltpu.VMEM((tm, wd), F32), pltpu.VMEM((1, wd), F32)],
        compiler_params=_params("parallel", "arbitrary"),
        name="lru",
    )(h, w["g_mix"], w["wrec"], w["wgate"], w["conv_w"], w["conv_b"], w["wr"], w["wi"], w["br"], w["bi"],
      w["lam"], conv_prev8, h0)


def _group_rms(x, g):
    low = lax.broadcasted_iota(jnp.int32, (1, LANES), 1) < FOX_HEAD_DIM
    out = []
    for j in range(x.shape[1] // LANES):
        xb = x[:, j * LANES:(j + 1) * LANES]
        sq = xb * xb
        s_low = jnp.sum(jnp.where(low, sq, 0.0), axis=-1, keepdims=True)
        s_high = jnp.sum(jnp.where(low, 0.0, sq), axis=-1, keepdims=True)
        ss = jnp.where(low, s_low, s_high)
        out.append(xb * lax.rsqrt(ss * (1.0 / FOX_HEAD_DIM) + NORM_EPS))
    return jnp.concatenate(out, axis=-1) * g


def _cumsum_rows(x, ltri):
    out = None
    for p in _split_bf16(x, 3):
        d = _dot(ltri, p)
        out = d if out is None else out + d
    return out


def _fox_aug(x, keep, bias):
    blocks = [x[:, (hd // 2) * LANES:(hd // 2 + 1) * LANES] for hd in range(FOX_HEADS)]
    return jnp.concatenate(blocks, axis=-1) * keep + bias


def _fox_key_bias(c, place):
    lane = lax.broadcasted_iota(jnp.int32, (1, LANES), 1)
    hi, mid, lo = _split_bf16(jnp.where(lane < FOX_HEADS, c * (-LOG2E), 0.0), 3)
    packed = (hi.astype(F32) + pltpu.roll(mid.astype(F32), FOX_HEADS, 1)
              + pltpu.roll(lo.astype(F32), 2 * FOX_HEADS, 1))
    return _dot(packed.astype(BF16), place)


def _fox_in_body(h_ref, g_ref, wq_ref, wk_ref, wv_ref, wf_ref, bf_ref, gq_ref, gk_ref,
                 keep_ref, ones_ref, place_ref, ltri_ref, c0_ref,
                 q_ref, k32_ref, v32_ref, ka_ref, vb_ref, lf_ref, lf_s, carry, *, tm, tc, values_transposed):
    @pl.when(pl.program_id(1) == 0)
    def _():
        carry[...] = c0_ref[0]

    hb = _rms(h_ref[0], g_ref[...]).astype(BF16)
    keep = keep_ref[...]
    q = _group_rms(_dot(hb, wq_ref[...]), gq_ref[...])
    q_ref[0] = _fox_aug(q, keep, ones_ref[...]).astype(BF16)
    k = _group_rms(_dot(hb, wk_ref[...]), gk_ref[...])
    k32_ref[0] = k
    v = _dot(hb, wv_ref[...])
    v32_ref[0] = v
    vb_ref[0] = (v.T if values_transposed else v).astype(BF16)
    logf =-_softplus(-(_dot(hb, wf_ref[...]) + bf_ref[...]))
    lf_ref[0] = logf[:, :FOX_HEADS]
    if tc > tm:
        lf_s[...] = jnp.zeros_like(lf_s)
    lf_s[0:tm, :] = logf
    c = carry[...] + _cumsum_rows(lf_s[...], ltri_ref[...])[0:tm, :]
    carry[...] = c[tm - 1:tm, :]
    ka_ref[0] = _fox_aug(k, keep, _fox_key_bias(c, place_ref[...])).astype(BF16)


def _fox_in(h, w, c0):
    b, t, d = h.shape
    tm = _row_tile(t, ROW_TILE)
    tc = max(tm, LANES)
    ltri = jnp.tril(jnp.ones((tc, tc), F32)).astype(BF16)
    row = lambda n: pl.BlockSpec((1, tm, n), lambda i, j: (i, j, 0))
    fw, aw = FOX_WIDTH, FOX_HEADS * LANES
    values_transposed = tm % LANES == 0
    v_spec = pl.BlockSpec((1, fw, tm), lambda i, j: (i, 0, j)) if values_transposed else row(fw)
    v_shape = (b, fw, t) if values_transposed else (b, t, fw)
    outs = pl.pallas_call(
        functools.partial(_fox_in_body, tm=tm, tc=tc, values_transposed=values_transposed),
        grid=(b, t // tm),
        in_specs=[row(d), _const_spec((1, d)),
                  _const_spec((d, fw)), _const_spec((d, fw)), _const_spec((d, fw)), _const_spec((d, LANES)),
                  _const_spec((1, LANES)), _const_spec((1, fw)), _const_spec((1, fw)),
                  _const_spec((1, aw)), _const_spec((1, aw)), _const_spec((LANES, aw)),
                  _const_spec((tc, tc)), pl.BlockSpec((1, 1, LANES), lambda i, j: (i, 0, 0))],
        out_specs=[row(aw), row(fw), row(fw), row(aw), v_spec, row(FOX_HEADS)],
        out_shape=[jax.ShapeDtypeStruct((b, t, aw), BF16), jax.ShapeDtypeStruct((b, t, fw), F32),
                   jax.ShapeDtypeStruct((b, t, fw), F32), jax.ShapeDtypeStruct((b, t, aw), BF16),
                   jax.ShapeDtypeStruct(v_shape, BF16), jax.ShapeDtypeStruct((b, t, FOX_HEADS), F32)],
        scratch_shapes=[pltpu.VMEM((tc, LANES), F32), pltpu.VMEM((1, LANES), F32)],
        compiler_params=_params("parallel", "arbitrary"),
        name="fox_in",
    )(h, w["g_mix"], w["wq"], w["wk"], w["wv"], w["wf"], w["bf"], w["gq"], w["gk"],
      w["keep"], w["ones"], w["place"], ltri, c0)
    q, k32, v32, ka, vb, logf = outs
    return q, k32, v32, ka, (vb if values_transposed else jnp.swapaxes(vb, 1, 2)), logf


def _fox_past_body(k_ref, lf_ref, keep_ref, place_ref, ltri_ref, ka_ref, c_ref, carry):
    @pl.when(pl.program_id(1) == 0)
    def _():
        carry[...] = jnp.zeros_like(carry)

    c = carry[...] + _cumsum_rows(lf_ref[0], ltri_ref[...])
    c_ref[0] = c
    carry[...] = c[c.shape[0] - 1:, :]
    ka_ref[0] = _fox_aug(k_ref[0], keep_ref[...], _fox_key_bias(c, place_ref[...])).astype(BF16)


def _fox_past(past_k, past_logf, w):
    b, p, n = past_logf.shape
    tc = _row_tile(p, ROW_TILE)
    ltri = jnp.tril(jnp.ones((tc, tc), F32)).astype(BF16)
    fw, aw = FOX_WIDTH, FOX_HEADS * LANES
    row = lambda m: pl.BlockSpec((1, tc, m), lambda i, j: (i, j, 0))
    return pl.pallas_call(
        _fox_past_body,
        grid=(b, p // tc),
        in_specs=[row(fw), row(n), _const_spec((1, aw)), _const_spec((LANES, aw)), _const_spec((tc, tc))],
        out_specs=[row(aw), row(n)],
        out_shape=[jax.ShapeDtypeStruct((b, p, aw), BF16), jax.ShapeDtypeStruct((b, p, n), F32)],
        scratch_shapes=[pltpu.VMEM((1, n), F32)],
        compiler_params=_params("parallel", "arbitrary"),
        name="fox_past",
    )(past_k, past_logf, w["keep"], w["place"], ltri)


def _flash_body(q_ref, k_ref, vt_ref, o_ref, m_s, l_s, acc_s, sa_s, sb_s, *, tq, tqs, tk, tks, fr, fc, n_k, past,
                kv_len, chunk_causal, diag_aligned):
    q_start = past + pl.program_id(2) * tq
    q = q_ref[0]
    m_s[...] = jnp.full_like(m_s, NEG_INF)
    l_s[...] = jnp.zeros_like(l_s)
    acc_s[...] = jnp.zeros_like(acc_s)
    shift = int(math.log2(CHUNK))
    hv = LANES // 2

    def block_kind(r0, rn, c0, cn):
        if chunk_causal:
            k_lo, k_hi, q_lo, q_hi = r0 >> shift, (r0 + rn - 1) >> shift, c0 >> shift, (c0 + cn - 1) >> shift
        else:
            k_lo, k_hi, q_lo, q_hi = r0, r0 + rn - 1, c0, c0 + cn - 1
        return "visible" if k_hi <= q_lo else ("hidden" if k_lo > q_hi else "partial")

    def diag_streams(d):
        out = []
        for r0 in range(0, tk, tks):
            for hh in range(2):
                for c0 in range(0, tq, tqs):
                    halves = [(c, block_kind(d * tk + r0, tks, c, tks)) for c in range(c0, c0 + tqs, tks)]
                    if all(kind == "visible" for _, kind in halves):
                        out.append((hh, r0, tks, c0, tqs, "visible"))
                    else:
                        out += [(hh, r0, tks, c, tks, kind) for c, kind in halves if kind != "hidden"]
        return out

    full_streams = [(hh, r0, fr, c0, fc, "visible") for r0 in range(0, tk, fr) for hh in range(2)
                    for c0 in range(0, tq, fc)]
    mask_streams = [st[:5] + ("partial",) for st in full_streams]

    def score(kblk, stream):
        hh, r0, rn, c0, cn, _ = stream
        head = slice(hh * LANES, (hh + 1) * LANES)
        return _dot_nt(kblk[r0:r0 + rn, head], q[c0:c0 + cn, head])

    def key_block(kt):
        return k_ref[0, pl.ds(pl.multiple_of(jnp.minimum(kt, n_k - 1) * tk, tk), tk), :]

    def absorb(s, stream, kt):
        hh, r0, rn, c0, cn, kind = stream
        cols = slice(c0, c0 + cn)
        k0 = pl.multiple_of(kt * tk, tk)
        if kind == "partial":
            kpos = k0 + r0 + lax.broadcasted_iota(jnp.int32, (rn, cn), 0)
            qpos = q_start + c0 + lax.broadcasted_iota(jnp.int32, (rn, cn), 1)
            if chunk_causal:
                vis = lax.shift_right_logical(kpos, shift) <= lax.shift_right_logical(qpos, shift)
            else:
                vis = kpos <= qpos
            s = jnp.where(jnp.logical_and(vis, kpos < kv_len), s, NEG_INF)
        m_old = m_s[hh, :, cols]
        m_new = jnp.maximum(m_old, jnp.max(s, axis=0, keepdims=True))
        alpha = jnp.exp2(m_old - m_new)
        p = jnp.exp2(s - m_new)
        l_s[hh, :, cols] = alpha * l_s[hh, :, cols] + jnp.sum(p, axis=0, keepdims=True)
        vt = vt_ref[0, hh * hv:(hh + 1) * hv, pl.ds(pl.multiple_of(k0 + r0, LANES), rn)]
        acc_s[hh, :, cols] = alpha * acc_s[hh, :, cols] + _dot(vt, p.astype(BF16))
        m_s[hh, :, cols] = m_new

    def single_tile(kt, streams, after_first_scores=None):
        kblk = key_block(kt)
        ahead = 3
        pending = [score(kblk, st) for st in streams[:ahead]]
        if after_first_scores is not None:
            after_first_scores()
        for idx, st in enumerate(streams):
            s = pending.pop(0)
            if idx + ahead < len(streams):
                pending.append(score(kblk, streams[idx + ahead]))
            absorb(s, st, kt)

    def store_scores(buf, kt):
        kblk = key_block(kt)
        for st in full_streams:
            hh, r0, rn, c0, cn, _ = st
            buf[hh, r0:r0 + rn, c0:c0 + cn] = score(kblk, st)

    def absorb_stored(buf, kt):
        for st in full_streams:
            hh, r0, rn, c0, cn, _ = st
            absorb(buf[hh, r0:r0 + rn, c0:c0 + cn], st, kt)

    def tile_pair(i, carry):
        kt = 2 * i
        store_scores(sb_s, kt + 1)
        absorb_stored(sa_s, kt)
        store_scores(sa_s, kt + 2)
        absorb_stored(sb_s, kt + 1)
        return carry

    def full_tile(kt, carry):
        single_tile(kt, full_streams)
        return carry

    def masked_tile(kt, carry):
        single_tile(kt, mask_streams)
        return carry

    n_full = jnp.minimum(q_start // tk, kv_len // tk)
    q_last = q_start + tq - 1
    k_hi = (q_last // CHUNK + 1) * CHUNK if chunk_causal else q_last + 1
    n_end = jnp.minimum((k_hi + tk - 1) // tk, n_k)
    n_pair = n_full // 2
    if diag_aligned:
        for d in range(tq // tk):
            single_tile(n_full + d, diag_streams(d),
                        after_first_scores=(lambda: store_scores(sa_s, 0)) if d == 0 else None)
        lax.fori_loop(0, n_pair, tile_pair, 0)
        lax.fori_loop(2 * n_pair, n_full, full_tile, 0)
    else:
        @pl.when(n_pair > 0)
        def _():
            store_scores(sa_s, 0)

        lax.fori_loop(0, n_pair, tile_pair, 0)
        lax.fori_loop(2 * n_pair, n_full, full_tile, 0)
        lax.fori_loop(n_full, n_end, masked_tile, 0)
    out_t = jnp.concatenate([acc_s[0] / l_s[0], acc_s[1] / l_s[1]], axis=0)
    o_ref[0] = out_t.T.astype(BF16)


def _flash(q, k, vt, *, past, kv_len, chunk_causal):
    b, t, w = q.shape
    lp = k.shape[1]
    n_pairs = w // (2 * LANES)
    t_pad = max(t, LANES)
    if t_pad > t:
        q = _pad_rows(q, t_pad)
    if past == 0 and t % KEY_TILE == 0 and lp % KEY_TILE == 0:
        tk = KEY_TILE
        tq = QUERY_TILE if t % QUERY_TILE == 0 else KEY_TILE
    elif t_pad == LANES:
        tq, tk = LANES, lp
    else:
        tq = tk = LANES
    assert t_pad % tq == 0 and lp % tk == 0, (t, lp, tq, tk)
    tks = MXU_TILE if tk % MXU_TILE == 0 and tq > LANES else tk
    tqs = min(tq, 2 * MXU_TILE)
    fr, fc = (tk, MXU_TILE) if tks < tk and tq % MXU_TILE == 0 else (tks, tqs)
    out = pl.pallas_call(
        functools.partial(_flash_body, tq=tq, tqs=tqs, tk=tk, tks=tks, fr=fr, fc=fc, n_k=lp // tk, past=past,
                          kv_len=kv_len, chunk_causal=chunk_causal,
                          diag_aligned=(past == 0 and tq % tk == 0 and kv_len == lp and tks < tk)),
        grid=(b, n_pairs, t_pad // tq),
        in_specs=[pl.BlockSpec((1, tq, 2 * LANES), lambda i, j, s: (i, s, j)),
                  pl.BlockSpec((1, lp, 2 * LANES), lambda i, j, s: (i, 0, j)),
                  pl.BlockSpec((1, LANES, lp), lambda i, j, s: (i, j, 0))],
        out_specs=pl.BlockSpec((1, tq, LANES), lambda i, j, s: (i, s, j)),
        out_shape=jax.ShapeDtypeStruct((b, t_pad, n_pairs * LANES), BF16),
        scratch_shapes=[pltpu.VMEM((2, 1, tq), F32), pltpu.VMEM((2, 1, tq), F32),
                        pltpu.VMEM((2, LANES // 2, tq), F32),
                        pltpu.VMEM((2, tk, tq), F32), pltpu.VMEM((2, tk, tq), F32)],
        compiler_params=_params("parallel", "parallel", "arbitrary"),
        name="flash_mla" if chunk_causal else "flash_fox",
    )(q, k, vt)
    return out[:, :t]


def _row(v):
    return v.reshape(1, -1).astype(F32)


def _pad_lanes(x, lo, total):
    pad = [(0, 0)] * (x.ndim - 1) + [(lo, total - lo - x.shape[-1])]
    return jnp.pad(x, pad)


def _ffn_weights(g, w_in, w_out):
    f = w_out.shape[0]
    return dict(g=_row(g), wg=w_in[:, :f].astype(BF16), wu=w_in[:, f:].astype(BF16), wo=w_out.astype(BF16))


def _mem_weights(g, g_src, w_q, w_kv, w_o, g_q, g_k):
    d = w_q.shape[0]
    kv = w_kv.reshape(d, MEM_HEADS, 2, MEM_HEAD_DIM)
    return dict(g=_row(g), g_src=_row(g_src), wq=w_q.astype(BF16), wo=w_o.astype(BF16),
                wk=kv[:, :, 0].reshape(d, MEM_WIDTH).astype(BF16),
                wv=kv[:, :, 1].reshape(d, MEM_WIDTH).astype(BF16),
                gq=_row(g_q) * (MEM_HEAD_DIM ** -0.5), gk=_row(g_k))


def _even_weights(g_mix, w_in, g_qlat, g_kvlat, w_uq, w_ukv, g_q, g_k, conv_w, conv_b, gate_w, gate_b, lam, w_out):
    d = w_in.shape[0]
    o1 = MLA_Q_LORA
    o2 = o1 + MLA_KV_LORA
    o3 = o2 + MLA_ROPE
    o4 = o3 + LRU_WIDTH
    half = MLA_ROPE // 2
    swap_halves = lambda r: jnp.concatenate([r[..., half:], r[..., :half]], axis=-1)
    uq3 = w_uq.reshape(MLA_Q_LORA, MLA_HEADS, MLA_QK)
    uq = _pad_lanes(uq3, 0, LANES)
    uq_swap = _pad_lanes(swap_halves(uq3[:, :, MLA_NOPE:]), MLA_NOPE, LANES)
    kr = w_in[:, o2:o3]
    ukv = w_ukv.reshape(MLA_KV_LORA, MLA_HEADS, MLA_NOPE + MLA_V)
    uk = _pad_lanes(ukv[:, :, :MLA_NOPE], 0, LANES)
    blk = LRU_WIDTH // LRU_BLOCKS
    eye = jnp.eye(LRU_BLOCKS, dtype=F32)
    wr = jnp.einsum("ncd,nm->ncmd", gate_w[:, :, :blk], eye).reshape(LRU_WIDTH, LRU_WIDTH)
    wi = jnp.einsum("ncd,nm->ncmd", gate_w[:, :, blk:], eye).reshape(LRU_WIDTH, LRU_WIDTH)
    return dict(
        g_mix=_row(g_mix), wcq=w_in[:, :o1].astype(BF16), g_qlat=_row(g_qlat),
        wuq=uq.reshape(MLA_Q_LORA, MLA_HEADS * LANES).astype(BF16),
        wuq_swap=uq_swap.reshape(MLA_Q_LORA, MLA_HEADS * LANES).astype(BF16),
        gq=_pad_lanes(_row(g_q), 0, LANES) * (MLA_QK ** -0.5 * LOG2E),
        wckv=w_in[:, o1:o2].astype(BF16), g_kvlat=_row(g_kvlat),
        wkr=_pad_lanes(kr, MLA_NOPE, LANES).astype(BF16),
        wkr_swap=_pad_lanes(swap_halves(kr), MLA_NOPE, LANES).astype(BF16),
        wuk=uk.reshape(MLA_KV_LORA, MLA_HEADS * LANES).astype(BF16),
        wuvt=ukv[:, :, MLA_NOPE:].reshape(MLA_KV_LORA, MLA_HEADS * MLA_V).T.astype(BF16),
        gk=_pad_lanes(_row(g_k), 0, LANES),
        wrec=w_in[:, o3:o4].astype(BF16), wgate=w_in[:, o4:].astype(BF16),
        conv_w=conv_w.astype(F32), conv_b=_row(conv_b), wr=wr.astype(BF16), wi=wi.astype(BF16),
        br=_row(gate_b[:, :blk]), bi=_row(gate_b[:, blk:]), lam=_row(lam),
        wo_attn=w_out[:MLA_HEADS * MLA_V].astype(BF16), wo_rec=w_out[MLA_HEADS * MLA_V:].astype(BF16))


def _odd_weights(g_mix, w_in, b_f, g_q, g_k, w_out):
    fw = FOX_WIDTH
    lane = jnp.arange(FOX_HEADS * LANES)
    hd, within = lane // LANES, lane % LANES
    own_low = hd % 2 == 0
    keep = jnp.where(own_low, within < FOX_HEAD_DIM, within >= FOX_HEAD_DIM)
    part = within - jnp.where(own_low, FOX_HEAD_DIM, 0)
    is_bias = (part >= 0) & (part < 3)
    src = part * FOX_HEADS + hd
    place = ((jnp.arange(LANES)[:, None] == src[None, :]) & is_bias[None, :]).astype(BF16)
    return dict(
        keep=keep.astype(F32)[None, :], ones=is_bias.astype(F32)[None, :], place=place,
        g_mix=_row(g_mix), wq=w_in[:, :fw].astype(BF16), wk=w_in[:, fw:2 * fw].astype(BF16),
        wv=w_in[:, 2 * fw:3 * fw].astype(BF16), wf=_pad_lanes(w_in[:, 3 * fw:], 0, LANES).astype(BF16),
        bf=_pad_lanes(_row(b_f), 0, LANES),
        gq=jnp.tile(_row(g_q), (1, FOX_HEADS)) * (FOX_HEAD_DIM ** -0.5 * LOG2E),
        gk=jnp.tile(_row(g_k), (1, FOX_HEADS)), wo=w_out.astype(BF16))


def _rope_tables(pos):
    half = MLA_ROPE // 2
    inv_freq = ROPE_THETA ** (-jnp.arange(half, dtype=F32) / half)
    ang = pos.astype(F32)[:, None] * inv_freq[None, :]
    cos, sin = jnp.cos(ang), jnp.sin(ang)
    c = jnp.concatenate([jnp.ones((pos.shape[0], MLA_NOPE), F32), cos, cos,
                         jnp.ones((pos.shape[0], LANES - MLA_QK), F32)], axis=-1)
    s = _pad_lanes(jnp.concatenate([-sin, sin], axis=-1), MLA_NOPE, LANES)
    return c, s


def _pad_rows(x, total):
    return jnp.pad(x, [(0, 0), (0, total - x.shape[1])] + [(0, 0)] * (x.ndim - 2))


def _kv_pad_len(t, past):
    l = past + t
    if past == 0 and t % KEY_TILE == 0:
        return l
    return -(-l // LANES) * LANES


def _even_layer(h, past, w, state):
    b, t, d = h.shape
    past_lat, past_krope, h0, conv_prev = state
    lp = _kv_pad_len(t, past)
    tables = _rope_tables(past + jnp.arange(t))
    if past == 0 and lp == t and t % LANES == 0:
        q, lat_new, krope_new, k, vt = _mla_in(h, w, tables, with_kv=True)
    else:
        q, lat_new, krp_new = _mla_in(h, w, tables, with_kv=False)
        krope_new = krp_new[:, :, MLA_NOPE:MLA_QK]
        lat_all = _pad_rows(jnp.concatenate([past_lat, lat_new], axis=1), lp)
        krp_all = _pad_rows(jnp.concatenate([_pad_lanes(past_krope, MLA_NOPE, LANES), krp_new], axis=1), lp)
        k, vt = _mla_kv(lat_all, krp_all, w)
    attn = _flash(q, k, vt, past=past, kv_len=past + t, chunk_causal=True)
    conv_prev8 = jnp.pad(conv_prev, ((0, 0), (HALO - (CONV_WIDTH - 1), 0), (0, 0)))
    y_rec, h_last, conv_last = _lru(h, w, conv_prev8, h0[:, None, :])
    new = (lat_new, krope_new, h_last[:, 0], conv_last[:, HALO - (CONV_WIDTH - 1):])
    return [attn, y_rec], [w["wo_attn"], w["wo_rec"]], new


def _odd_layer(h, past, w, state):
    b, t, d = h.shape
    past_k, past_v, past_logf = state
    if past > 0:
        ka_past, c_past = _fox_past(past_k.reshape(b, past, FOX_WIDTH),
                                    _pad_lanes(past_logf.astype(F32), 0, LANES), w)
        c0 = c_past[:, past - 1:past, :]
    else:
        ka_past = jnp.zeros((b, 0, FOX_HEADS * LANES), BF16)
        c0 = jnp.zeros((b, 1, LANES), F32)
    q, k32, v32, ka_new, vt_new, logf = _fox_in(h, w, c0)
    lp = _kv_pad_len(t, past)
    k_all = _pad_rows(jnp.concatenate([ka_past, ka_new], axis=1), lp)
    vt_past = jnp.swapaxes(past_v.reshape(b, past, FOX_WIDTH).astype(BF16), 1, 2)
    vt_all = jnp.pad(jnp.concatenate([vt_past, vt_new], axis=2), ((0, 0), (0, 0), (0, lp - past - t)))
    attn = _flash(q, k_all, vt_all, past=past, kv_len=past + t, chunk_causal=False)
    new = (k32.reshape(b, t, FOX_HEADS, FOX_HEAD_DIM), v32.reshape(b, t, FOX_HEADS, FOX_HEAD_DIM),
           logf)
    return [attn], [w["wo"]], new


def _trunk(x, past, layers, mem_kvs, even_states, odd_states):
    b, t, d = x.shape
    even_new, odd_new = [], []
    for li, lw in enumerate(layers):
        h = _ffn(x.reshape(b * t, d), lw["ffn1"]).reshape(b, t, d)
        if li % 2 == 0:
            parts, w_parts, new = _even_layer(h, past, lw["mix"], even_states[li // 2])
            even_new.append(new)
        else:
            parts, w_parts, new = _odd_layer(h, past, lw["mix"], odd_states[li // 2])
            odd_new.append(new)
        x = _post_mixer(h, parts, w_parts, mem_kvs[li][0], mem_kvs[li][1], lw["mem"], lw["ffn2"])
    return x, even_new, odd_new


def kernel(x_prompt, x_sample, mem_prompt, cache_mla_latent, cache_mla_krope, state_lru_h, state_lru_conv, cache_fox_k, cache_fox_v, cache_fox_logf, cache_mem_k, cache_mem_v, norm_ffn1, ffn1_w_in, ffn1_w_out, norm_mix, norm_mem, norm_mem_src, mem_w_q, mem_w_kv, mem_w_o, mem_g_q, mem_g_k, norm_ffn2, ffn2_w_in, ffn2_w_out, ev_w_in, ev_g_qlat, ev_g_kvlat, ev_w_uq, ev_w_ukv, ev_g_q, ev_g_k, ev_conv_w, ev_conv_b, ev_gate_w, ev_gate_b, ev_lambda, ev_w_out, od_w_in, od_b_f, od_g_q, od_g_k, od_w_out):
    depth = norm_ffn1.shape[0]
    n_even, n_odd = (depth + 1) // 2, depth // 2
    b, _, _ = x_prompt.shape
    bs = x_sample.shape[0]
    past = cache_mla_latent.shape[2] if n_even else cache_fox_k.shape[2]

    layers = []
    for li in range(depth):
        j = li // 2
        if li % 2 == 0:
            mix = _even_weights(norm_mix[li], ev_w_in[j], ev_g_qlat[j], ev_g_kvlat[j], ev_w_uq[j], ev_w_ukv[j],
                                ev_g_q[j], ev_g_k[j], ev_conv_w[j], ev_conv_b[j], ev_gate_w[j], ev_gate_b[j],
                                ev_lambda[j], ev_w_out[j])
        else:
            mix = _odd_weights(norm_mix[li], od_w_in[j], od_b_f[j], od_g_q[j], od_g_k[j], od_w_out[j])
        layers.append(dict(
            ffn1=_ffn_weights(norm_ffn1[li], ffn1_w_in[li], ffn1_w_out[li]),
            ffn2=_ffn_weights(norm_ffn2[li], ffn2_w_in[li], ffn2_w_out[li]),
            mem=_mem_weights(norm_mem[li], norm_mem_src[li], mem_w_q[li], mem_w_kv[li], mem_w_o[li],
                             mem_g_q[li], mem_g_k[li]),
            mix=mix))

    p_mem_k, p_mem_v, mem_kb, mem_vb = _mem_kv(mem_prompt, [lw["mem"] for lw in layers])
    ev0 = [(jnp.zeros((b, 0, MLA_KV_LORA), F32), jnp.zeros((b, 0, MLA_ROPE), F32),
            jnp.zeros((b, LRU_WIDTH), F32), jnp.zeros((b, CONV_WIDTH - 1, LRU_WIDTH), F32))
           for _ in range(n_even)]
    od0 = [(jnp.zeros((b, 0, FOX_HEADS, FOX_HEAD_DIM), F32), jnp.zeros((b, 0, FOX_HEADS, FOX_HEAD_DIM), F32),
            jnp.zeros((b, 0, FOX_HEADS), F32)) for _ in range(n_odd)]
    y_prompt, ev_p, od_p = _trunk(x_prompt, 0, layers, [(mem_kb[li], mem_vb[li]) for li in range(depth)],
                                  ev0, od0)

    m_tok = cache_mem_k.shape[2]
    mem_s = [(cache_mem_k[li].reshape(bs, m_tok, MEM_WIDTH).astype(BF16),
              cache_mem_v[li].reshape(bs, m_tok, MEM_WIDTH).astype(BF16)) for li in range(depth)]
    ev_s = [(cache_mla_latent[j], cache_mla_krope[j], state_lru_h[j], state_lru_conv[j]) for j in range(n_even)]
    od_s = [(cache_fox_k[j], cache_fox_v[j], cache_fox_logf[j]) for j in range(n_odd)]
    y_sample, ev_n, od_n = _trunk(x_sample, past, layers, mem_s, ev_s, od_s)

    p_even = [jnp.stack([s[f] for s in ev_p]) for f in range(4)]
    p_odd = [jnp.stack([s[f] for s in od_p]) for f in range(3)]
    s_even = [jnp.stack([s[f] for s in ev_n]) for f in range(4)]
    s_odd = [jnp.stack([s[f] for s in od_n]) for f in range(3)]
    return (y_prompt, y_sample, *p_even, *p_odd, p_mem_k, p_mem_v, *s_even, *s_odd)
```

```python
import functools
import math

import jax
import jax.numpy as jnp
from jax import lax
from jax.experimental import pallas as pl
from jax.experimental.pallas import tpu as pltpu

F32 = jnp.float32
BF16 = jnp.bfloat16

NORM_EPS = 1e-6
NEG_INF = -1e30
LOG2E = math.log2(math.e)
CHUNK = 64
LANES = 128
SUBLANES = 8
ROPE_THETA = 10000.0
LRU_C = 8.0
MLA_HEADS = 8
MLA_NOPE = 64
MLA_ROPE = 32
MLA_QK = MLA_NOPE + MLA_ROPE
MLA_V = 64
MLA_Q_LORA = 256
MLA_KV_LORA = 128
LRU_WIDTH = 512
LRU_BLOCKS = 8
CONV_WIDTH = 4
FOX_HEADS = 16
FOX_HEAD_DIM = 64
FOX_WIDTH = FOX_HEADS * FOX_HEAD_DIM
MEM_HEADS = 4
MEM_HEAD_DIM = 128
MEM_WIDTH = MEM_HEADS * MEM_HEAD_DIM
HALO = 8

MXU_TILE = 256
ROW_TILE = 512
KEY_TILE = 512
QUERY_TILE = 2048

VMEM_LIMIT = 56 * 1024 * 1024


def _dot(a, b):
    return jnp.dot(a, b, preferred_element_type=F32)


def _dot_nt(a, b):
    return lax.dot_general(a, b, (((1,), (1,)), ((), ())), preferred_element_type=F32)


def _rms(x, g):
    return x * lax.rsqrt(jnp.mean(x * x, axis=-1, keepdims=True) + NORM_EPS) * g


def _head_rms(x, g, n_live):
    ss = jnp.sum(x * x, axis=-1, keepdims=True) * (1.0 / n_live)
    return x * lax.rsqrt(ss + NORM_EPS) * g


def _sigmoid(x):
    return 1.0 / (1.0 + jnp.exp(-x))


def _log1p(y):
    u = 1.0 + y
    d = u - 1.0
    return jnp.where(d == 0.0, y, jnp.log(u) * (y / jnp.where(d == 0.0, 1.0, d)))


def _softplus(x):
    return jnp.maximum(x, 0.0) + _log1p(jnp.exp(-jnp.abs(x)))


def _gelu_tanh(x):
    return 0.5 * x * (1.0 + jnp.tanh(math.sqrt(2.0 / math.pi) * (x + 0.044715 * (x * x * x))))


def _split_bf16(x, parts):
    out = []
    r = x
    for _ in range(parts):
        p = r.astype(BF16)
        out.append(p)
        r = r - p.astype(F32)
    return out


def _const_spec(shape):
    nd = len(shape)
    return pl.BlockSpec(shape, lambda *_: (0,) * nd, pipeline_mode=pl.Buffered(1))


def _params(*sem):
    return pltpu.CompilerParams(dimension_semantics=sem, vmem_limit_bytes=VMEM_LIMIT)


def _row_tile(n, cap):
    t = min(n, cap)
    assert n % t == 0, (n, t)
    return t


FFN_CHUNKS = 2


def _ffn_chunk_bounds(f):
    tiles = -(-f // MXU_TILE)
    per = -(-tiles // FFN_CHUNKS) * MXU_TILE
    edges = [min(i * per, f) for i in range(FFN_CHUNKS + 1)]
    return [(lo, hi) for lo, hi in zip(edges[:-1], edges[1:]) if hi > lo]


def _swiglu_half_step(x, g_ref, wg_ref, wu_ref, wo_ref):
    hb = _rms(x, g_ref[...]).astype(BF16)
    acc = jnp.zeros_like(x)
    for lo, hi in _ffn_chunk_bounds(wg_ref.shape[1]):
        sl = slice(lo, hi)
        gate = _dot(hb, wg_ref[:, sl])
        up = _dot(hb, wu_ref[:, sl])
        act = (gate * _sigmoid(gate) * up).astype(BF16)
        acc = acc + _dot(act, wo_ref[sl, :])
    return x + 0.5 * acc


def _ffn_body(x_ref, g_ref, wg_ref, wu_ref, wo_ref, o_ref):
    o_ref[...] = _swiglu_half_step(x_ref[...], g_ref, wg_ref, wu_ref, wo_ref)


def _ffn(x2, w):
    n, d = x2.shape
    f = w["wg"].shape[1]
    tm = _row_tile(n, ROW_TILE)
    return pl.pallas_call(
        _ffn_body,
        grid=(n // tm,),
        in_specs=[pl.BlockSpec((tm, d), lambda i: (i, 0)), _const_spec((1, d)),
                  _const_spec((d, f)), _const_spec((d, f)), _const_spec((f, d))],
        out_specs=pl.BlockSpec((tm, d), lambda i: (i, 0)),
        out_shape=jax.ShapeDtypeStruct((n, d), F32),
        compiler_params=_params("parallel"),
        name="ffn",
    )(x2, w["g"], w["wg"], w["wu"], w["wo"])


def _mem_cross_attention(h, tm, g_ref, wq_ref, gq_ref, mk_ref, mv_ref, wo_ref):
    hb = _rms(h, g_ref[...]).astype(BF16)
    q = _dot(hb, wq_ref[...])
    rows = []
    for i in range(h.shape[0] // tm):
        outs = []
        for hd in range(MEM_HEADS):
            sl = slice(hd * MEM_HEAD_DIM, (hd + 1) * MEM_HEAD_DIM)
            qh = _head_rms(q[i * tm:(i + 1) * tm, sl], gq_ref[...], MEM_HEAD_DIM).astype(BF16)
            s = _dot_nt(qh, mk_ref[i, :, sl])
            e = jnp.exp(s - jnp.max(s, axis=-1, keepdims=True))
            p = e / jnp.sum(e, axis=-1, keepdims=True)
            outs.append(_dot(p.astype(BF16), mv_ref[i, :, sl]).astype(BF16))
        rows.append(jnp.concatenate(outs, axis=-1))
    o = rows[0] if len(rows) == 1 else jnp.concatenate(rows, axis=0)
    return h + _dot(o, wo_ref[...])


def _post_mixer_body(*refs, n_parts):
    h_ref = refs[0]
    parts = refs[1:1 + n_parts]
    w_parts = refs[1 + n_parts:1 + 2 * n_parts]
    (mg_ref, mwq_ref, mgq_ref, mk_ref, mv_ref, mwo_ref,
     fg_ref, fwg_ref, fwu_ref, fwo_ref, o_ref) = refs[1 + 2 * n_parts:]
    bb, tm, d = h_ref.shape
    h = h_ref[...].reshape(bb * tm, d)
    for p_ref, w_ref in zip(parts, w_parts):
        h = h + _dot(p_ref[...].reshape(bb * tm, p_ref.shape[2]), w_ref[...])
    h = _mem_cross_attention(h, tm, mg_ref, mwq_ref, mgq_ref, mk_ref, mv_ref, mwo_ref)
    o_ref[...] = _swiglu_half_step(h, fg_ref, fwg_ref, fwu_ref, fwo_ref).reshape(bb, tm, d)


def _post_mixer(h, parts, w_parts, mk, mv, wm, wf):
    b, t, d = h.shape
    m = mk.shape[1]
    f = wf["wg"].shape[1]
    tm = _row_tile(t, ROW_TILE)
    bb = b if b * t <= ROW_TILE and tm % SUBLANES == 0 else 1
    row = lambda n: pl.BlockSpec((bb, tm, n), lambda i, j: (i, j, 0))
    mem = lambda: pl.BlockSpec((bb, m, MEM_WIDTH), lambda i, j: (i, 0, 0))
    in_specs = [row(d)] + [row(p.shape[2]) for p in parts] + [_const_spec(w.shape) for w in w_parts]
    in_specs += [_const_spec((1, d)), _const_spec((d, MEM_WIDTH)), _const_spec((1, MEM_HEAD_DIM)), mem(), mem(),
                 _const_spec((MEM_WIDTH, d)),
                 _const_spec((1, d)), _const_spec((d, f)), _const_spec((d, f)), _const_spec((f, d))]
    return pl.pallas_call(
        functools.partial(_post_mixer_body, n_parts=len(parts)),
        grid=(b // bb, t // tm),
        in_specs=in_specs,
        out_specs=row(d),
        out_shape=jax.ShapeDtypeStruct((b, t, d), F32),
        compiler_params=_params("parallel", "parallel"),
        name="post_mixer",
    )(h, *parts, *w_parts, wm["g"], wm["wq"], wm["gq"], mk, mv, wm["wo"],
      wf["g"], wf["wg"], wf["wu"], wf["wo"])


def _mem_kv_body(m_ref, g_ref, wk_ref, wv_ref, gk_ref, k32_ref, v32_ref, kb_ref, vb_ref):
    hb = _rms(m_ref[0], g_ref[0]).astype(BF16)
    k = _dot(hb, wk_ref[0])
    v = _dot(hb, wv_ref[0])
    for hd in range(MEM_HEADS):
        sl = slice(hd * MEM_HEAD_DIM, (hd + 1) * MEM_HEAD_DIM)
        kh = _head_rms(k[:, sl], gk_ref[0], MEM_HEAD_DIM)
        k32_ref[0, 0, :, hd, :] = kh
        v32_ref[0, 0, :, hd, :] = v[:, sl]
        kb_ref[0, 0, :, sl] = kh.astype(BF16)
    vb_ref[0, 0] = v.astype(BF16)


def _mem_kv(mem, ws):
    b, m, d = mem.shape
    n = len(ws)
    stack = lambda name: jnp.stack([w[name] for w in ws])
    per_layer = lambda *shape: pl.BlockSpec((1,) + shape, lambda l, i: (l,) + (0,) * len(shape))
    blk = lambda: pl.BlockSpec((1, 1, m, MEM_WIDTH), lambda l, i: (l, i, 0, 0))
    blk5 = lambda: pl.BlockSpec((1, 1, m, MEM_HEADS, MEM_HEAD_DIM), lambda l, i: (l, i, 0, 0, 0))
    shape5 = (n, b, m, MEM_HEADS, MEM_HEAD_DIM)
    return pl.pallas_call(
        _mem_kv_body,
        grid=(n, b),
        in_specs=[pl.BlockSpec((1, m, d), lambda l, i: (i, 0, 0)), per_layer(1, d),
                  per_layer(d, MEM_WIDTH), per_layer(d, MEM_WIDTH), per_layer(1, MEM_HEAD_DIM)],
        out_specs=[blk5(), blk5(), blk(), blk()],
        out_shape=[jax.ShapeDtypeStruct(shape5, F32), jax.ShapeDtypeStruct(shape5, F32),
                   jax.ShapeDtypeStruct((n, b, m, MEM_WIDTH), BF16),
                   jax.ShapeDtypeStruct((n, b, m, MEM_WIDTH), BF16)],
        compiler_params=_params("parallel", "parallel"),
        name="mem_kv",
    )(mem, stack("g_src"), stack("wk"), stack("wv"), stack("gk"))


def _mla_keys_values(lat, krp, wuk_ref, wuvt_ref, gk_ref, k_ref, vt_ref):
    lb = lat.astype(BF16)
    kn = _dot(lb, wuk_ref[...])
    for hd in range(MLA_HEADS):
        sl = slice(hd * LANES, (hd + 1) * LANES)
        k_ref[0, :, sl] = _head_rms(kn[:, sl] + krp, gk_ref[...], MLA_QK).astype(BF16)
    vt_ref[0] = _dot_nt(wuvt_ref[...], lb).astype(BF16)


def _mla_in_body(h_ref, g_ref, wcq_ref, gql_ref, wuq_ref, wuqs_ref, gq_ref, wckv_ref, gkv_ref, wkr_ref, wkrs_ref,
                 c_ref, s_ref, *rest):
    q_ref, lat_ref, krp_ref = rest[-5:-2] if len(rest) > 3 else rest
    hb = _rms(h_ref[0], g_ref[...]).astype(BF16)
    c, s = c_ref[...], s_ref[...]
    cq = _rms(_dot(hb, wcq_ref[...]), gql_ref[...]).astype(BF16)
    q = _dot(cq, wuq_ref[...])
    q_partner = _dot(cq, wuqs_ref[...])
    for hd in range(MLA_HEADS):
        sl = slice(hd * LANES, (hd + 1) * LANES)
        qh = q[:, sl] * c + q_partner[:, sl] * s
        q_ref[0, :, sl] = _head_rms(qh, gq_ref[...], MLA_QK).astype(BF16)
    lat = _rms(_dot(hb, wckv_ref[...]), gkv_ref[...])
    krp = _dot(hb, wkr_ref[...]) * c + _dot(hb, wkrs_ref[...]) * s
    lat_ref[0] = lat
    if len(rest) > 3:
        krp_ref[0] = krp[:, MLA_NOPE:MLA_QK]
        wuk_ref, wuvt_ref, gk_ref = rest[:3]
        _mla_keys_values(lat, krp, wuk_ref, wuvt_ref, gk_ref, *rest[-2:])
    else:
        krp_ref[0] = krp


def _mla_in(h, w, tables, with_kv):
    b, t, d = h.shape
    tm = _row_tile(t, ROW_TILE)
    row = lambda n: pl.BlockSpec((1, tm, n), lambda i, j: (i, j, 0))
    tab = lambda: pl.BlockSpec((tm, LANES), lambda i, j: (j, 0))
    vw = MLA_HEADS * MLA_V
    in_specs = [row(d), _const_spec((1, d)),
                _const_spec((d, MLA_Q_LORA)), _const_spec((1, MLA_Q_LORA)),
                _const_spec((MLA_Q_LORA, MLA_HEADS * LANES)), _const_spec((MLA_Q_LORA, MLA_HEADS * LANES)),
                _const_spec((1, LANES)),
                _const_spec((d, MLA_KV_LORA)), _const_spec((1, MLA_KV_LORA)),
                _const_spec((d, LANES)), _const_spec((d, LANES)),
                tab(), tab()]
    args = [h, w["g_mix"], w["wcq"], w["g_qlat"], w["wuq"], w["wuq_swap"], w["gq"], w["wckv"], w["g_kvlat"],
            w["wkr"], w["wkr_swap"], *tables]
    kr_w = MLA_ROPE if with_kv else LANES
    out_specs = [row(MLA_HEADS * LANES), row(MLA_KV_LORA), row(kr_w)]
    out_shape = [jax.ShapeDtypeStruct((b, t, MLA_HEADS * LANES), BF16),
                 jax.ShapeDtypeStruct((b, t, MLA_KV_LORA), F32),
                 jax.ShapeDtypeStruct((b, t, kr_w), F32)]
    if with_kv:
        in_specs += [_const_spec((MLA_KV_LORA, MLA_HEADS * LANES)), _const_spec((vw, MLA_KV_LORA)),
                     _const_spec((1, LANES))]
        args += [w["wuk"], w["wuvt"], w["gk"]]
        out_specs += [row(MLA_HEADS * LANES), pl.BlockSpec((1, vw, tm), lambda i, j: (i, 0, j))]
        out_shape += [jax.ShapeDtypeStruct((b, t, MLA_HEADS * LANES), BF16),
                      jax.ShapeDtypeStruct((b, vw, t), BF16)]
    return pl.pallas_call(
        _mla_in_body,
        grid=(b, t // tm),
        in_specs=in_specs,
        out_specs=out_specs,
        out_shape=out_shape,
        compiler_params=_params("parallel", "parallel"),
        name="mla_in",
    )(*args)


def _mla_kv_body(lat_ref, krp_ref, wuk_ref, wuvt_ref, gk_ref, k_ref, vt_ref):
    _mla_keys_values(lat_ref[0], krp_ref[0], wuk_ref, wuvt_ref, gk_ref, k_ref, vt_ref)


def _mla_kv(lat, krp, w):
    b, l, _ = lat.shape
    tl = ROW_TILE if l % ROW_TILE == 0 else l
    row = lambda n: pl.BlockSpec((1, tl, n), lambda i, j: (i, j, 0))
    vw = MLA_HEADS * MLA_V
    return pl.pallas_call(
        _mla_kv_body,
        grid=(b, l // tl),
        in_specs=[row(MLA_KV_LORA), row(LANES), _const_spec((MLA_KV_LORA, MLA_HEADS * LANES)),
                  _const_spec((vw, MLA_KV_LORA)), _const_spec((1, LANES))],
        out_specs=[row(MLA_HEADS * LANES), pl.BlockSpec((1, vw, tl), lambda i, j: (i, 0, j))],
        out_shape=[jax.ShapeDtypeStruct((b, l, MLA_HEADS * LANES), BF16),
                   jax.ShapeDtypeStruct((b, vw, l), BF16)],
        compiler_params=_params("parallel", "parallel"),
        name="mla_kv",
    )(lat, krp, w["wuk"], w["wuvt"], w["gk"])


def _lru_body(h_ref, g_ref, wrec_ref, wgate_ref, cw_ref, cb_ref, wr_ref, wi_ref, br_ref, bi_ref, lam_ref,
              cprev_ref, h0_ref, y_ref, hl_ref, cl_ref, buf, a_s, b_s, hcar, *, tm):
    @pl.when(pl.program_id(1) == 0)
    def _():
        buf[0:HALO, :] = cprev_ref[0]
        hcar[...] = h0_ref[0]

    hb = _rms(h_ref[0], g_ref[...]).astype(BF16)
    xr = _dot(hb, wrec_ref[...])
    xg = _dot(hb, wgate_ref[...])
    buf[HALO:HALO + tm, :] = xr
    xc = cb_ref[...] + xr * cw_ref[CONV_WIDTH - 1:CONV_WIDTH, :]
    for j in range(CONV_WIDTH - 1):
        off = HALO - (CONV_WIDTH - 1) + j
        xc = xc + cw_ref[j:j + 1, :] * buf[off:off + tm, :]
    xcb = xc.astype(BF16)
    r = _sigmoid(_dot(xcb, wr_ref[...]) + br_ref[...])
    i = _sigmoid(_dot(xcb, wi_ref[...]) + bi_ref[...])
    log_a = (-LRU_C) * r * _softplus(-lam_ref[...])
    a = jnp.exp(log_a)
    z = -jnp.tanh(log_a) * (a * a + 1.0)
    b = jnp.where(z > 0.0, z * lax.rsqrt(z), 0.0) * (i * xc)
    a_s[...] = a
    b_s[...] = b

    row = lax.broadcasted_iota(jnp.int32, (SUBLANES, a.shape[1]), 0)

    def step(g, hprev):
        r0 = pl.multiple_of(g * SUBLANES, SUBLANES)
        ag = a_s[pl.ds(r0, SUBLANES), :]
        bg = b_s[pl.ds(r0, SUBLANES), :]
        d = 1
        while d < SUBLANES:
            keep = row >= d
            a_up = jnp.where(keep, pltpu.roll(ag, d, 0), 1.0)
            b_up = jnp.where(keep, pltpu.roll(bg, d, 0), 0.0)
            bg = ag * b_up + bg
            ag = ag * a_up
            d *= 2
        hg = ag * hprev + bg
        b_s[pl.ds(r0, SUBLANES), :] = hg
        return hg[SUBLANES - 1:, :]

    hfin = lax.fori_loop(0, tm // SUBLANES, step, hcar[...], unroll=2)
    hcar[...] = hfin
    y_ref[0] = (_gelu_tanh(xg) * b_s[...]).astype(BF16)
    hl_ref[0] = hfin
    tail = buf[tm:tm + HALO, :]
    buf[0:HALO, :] = tail
    cl_ref[0] = tail


def _lru(h, w, conv_prev8, h0):
    b, t, d = h.shape
    tm = _row_tile(t, ROW_TILE)
    wd = LRU_WIDTH
    vec = lambda: _const_spec((1, wd))
    return pl.pallas_call(
        functools.partial(_lru_body, tm=tm),
        grid=(b, t // tm),
        in_specs=[pl.BlockSpec((1, tm, d), lambda i, j: (i, j, 0)), _const_spec((1, d)),
                  _const_spec((d, wd)), _const_spec((d, wd)), _const_spec((CONV_WIDTH, wd)), vec(),
                  _const_spec((wd, wd)), _const_spec((wd, wd)), vec(), vec(), vec(),
                  pl.BlockSpec((1, HALO, wd), lambda i, j: (i, 0, 0)),
                  pl.BlockSpec((1, 1, wd), lambda i, j: (i, 0, 0))],
        out_specs=[pl.BlockSpec((1, tm, wd), lambda i, j: (i, j, 0)),
                   pl.BlockSpec((1, 1, wd), lambda i, j: (i, 0, 0)),
                   pl.BlockSpec((1, HALO, wd), lambda i, j: (i, 0, 0))],
        out_shape=[jax.ShapeDtypeStruct((b, t, wd), BF16), jax.ShapeDtypeStruct((b, 1, wd), F32),
                   jax.ShapeDtypeStruct((b, HALO, wd), F32)],
        scratch_shapes=[pltpu.VMEM((tm + HALO, wd), F32), pltpu.VMEM((tm, wd), F32),
                        pltpu.VMEM((tm, wd), F32), pltpu.VMEM((1, wd), F32)],
        compiler_params=_params("parallel", "arbitrary"),
        name="lru",
    )(h, w["g_mix"], w["wrec"], w["wgate"], w["conv_w"], w["conv_b"], w["wr"], w["wi"], w["br"], w["bi"],
      w["lam"], conv_prev8, h0)


def _group_rms(x, g):
    low = lax.broadcasted_iota(jnp.int32, (1, LANES), 1) < FOX_HEAD_DIM
    out = []
    for j in range(x.shape[1] // LANES):
        xb = x[:, j * LANES:(j + 1) * LANES]
        sq = xb * xb
        s_low = jnp.sum(jnp.where(low, sq, 0.0), axis=-1, keepdims=True)
        s_high = jnp.sum(jnp.where(low, 0.0, sq), axis=-1, keepdims=True)
        ss = jnp.where(low, s_low, s_high)
        out.append(xb * lax.rsqrt(ss * (1.0 / FOX_HEAD_DIM) + NORM_EPS))
    return jnp.concatenate(out, axis=-1) * g


def _cumsum_rows(x, ltri):
    out = None
    for p in _split_bf16(x, 3):
        d = _dot(ltri, p)
        out = d if out is None else out + d
    return out


def _fox_aug(x, keep, bias):
    blocks = [x[:, (hd // 2) * LANES:(hd // 2 + 1) * LANES] for hd in range(FOX_HEADS)]
    return jnp.concatenate(blocks, axis=-1) * keep + bias


def _fox_key_bias(c, place):
    lane = lax.broadcasted_iota(jnp.int32, (1, LANES), 1)
    hi, mid, lo = _split_bf16(jnp.where(lane < FOX_HEADS, c * (-LOG2E), 0.0), 3)
    packed = (hi.astype(F32) + pltpu.roll(mid.astype(F32), FOX_HEADS, 1)
              + pltpu.roll(lo.astype(F32), 2 * FOX_HEADS, 1))
    return _dot(packed.astype(BF16), place)


def _fox_in_body(h_ref, g_ref, wq_ref, wk_ref, wv_ref, wf_ref, bf_ref, gq_ref, gk_ref,
                 keep_ref, ones_ref, place_ref, ltri_ref, c0_ref,
                 q_ref, k32_ref, v32_ref, ka_ref, vb_ref, lf_ref, lf_s, carry, *, tm, tc, values_transposed):
    @pl.when(pl.program_id(1) == 0)
    def _():
        carry[...] = c0_ref[0]

    hb = _rms(h_ref[0], g_ref[...]).astype(BF16)
    keep = keep_ref[...]
    q = _group_rms(_dot(hb, wq_ref[...]), gq_ref[...])
    q_ref[0] = _fox_aug(q, keep, ones_ref[...]).astype(BF16)
    k = _group_rms(_dot(hb, wk_ref[...]), gk_ref[...])
    k32_ref[0] = k
    v = _dot(hb, wv_ref[...])
    v32_ref[0] = v
    vb_ref[0] = (v.T if values_transposed else v).astype(BF16)
    logf =-_softplus(-(_dot(hb, wf_ref[...]) + bf_ref[...]))
    lf_ref[0] = logf[:, :FOX_HEADS]
    if tc > tm:
        lf_s[...] = jnp.zeros_like(lf_s)
    lf_s[0:tm, :] = logf
    c = carry[...] + _cumsum_rows(lf_s[...], ltri_ref[...])[0:tm, :]
    carry[...] = c[tm - 1:tm, :]
    ka_ref[0] = _fox_aug(k, keep, _fox_key_bias(c, place_ref[...])).astype(BF16)


def _fox_in(h, w, c0):
    b, t, d = h.shape
    tm = _row_tile(t, ROW_TILE)
    tc = max(tm, LANES)
    ltri = jnp.tril(jnp.ones((tc, tc), F32)).astype(BF16)
    row = lambda n: pl.BlockSpec((1, tm, n), lambda i, j: (i, j, 0))
    fw, aw = FOX_WIDTH, FOX_HEADS * LANES
    values_transposed = tm % LANES == 0
    v_spec = pl.BlockSpec((1, fw, tm), lambda i, j: (i, 0, j)) if values_transposed else row(fw)
    v_shape = (b, fw, t) if values_transposed else (b, t, fw)
    outs = pl.pallas_call(
        functools.partial(_fox_in_body, tm=tm, tc=tc, values_transposed=values_transposed),
        grid=(b, t // tm),
        in_specs=[row(d), _const_spec((1, d)),
                  _const_spec((d, fw)), _const_spec((d, fw)), _const_spec((d, fw)), _const_spec((d, LANES)),
                  _const_spec((1, LANES)), _const_spec((1, fw)), _const_spec((1, fw)),
                  _const_spec((1, aw)), _const_spec((1, aw)), _const_spec((LANES, aw)),
                  _const_spec((tc, tc)), pl.BlockSpec((1, 1, LANES), lambda i, j: (i, 0, 0))],
        out_specs=[row(aw), row(fw), row(fw), row(aw), v_spec, row(FOX_HEADS)],
        out_shape=[jax.ShapeDtypeStruct((b, t, aw), BF16), jax.ShapeDtypeStruct((b, t, fw), F32),
                   jax.ShapeDtypeStruct((b, t, fw), F32), jax.ShapeDtypeStruct((b, t, aw), BF16),
                   jax.ShapeDtypeStruct(v_shape, BF16), jax.ShapeDtypeStruct((b, t, FOX_HEADS), F32)],
        scratch_shapes=[pltpu.VMEM((tc, LANES), F32), pltpu.VMEM((1, LANES), F32)],
        compiler_params=_params("parallel", "arbitrary"),
        name="fox_in",
    )(h, w["g_mix"], w["wq"], w["wk"], w["wv"], w["wf"], w["bf"], w["gq"], w["gk"],
      w["keep"], w["ones"], w["place"], ltri, c0)
    q, k32, v32, ka, vb, logf = outs
    return q, k32, v32, ka, (vb if values_transposed else jnp.swapaxes(vb, 1, 2)), logf


def _fox_past_body(k_ref, lf_ref, keep_ref, place_ref, ltri_ref, ka_ref, c_ref, carry):
    @pl.when(pl.program_id(1) == 0)
    def _():
        carry[...] = jnp.zeros_like(carry)

    c = carry[...] + _cumsum_rows(lf_ref[0], ltri_ref[...])
    c_ref[0] = c
    carry[...] = c[c.shape[0] - 1:, :]
    ka_ref[0] = _fox_aug(k_ref[0], keep_ref[...], _fox_key_bias(c, place_ref[...])).astype(BF16)


def _fox_past(past_k, past_logf, w):
    b, p, n = past_logf.shape
    tc = _row_tile(p, ROW_TILE)
    ltri = jnp.tril(jnp.ones((tc, tc), F32)).astype(BF16)
    fw, aw = FOX_WIDTH, FOX_HEADS * LANES
    row = lambda m: pl.BlockSpec((1, tc, m), lambda i, j: (i, j, 0))
    return pl.pallas_call(
        _fox_past_body,
        grid=(b, p // tc),
        in_specs=[row(fw), row(n), _const_spec((1, aw)), _const_spec((LANES, aw)), _const_spec((tc, tc))],
        out_specs=[row(aw), row(n)],
        out_shape=[jax.ShapeDtypeStruct((b, p, aw), BF16), jax.ShapeDtypeStruct((b, p, n), F32)],
        scratch_shapes=[pltpu.VMEM((1, n), F32)],
        compiler_params=_params("parallel", "arbitrary"),
        name="fox_past",
    )(past_k, past_logf, w["keep"], w["place"], ltri)


def _flash_body(q_ref, k_ref, vt_ref, o_ref, m_s, l_s, acc_s, sa_s, sb_s, *, tq, tqs, tk, tks, fr, fc, n_k, past,
                kv_len, chunk_causal, diag_aligned):
    q_start = past + pl.program_id(2) * tq
    q = q_ref[0]
    m_s[...] = jnp.full_like(m_s, NEG_INF)
    l_s[...] = jnp.zeros_like(l_s)
    acc_s[...] = jnp.zeros_like(acc_s)
    shift = int(math.log2(CHUNK))
    hv = LANES // 2

    def block_kind(r0, rn, c0, cn):
        if chunk_causal:
            k_lo, k_hi, q_lo, q_hi = r0 >> shift, (r0 + rn - 1) >> shift, c0 >> shift, (c0 + cn - 1) >> shift
        else:
            k_lo, k_hi, q_lo, q_hi = r0, r0 + rn - 1, c0, c0 + cn - 1
        return "visible" if k_hi <= q_lo else ("hidden" if k_lo > q_hi else "partial")

    def diag_streams(d):
        out = []
        for r0 in range(0, tk, tks):
            for hh in range(2):
                for c0 in range(0, tq, tqs):
                    halves = [(c, block_kind(d * tk + r0, tks, c, tks)) for c in range(c0, c0 + tqs, tks)]
                    if all(kind == "visible" for _, kind in halves):
                        out.append((hh, r0, tks, c0, tqs, "visible"))
                    else:
                        out += [(hh, r0, tks, c, tks, kind) for c, kind in halves if kind != "hidden"]
        return out

    full_streams = [(hh, r0, fr, c0, fc, "visible") for r0 in range(0, tk, fr) for hh in range(2)
                    for c0 in range(0, tq, fc)]
    mask_streams = [st[:5] + ("partial",) for st in full_streams]

    def score(kblk, stream):
        hh, r0, rn, c0, cn, _ = stream
        head = slice(hh * LANES, (hh + 1) * LANES)
        return _dot_nt(kblk[r0:r0 + rn, head], q[c0:c0 + cn, head])

    def key_block(kt):
        return k_ref[0, pl.ds(pl.multiple_of(jnp.minimum(kt, n_k - 1) * tk, tk), tk), :]

    def absorb(s, stream, kt):
        hh, r0, rn, c0, cn, kind = stream
        cols = slice(c0, c0 + cn)
        k0 = pl.multiple_of(kt * tk, tk)
        if kind == "partial":
            kpos = k0 + r0 + lax.broadcasted_iota(jnp.int32, (rn, cn), 0)
            qpos = q_start + c0 + lax.broadcasted_iota(jnp.int32, (rn, cn), 1)
            if chunk_causal:
                vis = lax.shift_right_logical(kpos, shift) <= lax.shift_right_logical(qpos, shift)
            else:
                vis = kpos <= qpos
            s = jnp.where(jnp.logical_and(vis, kpos < kv_len), s, NEG_INF)
        m_old = m_s[hh, :, cols]
        m_new = jnp.maximum(m_old, jnp.max(s, axis=0, keepdims=True))
        alpha = jnp.exp2(m_old - m_new)
        p = jnp.exp2(s - m_new)
        l_s[hh, :, cols] = alpha * l_s[hh, :, cols] + jnp.sum(p, axis=0, keepdims=True)
        vt = vt_ref[0, hh * hv:(hh + 1) * hv, pl.ds(pl.multiple_of(k0 + r0, LANES), rn)]
        acc_s[hh, :, cols] = alpha * acc_s[hh, :, cols] + _dot(vt, p.astype(BF16))
        m_s[hh, :, cols] = m_new

    def single_tile(kt, streams, after_first_scores=None):
        kblk = key_block(kt)
        ahead = 3
        pending = [score(kblk, st) for st in streams[:ahead]]
        if after_first_scores is not None:
            after_first_scores()
        for idx, st in enumerate(streams):
            s = pending.pop(0)
            if idx + ahead < len(streams):
                pending.append(score(kblk, streams[idx + ahead]))
            absorb(s, st, kt)

    def store_scores(buf, kt):
        kblk = key_block(kt)
        for st in full_streams:
            hh, r0, rn, c0, cn, _ = st
            buf[hh, r0:r0 + rn, c0:c0 + cn] = score(kblk, st)

    def absorb_stored(buf, kt):
        for st in full_streams:
            hh, r0, rn, c0, cn, _ = st
            absorb(buf[hh, r0:r0 + rn, c0:c0 + cn], st, kt)

    def store_and_absorb(store_buf, store_kt, absorb_buf, absorb_kt):
        kblk = key_block(store_kt)
        for st in full_streams:
            hh, r0, rn, c0, cn, _ = st
            store_buf[hh, r0:r0 + rn, c0:c0 + cn] = score(kblk, st)
            absorb(absorb_buf[hh, r0:r0 + rn, c0:c0 + cn], st, absorb_kt)

    def tile_pair(i, carry):
        kt = 2 * i
        store_and_absorb(sb_s, kt + 1, sa_s, kt)
        store_and_absorb(sa_s, kt + 2, sb_s, kt + 1)
        return carry

    def full_tile(kt, carry):
        single_tile(kt, full_streams)
        return carry

    def masked_tile(kt, carry):
        single_tile(kt, mask_streams)
        return carry

    n_full = jnp.minimum(q_start // tk, kv_len // tk)
    q_last = q_start + tq - 1
    k_hi = (q_last // CHUNK + 1) * CHUNK if chunk_causal else q_last + 1
    n_end = jnp.minimum((k_hi + tk - 1) // tk, n_k)
    n_pair = n_full // 2
    if diag_aligned:
        for d in range(tq // tk):
            single_tile(n_full + d, diag_streams(d),
                        after_first_scores=(lambda: store_scores(sa_s, 0)) if d == 0 else None)
        lax.fori_loop(0, n_pair, tile_pair, 0)
        lax.fori_loop(2 * n_pair, n_full, full_tile, 0)
    else:
        @pl.when(n_pair > 0)
        def _():
            store_scores(sa_s, 0)

        lax.fori_loop(0, n_pair, tile_pair, 0)
        lax.fori_loop(2 * n_pair, n_full, full_tile, 0)
        lax.fori_loop(n_full, n_end, masked_tile, 0)
    out_t = jnp.concatenate([acc_s[0] / l_s[0], acc_s[1] / l_s[1]], axis=0)
    o_ref[0] = out_t.T.astype(BF16)


def _flash(q, k, vt, *, past, kv_len, chunk_causal):
    b, t, w = q.shape
    lp = k.shape[1]
    n_pairs = w // (2 * LANES)
    t_pad = max(t, LANES)
    if t_pad > t:
        q = _pad_rows(q, t_pad)
    if past == 0 and t % KEY_TILE == 0 and lp % KEY_TILE == 0:
        tk = KEY_TILE
        tq = QUERY_TILE if t % QUERY_TILE == 0 else KEY_TILE
    elif t_pad == LANES:
        tq, tk = LANES, lp
    else:
        tq = tk = LANES
    assert t_pad % tq == 0 and lp % tk == 0, (t, lp, tq, tk)
    tks = MXU_TILE if tk % MXU_TILE == 0 and tq > LANES else tk
    tqs = min(tq, 2 * MXU_TILE)
    fr, fc = (tk, MXU_TILE) if tks < tk and tq % MXU_TILE == 0 else (tks, tqs)
    out = pl.pallas_call(
        functools.partial(_flash_body, tq=tq, tqs=tqs, tk=tk, tks=tks, fr=fr, fc=fc, n_k=lp // tk, past=past,
                          kv_len=kv_len, chunk_causal=chunk_causal,
                          diag_aligned=(past == 0 and tq % tk == 0 and kv_len == lp and tks < tk)),
        grid=(b, n_pairs, t_pad // tq),
        in_specs=[pl.BlockSpec((1, tq, 2 * LANES), lambda i, j, s: (i, s, j)),
                  pl.BlockSpec((1, lp, 2 * LANES), lambda i, j, s: (i, 0, j)),
                  pl.BlockSpec((1, LANES, lp), lambda i, j, s: (i, j, 0))],
        out_specs=pl.BlockSpec((1, tq, LANES), lambda i, j, s: (i, s, j)),
        out_shape=jax.ShapeDtypeStruct((b, t_pad, n_pairs * LANES), BF16),
        scratch_shapes=[pltpu.VMEM((2, 1, tq), F32), pltpu.VMEM((2, 1, tq), F32),
                        pltpu.VMEM((2, LANES // 2, tq), F32),
                        pltpu.VMEM((2, tk, tq), F32), pltpu.VMEM((2, tk, tq), F32)],
        compiler_params=_params("parallel", "parallel", "arbitrary"),
        name="flash_mla" if chunk_causal else "flash_fox",
    )(q, k, vt)
    return out[:, :t]


def _row(v):
    return v.reshape(1, -1).astype(F32)


def _pad_lanes(x, lo, total):
    pad = [(0, 0)] * (x.ndim - 1) + [(lo, total - lo - x.shape[-1])]
    return jnp.pad(x, pad)


def _ffn_weights(g, w_in, w_out):
    f = w_out.shape[0]
    return dict(g=_row(g), wg=w_in[:, :f].astype(BF16), wu=w_in[:, f:].astype(BF16), wo=w_out.astype(BF16))


def _mem_weights(g, g_src, w_q, w_kv, w_o, g_q, g_k):
    d = w_q.shape[0]
    kv = w_kv.reshape(d, MEM_HEADS, 2, MEM_HEAD_DIM)
    return dict(g=_row(g), g_src=_row(g_src), wq=w_q.astype(BF16), wo=w_o.astype(BF16),
                wk=kv[:, :, 0].reshape(d, MEM_WIDTH).astype(BF16),
                wv=kv[:, :, 1].reshape(d, MEM_WIDTH).astype(BF16),
                gq=_row(g_q) * (MEM_HEAD_DIM ** -0.5), gk=_row(g_k))


def _even_weights(g_mix, w_in, g_qlat, g_kvlat, w_uq, w_ukv, g_q, g_k, conv_w, conv_b, gate_w, gate_b, lam, w_out):
    d = w_in.shape[0]
    o1 = MLA_Q_LORA
    o2 = o1 + MLA_KV_LORA
    o3 = o2 + MLA_ROPE
    o4 = o3 + LRU_WIDTH
    half = MLA_ROPE // 2
    swap_halves = lambda r: jnp.concatenate([r[..., half:], r[..., :half]], axis=-1)
    uq3 = w_uq.reshape(MLA_Q_LORA, MLA_HEADS, MLA_QK)
    uq = _pad_lanes(uq3, 0, LANES)
    uq_swap = _pad_lanes(swap_halves(uq3[:, :, MLA_NOPE:]), MLA_NOPE, LANES)
    kr = w_in[:, o2:o3]
    ukv = w_ukv.reshape(MLA_KV_LORA, MLA_HEADS, MLA_NOPE + MLA_V)
    uk = _pad_lanes(ukv[:, :, :MLA_NOPE], 0, LANES)
    blk = LRU_WIDTH // LRU_BLOCKS
    eye = jnp.eye(LRU_BLOCKS, dtype=F32)
    wr = jnp.einsum("ncd,nm->ncmd", gate_w[:, :, :blk], eye).reshape(LRU_WIDTH, LRU_WIDTH)
    wi = jnp.einsum("ncd,nm->ncmd", gate_w[:, :, blk:], eye).reshape(LRU_WIDTH, LRU_WIDTH)
    return dict(
        g_mix=_row(g_mix), wcq=w_in[:, :o1].astype(BF16), g_qlat=_row(g_qlat),
        wuq=uq.reshape(MLA_Q_LORA, MLA_HEADS * LANES).astype(BF16),
        wuq_swap=uq_swap.reshape(MLA_Q_LORA, MLA_HEADS * LANES).astype(BF16),
        gq=_pad_lanes(_row(g_q), 0, LANES) * (MLA_QK ** -0.5 * LOG2E),
        wckv=w_in[:, o1:o2].astype(BF16), g_kvlat=_row(g_kvlat),
        wkr=_pad_lanes(kr, MLA_NOPE, LANES).astype(BF16),
        wkr_swap=_pad_lanes(swap_halves(kr), MLA_NOPE, LANES).astype(BF16),
        wuk=uk.reshape(MLA_KV_LORA, MLA_HEADS * LANES).astype(BF16),
        wuvt=ukv[:, :, MLA_NOPE:].reshape(MLA_KV_LORA, MLA_HEADS * MLA_V).T.astype(BF16),
        gk=_pad_lanes(_row(g_k), 0, LANES),
        wrec=w_in[:, o3:o4].astype(BF16), wgate=w_in[:, o4:].astype(BF16),
        conv_w=conv_w.astype(F32), conv_b=_row(conv_b), wr=wr.astype(BF16), wi=wi.astype(BF16),
        br=_row(gate_b[:, :blk]), bi=_row(gate_b[:, blk:]), lam=_row(lam),
        wo_attn=w_out[:MLA_HEADS * MLA_V].astype(BF16), wo_rec=w_out[MLA_HEADS * MLA_V:].astype(BF16))


def _odd_weights(g_mix, w_in, b_f, g_q, g_k, w_out):
    fw = FOX_WIDTH
    lane = jnp.arange(FOX_HEADS * LANES)
    hd, within = lane // LANES, lane % LANES
    own_low = hd % 2 == 0
    keep = jnp.where(own_low, within < FOX_HEAD_DIM, within >= FOX_HEAD_DIM)
    part = within - jnp.where(own_low, FOX_HEAD_DIM, 0)
    is_bias = (part >= 0) & (part < 3)
    src = part * FOX_HEADS + hd
    place = ((jnp.arange(LANES)[:, None] == src[None, :]) & is_bias[None, :]).astype(BF16)
    return dict(
        keep=keep.astype(F32)[None, :], ones=is_bias.astype(F32)[None, :], place=place,
        g_mix=_row(g_mix), wq=w_in[:, :fw].astype(BF16), wk=w_in[:, fw:2 * fw].astype(BF16),
        wv=w_in[:, 2 * fw:3 * fw].astype(BF16), wf=_pad_lanes(w_in[:, 3 * fw:], 0, LANES).astype(BF16),
        bf=_pad_lanes(_row(b_f), 0, LANES),
        gq=jnp.tile(_row(g_q), (1, FOX_HEADS)) * (FOX_HEAD_DIM ** -0.5 * LOG2E),
        gk=jnp.tile(_row(g_k), (1, FOX_HEADS)), wo=w_out.astype(BF16))


def _rope_tables(pos):
    half = MLA_ROPE // 2
    inv_freq = ROPE_THETA ** (-jnp.arange(half, dtype=F32) / half)
    ang = pos.astype(F32)[:, None] * inv_freq[None, :]
    cos, sin = jnp.cos(ang), jnp.sin(ang)
    c = jnp.concatenate([jnp.ones((pos.shape[0], MLA_NOPE), F32), cos, cos,
                         jnp.ones((pos.shape[0], LANES - MLA_QK), F32)], axis=-1)
    s = _pad_lanes(jnp.concatenate([-sin, sin], axis=-1), MLA_NOPE, LANES)
    return c, s


def _pad_rows(x, total):
    return jnp.pad(x, [(0, 0), (0, total - x.shape[1])] + [(0, 0)] * (x.ndim - 2))


def _kv_pad_len(t, past):
    l = past + t
    if past == 0 and t % KEY_TILE == 0:
        return l
    return -(-l // LANES) * LANES


def _even_layer(h, past, w, state):
    b, t, d = h.shape
    past_lat, past_krope, h0, conv_prev = state
    lp = _kv_pad_len(t, past)
    tables = _rope_tables(past + jnp.arange(t))
    if past == 0 and lp == t and t % LANES == 0:
        q, lat_new, krope_new, k, vt = _mla_in(h, w, tables, with_kv=True)
    else:
        q, lat_new, krp_new = _mla_in(h, w, tables, with_kv=False)
        krope_new = krp_new[:, :, MLA_NOPE:MLA_QK]
        lat_all = _pad_rows(jnp.concatenate([past_lat, lat_new], axis=1), lp)
        krp_all = _pad_rows(jnp.concatenate([_pad_lanes(past_krope, MLA_NOPE, LANES), krp_new], axis=1), lp)
        k, vt = _mla_kv(lat_all, krp_all, w)
    attn = _flash(q, k, vt, past=past, kv_len=past + t, chunk_causal=True)
    conv_prev8 = jnp.pad(conv_prev, ((0, 0), (HALO - (CONV_WIDTH - 1), 0), (0, 0)))
    y_rec, h_last, conv_last = _lru(h, w, conv_prev8, h0[:, None, :])
    new = (lat_new, krope_new, h_last[:, 0], conv_last[:, HALO - (CONV_WIDTH - 1):])
    return [attn, y_rec], [w["wo_attn"], w["wo_rec"]], new


def _odd_layer(h, past, w, state):
    b, t, d = h.shape
    past_k, past_v, past_logf = state
    if past > 0:
        ka_past, c_past = _fox_past(past_k.reshape(b, past, FOX_WIDTH),
                                    _pad_lanes(past_logf.astype(F32), 0, LANES), w)
        c0 = c_past[:, past - 1:past, :]
    else:
        ka_past = jnp.zeros((b, 0, FOX_HEADS * LANES), BF16)
        c0 = jnp.zeros((b, 1, LANES), F32)
    q, k32, v32, ka_new, vt_new, logf = _fox_in(h, w, c0)
    lp = _kv_pad_len(t, past)
    k_all = _pad_rows(jnp.concatenate([ka_past, ka_new], axis=1), lp)
    vt_past = jnp.swapaxes(past_v.reshape(b, past, FOX_WIDTH).astype(BF16), 1, 2)
    vt_all = jnp.pad(jnp.concatenate([vt_past, vt_new], axis=2), ((0, 0), (0, 0), (0, lp - past - t)))
    attn = _flash(q, k_all, vt_all, past=past, kv_len=past + t, chunk_causal=False)
    new = (k32.reshape(b, t, FOX_HEADS, FOX_HEAD_DIM), v32.reshape(b, t, FOX_HEADS, FOX_HEAD_DIM),
           logf)
    return [attn], [w["wo"]], new


def _trunk(x, past, layers, mem_kvs, even_states, odd_states):
    b, t, d = x.shape
    even_new, odd_new = [], []
    for li, lw in enumerate(layers):
        h = _ffn(x.reshape(b * t, d), lw["ffn1"]).reshape(b, t, d)
        if li % 2 == 0:
            parts, w_parts, new = _even_layer(h, past, lw["mix"], even_states[li // 2])
            even_new.append(new)
        else:
            parts, w_parts, new = _odd_layer(h, past, lw["mix"], odd_states[li // 2])
            odd_new.append(new)
        x = _post_mixer(h, parts, w_parts, mem_kvs[li][0], mem_kvs[li][1], lw["mem"], lw["ffn2"])
    return x, even_new, odd_new


def kernel(x_prompt, x_sample, mem_prompt, cache_mla_latent, cache_mla_krope, state_lru_h, state_lru_conv, cache_fox_k, cache_fox_v, cache_fox_logf, cache_mem_k, cache_mem_v, norm_ffn1, ffn1_w_in, ffn1_w_out, norm_mix, norm_mem, norm_mem_src, mem_w_q, mem_w_kv, mem_w_o, mem_g_q, mem_g_k, norm_ffn2, ffn2_w_in, ffn2_w_out, ev_w_in, ev_g_qlat, ev_g_kvlat, ev_w_uq, ev_w_ukv, ev_g_q, ev_g_k, ev_conv_w, ev_conv_b, ev_gate_w, ev_gate_b, ev_lambda, ev_w_out, od_w_in, od_b_f, od_g_q, od_g_k, od_w_out):
    depth = norm_ffn1.shape[0]
    n_even, n_odd = (depth + 1) // 2, depth // 2
    b, _, _ = x_prompt.shape
    bs = x_sample.shape[0]
    past = cache_mla_latent.shape[2] if n_even else cache_fox_k.shape[2]

    layers = []
    for li in range(depth):
        j = li // 2
        if li % 2 == 0:
            mix = _even_weights(norm_mix[li], ev_w_in[j], ev_g_qlat[j], ev_g_kvlat[j], ev_w_uq[j], ev_w_ukv[j],
                                ev_g_q[j], ev_g_k[j], ev_conv_w[j], ev_conv_b[j], ev_gate_w[j], ev_gate_b[j],
                                ev_lambda[j], ev_w_out[j])
        else:
            mix = _odd_weights(norm_mix[li], od_w_in[j], od_b_f[j], od_g_q[j], od_g_k[j], od_w_out[j])
        layers.append(dict(
            ffn1=_ffn_weights(norm_ffn1[li], ffn1_w_in[li], ffn1_w_out[li]),
            ffn2=_ffn_weights(norm_ffn2[li], ffn2_w_in[li], ffn2_w_out[li]),
            mem=_mem_weights(norm_mem[li], norm_mem_src[li], mem_w_q[li], mem_w_kv[li], mem_w_o[li],
                             mem_g_q[li], mem_g_k[li]),
            mix=mix))

    p_mem_k, p_mem_v, mem_kb, mem_vb = _mem_kv(mem_prompt, [lw["mem"] for lw in layers])
    ev0 = [(jnp.zeros((b, 0, MLA_KV_LORA), F32), jnp.zeros((b, 0, MLA_ROPE), F32),
            jnp.zeros((b, LRU_WIDTH), F32), jnp.zeros((b, CONV_WIDTH - 1, LRU_WIDTH), F32))
           for _ in range(n_even)]
    od0 = [(jnp.zeros((b, 0, FOX_HEADS, FOX_HEAD_DIM), F32), jnp.zeros((b, 0, FOX_HEADS, FOX_HEAD_DIM), F32),
            jnp.zeros((b, 0, FOX_HEADS), F32)) for _ in range(n_odd)]
    y_prompt, ev_p, od_p = _trunk(x_prompt, 0, layers, [(mem_kb[li], mem_vb[li]) for li in range(depth)],
                                  ev0, od0)

    m_tok = cache_mem_k.shape[2]
    mem_s = [(cache_mem_k[li].reshape(bs, m_tok, MEM_WIDTH).astype(BF16),
              cache_mem_v[li].reshape(bs, m_tok, MEM_WIDTH).astype(BF16)) for li in range(depth)]
    ev_s = [(cache_mla_latent[j], cache_mla_krope[j], state_lru_h[j], state_lru_conv[j]) for j in range(n_even)]
    od_s = [(cache_fox_k[j], cache_fox_v[j], cache_fox_logf[j]) for j in range(n_odd)]
    y_sample, ev_n, od_n = _trunk(x_sample, past, layers, mem_s, ev_s, od_s)

    p_even = [jnp.stack([s[f] for s in ev_p]) for f in range(4)]
    p_odd = [jnp.stack([s[f] for s in od_p]) for f in range(3)]
    s_even = [jnp.stack([s[f] for s in ev_n]) for f in range(4)]
    s_odd = [jnp.stack([s[f] for s in od_n]) for f in range(3)]
    return (y_prompt, y_sample, *p_even, *p_odd, p_mem_k, p_mem_v, *s_even, *s_odd)
```

```python
import functools
import math

import jax
import jax.numpy as jnp
from jax import lax
from jax.experimental import pallas as pl
from jax.experimental.pallas import tpu as pltpu

F32 = jnp.float32
BF16 = jnp.bfloat16

NORM_EPS = 1e-6
NEG_INF = -1e30
LOG2E = math.log2(math.e)
CHUNK = 64
LANES = 128
SUBLANES = 8
ROPE_THETA = 10000.0
LRU_C = 8.0
MLA_HEADS = 8
MLA_NOPE = 64
MLA_ROPE = 32
MLA_QK = MLA_NOPE + MLA_ROPE
MLA_V = 64
MLA_Q_LORA = 256
MLA_KV_LORA = 128
LRU_WIDTH = 512
LRU_BLOCKS = 8
CONV_WIDTH = 4
FOX_HEADS = 16
FOX_HEAD_DIM = 64
FOX_WIDTH = FOX_HEADS * FOX_HEAD_DIM
MEM_HEADS = 4
MEM_HEAD_DIM = 128
MEM_WIDTH = MEM_HEADS * MEM_HEAD_DIM
HALO = 8

MXU_TILE = 256
ROW_TILE = 512
KEY_TILE = 512
QUERY_TILE = 2048

VMEM_LIMIT = 56 * 1024 * 1024


def _dot(a, b):
    return jnp.dot(a, b, preferred_element_type=F32)


def _dot_nt(a, b):
    return lax.dot_general(a, b, (((1,), (1,)), ((), ())), preferred_element_type=F32)


def _rms(x, g):
    return x * lax.rsqrt(jnp.mean(x * x, axis=-1, keepdims=True) + NORM_EPS) * g


def _head_rms(x, g, n_live):
    ss = jnp.sum(x * x, axis=-1, keepdims=True) * (1.0 / n_live)
    return x * lax.rsqrt(ss + NORM_EPS) * g


def _sigmoid(x):
    return 1.0 / (1.0 + jnp.exp(-x))


def _log1p(y):
    u = 1.0 + y
    d = u - 1.0
    return jnp.where(d == 0.0, y, jnp.log(u) * (y / jnp.where(d == 0.0, 1.0, d)))


def _softplus(x):
    return jnp.maximum(x, 0.0) + _log1p(jnp.exp(-jnp.abs(x)))


def _gelu_tanh(x):
    return 0.5 * x * (1.0 + jnp.tanh(math.sqrt(2.0 / math.pi) * (x + 0.044715 * (x * x * x))))


def _split_bf16(x, parts):
    out = []
    r = x
    for _ in range(parts):
        p = r.astype(BF16)
        out.append(p)
        r = r - p.astype(F32)
    return out


def _const_spec(shape):
    nd = len(shape)
    return pl.BlockSpec(shape, lambda *_: (0,) * nd, pipeline_mode=pl.Buffered(1))


def _params(*sem):
    return pltpu.CompilerParams(dimension_semantics=sem, vmem_limit_bytes=VMEM_LIMIT)


def _row_tile(n, cap):
    t = min(n, cap)
    assert n % t == 0, (n, t)
    return t


FFN_CHUNKS = 2


def _ffn_chunk_bounds(f):
    tiles = -(-f // MXU_TILE)
    per = -(-tiles // FFN_CHUNKS) * MXU_TILE
    edges = [min(i * per, f) for i in range(FFN_CHUNKS + 1)]
    return [(lo, hi) for lo, hi in zip(edges[:-1], edges[1:]) if hi > lo]


def _swiglu_half_step(x, g_ref, wg_ref, wu_ref, wo_ref):
    hb = _rms(x, g_ref[...]).astype(BF16)
    acc = jnp.zeros_like(x)
    for lo, hi in _ffn_chunk_bounds(wg_ref.shape[1]):
        sl = slice(lo, hi)
        gate = _dot(hb, wg_ref[:, sl])
        up = _dot(hb, wu_ref[:, sl])
        act = (gate * _sigmoid(gate) * up).astype(BF16)
        acc = acc + _dot(act, wo_ref[sl, :])
    return x + 0.5 * acc


def _ffn_body(x_ref, g_ref, wg_ref, wu_ref, wo_ref, o_ref):
    o_ref[...] = _swiglu_half_step(x_ref[...], g_ref, wg_ref, wu_ref, wo_ref)


def _ffn(x2, w):
    n, d = x2.shape
    f = w["wg"].shape[1]
    tm = _row_tile(n, ROW_TILE)
    return pl.pallas_call(
        _ffn_body,
        grid=(n // tm,),
        in_specs=[pl.BlockSpec((tm, d), lambda i: (i, 0)), _const_spec((1, d)),
                  _const_spec((d, f)), _const_spec((d, f)), _const_spec((f, d))],
        out_specs=pl.BlockSpec((tm, d), lambda i: (i, 0)),
        out_shape=jax.ShapeDtypeStruct((n, d), F32),
        compiler_params=_params("parallel"),
        name="ffn",
    )(x2, w["g"], w["wg"], w["wu"], w["wo"])


def _mem_cross_attention(h, tm, g_ref, wq_ref, gq_ref, mk_ref, mv_ref, wo_ref):
    hb = _rms(h, g_ref[...]).astype(BF16)
    q = _dot(hb, wq_ref[...])
    rows = []
    for i in range(h.shape[0] // tm):
        outs = []
        for hd in range(MEM_HEADS):
            sl = slice(hd * MEM_HEAD_DIM, (hd + 1) * MEM_HEAD_DIM)
            qh = _head_rms(q[i * tm:(i + 1) * tm, sl], gq_ref[...], MEM_HEAD_DIM).astype(BF16)
            s = _dot_nt(qh, mk_ref[i, :, sl])
            e = jnp.exp(s - jnp.max(s, axis=-1, keepdims=True))
            o = _dot(e.astype(BF16), mv_ref[i, :, sl]) / jnp.sum(e, axis=-1, keepdims=True)
            outs.append(o.astype(BF16))
        rows.append(jnp.concatenate(outs, axis=-1))
    o = rows[0] if len(rows) == 1 else jnp.concatenate(rows, axis=0)
    return h + _dot(o, wo_ref[...])


def _post_mixer_body(*refs, n_parts):
    h_ref = refs[0]
    parts = refs[1:1 + n_parts]
    w_parts = refs[1 + n_parts:1 + 2 * n_parts]
    (mg_ref, mwq_ref, mgq_ref, mk_ref, mv_ref, mwo_ref,
     fg_ref, fwg_ref, fwu_ref, fwo_ref, o_ref) = refs[1 + 2 * n_parts:]
    bb, tm, d = h_ref.shape
    h = h_ref[...].reshape(bb * tm, d)
    for p_ref, w_ref in zip(parts, w_parts):
        h = h + _dot(p_ref[...].reshape(bb * tm, p_ref.shape[2]), w_ref[...])
    h = _mem_cross_attention(h, tm, mg_ref, mwq_ref, mgq_ref, mk_ref, mv_ref, mwo_ref)
    o_ref[...] = _swiglu_half_step(h, fg_ref, fwg_ref, fwu_ref, fwo_ref).reshape(bb, tm, d)


def _post_mixer(h, parts, w_parts, mk, mv, wm, wf):
    b, t, d = h.shape
    m = mk.shape[1]
    f = wf["wg"].shape[1]
    tm = _row_tile(t, ROW_TILE)
    bb = b if b * t <= ROW_TILE and tm % SUBLANES == 0 else 1
    row = lambda n: pl.BlockSpec((bb, tm, n), lambda i, j: (i, j, 0))
    mem = lambda: pl.BlockSpec((bb, m, MEM_WIDTH), lambda i, j: (i, 0, 0))
    in_specs = [row(d)] + [row(p.shape[2]) for p in parts] + [_const_spec(w.shape) for w in w_parts]
    in_specs += [_const_spec((1, d)), _const_spec((d, MEM_WIDTH)), _const_spec((1, MEM_HEAD_DIM)), mem(), mem(),
                 _const_spec((MEM_WIDTH, d)),
                 _const_spec((1, d)), _const_spec((d, f)), _const_spec((d, f)), _const_spec((f, d))]
    return pl.pallas_call(
        functools.partial(_post_mixer_body, n_parts=len(parts)),
        grid=(b // bb, t // tm),
        in_specs=in_specs,
        out_specs=row(d),
        out_shape=jax.ShapeDtypeStruct((b, t, d), F32),
        compiler_params=_params("parallel", "parallel"),
        name="post_mixer",
    )(h, *parts, *w_parts, wm["g"], wm["wq"], wm["gq"], mk, mv, wm["wo"],
      wf["g"], wf["wg"], wf["wu"], wf["wo"])


def _mem_kv_body(m_ref, g_ref, wk_ref, wv_ref, gk_ref, k32_ref, v32_ref, kb_ref, vb_ref):
    hb = _rms(m_ref[0], g_ref[0]).astype(BF16)
    k = _dot(hb, wk_ref[0])
    v = _dot(hb, wv_ref[0])
    for hd in range(MEM_HEADS):
        sl = slice(hd * MEM_HEAD_DIM, (hd + 1) * MEM_HEAD_DIM)
        kh = _head_rms(k[:, sl], gk_ref[0], MEM_HEAD_DIM)
        k32_ref[0, 0, :, hd, :] = kh
        v32_ref[0, 0, :, hd, :] = v[:, sl]
        kb_ref[0, 0, :, sl] = kh.astype(BF16)
    vb_ref[0, 0] = v.astype(BF16)


def _mem_kv(mem, ws):
    b, m, d = mem.shape
    n = len(ws)
    stack = lambda name: jnp.stack([w[name] for w in ws])
    per_layer = lambda *shape: pl.BlockSpec((1,) + shape, lambda l, i: (l,) + (0,) * len(shape))
    blk = lambda: pl.BlockSpec((1, 1, m, MEM_WIDTH), lambda l, i: (l, i, 0, 0))
    blk5 = lambda: pl.BlockSpec((1, 1, m, MEM_HEADS, MEM_HEAD_DIM), lambda l, i: (l, i, 0, 0, 0))
    shape5 = (n, b, m, MEM_HEADS, MEM_HEAD_DIM)
    return pl.pallas_call(
        _mem_kv_body,
        grid=(n, b),
        in_specs=[pl.BlockSpec((1, m, d), lambda l, i: (i, 0, 0)), per_layer(1, d),
                  per_layer(d, MEM_WIDTH), per_layer(d, MEM_WIDTH), per_layer(1, MEM_HEAD_DIM)],
        out_specs=[blk5(), blk5(), blk(), blk()],
        out_shape=[jax.ShapeDtypeStruct(shape5, F32), jax.ShapeDtypeStruct(shape5, F32),
                   jax.ShapeDtypeStruct((n, b, m, MEM_WIDTH), BF16),
                   jax.ShapeDtypeStruct((n, b, m, MEM_WIDTH), BF16)],
        compiler_params=_params("parallel", "parallel"),
        name="mem_kv",
    )(mem, stack("g_src"), stack("wk"), stack("wv"), stack("gk"))


def _mla_keys_values(lat, krp, wuk_ref, wuvt_ref, gk_ref, k_ref, vt_ref):
    lb = lat.astype(BF16)
    kn = _dot(lb, wuk_ref[...])
    for hd in range(MLA_HEADS):
        sl = slice(hd * LANES, (hd + 1) * LANES)
        k_ref[0, :, sl] = _head_rms(kn[:, sl] + krp, gk_ref[...], MLA_QK).astype(BF16)
    vt_ref[0] = _dot_nt(wuvt_ref[...], lb).astype(BF16)


def _mla_in_body(h_ref, g_ref, wcq_ref, gql_ref, wuq_ref, wuqs_ref, gq_ref, wckv_ref, gkv_ref, wkr_ref, wkrs_ref,
                 c_ref, s_ref, *rest):
    q_ref, lat_ref, krp_ref = rest[-5:-2] if len(rest) > 3 else rest
    hb = _rms(h_ref[0], g_ref[...]).astype(BF16)
    c, s = c_ref[...], s_ref[...]
    cq = _rms(_dot(hb, wcq_ref[...]), gql_ref[...]).astype(BF16)
    q = _dot(cq, wuq_ref[...])
    q_partner = _dot(cq, wuqs_ref[...])
    for hd in range(MLA_HEADS):
        sl = slice(hd * LANES, (hd + 1) * LANES)
        qh = q[:, sl] * c + q_partner[:, sl] * s
        q_ref[0, :, sl] = _head_rms(qh, gq_ref[...], MLA_QK).astype(BF16)
    lat = _rms(_dot(hb, wckv_ref[...]), gkv_ref[...])
    krp = _dot(hb, wkr_ref[...]) * c + _dot(hb, wkrs_ref[...]) * s
    lat_ref[0] = lat
    if len(rest) > 3:
        krp_ref[0] = krp[:, MLA_NOPE:MLA_QK]
        wuk_ref, wuvt_ref, gk_ref = rest[:3]
        _mla_keys_values(lat, krp, wuk_ref, wuvt_ref, gk_ref, *rest[-2:])
    else:
        krp_ref[0] = krp


def _mla_in(h, w, tables, with_kv):
    b, t, d = h.shape
    tm = _row_tile(t, ROW_TILE)
    row = lambda n: pl.BlockSpec((1, tm, n), lambda i, j: (i, j, 0))
    tab = lambda: pl.BlockSpec((tm, LANES), lambda i, j: (j, 0))
    vw = MLA_HEADS * MLA_V
    in_specs = [row(d), _const_spec((1, d)),
                _const_spec((d, MLA_Q_LORA)), _const_spec((1, MLA_Q_LORA)),
                _const_spec((MLA_Q_LORA, MLA_HEADS * LANES)), _const_spec((MLA_Q_LORA, MLA_HEADS * LANES)),
                _const_spec((1, LANES)),
                _const_spec((d, MLA_KV_LORA)), _const_spec((1, MLA_KV_LORA)),
                _const_spec((d, LANES)), _const_spec((d, LANES)),
                tab(), tab()]
    args = [h, w["g_mix"], w["wcq"], w["g_qlat"], w["wuq"], w["wuq_swap"], w["gq"], w["wckv"], w["g_kvlat"],
            w["wkr"], w["wkr_swap"], *tables]
    kr_w = MLA_ROPE if with_kv else LANES
    out_specs = [row(MLA_HEADS * LANES), row(MLA_KV_LORA), row(kr_w)]
    out_shape = [jax.ShapeDtypeStruct((b, t, MLA_HEADS * LANES), BF16),
                 jax.ShapeDtypeStruct((b, t, MLA_KV_LORA), F32),
                 jax.ShapeDtypeStruct((b, t, kr_w), F32)]
    if with_kv:
        in_specs += [_const_spec((MLA_KV_LORA, MLA_HEADS * LANES)), _const_spec((vw, MLA_KV_LORA)),
                     _const_spec((1, LANES))]
        args += [w["wuk"], w["wuvt"], w["gk"]]
        out_specs += [row(MLA_HEADS * LANES), pl.BlockSpec((1, vw, tm), lambda i, j: (i, 0, j))]
        out_shape += [jax.ShapeDtypeStruct((b, t, MLA_HEADS * LANES), BF16),
                      jax.ShapeDtypeStruct((b, vw, t), BF16)]
    return pl.pallas_call(
        _mla_in_body,
        grid=(b, t // tm),
        in_specs=in_specs,
        out_specs=out_specs,
        out_shape=out_shape,
        compiler_params=_params("parallel", "parallel"),
        name="mla_in",
    )(*args)


def _mla_kv_body(lat_ref, krp_ref, wuk_ref, wuvt_ref, gk_ref, k_ref, vt_ref):
    _mla_keys_values(lat_ref[0], krp_ref[0], wuk_ref, wuvt_ref, gk_ref, k_ref, vt_ref)


def _mla_kv(lat, krp, w):
    b, l, _ = lat.shape
    tl = ROW_TILE if l % ROW_TILE == 0 else l
    row = lambda n: pl.BlockSpec((1, tl, n), lambda i, j: (i, j, 0))
    vw = MLA_HEADS * MLA_V
    return pl.pallas_call(
        _mla_kv_body,
        grid=(b, l // tl),
        in_specs=[row(MLA_KV_LORA), row(LANES), _const_spec((MLA_KV_LORA, MLA_HEADS * LANES)),
                  _const_spec((vw, MLA_KV_LORA)), _const_spec((1, LANES))],
        out_specs=[row(MLA_HEADS * LANES), pl.BlockSpec((1, vw, tl), lambda i, j: (i, 0, j))],
        out_shape=[jax.ShapeDtypeStruct((b, l, MLA_HEADS * LANES), BF16),
                   jax.ShapeDtypeStruct((b, vw, l), BF16)],
        compiler_params=_params("parallel", "parallel"),
        name="mla_kv",
    )(lat, krp, w["wuk"], w["wuvt"], w["gk"])


def _lru_body(h_ref, g_ref, wrec_ref, wgate_ref, cw_ref, cb_ref, wr_ref, wi_ref, br_ref, bi_ref, lam_ref,
              cprev_ref, h0_ref, y_ref, hl_ref, cl_ref, buf, a_s, b_s, hcar, *, tm):
    @pl.when(pl.program_id(1) == 0)
    def _():
        buf[0:HALO, :] = cprev_ref[0]
        hcar[...] = h0_ref[0]

    hb = _rms(h_ref[0], g_ref[...]).astype(BF16)
    xr = _dot(hb, wrec_ref[...])
    xg = _dot(hb, wgate_ref[...])
    buf[HALO:HALO + tm, :] = xr
    xc = cb_ref[...] + xr * cw_ref[CONV_WIDTH - 1:CONV_WIDTH, :]
    for j in range(CONV_WIDTH - 1):
        off = HALO - (CONV_WIDTH - 1) + j
        xc = xc + cw_ref[j:j + 1, :] * buf[off:off + tm, :]
    xcb = xc.astype(BF16)
    r = _sigmoid(_dot(xcb, wr_ref[...]) + br_ref[...])
    i = _sigmoid(_dot(xcb, wi_ref[...]) + bi_ref[...])
    log_a = (-LRU_C) * r * _softplus(-lam_ref[...])
    a = jnp.exp(log_a)
    z = -jnp.tanh(log_a) * (a * a + 1.0)
    b = jnp.where(z > 0.0, z * lax.rsqrt(z), 0.0) * (i * xc)
    a_s[...] = a
    b_s[...] = b

    row = lax.broadcasted_iota(jnp.int32, (SUBLANES, a.shape[1]), 0)

    def step(g, hprev):
        r0 = pl.multiple_of(g * SUBLANES, SUBLANES)
        ag = a_s[pl.ds(r0, SUBLANES), :]
        bg = b_s[pl.ds(r0, SUBLANES), :]
        d = 1
        while d < SUBLANES:
            keep = row >= d
            a_up = jnp.where(keep, pltpu.roll(ag, d, 0), 1.0)
            b_up = jnp.where(keep, pltpu.roll(bg, d, 0), 0.0)
            bg = ag * b_up + bg
            ag = ag * a_up
            d *= 2
        hg = ag * hprev + bg
        b_s[pl.ds(r0, SUBLANES), :] = hg
        return hg[SUBLANES - 1:, :]

    hfin = lax.fori_loop(0, tm // SUBLANES, step, hcar[...], unroll=2)
    hcar[...] = hfin
    y_ref[0] = (_gelu_tanh(xg) * b_s[...]).astype(BF16)
    hl_ref[0] = hfin
    tail = buf[tm:tm + HALO, :]
    buf[0:HALO, :] = tail
    cl_ref[0] = tail


def _lru(h, w, conv_prev8, h0):
    b, t, d = h.shape
    tm = _row_tile(t, ROW_TILE)
    wd = LRU_WIDTH
    vec = lambda: _const_spec((1, wd))
    return pl.pallas_call(
        functools.partial(_lru_body, tm=tm),
        grid=(b, t // tm),
        in_specs=[pl.BlockSpec((1, tm, d), lambda i, j: (i, j, 0)), _const_spec((1, d)),
                  _const_spec((d, wd)), _const_spec((d, wd)), _const_spec((CONV_WIDTH, wd)), vec(),
                  _const_spec((wd, wd)), _const_spec((wd, wd)), vec(), vec(), vec(),
                  pl.BlockSpec((1, HALO, wd), lambda i, j: (i, 0, 0)),
                  pl.BlockSpec((1, 1, wd), lambda i, j: (i, 0, 0))],
        out_specs=[pl.BlockSpec((1, tm, wd), lambda i, j: (i, j, 0)),
                   pl.BlockSpec((1, 1, wd), lambda i, j: (i, 0, 0)),
                   pl.BlockSpec((1, HALO, wd), lambda i, j: (i, 0, 0))],
        out_shape=[jax.ShapeDtypeStruct((b, t, wd), BF16), jax.ShapeDtypeStruct((b, 1, wd), F32),
                   jax.ShapeDtypeStruct((b, HALO, wd), F32)],
        scratch_shapes=[pltpu.VMEM((tm + HALO, wd), F32), pltpu.VMEM((tm, wd), F32),
                        pltpu.VMEM((tm, wd), F32), pltpu.VMEM((1, wd), F32)],
        compiler_params=_params("parallel", "arbitrary"),
        name="lru",
    )(h, w["g_mix"], w["wrec"], w["wgate"], w["conv_w"], w["conv_b"], w["wr"], w["wi"], w["br"], w["bi"],
      w["lam"], conv_prev8, h0)


def _group_rms(x, g):
    low = lax.broadcasted_iota(jnp.int32, (1, LANES), 1) < FOX_HEAD_DIM
    out = []
    for j in range(x.shape[1] // LANES):
        xb = x[:, j * LANES:(j + 1) * LANES]
        sq = xb * xb
        s_low = jnp.sum(jnp.where(low, sq, 0.0), axis=-1, keepdims=True)
        s_high = jnp.sum(jnp.where(low, 0.0, sq), axis=-1, keepdims=True)
        ss = jnp.where(low, s_low, s_high)
        out.append(xb * lax.rsqrt(ss * (1.0 / FOX_HEAD_DIM) + NORM_EPS))
    return jnp.concatenate(out, axis=-1) * g


def _cumsum_rows(x, ltri):
    out = None
    for p in _split_bf16(x, 3):
        d = _dot(ltri, p)
        out = d if out is None else out + d
    return out


def _fox_aug(x, keep, bias):
    blocks = [x[:, (hd // 2) * LANES:(hd // 2 + 1) * LANES] for hd in range(FOX_HEADS)]
    return jnp.concatenate(blocks, axis=-1) * keep + bias


def _fox_key_bias(c, place):
    lane = lax.broadcasted_iota(jnp.int32, (1, LANES), 1)
    hi, mid, lo = _split_bf16(jnp.where(lane < FOX_HEADS, c * (-LOG2E), 0.0), 3)
    packed = (hi.astype(F32) + pltpu.roll(mid.astype(F32), FOX_HEADS, 1)
              + pltpu.roll(lo.astype(F32), 2 * FOX_HEADS, 1))
    return _dot(packed.astype(BF16), place)


def _fox_in_body(h_ref, g_ref, wq_ref, wk_ref, wv_ref, wf_ref, bf_ref, gq_ref, gk_ref,
                 keep_ref, ones_ref, place_ref, ltri_ref, c0_ref,
                 q_ref, k32_ref, v32_ref, ka_ref, vb_ref, lf_ref, lf_s, carry, *, tm, tc, values_transposed):
    @pl.when(pl.program_id(1) == 0)
    def _():
        carry[...] = c0_ref[0]

    hb = _rms(h_ref[0], g_ref[...]).astype(BF16)
    keep = keep_ref[...]
    q = _group_rms(_dot(hb, wq_ref[...]), gq_ref[...])
    q_ref[0] = _fox_aug(q, keep, ones_ref[...]).astype(BF16)
    k = _group_rms(_dot(hb, wk_ref[...]), gk_ref[...])
    k32_ref[0] = k
    v = _dot(hb, wv_ref[...])
    v32_ref[0] = v
    vb_ref[0] = (v.T if values_transposed else v).astype(BF16)
    logf =-_softplus(-(_dot(hb, wf_ref[...]) + bf_ref[...]))
    lf_ref[0] = logf[:, :FOX_HEADS]
    if tc > tm:
        lf_s[...] = jnp.zeros_like(lf_s)
    lf_s[0:tm, :] = logf
    c = carry[...] + _cumsum_rows(lf_s[...], ltri_ref[...])[0:tm, :]
    carry[...] = c[tm - 1:tm, :]
    ka_ref[0] = _fox_aug(k, keep, _fox_key_bias(c, place_ref[...])).astype(BF16)


def _fox_in(h, w, c0):
    b, t, d = h.shape
    tm = _row_tile(t, ROW_TILE)
    tc = max(tm, LANES)
    ltri = jnp.tril(jnp.ones((tc, tc), F32)).astype(BF16)
    row = lambda n: pl.BlockSpec((1, tm, n), lambda i, j: (i, j, 0))
    fw, aw = FOX_WIDTH, FOX_HEADS * LANES
    values_transposed = tm % LANES == 0
    v_spec = pl.BlockSpec((1, fw, tm), lambda i, j: (i, 0, j)) if values_transposed else row(fw)
    v_shape = (b, fw, t) if values_transposed else (b, t, fw)
    outs = pl.pallas_call(
        functools.partial(_fox_in_body, tm=tm, tc=tc, values_transposed=values_transposed),
        grid=(b, t // tm),
        in_specs=[row(d), _const_spec((1, d)),
                  _const_spec((d, fw)), _const_spec((d, fw)), _const_spec((d, fw)), _const_spec((d, LANES)),
                  _const_spec((1, LANES)), _const_spec((1, fw)), _const_spec((1, fw)),
                  _const_spec((1, aw)), _const_spec((1, aw)), _const_spec((LANES, aw)),
                  _const_spec((tc, tc)), pl.BlockSpec((1, 1, LANES), lambda i, j: (i, 0, 0))],
        out_specs=[row(aw), row(fw), row(fw), row(aw), v_spec, row(FOX_HEADS)],
        out_shape=[jax.ShapeDtypeStruct((b, t, aw), BF16), jax.ShapeDtypeStruct((b, t, fw), F32),
                   jax.ShapeDtypeStruct((b, t, fw), F32), jax.ShapeDtypeStruct((b, t, aw), BF16),
                   jax.ShapeDtypeStruct(v_shape, BF16), jax.ShapeDtypeStruct((b, t, FOX_HEADS), F32)],
        scratch_shapes=[pltpu.VMEM((tc, LANES), F32), pltpu.VMEM((1, LANES), F32)],
        compiler_params=_params("parallel", "arbitrary"),
        name="fox_in",
    )(h, w["g_mix"], w["wq"], w["wk"], w["wv"], w["wf"], w["bf"], w["gq"], w["gk"],
      w["keep"], w["ones"], w["place"], ltri, c0)
    q, k32, v32, ka, vb, logf = outs
    return q, k32, v32, ka, (vb if values_transposed else jnp.swapaxes(vb, 1, 2)), logf


def _fox_past_body(k_ref, lf_ref, keep_ref, place_ref, ltri_ref, ka_ref, c_ref, carry):
    @pl.when(pl.program_id(1) == 0)
    def _():
        carry[...] = jnp.zeros_like(carry)

    c = carry[...] + _cumsum_rows(lf_ref[0], ltri_ref[...])
    c_ref[0] = c
    carry[...] = c[c.shape[0] - 1:, :]
    ka_ref[0] = _fox_aug(k_ref[0], keep_ref[...], _fox_key_bias(c, place_ref[...])).astype(BF16)


def _fox_past(past_k, past_logf, w):
    b, p, n = past_logf.shape
    tc = _row_tile(p, ROW_TILE)
    ltri = jnp.tril(jnp.ones((tc, tc), F32)).astype(BF16)
    fw, aw = FOX_WIDTH, FOX_HEADS * LANES
    row = lambda m: pl.BlockSpec((1, tc, m), lambda i, j: (i, j, 0))
    return pl.pallas_call(
        _fox_past_body,
        grid=(b, p // tc),
        in_specs=[row(fw), row(n), _const_spec((1, aw)), _const_spec((LANES, aw)), _const_spec((tc, tc))],
        out_specs=[row(aw), row(n)],
        out_shape=[jax.ShapeDtypeStruct((b, p, aw), BF16), jax.ShapeDtypeStruct((b, p, n), F32)],
        scratch_shapes=[pltpu.VMEM((1, n), F32)],
        compiler_params=_params("parallel", "arbitrary"),
        name="fox_past",
    )(past_k, past_logf, w["keep"], w["place"], ltri)


def _flash_body(q_ref, k_ref, vt_ref, o_ref, m_s, l_s, acc_s, sa_s, sb_s, *, tq, tqs, tk, tks, fr, fc, n_k, past,
                kv_len, chunk_causal, diag_aligned):
    q_start = past + pl.program_id(2) * tq
    q = q_ref[0]
    m_s[...] = jnp.full_like(m_s, NEG_INF)
    l_s[...] = jnp.zeros_like(l_s)
    acc_s[...] = jnp.zeros_like(acc_s)
    shift = int(math.log2(CHUNK))
    hv = LANES // 2

    def block_kind(r0, rn, c0, cn):
        if chunk_causal:
            k_lo, k_hi, q_lo, q_hi = r0 >> shift, (r0 + rn - 1) >> shift, c0 >> shift, (c0 + cn - 1) >> shift
        else:
            k_lo, k_hi, q_lo, q_hi = r0, r0 + rn - 1, c0, c0 + cn - 1
        return "visible" if k_hi <= q_lo else ("hidden" if k_lo > q_hi else "partial")

    def diag_streams(d):
        out = []
        for r0 in range(0, tk, tks):
            for hh in range(2):
                for c0 in range(0, tq, tqs):
                    halves = [(c, block_kind(d * tk + r0, tks, c, tks)) for c in range(c0, c0 + tqs, tks)]
                    if all(kind == "visible" for _, kind in halves):
                        out.append((hh, r0, tks, c0, tqs, "visible"))
                    else:
                        out += [(hh, r0, tks, c, tks, kind) for c, kind in halves if kind != "hidden"]
        return out

    full_streams = [(hh, r0, fr, c0, fc, "visible") for r0 in range(0, tk, fr) for hh in range(2)
                    for c0 in range(0, tq, fc)]
    mask_streams = [st[:5] + ("partial",) for st in full_streams]

    def score(kblk, stream):
        hh, r0, rn, c0, cn, _ = stream
        head = slice(hh * LANES, (hh + 1) * LANES)
        return _dot_nt(kblk[r0:r0 + rn, head], q[c0:c0 + cn, head])

    def key_block(kt):
        return k_ref[0, pl.ds(pl.multiple_of(jnp.minimum(kt, n_k - 1) * tk, tk), tk), :]

    def absorb(s, stream, kt):
        hh, r0, rn, c0, cn, kind = stream
        cols = slice(c0, c0 + cn)
        k0 = pl.multiple_of(kt * tk, tk)
        if kind == "partial":
            kpos = k0 + r0 + lax.broadcasted_iota(jnp.int32, (rn, cn), 0)
            qpos = q_start + c0 + lax.broadcasted_iota(jnp.int32, (rn, cn), 1)
            if chunk_causal:
                vis = lax.shift_right_logical(kpos, shift) <= lax.shift_right_logical(qpos, shift)
            else:
                vis = kpos <= qpos
            s = jnp.where(jnp.logical_and(vis, kpos < kv_len), s, NEG_INF)
        m_old = m_s[hh, :, cols]
        m_new = jnp.maximum(m_old, jnp.max(s, axis=0, keepdims=True))
        alpha = jnp.exp2(m_old - m_new)
        p = jnp.exp2(s - m_new)
        l_s[hh, :, cols] = alpha * l_s[hh, :, cols] + jnp.sum(p, axis=0, keepdims=True)
        vt = vt_ref[0, hh * hv:(hh + 1) * hv, pl.ds(pl.multiple_of(k0 + r0, LANES), rn)]
        acc_s[hh, :, cols] = alpha * acc_s[hh, :, cols] + _dot(vt, p.astype(BF16))
        m_s[hh, :, cols] = m_new

    def single_tile(kt, streams, after_first_scores=None):
        kblk = key_block(kt)
        ahead = 3
        pending = [score(kblk, st) for st in streams[:ahead]]
        if after_first_scores is not None:
            after_first_scores()
        for idx, st in enumerate(streams):
            s = pending.pop(0)
            if idx + ahead < len(streams):
                pending.append(score(kblk, streams[idx + ahead]))
            absorb(s, st, kt)

    def store_scores(buf, kt):
        kblk = key_block(kt)
        for st in full_streams:
            hh, r0, rn, c0, cn, _ = st
            buf[hh, r0:r0 + rn, c0:c0 + cn] = score(kblk, st)

    def absorb_stored(buf, kt):
        for st in full_streams:
            hh, r0, rn, c0, cn, _ = st
            absorb(buf[hh, r0:r0 + rn, c0:c0 + cn], st, kt)

    def store_and_absorb(store_buf, store_kt, absorb_buf, absorb_kt):
        kblk = key_block(store_kt)
        for st in full_streams:
            hh, r0, rn, c0, cn, _ = st
            store_buf[hh, r0:r0 + rn, c0:c0 + cn] = score(kblk, st)
            absorb(absorb_buf[hh, r0:r0 + rn, c0:c0 + cn], st, absorb_kt)

    def tile_pair(i, carry):
        kt = 2 * i
        store_and_absorb(sb_s, kt + 1, sa_s, kt)
        store_and_absorb(sa_s, kt + 2, sb_s, kt + 1)
        return carry

    def full_tile(kt, carry):
        single_tile(kt, full_streams)
        return carry

    def masked_tile(kt, carry):
        single_tile(kt, mask_streams)
        return carry

    n_full = jnp.minimum(q_start // tk, kv_len // tk)
    q_last = q_start + tq - 1
    k_hi = (q_last // CHUNK + 1) * CHUNK if chunk_causal else q_last + 1
    n_end = jnp.minimum((k_hi + tk - 1) // tk, n_k)
    n_pair = n_full // 2
    if diag_aligned:
        for d in range(tq // tk):
            single_tile(n_full + d, diag_streams(d),
                        after_first_scores=(lambda: store_scores(sa_s, 0)) if d == 0 else None)
        lax.fori_loop(0, n_pair, tile_pair, 0)
        lax.fori_loop(2 * n_pair, n_full, full_tile, 0)
    else:
        @pl.when(n_pair > 0)
        def _():
            store_scores(sa_s, 0)

        lax.fori_loop(0, n_pair, tile_pair, 0)
        lax.fori_loop(2 * n_pair, n_full, full_tile, 0)
        lax.fori_loop(n_full, n_end, masked_tile, 0)
    out_t = jnp.concatenate([acc_s[0] / l_s[0], acc_s[1] / l_s[1]], axis=0)
    o_ref[0] = out_t.T.astype(BF16)


def _flash(q, k, vt, *, past, kv_len, chunk_causal):
    b, t, w = q.shape
    lp = k.shape[1]
    n_pairs = w // (2 * LANES)
    t_pad = max(t, LANES)
    if t_pad > t:
        q = _pad_rows(q, t_pad)
    if past == 0 and t % KEY_TILE == 0 and lp % KEY_TILE == 0:
        tk = KEY_TILE
        tq = QUERY_TILE if t % QUERY_TILE == 0 else KEY_TILE
    elif t_pad == LANES:
        tq, tk = LANES, lp
    else:
        tq = tk = LANES
    assert t_pad % tq == 0 and lp % tk == 0, (t, lp, tq, tk)
    tks = MXU_TILE if tk % MXU_TILE == 0 and tq > LANES else tk
    tqs = min(tq, 2 * MXU_TILE)
    fr, fc = (tk, MXU_TILE) if tks < tk and tq % MXU_TILE == 0 else (tks, tqs)
    out = pl.pallas_call(
        functools.partial(_flash_body, tq=tq, tqs=tqs, tk=tk, tks=tks, fr=fr, fc=fc, n_k=lp // tk, past=past,
                          kv_len=kv_len, chunk_causal=chunk_causal,
                          diag_aligned=(past == 0 and tq % tk == 0 and kv_len == lp and tks < tk)),
        grid=(b, n_pairs, t_pad // tq),
        in_specs=[pl.BlockSpec((1, tq, 2 * LANES), lambda i, j, s: (i, s, j)),
                  pl.BlockSpec((1, lp, 2 * LANES), lambda i, j, s: (i, 0, j)),
                  pl.BlockSpec((1, LANES, lp), lambda i, j, s: (i, j, 0))],
        out_specs=pl.BlockSpec((1, tq, LANES), lambda i, j, s: (i, s, j)),
        out_shape=jax.ShapeDtypeStruct((b, t_pad, n_pairs * LANES), BF16),
        scratch_shapes=[pltpu.VMEM((2, 1, tq), F32), pltpu.VMEM((2, 1, tq), F32),
                        pltpu.VMEM((2, LANES // 2, tq), F32),
                        pltpu.VMEM((2, tk, tq), F32), pltpu.VMEM((2, tk, tq), F32)],
        compiler_params=_params("parallel", "parallel", "arbitrary"),
        name="flash_mla" if chunk_causal else "flash_fox",
    )(q, k, vt)
    return out[:, :t]


def _row(v):
    return v.reshape(1, -1).astype(F32)


def _pad_lanes(x, lo, total):
    pad = [(0, 0)] * (x.ndim - 1) + [(lo, total - lo - x.shape[-1])]
    return jnp.pad(x, pad)


def _ffn_weights(g, w_in, w_out):
    f = w_out.shape[0]
    return dict(g=_row(g), wg=w_in[:, :f].astype(BF16), wu=w_in[:, f:].astype(BF16), wo=w_out.astype(BF16))


def _mem_weights(g, g_src, w_q, w_kv, w_o, g_q, g_k):
    d = w_q.shape[0]
    kv = w_kv.reshape(d, MEM_HEADS, 2, MEM_HEAD_DIM)
    return dict(g=_row(g), g_src=_row(g_src), wq=w_q.astype(BF16), wo=w_o.astype(BF16),
                wk=kv[:, :, 0].reshape(d, MEM_WIDTH).astype(BF16),
                wv=kv[:, :, 1].reshape(d, MEM_WIDTH).astype(BF16),
                gq=_row(g_q) * (MEM_HEAD_DIM ** -0.5), gk=_row(g_k))


def _even_weights(g_mix, w_in, g_qlat, g_kvlat, w_uq, w_ukv, g_q, g_k, conv_w, conv_b, gate_w, gate_b, lam, w_out):
    d = w_in.shape[0]
    o1 = MLA_Q_LORA
    o2 = o1 + MLA_KV_LORA
    o3 = o2 + MLA_ROPE
    o4 = o3 + LRU_WIDTH
    half = MLA_ROPE // 2
    swap_halves = lambda r: jnp.concatenate([r[..., half:], r[..., :half]], axis=-1)
    uq3 = w_uq.reshape(MLA_Q_LORA, MLA_HEADS, MLA_QK)
    uq = _pad_lanes(uq3, 0, LANES)
    uq_swap = _pad_lanes(swap_halves(uq3[:, :, MLA_NOPE:]), MLA_NOPE, LANES)
    kr = w_in[:, o2:o3]
    ukv = w_ukv.reshape(MLA_KV_LORA, MLA_HEADS, MLA_NOPE + MLA_V)
    uk = _pad_lanes(ukv[:, :, :MLA_NOPE], 0, LANES)
    blk = LRU_WIDTH // LRU_BLOCKS
    eye = jnp.eye(LRU_BLOCKS, dtype=F32)
    wr = jnp.einsum("ncd,nm->ncmd", gate_w[:, :, :blk], eye).reshape(LRU_WIDTH, LRU_WIDTH)
    wi = jnp.einsum("ncd,nm->ncmd", gate_w[:, :, blk:], eye).reshape(LRU_WIDTH, LRU_WIDTH)
    return dict(
        g_mix=_row(g_mix), wcq=w_in[:, :o1].astype(BF16), g_qlat=_row(g_qlat),
        wuq=uq.reshape(MLA_Q_LORA, MLA_HEADS * LANES).astype(BF16),
        wuq_swap=uq_swap.reshape(MLA_Q_LORA, MLA_HEADS * LANES).astype(BF16),
        gq=_pad_lanes(_row(g_q), 0, LANES) * (MLA_QK ** -0.5 * LOG2E),
        wckv=w_in[:, o1:o2].astype(BF16), g_kvlat=_row(g_kvlat),
        wkr=_pad_lanes(kr, MLA_NOPE, LANES).astype(BF16),
        wkr_swap=_pad_lanes(swap_halves(kr), MLA_NOPE, LANES).astype(BF16),
        wuk=uk.reshape(MLA_KV_LORA, MLA_HEADS * LANES).astype(BF16),
        wuvt=ukv[:, :, MLA_NOPE:].reshape(MLA_KV_LORA, MLA_HEADS * MLA_V).T.astype(BF16),
        gk=_pad_lanes(_row(g_k), 0, LANES),
        wrec=w_in[:, o3:o4].astype(BF16), wgate=w_in[:, o4:].astype(BF16),
        conv_w=conv_w.astype(F32), conv_b=_row(conv_b), wr=wr.astype(BF16), wi=wi.astype(BF16),
        br=_row(gate_b[:, :blk]), bi=_row(gate_b[:, blk:]), lam=_row(lam),
        wo_attn=w_out[:MLA_HEADS * MLA_V].astype(BF16), wo_rec=w_out[MLA_HEADS * MLA_V:].astype(BF16))


def _odd_weights(g_mix, w_in, b_f, g_q, g_k, w_out):
    fw = FOX_WIDTH
    lane = jnp.arange(FOX_HEADS * LANES)
    hd, within = lane // LANES, lane % LANES
    own_low = hd % 2 == 0
    keep = jnp.where(own_low, within < FOX_HEAD_DIM, within >= FOX_HEAD_DIM)
    part = within - jnp.where(own_low, FOX_HEAD_DIM, 0)
    is_bias = (part >= 0) & (part < 3)
    src = part * FOX_HEADS + hd
    place = ((jnp.arange(LANES)[:, None] == src[None, :]) & is_bias[None, :]).astype(BF16)
    return dict(
        keep=keep.astype(F32)[None, :], ones=is_bias.astype(F32)[None, :], place=place,
        g_mix=_row(g_mix), wq=w_in[:, :fw].astype(BF16), wk=w_in[:, fw:2 * fw].astype(BF16),
        wv=w_in[:, 2 * fw:3 * fw].astype(BF16), wf=_pad_lanes(w_in[:, 3 * fw:], 0, LANES).astype(BF16),
        bf=_pad_lanes(_row(b_f), 0, LANES),
        gq=jnp.tile(_row(g_q), (1, FOX_HEADS)) * (FOX_HEAD_DIM ** -0.5 * LOG2E),
        gk=jnp.tile(_row(g_k), (1, FOX_HEADS)), wo=w_out.astype(BF16))


def _rope_tables(pos):
    half = MLA_ROPE // 2
    inv_freq = ROPE_THETA ** (-jnp.arange(half, dtype=F32) / half)
    ang = pos.astype(F32)[:, None] * inv_freq[None, :]
    cos, sin = jnp.cos(ang), jnp.sin(ang)
    c = jnp.concatenate([jnp.ones((pos.shape[0], MLA_NOPE), F32), cos, cos,
                         jnp.ones((pos.shape[0], LANES - MLA_QK), F32)], axis=-1)
    s = _pad_lanes(jnp.concatenate([-sin, sin], axis=-1), MLA_NOPE, LANES)
    return c, s


def _pad_rows(x, total):
    return jnp.pad(x, [(0, 0), (0, total - x.shape[1])] + [(0, 0)] * (x.ndim - 2))


def _kv_pad_len(t, past):
    l = past + t
    if past == 0 and t % KEY_TILE == 0:
        return l
    return -(-l // LANES) * LANES


def _even_layer(h, past, w, state):
    b, t, d = h.shape
    past_lat, past_krope, h0, conv_prev = state
    lp = _kv_pad_len(t, past)
    tables = _rope_tables(past + jnp.arange(t))
    if past == 0 and lp == t and t % LANES == 0:
        q, lat_new, krope_new, k, vt = _mla_in(h, w, tables, with_kv=True)
    else:
        q, lat_new, krp_new = _mla_in(h, w, tables, with_kv=False)
        krope_new = krp_new[:, :, MLA_NOPE:MLA_QK]
        lat_all = _pad_rows(jnp.concatenate([past_lat, lat_new], axis=1), lp)
        krp_all = _pad_rows(jnp.concatenate([_pad_lanes(past_krope, MLA_NOPE, LANES), krp_new], axis=1), lp)
        k, vt = _mla_kv(lat_all, krp_all, w)
    attn = _flash(q, k, vt, past=past, kv_len=past + t, chunk_causal=True)
    conv_prev8 = jnp.pad(conv_prev, ((0, 0), (HALO - (CONV_WIDTH - 1), 0), (0, 0)))
    y_rec, h_last, conv_last = _lru(h, w, conv_prev8, h0[:, None, :])
    new = (lat_new, krope_new, h_last[:, 0], conv_last[:, HALO - (CONV_WIDTH - 1):])
    return [attn, y_rec], [w["wo_attn"], w["wo_rec"]], new


def _odd_layer(h, past, w, state):
    b, t, d = h.shape
    past_k, past_v, past_logf = state
    if past > 0:
        ka_past, c_past = _fox_past(past_k.reshape(b, past, FOX_WIDTH),
                                    _pad_lanes(past_logf.astype(F32), 0, LANES), w)
        c0 = c_past[:, past - 1:past, :]
    else:
        ka_past = jnp.zeros((b, 0, FOX_HEADS * LANES), BF16)
        c0 = jnp.zeros((b, 1, LANES), F32)
    q, k32, v32, ka_new, vt_new, logf = _fox_in(h, w, c0)
    lp = _kv_pad_len(t, past)
    k_all = _pad_rows(jnp.concatenate([ka_past, ka_new], axis=1), lp)
    vt_past = jnp.swapaxes(past_v.reshape(b, past, FOX_WIDTH).astype(BF16), 1, 2)
    vt_all = jnp.pad(jnp.concatenate([vt_past, vt_new], axis=2), ((0, 0), (0, 0), (0, lp - past - t)))
    attn = _flash(q, k_all, vt_all, past=past, kv_len=past + t, chunk_causal=False)
    new = (k32.reshape(b, t, FOX_HEADS, FOX_HEAD_DIM), v32.reshape(b, t, FOX_HEADS, FOX_HEAD_DIM),
           logf)
    return [attn], [w["wo"]], new


def _trunk(x, past, layers, mem_kvs, even_states, odd_states):
    b, t, d = x.shape
    even_new, odd_new = [], []
    for li, lw in enumerate(layers):
        h = _ffn(x.reshape(b * t, d), lw["ffn1"]).reshape(b, t, d)
        if li % 2 == 0:
            parts, w_parts, new = _even_layer(h, past, lw["mix"], even_states[li // 2])
            even_new.append(new)
        else:
            parts, w_parts, new = _odd_layer(h, past, lw["mix"], odd_states[li // 2])
            odd_new.append(new)
        x = _post_mixer(h, parts, w_parts, mem_kvs[li][0], mem_kvs[li][1], lw["mem"], lw["ffn2"])
    return x, even_new, odd_new


def kernel(x_prompt, x_sample, mem_prompt, cache_mla_latent, cache_mla_krope, state_lru_h, state_lru_conv, cache_fox_k, cache_fox_v, cache_fox_logf, cache_mem_k, cache_mem_v, norm_ffn1, ffn1_w_in, ffn1_w_out, norm_mix, norm_mem, norm_mem_src, mem_w_q, mem_w_kv, mem_w_o, mem_g_q, mem_g_k, norm_ffn2, ffn2_w_in, ffn2_w_out, ev_w_in, ev_g_qlat, ev_g_kvlat, ev_w_uq, ev_w_ukv, ev_g_q, ev_g_k, ev_conv_w, ev_conv_b, ev_gate_w, ev_gate_b, ev_lambda, ev_w_out, od_w_in, od_b_f, od_g_q, od_g_k, od_w_out):
    depth = norm_ffn1.shape[0]
    n_even, n_odd = (depth + 1) // 2, depth // 2
    b, _, _ = x_prompt.shape
    bs = x_sample.shape[0]
    past = cache_mla_latent.shape[2] if n_even else cache_fox_k.shape[2]

    layers = []
    for li in range(depth):
        j = li // 2
        if li % 2 == 0:
            mix = _even_weights(norm_mix[li], ev_w_in[j], ev_g_qlat[j], ev_g_kvlat[j], ev_w_uq[j], ev_w_ukv[j],
                                ev_g_q[j], ev_g_k[j], ev_conv_w[j], ev_conv_b[j], ev_gate_w[j], ev_gate_b[j],
                                ev_lambda[j], ev_w_out[j])
        else:
            mix = _odd_weights(norm_mix[li], od_w_in[j], od_b_f[j], od_g_q[j], od_g_k[j], od_w_out[j])
        layers.append(dict(
            ffn1=_ffn_weights(norm_ffn1[li], ffn1_w_in[li], ffn1_w_out[li]),
            ffn2=_ffn_weights(norm_ffn2[li], ffn2_w_in[li], ffn2_w_out[li]),
            mem=_mem_weights(norm_mem[li], norm_mem_src[li], mem_w_q[li], mem_w_kv[li], mem_w_o[li],
                             mem_g_q[li], mem_g_k[li]),
            mix=mix))

    p_mem_k, p_mem_v, mem_kb, mem_vb = _mem_kv(mem_prompt, [lw["mem"] for lw in layers])
    ev0 = [(jnp.zeros((b, 0, MLA_KV_LORA), F32), jnp.zeros((b, 0, MLA_ROPE), F32),
            jnp.zeros((b, LRU_WIDTH), F32), jnp.zeros((b, CONV_WIDTH - 1, LRU_WIDTH), F32))
           for _ in range(n_even)]
    od0 = [(jnp.zeros((b, 0, FOX_HEADS, FOX_HEAD_DIM), F32), jnp.zeros((b, 0, FOX_HEADS, FOX_HEAD_DIM), F32),
            jnp.zeros((b, 0, FOX_HEADS), F32)) for _ in range(n_odd)]
    y_prompt, ev_p, od_p = _trunk(x_prompt, 0, layers, [(mem_kb[li], mem_vb[li]) for li in range(depth)],
                                  ev0, od0)

    m_tok = cache_mem_k.shape[2]
    mem_s = [(cache_mem_k[li].reshape(bs, m_tok, MEM_WIDTH).astype(BF16),
              cache_mem_v[li].reshape(bs, m_tok, MEM_WIDTH).astype(BF16)) for li in range(depth)]
    ev_s = [(cache_mla_latent[j], cache_mla_krope[j], state_lru_h[j], state_lru_conv[j]) for j in range(n_even)]
    od_s = [(cache_fox_k[j], cache_fox_v[j], cache_fox_logf[j]) for j in range(n_odd)]
    y_sample, ev_n, od_n = _trunk(x_sample, past, layers, mem_s, ev_s, od_s)

    p_even = [jnp.stack([s[f] for s in ev_p]) for f in range(4)]
    p_odd = [jnp.stack([s[f] for s in od_p]) for f in range(3)]
    s_even = [jnp.stack([s[f] for s in ev_n]) for f in range(4)]
    s_odd = [jnp.stack([s[f] for s in od_n]) for f in range(3)]
    return (y_prompt, y_sample, *p_even, *p_odd, p_mem_k, p_mem_v, *s_even, *s_odd)
```
